```python
import math
import jax
import jax.numpy as jnp
from jax import lax
import numpy as np

D_MODEL = 1024
BATCH = 8
SEQ = 2048
DEPTH = 2
DEC_BATCH = 16
DEC_SEQ = 16
PAST_LEN = 1024

CHUNK = 64
SGU_CHUNK = 128
Q_BLOCK = 128
MIX_WIDTH = D_MODEL
WIDTH_A = MIX_WIDTH // 2
WIDTH_B = MIX_WIDTH // 2
N_HEADS_A = 4
HEAD_DIM_A = WIDTH_A // N_HEADS_A
N_HEADS_B = 4
HEAD_DIM_V = WIDTH_B // N_HEADS_B
HEAD_DIM_QK = HEAD_DIM_V // 2
QK_WIDTH = N_HEADS_B * 2 * HEAD_DIM_QK
IN_WIDTH = 3 * WIDTH_A + 2 * QK_WIDTH + 2 * WIDTH_B
SPLIT_POINTS = (WIDTH_A, 2 * WIDTH_A, 3 * WIDTH_A,
                3 * WIDTH_A + QK_WIDTH, 3 * WIDTH_A + 2 * QK_WIDTH,
                3 * WIDTH_A + 2 * QK_WIDTH + WIDTH_B)
NORM_EPS = 1e-6
NEG_INF = -1e30

kernel_name = "hybrid_gmlp_diffattn_stream_step"


def rms_norm(x, g):
    xf = x.astype(jnp.float32)
    y = xf * lax.rsqrt(jnp.mean(xf * xf, axis=-1, keepdims=True) + NORM_EPS)
    return (y * g.astype(jnp.float32)).astype(x.dtype)


def alibi_slopes():
    h = jnp.arange(1, N_HEADS_B + 1, dtype=jnp.float32)
    return jnp.exp2(-8.0 * h / N_HEADS_B)


def spatial_gate(v, g, w_s, b_s):
    bsz, t = v.shape[0], v.shape[1]
    length = min(t, SGU_CHUNK)
    n_chunks = t // length
    vn = rms_norm(v, g).reshape(bsz, n_chunks, length, N_HEADS_A, HEAD_DIM_A)
    causal = jnp.tril(jnp.ones((length, length), dtype=w_s.dtype))
    w = w_s[:, :length, :length] * causal
    bias = jnp.transpose(b_s[:, :length])[:, :, None]
    out = jnp.einsum("hts,bnshd->bnthd", w, vn) + bias
    return out.reshape(bsz, t, N_HEADS_A, HEAD_DIM_A)


def diff_attend(q, k, v, q_pos, k_pos, lam):
    s = jnp.einsum("bqhcd,bkhcd->bchqk", q.astype(jnp.float32), k.astype(jnp.float32)) * (HEAD_DIM_QK ** -0.5)
    dist = jnp.abs(q_pos[:, None] - k_pos[None, :]).astype(jnp.float32)
    allowed = (k_pos[None, :] // CHUNK) <= (q_pos[:, None] // CHUNK)
    s = s - alibi_slopes()[None, None, :, None, None] * dist
    s = jnp.where(allowed, s, NEG_INF)
    p = jax.nn.softmax(s, axis=-1)
    a = p[:, 0] - lam * p[:, 1]
    return jnp.einsum("bhqk,bkhd->bqhd", a, v.astype(jnp.float32)).astype(v.dtype)


def mixer_layer(x, pos0, past_k, past_v, layer_idx, blocked,
                norm_g, w_in, sgu_norm_g, sgu_w, sgu_b, q_norm_g, k_norm_g,
                lambda_q1, lambda_k1, lambda_q2, lambda_k2, subln_g, w_out):
    bsz, t = x.shape[0], x.shape[1]
    h = rms_norm(x, norm_g)
    z = jnp.einsum("btd,de->bte", h, w_in)
    u_a, v_a, g_a, q, k, v, g_b = jnp.split(z, SPLIT_POINTS, axis=-1)

    y_a = u_a.reshape(bsz, t, N_HEADS_A, HEAD_DIM_A) * spatial_gate(
        v_a.reshape(bsz, t, N_HEADS_A, HEAD_DIM_A), sgu_norm_g, sgu_w, sgu_b)
    y_a = y_a.reshape(bsz, t, WIDTH_A) * jax.nn.silu(g_a)

    q = rms_norm(q.reshape(bsz, t, N_HEADS_B, 2, HEAD_DIM_QK), q_norm_g)
    k = rms_norm(k.reshape(bsz, t, N_HEADS_B, 2, HEAD_DIM_QK), k_norm_g)
    v = v.reshape(bsz, t, N_HEADS_B, HEAD_DIM_V)
    lam_init = 0.8 - 0.6 * math.exp(-0.3 * layer_idx)
    lam = (jnp.exp(jnp.sum(lambda_q1.astype(jnp.float32) * lambda_k1.astype(jnp.float32)))
           - jnp.exp(jnp.sum(lambda_q2.astype(jnp.float32) * lambda_k2.astype(jnp.float32)))
           + lam_init)
    q_pos = pos0 + jnp.arange(t, dtype=jnp.int32)
    if past_k is None:
        k_all, v_all, k_pos = k, v, q_pos
    else:
        k_all = jnp.concatenate([past_k.astype(k.dtype), k], axis=1)
        v_all = jnp.concatenate([past_v.astype(v.dtype), v], axis=1)
        k_pos = jnp.arange(k_all.shape[1], dtype=jnp.int32)
    if blocked:
        n_blocks = t // Q_BLOCK
        q_blocks = jnp.moveaxis(q.reshape(bsz, n_blocks, Q_BLOCK, N_HEADS_B, 2, HEAD_DIM_QK), 1, 0)
        pos_blocks = q_pos.reshape(n_blocks, Q_BLOCK)
        o = lax.map(lambda args: diff_attend(args[0], k_all, v_all, args[1], k_pos, lam),
                    (q_blocks, pos_blocks))
        o = jnp.moveaxis(o, 0, 1).reshape(bsz, t, N_HEADS_B, HEAD_DIM_V)
    else:
        o = diff_attend(q, k_all, v_all, q_pos, k_pos, lam)
    o = rms_norm(o, subln_g) * (1.0 - lam_init)
    y_b = o.reshape(bsz, t, WIDTH_B) * jax.nn.silu(g_b)

    y = jnp.einsum("bte,ed->btd", jnp.concatenate([y_a, y_b], axis=-1), w_out)
    return (x + y, k, v, v_a)


def setup_inputs(seed: int = 0) -> dict:
    key = jax.random.key(seed)
    ks = jax.random.split(key, 18)
    f32 = jnp.float32

    def nrm(k, shape, scale):
        return scale * jax.random.normal(k, shape, f32)

    return {
        "x_prompt": nrm(ks[0], (BATCH, SEQ, D_MODEL), 1.0),
        "x_sample": nrm(ks[1], (DEC_BATCH, DEC_SEQ, D_MODEL), 1.0),
        "cache_k": nrm(ks[2], (DEPTH, DEC_BATCH, PAST_LEN, N_HEADS_B, 2, HEAD_DIM_QK), 1.0),
        "cache_v": nrm(ks[3], (DEPTH, DEC_BATCH, PAST_LEN, N_HEADS_B, HEAD_DIM_V), 1.0),
        "norm_g": 1.0 + nrm(ks[4], (DEPTH, D_MODEL), 0.05),
        "w_in": nrm(ks[5], (DEPTH, D_MODEL, IN_WIDTH), D_MODEL ** -0.5),
        "sgu_norm_g": 1.0 + nrm(ks[6], (DEPTH, N_HEADS_A, HEAD_DIM_A), 0.05),
        "sgu_w": nrm(ks[7], (DEPTH, N_HEADS_A, SGU_CHUNK, SGU_CHUNK), 0.5 * SGU_CHUNK ** -0.5),
        "sgu_b": 1.0 + nrm(ks[8], (DEPTH, N_HEADS_A, SGU_CHUNK), 0.1),
        "q_norm_g": 1.0 + nrm(ks[9], (DEPTH, HEAD_DIM_QK), 0.05),
        "k_norm_g": 1.0 + nrm(ks[10], (DEPTH, HEAD_DIM_QK), 0.05),
        "lambda_q1": nrm(ks[11], (DEPTH, HEAD_DIM_QK), 0.1),
        "lambda_k1": nrm(ks[12], (DEPTH, HEAD_DIM_QK), 0.1),
        "lambda_q2": nrm(ks[13], (DEPTH, HEAD_DIM_QK), 0.1),
        "lambda_k2": nrm(ks[14], (DEPTH, HEAD_DIM_QK), 0.1),
        "subln_g": 1.0 + nrm(ks[15], (DEPTH, HEAD_DIM_V), 0.05),
        "w_out": nrm(ks[16], (DEPTH, MIX_WIDTH, D_MODEL), MIX_WIDTH ** -0.5),
    }


def reference(x_prompt, x_sample, cache_k, cache_v, norm_g, w_in, sgu_norm_g, sgu_w, sgu_b,
              q_norm_g, k_norm_g, lambda_q1, lambda_k1, lambda_q2, lambda_k2, subln_g, w_out):
    xp, xs = x_prompt, x_sample
    kp_rows, vp_rows, ks_rows, vs_rows, sgu_rows = [], [], [], [], []
    for i in range(DEPTH):
        params = (norm_g[i], w_in[i], sgu_norm_g[i], sgu_w[i], sgu_b[i], q_norm_g[i], k_norm_g[i],
                  lambda_q1[i], lambda_k1[i], lambda_q2[i], lambda_k2[i], subln_g[i], w_out[i])
        xp, kp, vp, _ = mixer_layer(xp, 0, None, None, i, True, *params)
        xs, kn, vn, va = mixer_layer(xs, PAST_LEN, cache_k[i], cache_v[i], i, False, *params)
        kp_rows.append(kp)
        vp_rows.append(vp)
        ks_rows.append(kn)
        vs_rows.append(vn)
        sgu_rows.append(va)
    new_k_prompt = jnp.stack(kp_rows)
    new_v_prompt = jnp.stack(vp_rows)
    new_k_sample = jnp.stack(ks_rows)
    new_v_sample = jnp.stack(vs_rows)
    new_sgu_v_sample = jnp.stack(sgu_rows)
    return (xp, xs, new_k_prompt, new_v_prompt, new_k_sample, new_v_sample, new_sgu_v_sample)
```

```python
import functools
import math

import jax
import jax.numpy as jnp
from jax import lax
from jax.experimental import pallas as pl
from jax.experimental.pallas import tpu as pltpu

F32 = jnp.float32
BF16 = jnp.bfloat16

D_MODEL = 1024
N_HEADS = 4
HEAD_DIM = 128
HALF_DIM = 64
WIDTH = N_HEADS * HEAD_DIM
IN_WIDTH = 7 * WIDTH
CHUNK = 64
SGU_CHUNK = 128
NORM_EPS = 1e-6
NEG_INF = -1e30
QK_SCALE = HALF_DIM ** -0.5

COL_U, COL_VA, COL_GA, COL_Q, COL_K, COL_V, COL_GB = (i * WIDTH for i in range(7))

VMEM_LIMIT_BYTES = 56 * 1024 * 1024

TOKEN_TILE = 512
ATTN_TILE = 512


def _lam_init(layer_idx):
    return 0.8 - 0.6 * math.exp(-0.3 * layer_idx)


def _compiler_params(n_axes):
    return pltpu.CompilerParams(
        dimension_semantics=("arbitrary",) * n_axes,
        vmem_limit_bytes=VMEM_LIMIT_BYTES,
    )


def _dot(a, b):
    return jnp.dot(a, b, preferred_element_type=F32)


def _lam_value(lq1_ref, lk1_ref, lq2_ref, lk2_ref, lam_init):
    d1 = jnp.sum(lq1_ref[...] * lk1_ref[...], axis=-1, keepdims=True)
    d2 = jnp.sum(lq2_ref[...] * lk2_ref[...], axis=-1, keepdims=True)
    return jnp.exp(d1) - jnp.exp(d2) + lam_init


def _inproj_kernel(x_ref, ng_ref, w_ref, sgug_ref, sguw_ref, sgub_ref, gq_ref, gk_ref,
                   gm128_ref, gm64_ref, ya_ref, q_ref, k_ref, v_ref, gb_ref, *maybe_va_ref):
    x = x_ref[...]
    ms = jnp.mean(x * x, axis=-1, keepdims=True)
    hb = ((x * lax.rsqrt(ms + NORM_EPS)) * ng_ref[...]).astype(BF16)

    def proj(col):
        return _dot(hb, w_ref[:, col:col + WIDTH])

    def group_rms_scale(z, gm_ref):
        return lax.rsqrt(_dot((z * z).astype(BF16), gm_ref[...]) + NORM_EPS)

    zq = proj(COL_Q)
    q_ref[...] = ((zq * group_rms_scale(zq, gm64_ref)) * gq_ref[...] * QK_SCALE).astype(BF16)
    zk = proj(COL_K)
    k_ref[...] = (zk * group_rms_scale(zk, gm64_ref)) * gk_ref[...]
    v_ref[...] = proj(COL_V)
    gb_ref[...] = jax.nn.silu(proj(COL_GB)).astype(BF16)

    va = proj(COL_VA)
    if maybe_va_ref:
        maybe_va_ref[0][...] = va
    vn = ((va * group_rms_scale(va, gm128_ref)) * sgug_ref[...]).astype(BF16)
    gate = proj(COL_U) * jax.nn.silu(proj(COL_GA))
    for c in range(x.shape[0] // SGU_CHUNK):
        rows = slice(c * SGU_CHUNK, (c + 1) * SGU_CHUNK)
        for g in range(N_HEADS):
            cols = slice(g * HEAD_DIM, (g + 1) * HEAD_DIM)
            mixed = _dot(sguw_ref[g], vn[rows, cols]) + sgub_ref[g]
            ya_ref[rows, cols] = (gate[rows, cols] * mixed).astype(BF16)


def _inproj(x, ng, w_bf, sgug, sguw, sgub, gq, gk, gm128, gm64, *, tile, emit_va):
    n = x.shape[0]
    row_spec = lambda width: pl.BlockSpec((tile, width), lambda i: (i, 0))
    full = lambda a: pl.BlockSpec(a.shape, lambda i: (0,) * a.ndim)
    out_shape = [
        jax.ShapeDtypeStruct((n, WIDTH), BF16),
        jax.ShapeDtypeStruct((n, WIDTH), BF16),
        jax.ShapeDtypeStruct((n, WIDTH), F32),
        jax.ShapeDtypeStruct((n, WIDTH), F32),
        jax.ShapeDtypeStruct((n, WIDTH), BF16),
    ]
    if emit_va:
        out_shape.append(jax.ShapeDtypeStruct((n, WIDTH), F32))
    operands = (x, ng, w_bf, sgug, sguw, sgub, gq, gk, gm128, gm64)
    return pl.pallas_call(
        _inproj_kernel,
        grid=(n // tile,),
        in_specs=[row_spec(D_MODEL)] + [full(a) for a in operands[1:]],
        out_specs=[row_spec(WIDTH)] * len(out_shape),
        out_shape=out_shape,
        compiler_params=_compiler_params(1),
        name="inproj",
    )(*operands)


def _outproj_kernel(x_ref, ya_ref, yb_ref, w_ref, o_ref):
    y = _dot(ya_ref[...], w_ref[:WIDTH, :]) + _dot(yb_ref[...], w_ref[WIDTH:, :])
    o_ref[...] = x_ref[...] + y


def _outproj(x, ya, yb, w_bf, *, tile):
    n = x.shape[0]
    return pl.pallas_call(
        _outproj_kernel,
        grid=(n // tile,),
        in_specs=[
            pl.BlockSpec((tile, D_MODEL), lambda i: (i, 0)),
            pl.BlockSpec((tile, WIDTH), lambda i: (i, 0)),
            pl.BlockSpec((tile, WIDTH), lambda i: (i, 0)),
            pl.BlockSpec(w_bf.shape, lambda i: (0, 0)),
        ],
        out_specs=pl.BlockSpec((tile, D_MODEL), lambda i: (i, 0)),
        out_shape=jax.ShapeDtypeStruct((n, D_MODEL), F32),
        compiler_params=_compiler_params(1),
        name="outproj",
    )(x, ya, yb, w_bf)


def _attn_kernel(slopes_ref, q_ref, k_ref, v_ref, gb_ref, sg_ref, lq1_ref, lk1_ref,
                 lq2_ref, lk2_ref, o_ref, kaug_ref, vt_ref, dt_ref, m_ref, l_ref, acc_ref,
                 *, lam_init):
    t = ATTN_TILE
    n_tiles = kaug_ref.shape[0]
    head = pl.program_id(1)
    qi = pl.program_id(2)
    slope = slopes_ref[head]

    @pl.when(qi == 0)
    def _prepare_head():
        lane = lax.broadcasted_iota(jnp.int32, (t, HEAD_DIM), 1)
        row = lax.broadcasted_iota(jnp.int32, (t, HEAD_DIM), 0)
        for j in range(n_tiles):
            pos = row + j * t
            lo = jnp.bitwise_and(pos, CHUNK - 1)
            hi = pos - lo
            aug = jnp.where(lane == 0, slope * hi.astype(F32),
                            jnp.where(lane == 1, slope * lo.astype(F32), 0.0))
            kaug_ref[j, :, :HEAD_DIM] = k_ref[j * t:(j + 1) * t, :].astype(BF16)
            kaug_ref[j, :, HEAD_DIM:] = aug.astype(BF16)
            vt_ref[j] = v_ref[j * t:(j + 1) * t, :].T.astype(BF16)
        key = lax.broadcasted_iota(jnp.int32, (t, t), 0)
        qry = lax.broadcasted_iota(jnp.int32, (t, t), 1)
        allowed = jnp.right_shift(key, 6) <= jnp.right_shift(qry, 6)
        ahead = jnp.maximum(key - qry, 0).astype(F32)
        dt_ref[...] = jnp.where(allowed, (-2.0 * slope) * ahead, NEG_INF)

    q = q_ref[...]
    lane = lax.broadcasted_iota(jnp.int32, q.shape, 1)
    zero = jnp.zeros_like(q)
    ones_aug = jnp.where(lane < 2, 1.0, 0.0).astype(BF16)
    qbd = jnp.concatenate([
        jnp.concatenate([jnp.where(lane < HALF_DIM, q, zero), ones_aug], axis=1),
        jnp.concatenate([jnp.where(lane >= HALF_DIM, q, zero), ones_aug], axis=1),
    ], axis=0)

    m_ref[...] = jnp.full(m_ref.shape, NEG_INF, F32)
    l_ref[...] = jnp.zeros(l_ref.shape, F32)
    acc_ref[...] = jnp.zeros(acc_ref.shape, F32)

    def tile_step(j, diagonal):
        st = lax.dot_general(kaug_ref[j], qbd, (((1,), (1,)), ((), ())),
                             preferred_element_type=F32)
        if diagonal:
            d = dt_ref[...]
            st = st + jnp.concatenate([d, d], axis=1)
        m_old = m_ref[...]
        m_new = jnp.maximum(m_old, jnp.max(st, axis=0, keepdims=True))
        alpha = jnp.exp(m_old - m_new)
        p = jnp.exp(st - m_new)
        l_ref[...] = alpha * l_ref[...] + jnp.sum(p, axis=0, keepdims=True)
        acc_ref[...] = alpha * acc_ref[...] + _dot(vt_ref[j], p.astype(BF16))
        m_ref[...] = m_new

    def body(j, carry):
        tile_step(j, False)
        return carry

    lax.fori_loop(0, qi, body, 0)
    tile_step(qi, True)

    lam = _lam_value(lq1_ref, lk1_ref, lq2_ref, lk2_ref, lam_init)
    acc = acc_ref[...] * (1.0 / l_ref[...])
    ot = acc[:, :t] - lam * acc[:, t:]
    ot = ot * lax.rsqrt(jnp.mean(ot * ot, axis=0, keepdims=True) + NORM_EPS)
    o = (ot.T * sg_ref[...]) * (1.0 - lam_init)
    o_ref[...] = (o * gb_ref[...].astype(F32)).astype(BF16)


def _prompt_attention(q, k, v, gb, slopes, sg, lq1, lk1, lq2, lk2, *, batch, seq, lam_init):
    t = ATTN_TILE
    n_tiles = seq // t
    q3, k3, v3, gb3 = (a.reshape(batch, seq, WIDTH) for a in (q, k, v, gb))
    tile_spec = pl.BlockSpec((None, t, HEAD_DIM), lambda b, h, i: (b, i, h))
    seq_spec = pl.BlockSpec((None, seq, HEAD_DIM), lambda b, h, i: (b, 0, h))
    vec = lambda a: pl.BlockSpec(a.shape, lambda b, h, i: (0, 0))
    out = pl.pallas_call(
        functools.partial(_attn_kernel, lam_init=lam_init),
        grid=(batch, N_HEADS, n_tiles),
        in_specs=[pl.BlockSpec(memory_space=pltpu.SMEM), tile_spec, seq_spec, seq_spec,
                  tile_spec, vec(sg), vec(lq1), vec(lk1), vec(lq2), vec(lk2)],
        out_specs=tile_spec,
        out_shape=jax.ShapeDtypeStruct((batch, seq, WIDTH), BF16),
        scratch_shapes=[
            pltpu.VMEM((n_tiles, t, 2 * HEAD_DIM), BF16),
            pltpu.VMEM((n_tiles, HEAD_DIM, t), BF16),
            pltpu.VMEM((t, t), F32),
            pltpu.VMEM((1, 2 * t), F32),
            pltpu.VMEM((1, 2 * t), F32),
            pltpu.VMEM((HEAD_DIM, 2 * t), F32),
        ],
        compiler_params=_compiler_params(3),
        name="prompt_attn",
    )(slopes, q3, k3, v3, gb3, sg, lq1, lk1, lq2, lk2)
    return out.reshape(batch * seq, WIDTH)


def _sample_attn_kernel(slopes_ref, q_ref, kn_ref, vn_ref, ck_ref, cv_ref, gb_ref, sg_ref,
                        lq1_ref, lk1_ref, lq2_ref, lk2_ref, o_ref, *, lam_init, past_len):
    nq = q_ref.shape[0]
    lam = _lam_value(lq1_ref, lk1_ref, lq2_ref, lk2_ref, lam_init)
    qrow = lax.broadcasted_iota(jnp.int32, (2 * nq, past_len), 0)
    qpos_past = past_len + jnp.where(qrow >= nq, qrow - nq, qrow)
    dist_past = (qpos_past - lax.broadcasted_iota(jnp.int32, (2 * nq, past_len), 1)).astype(F32)
    pad = HEAD_DIM - nq
    qrow_n = lax.broadcasted_iota(jnp.int32, (2 * nq, HEAD_DIM), 0)
    qidx_n = jnp.where(qrow_n >= nq, qrow_n - nq, qrow_n)
    kidx_n = lax.broadcasted_iota(jnp.int32, (2 * nq, HEAD_DIM), 1)
    dist_new = jnp.abs(qidx_n - kidx_n).astype(F32)
    real_new = kidx_n < nq
    zpad = jnp.zeros((pad, HEAD_DIM), BF16)
    nt = (((1,), (1,)), ((), ()))
    for h in range(N_HEADS):
        cols = slice(h * HEAD_DIM, (h + 1) * HEAD_DIM)
        slope = slopes_ref[h]
        q = q_ref[:, cols]
        lane = lax.broadcasted_iota(jnp.int32, q.shape, 1)
        zero = jnp.zeros_like(q)
        qbd = jnp.concatenate([jnp.where(lane < HALF_DIM, q, zero),
                               jnp.where(lane >= HALF_DIM, q, zero)], axis=0)
        s_past = lax.dot_general(qbd, ck_ref[:, cols].astype(BF16), nt,
                                 preferred_element_type=F32) - slope * dist_past
        kn = jnp.concatenate([kn_ref[:, cols].astype(BF16), zpad], axis=0)
        vn = jnp.concatenate([vn_ref[:, cols].astype(BF16), zpad], axis=0)
        s_new = lax.dot_general(qbd, kn, nt, preferred_element_type=F32) - slope * dist_new
        s_new = jnp.where(real_new, s_new, NEG_INF)
        m =jnp.maximum(jnp.max(s_past, axis=-1, keepdims=True),
                        jnp.max(s_new, axis=-1, keepdims=True))
        p_past = jnp.exp(s_past - m)
        p_new = jnp.exp(s_new - m)
        l = jnp.sum(p_past, axis=-1, keepdims=True) + jnp.sum(p_new, axis=-1, keepdims=True)
        acc = (_dot(p_past.astype(BF16), cv_ref[:, cols].astype(BF16))
               + _dot(p_new.astype(BF16), vn))
        acc = acc * (1.0 / l)
        o = acc[:nq] - lam * acc[nq:]
        o = o * lax.rsqrt(jnp.mean(o * o, axis=-1, keepdims=True) + NORM_EPS)
        o = (o * sg_ref[...]) * (1.0 - lam_init)
        o_ref[:, cols] = (o * gb_ref[:, cols].astype(F32)).astype(BF16)


def _sample_attention(q, k_new, v_new, cache_k, cache_v, gb, slopes, sg, lq1, lk1, lq2, lk2,
                      *, n_streams, n_new, lam_init):
    past_len = cache_k.shape[1]
    new_spec = pl.BlockSpec((None, n_new, WIDTH), lambda b: (b, 0, 0))
    past_spec = pl.BlockSpec((None, past_len, WIDTH), lambda b: (b, 0, 0))
    vec = lambda a: pl.BlockSpec(a.shape, lambda b: (0, 0))
    r3 = lambda a: a.reshape(n_streams, n_new, WIDTH)
    out = pl.pallas_call(
        functools.partial(_sample_attn_kernel, lam_init=lam_init, past_len=past_len),
        grid=(n_streams,),
        in_specs=[pl.BlockSpec(memory_space=pltpu.SMEM), new_spec, new_spec, new_spec,
                  past_spec, past_spec, new_spec, vec(sg), vec(lq1), vec(lk1), vec(lq2),
                  vec(lk2)],
        out_specs=new_spec,
        out_shape=jax.ShapeDtypeStruct((n_streams, n_new, WIDTH), BF16),
        compiler_params=_compiler_params(1),
        name="sample_attn",
    )(slopes, r3(q), r3(k_new), r3(v_new), cache_k, cache_v, r3(gb), sg, lq1, lk1, lq2, lk2)
    return out.reshape(n_streams * n_new, WIDTH)


def _group_mean_matrix(group):
    idx = jnp.arange(WIDTH) // group
    return jnp.where(idx[:, None] == idx[None, :], 1.0 / group, 0.0).astype(BF16)


def kernel(x_prompt, x_sample, cache_k, cache_v, norm_g, w_in, sgu_norm_g, sgu_w, sgu_b,
           q_norm_g, k_norm_g, lambda_q1, lambda_k1, lambda_q2, lambda_k2, subln_g, w_out):
    depth = w_in.shape[0]
    batch, seq, _ = x_prompt.shape
    n_streams, n_new, _ = x_sample.shape
    past_len = cache_k.shape[2]
    assert seq % ATTN_TILE == 0 and (batch * seq) % TOKEN_TILE == 0
    assert SGU_CHUNK % n_new == 0 and past_len % CHUNK == 0 and n_new <= CHUNK
    sample_tile = min(n_streams * n_new, TOKEN_TILE)
    assert sample_tile % SGU_CHUNK == 0 and (n_streams * n_new) % sample_tile == 0

    gm128 = _group_mean_matrix(HEAD_DIM)
    gm64 = _group_mean_matrix(HALF_DIM)
    slopes = jnp.exp2(-8.0 * jnp.arange(1, N_HEADS + 1, dtype=F32) / N_HEADS)
    tril = jnp.tril(jnp.ones((SGU_CHUNK, SGU_CHUNK), F32))
    tril_new = jnp.tril(jnp.ones((n_new, n_new), F32))
    streams_per_chunk = SGU_CHUNK // n_new
    eye = jnp.eye(streams_per_chunk, dtype=F32)

    xp = x_prompt.reshape(batch * seq, D_MODEL)
    xs = x_sample.reshape(n_streams * n_new, D_MODEL)
    kp_rows, vp_rows, ks_rows, vs_rows, sgu_rows = [], [], [], [], []
    for i in range(depth):
        lam_init = _lam_init(i)
        w_in_bf = w_in[i].astype(BF16)
        w_out_bf = w_out[i].astype(BF16)
        row = lambda a: a.reshape(1, -1).astype(F32)
        ng = row(norm_g[i])
        sgug = row(sgu_norm_g[i])
        gq = row(jnp.tile(q_norm_g[i], WIDTH // HALF_DIM))
        gk = row(jnp.tile(k_norm_g[i], WIDTH // HALF_DIM))
        sg = row(subln_g[i])
        lams = tuple(row(a[i]) for a in (lambda_q1, lambda_k1, lambda_q2, lambda_k2))
        sguw_p = (sgu_w[i] * tril).astype(BF16)
        sgub_p = jnp.broadcast_to(sgu_b[i][:, :, None], (N_HEADS, SGU_CHUNK, HEAD_DIM))
        w_new = sgu_w[i][:, :n_new, :n_new] * tril_new
        sguw_s = jnp.einsum("ab,hts->hatbs", eye, w_new).reshape(
            N_HEADS, SGU_CHUNK, SGU_CHUNK).astype(BF16)
        sgub_s = jnp.broadcast_to(
            jnp.tile(sgu_b[i][:, :n_new], (1, streams_per_chunk))[:, :, None],
            (N_HEADS, SGU_CHUNK, HEAD_DIM))

        ya, q, k, v, gb = _inproj(xp, ng, w_in_bf, sgug, sguw_p, sgub_p, gq, gk, gm128, gm64,
                                  tile=TOKEN_TILE, emit_va=False)
        yb = _prompt_attention(q, k, v, gb, slopes, sg, *lams,
                               batch=batch, seq=seq, lam_init=lam_init)
        xp = _outproj(xp, ya, yb, w_out_bf, tile=TOKEN_TILE)
        kp_rows.append(k)
        vp_rows.append(v)

        ya, q, k, v, gb, va = _inproj(xs, ng, w_in_bf, sgug, sguw_s, sgub_s, gq, gk, gm128,
                                      gm64, tile=sample_tile, emit_va=True)
        yb = _sample_attention(q, k, v, cache_k[i].reshape(n_streams, past_len, WIDTH),
                               cache_v[i].reshape(n_streams, past_len, WIDTH), gb, slopes, sg,
                               *lams, n_streams=n_streams, n_new=n_new, lam_init=lam_init)
        xs = _outproj(xs, ya, yb, w_out_bf, tile=sample_tile)
        ks_rows.append(k)
        vs_rows.append(v)
        sgu_rows.append(va)

    return (
        xp.reshape(batch, seq, D_MODEL),
        xs.reshape(n_streams, n_new, D_MODEL),
        jnp.stack(kp_rows).reshape(depth, batch, seq, N_HEADS, 2, HALF_DIM),
        jnp.stack(vp_rows).reshape(depth, batch, seq, N_HEADS, HEAD_DIM),
        jnp.stack(ks_rows).reshape(depth, n_streams, n_new, N_HEADS, 2, HALF_DIM),
        jnp.stack(vs_rows).reshape(depth, n_streams, n_new, N_HEADS, HEAD_DIM),
        jnp.stack(sgu_rows).reshape(depth, n_streams, n_new, WIDTH),
    )
```

```python
import functools
import math

import jax
import jax.numpy as jnp
from jax import lax
from jax.experimental import pallas as pl
from jax.experimental.pallas import tpu as pltpu

F32 = jnp.float32
BF16 = jnp.bfloat16

D_MODEL = 1024
N_HEADS = 4
HEAD_DIM = 128
HALF_DIM = 64
WIDTH = N_HEADS * HEAD_DIM
IN_WIDTH = 7 * WIDTH
CHUNK = 64
SGU_CHUNK = 128
NORM_EPS = 1e-6
NEG_INF = -1e30
QK_SCALE = HALF_DIM ** -0.5

COL_U, COL_VA, COL_GA, COL_Q, COL_K, COL_V, COL_GB = (i * WIDTH for i in range(7))

VMEM_LIMIT_BYTES = 56 * 1024 * 1024

TOKEN_TILE = 512
ATTN_TILE = 512
LANE_CHUNK = 256


def _lam_init(layer_idx):
    return 0.8 - 0.6 * math.exp(-0.3 * layer_idx)


def _compiler_params(n_axes):
    return pltpu.CompilerParams(
        dimension_semantics=("arbitrary",) * n_axes,
        vmem_limit_bytes=VMEM_LIMIT_BYTES,
    )


def _dot(a, b):
    return jnp.dot(a, b, preferred_element_type=F32)


def _lam_value(lq1_ref, lk1_ref, lq2_ref, lk2_ref, lam_init):
    d1 = jnp.sum(lq1_ref[...] * lk1_ref[...], axis=-1, keepdims=True)
    d2 = jnp.sum(lq2_ref[...] * lk2_ref[...], axis=-1, keepdims=True)
    return jnp.exp(d1) - jnp.exp(d2) + lam_init


def _inproj_kernel(x_ref, ng_ref, w_ref, sgug_ref, sguw_ref, sgub_ref, gq_ref, gk_ref,
                   gm128_ref, gm64_ref, ya_ref, q_ref, k_ref, v_ref, gb_ref, *maybe_va_ref):
    x = x_ref[...]
    ms = jnp.mean(x * x, axis=-1, keepdims=True)
    hb = ((x * lax.rsqrt(ms + NORM_EPS)) * ng_ref[...]).astype(BF16)

    def proj(col):
        return _dot(hb, w_ref[:, col:col + WIDTH])

    def group_rms_scale(z, gm_ref):
        return lax.rsqrt(_dot((z * z).astype(BF16), gm_ref[...]) + NORM_EPS)

    zq = proj(COL_Q)
    q_ref[...] = ((zq * group_rms_scale(zq, gm64_ref)) * gq_ref[...] * QK_SCALE).astype(BF16)
    zk = proj(COL_K)
    k_ref[...] = (zk * group_rms_scale(zk, gm64_ref)) * gk_ref[...]
    v_ref[...] = proj(COL_V)
    gb_ref[...] = jax.nn.silu(proj(COL_GB)).astype(BF16)

    va = proj(COL_VA)
    if maybe_va_ref:
        maybe_va_ref[0][...] = va
    vn = ((va * group_rms_scale(va, gm128_ref)) * sgug_ref[...]).astype(BF16)
    gate = proj(COL_U) * jax.nn.silu(proj(COL_GA))
    for c in range(x.shape[0] // SGU_CHUNK):
        rows = slice(c * SGU_CHUNK, (c + 1) * SGU_CHUNK)
        for g in range(N_HEADS):
            cols = slice(g * HEAD_DIM, (g + 1) * HEAD_DIM)
            mixed = _dot(sguw_ref[g], vn[rows, cols]) + sgub_ref[g]
            ya_ref[rows, cols] = (gate[rows, cols] * mixed).astype(BF16)


def _inproj(x, ng, w_bf, sgug, sguw, sgub, gq, gk, gm128, gm64, *, tile, emit_va):
    n = x.shape[0]
    row_spec = lambda width: pl.BlockSpec((tile, width), lambda i: (i, 0))
    full = lambda a: pl.BlockSpec(a.shape, lambda i: (0,) * a.ndim)
    out_shape = [
        jax.ShapeDtypeStruct((n, WIDTH), BF16),
        jax.ShapeDtypeStruct((n, WIDTH), BF16),
        jax.ShapeDtypeStruct((n, WIDTH), F32),
        jax.ShapeDtypeStruct((n, WIDTH), F32),
        jax.ShapeDtypeStruct((n, WIDTH), BF16),
    ]
    if emit_va:
        out_shape.append(jax.ShapeDtypeStruct((n, WIDTH), F32))
    operands = (x, ng, w_bf, sgug, sguw, sgub, gq, gk, gm128, gm64)
    return pl.pallas_call(
        _inproj_kernel,
        grid=(n // tile,),
        in_specs=[row_spec(D_MODEL)] + [full(a) for a in operands[1:]],
        out_specs=[row_spec(WIDTH)] * len(out_shape),
        out_shape=out_shape,
        compiler_params=_compiler_params(1),
        name="inproj",
    )(*operands)


def _outproj_kernel(x_ref, ya_ref, yb_ref, w_ref, o_ref):
    y = _dot(ya_ref[...], w_ref[:WIDTH, :]) + _dot(yb_ref[...], w_ref[WIDTH:, :])
    o_ref[...] = x_ref[...] + y


def _outproj(x, ya, yb, w_bf, *, tile):
    n = x.shape[0]
    return pl.pallas_call(
        _outproj_kernel,
        grid=(n // tile,),
        in_specs=[
            pl.BlockSpec((tile, D_MODEL), lambda i: (i, 0)),
            pl.BlockSpec((tile, WIDTH), lambda i: (i, 0)),
            pl.BlockSpec((tile, WIDTH), lambda i: (i, 0)),
            pl.BlockSpec(w_bf.shape, lambda i: (0, 0)),
        ],
        out_specs=pl.BlockSpec((tile, D_MODEL), lambda i: (i, 0)),
        out_shape=jax.ShapeDtypeStruct((n, D_MODEL), F32),
        compiler_params=_compiler_params(1),
        name="outproj",
    )(x, ya, yb, w_bf)


def _attn_kernel(slopes_ref, q_ref, k_ref, v_ref, gb_ref, sg_ref, lq1_ref, lk1_ref,
                 lq2_ref, lk2_ref, o_ref, kaug_ref, vt_ref, dt_ref, qbd_ref, s_ref, m_ref,
                 l_ref, acc_ref, *, lam_init):
    t = ATTN_TILE
    n_tiles = kaug_ref.shape[0]
    head = pl.program_id(1)
    qi = pl.program_id(2)
    slope = slopes_ref[head]

    @pl.when(qi == 0)
    def _prepare_head():
        lane = lax.broadcasted_iota(jnp.int32, (t, HEAD_DIM), 1)
        row = lax.broadcasted_iota(jnp.int32, (t, HEAD_DIM), 0)
        for j in range(n_tiles):
            pos = row + j * t
            lo = jnp.bitwise_and(pos, CHUNK - 1)
            hi = pos - lo
            aug = jnp.where(lane == 0, slope * hi.astype(F32),
                            jnp.where(lane == 1, slope * lo.astype(F32), 0.0))
            kaug_ref[j, :, :HEAD_DIM] = k_ref[j * t:(j + 1) * t, :].astype(BF16)
            kaug_ref[j, :, HEAD_DIM:] = aug.astype(BF16)
            vt_ref[j] = v_ref[j * t:(j + 1) * t, :].T.astype(BF16)
        key = lax.broadcasted_iota(jnp.int32, (t, t), 0)
        qry = lax.broadcasted_iota(jnp.int32, (t, t), 1)
        allowed = jnp.right_shift(key, 6) <= jnp.right_shift(qry, 6)
        ahead = jnp.maximum(key - qry, 0).astype(F32)
        dt_ref[...] = jnp.where(allowed, (-2.0 * slope) * ahead, NEG_INF)

    q = q_ref[...]
    lane = lax.broadcasted_iota(jnp.int32, q.shape, 1)
    zero = jnp.zeros_like(q)
    ones_aug = jnp.where(lane < 2, 1.0, 0.0).astype(BF16)
    qbd_ref[:t, :HEAD_DIM] = jnp.where(lane < HALF_DIM, q, zero)
    qbd_ref[t:, :HEAD_DIM] = jnp.where(lane >= HALF_DIM, q, zero)
    qbd_ref[:t, HEAD_DIM:] = ones_aug
    qbd_ref[t:, HEAD_DIM:] = ones_aug

    m_ref[...] = jnp.full(m_ref.shape, NEG_INF, F32)
    l_ref[...] = jnp.zeros(l_ref.shape, F32)
    acc_ref[...] = jnp.zeros(acc_ref.shape, F32)

    n_chunks = 2 * t // LANE_CHUNK
    ahead_chunks = s_ref.shape[0]

    def scores(j, c):
        return lax.dot_general(kaug_ref[j], qbd_ref[c * LANE_CHUNK:(c + 1) * LANE_CHUNK, :],
                               (((1,), (1,)), ((), ())), preferred_element_type=F32)

    def softmax_pv(j, c, st, diagonal):
        lanes = slice(c * LANE_CHUNK, (c + 1) * LANE_CHUNK)
        if diagonal:
            q0 = (c * LANE_CHUNK) % t
            st = st + dt_ref[:, q0:q0 + LANE_CHUNK]
        m_old = m_ref[:, lanes]
        m_new = jnp.maximum(m_old, jnp.max(st, axis=0, keepdims=True))
        alpha = jnp.exp(m_old - m_new)
        p = jnp.exp(st - m_new)
        l_ref[:, lanes] = alpha * l_ref[:, lanes] + jnp.sum(p, axis=0, keepdims=True)
        acc_ref[:, lanes] = alpha * acc_ref[:, lanes] + _dot(vt_ref[j], p.astype(BF16))
        m_ref[:, lanes] = m_new

    def tile_step(j, diagonal):
        pending = {}
        for c in range(n_chunks):
            nxt = c + ahead_chunks
            if nxt < n_chunks:
                pending[nxt] = scores(j, nxt)
            elif not diagonal:
                s_next = scores(j + 1, nxt - n_chunks)
            st = pending.pop(c) if c in pending else s_ref[c]
            softmax_pv(j, c, st, diagonal)
            if nxt >= n_chunks and not diagonal:
                s_ref[nxt - n_chunks] = s_next

    for c in range(ahead_chunks):
        s_ref[c] = scores(0, c)

    def body(j, carry):
        tile_step(j, False)
        return carry

    lax.fori_loop(0, qi, body, 0)
    tile_step(qi, True)

    lam = _lam_value(lq1_ref, lk1_ref, lq2_ref, lk2_ref, lam_init)
    acc = acc_ref[...] * (1.0 / l_ref[...])
    ot = acc[:, :t] - lam * acc[:, t:]
    ot = ot * lax.rsqrt(jnp.mean(ot * ot, axis=0, keepdims=True) + NORM_EPS)
    o = (ot.T * sg_ref[...]) * (1.0 - lam_init)
    o_ref[...] = (o * gb_ref[...].astype(F32)).astype(BF16)


def _prompt_attention(q, k, v, gb, slopes, sg, lq1, lk1, lq2, lk2, *, batch, seq, lam_init):
    t = ATTN_TILE
    n_tiles = seq // t
    q3, k3, v3, gb3 = (a.reshape(batch, seq, WIDTH) for a in (q, k, v, gb))
    tile_spec = pl.BlockSpec((None, t, HEAD_DIM), lambda b, h, i: (b, i, h))
    seq_spec = pl.BlockSpec((None, seq, HEAD_DIM), lambda b, h, i: (b, 0, h))
    vec = lambda a: pl.BlockSpec(a.shape, lambda b, h, i: (0, 0))
    out = pl.pallas_call(
        functools.partial(_attn_kernel, lam_init=lam_init),
        grid=(batch, N_HEADS, n_tiles),
        in_specs=[pl.BlockSpec(memory_space=pltpu.SMEM), tile_spec, seq_spec, seq_spec,
                  tile_spec, vec(sg), vec(lq1), vec(lk1), vec(lq2), vec(lk2)],
        out_specs=tile_spec,
        out_shape=jax.ShapeDtypeStruct((batch, seq, WIDTH), BF16),
        scratch_shapes=[
            pltpu.VMEM((n_tiles, t, 2 * HEAD_DIM), BF16),
            pltpu.VMEM((n_tiles, HEAD_DIM, t), BF16),
            pltpu.VMEM((t, t), F32),
            pltpu.VMEM((2 * t, 2 * HEAD_DIM), BF16),
            pltpu.VMEM((2, t, LANE_CHUNK), F32),
            pltpu.VMEM((1, 2 * t), F32),
            pltpu.VMEM((1, 2 * t), F32),
            pltpu.VMEM((HEAD_DIM, 2 * t), F32),
        ],
        compiler_params=_compiler_params(3),
        name="prompt_attn",
    )(slopes, q3, k3, v3, gb3, sg, lq1, lk1, lq2, lk2)
    return out.reshape(batch * seq, WIDTH)


def _sample_attn_kernel(slopes_ref, q_ref, kn_ref, vn_ref, ck_ref, cv_ref, gb_ref, sg_ref,
                        lq1_ref, lk1_ref, lq2_ref, lk2_ref, o_ref, *, lam_init, past_len):
    nq = q_ref.shape[0]
    lam = _lam_value(lq1_ref, lk1_ref, lq2_ref, lk2_ref, lam_init)
    qrow = lax.broadcasted_iota(jnp.int32, (2 * nq, past_len), 0)
    qpos_past = past_len + jnp.where(qrow >= nq, qrow - nq, qrow)
    dist_past = (qpos_past - lax.broadcasted_iota(jnp.int32, (2 * nq, past_len), 1)).astype(F32)
    pad = HEAD_DIM - nq
    qrow_n = lax.broadcasted_iota(jnp.int32, (2 * nq, HEAD_DIM), 0)
    qidx_n = jnp.where(qrow_n >= nq, qrow_n - nq, qrow_n)
    kidx_n = lax.broadcasted_iota(jnp.int32, (2 * nq, HEAD_DIM), 1)
    dist_new = jnp.abs(qidx_n - kidx_n).astype(F32)
    real_new = kidx_n < nq
    zpad = jnp.zeros((pad, HEAD_DIM), BF16)
    nt = (((1,), (1,)), ((), ()))
    for h in range(N_HEADS):
        cols = slice(h * HEAD_DIM, (h + 1) * HEAD_DIM)
        slope = slopes_ref[h]
        q = q_ref[:, cols]
        lane = lax.broadcasted_iota(jnp.int32, q.shape, 1)
        zero = jnp.zeros_like(q)
        qbd = jnp.concatenate([jnp.where(lane < HALF_DIM, q, zero),
                               jnp.where(lane >= HALF_DIM, q, zero)], axis=0)
        s_past = lax.dot_general(qbd, ck_ref[:, cols].astype(BF16), nt,
                                 preferred_element_type=F32) - slope * dist_past
        kn = jnp.concatenate([kn_ref[:, cols].astype(BF16), zpad], axis=0)
        vn = jnp.concatenate([vn_ref[:, cols].astype(BF16), zpad], axis=0)
        s_new = lax.dot_general(qbd, kn, nt, preferred_element_type=F32) - slope * dist_new
        s_new = jnp.where(real_new, s_new, NEG_INF)
        m =jnp.maximum(jnp.max(s_past, axis=-1, keepdims=True),
                        jnp.max(s_new, axis=-1, keepdims=True))
        p_past = jnp.exp(s_past - m)
        p_new = jnp.exp(s_new - m)
        l = jnp.sum(p_past, axis=-1, keepdims=True) + jnp.sum(p_new, axis=-1, keepdims=True)
        acc = (_dot(p_past.astype(BF16), cv_ref[:, cols].astype(BF16))
               + _dot(p_new.astype(BF16), vn))
        acc = acc * (1.0 / l)
        o = acc[:nq] - lam * acc[nq:]
        o = o * lax.rsqrt(jnp.mean(o * o, axis=-1, keepdims=True) + NORM_EPS)
        o = (o * sg_ref[...]) * (1.0 - lam_init)
        o_ref[:, cols] = (o * gb_ref[:, cols].astype(F32)).astype(BF16)


def _sample_attention(q, k_new, v_new, cache_k, cache_v, gb, slopes, sg, lq1, lk1, lq2, lk2,
                      *, n_streams, n_new, lam_init):
    past_len = cache_k.shape[1]
    new_spec = pl.BlockSpec((None, n_new, WIDTH), lambda b: (b, 0, 0))
    past_spec = pl.BlockSpec((None, past_len, WIDTH), lambda b: (b, 0, 0))
    vec = lambda a: pl.BlockSpec(a.shape, lambda b: (0, 0))
    r3 = lambda a: a.reshape(n_streams, n_new, WIDTH)
    out = pl.pallas_call(
        functools.partial(_sample_attn_kernel, lam_init=lam_init, past_len=past_len),
        grid=(n_streams,),
        in_specs=[pl.BlockSpec(memory_space=pltpu.SMEM), new_spec, new_spec, new_spec,
                  past_spec, past_spec, new_spec, vec(sg), vec(lq1), vec(lk1), vec(lq2),
                  vec(lk2)],
        out_specs=new_spec,
        out_shape=jax.ShapeDtypeStruct((n_streams, n_new, WIDTH), BF16),
        compiler_params=_compiler_params(1),
        name="sample_attn",
    )(slopes, r3(q), r3(k_new), r3(v_new), cache_k, cache_v, r3(gb), sg, lq1, lk1, lq2, lk2)
    return out.reshape(n_streams * n_new, WIDTH)


def _group_mean_matrix(group):
    idx = jnp.arange(WIDTH) // group
    return jnp.where(idx[:, None] == idx[None, :], 1.0 / group, 0.0).astype(BF16)


def kernel(x_prompt, x_sample, cache_k, cache_v, norm_g, w_in, sgu_norm_g, sgu_w, sgu_b,
           q_norm_g, k_norm_g, lambda_q1, lambda_k1, lambda_q2, lambda_k2, subln_g, w_out):
    depth = w_in.shape[0]
    batch, seq, _ = x_prompt.shape
    n_streams, n_new, _ = x_sample.shape
    past_len = cache_k.shape[2]
    assert seq % ATTN_TILE == 0 and (batch * seq) % TOKEN_TILE == 0
    assert SGU_CHUNK % n_new == 0 and past_len % CHUNK == 0 and n_new <= CHUNK
    sample_tile = min(n_streams * n_new, TOKEN_TILE)
    assert sample_tile % SGU_CHUNK == 0 and (n_streams * n_new) % sample_tile == 0

    gm128 = _group_mean_matrix(HEAD_DIM)
    gm64 = _group_mean_matrix(HALF_DIM)
    slopes = jnp.exp2(-8.0 * jnp.arange(1, N_HEADS + 1, dtype=F32) / N_HEADS)
    tril = jnp.tril(jnp.ones((SGU_CHUNK, SGU_CHUNK), F32))
    tril_new = jnp.tril(jnp.ones((n_new, n_new), F32))
    streams_per_chunk = SGU_CHUNK // n_new
    eye = jnp.eye(streams_per_chunk, dtype=F32)

    xp = x_prompt.reshape(batch * seq, D_MODEL)
    xs = x_sample.reshape(n_streams * n_new, D_MODEL)
    kp_rows, vp_rows, ks_rows, vs_rows, sgu_rows = [], [], [], [], []
    for i in range(depth):
        lam_init = _lam_init(i)
        w_in_bf = w_in[i].astype(BF16)
        w_out_bf = w_out[i].astype(BF16)
        row = lambda a: a.reshape(1, -1).astype(F32)
        ng = row(norm_g[i])
        sgug = row(sgu_norm_g[i])
        gq = row(jnp.tile(q_norm_g[i], WIDTH // HALF_DIM))
        gk = row(jnp.tile(k_norm_g[i], WIDTH // HALF_DIM))
        sg = row(subln_g[i])
        lams = tuple(row(a[i]) for a in (lambda_q1, lambda_k1, lambda_q2, lambda_k2))
        sguw_p = (sgu_w[i] * tril).astype(BF16)
        sgub_p = jnp.broadcast_to(sgu_b[i][:, :, None], (N_HEADS, SGU_CHUNK, HEAD_DIM))
        w_new = sgu_w[i][:, :n_new, :n_new] * tril_new
        sguw_s = jnp.einsum("ab,hts->hatbs", eye, w_new).reshape(
            N_HEADS, SGU_CHUNK, SGU_CHUNK).astype(BF16)
        sgub_s = jnp.broadcast_to(
            jnp.tile(sgu_b[i][:, :n_new], (1, streams_per_chunk))[:, :, None],
            (N_HEADS, SGU_CHUNK, HEAD_DIM))

        ya, q, k, v, gb = _inproj(xp, ng, w_in_bf, sgug, sguw_p, sgub_p, gq, gk, gm128, gm64,
                                  tile=TOKEN_TILE, emit_va=False)
        yb = _prompt_attention(q, k, v, gb, slopes, sg, *lams,
                               batch=batch, seq=seq, lam_init=lam_init)
        xp = _outproj(xp, ya, yb, w_out_bf, tile=TOKEN_TILE)
        kp_rows.append(k)
        vp_rows.append(v)

        ya, q, k, v, gb, va = _inproj(xs, ng, w_in_bf, sgug, sguw_s, sgub_s, gq, gk, gm128,
                                      gm64, tile=sample_tile, emit_va=True)
        yb = _sample_attention(q, k, v, cache_k[i].reshape(n_streams, past_len, WIDTH),
                               cache_v[i].reshape(n_streams, past_len, WIDTH), gb, slopes, sg,
                               *lams, n_streams=n_streams, n_new=n_new, lam_init=lam_init)
        xs = _outproj(xs, ya, yb, w_out_bf, tile=sample_tile)
        ks_rows.append(k)
        vs_rows.append(v)
        sgu_rows.append(va)

    return (
        xp.reshape(batch, seq, D_MODEL),
        xs.reshape(n_streams, n_new, D_MODEL),
        jnp.stack(kp_rows).reshape(depth, batch, seq, N_HEADS, 2, HALF_DIM),
        jnp.stack(vp_rows).reshape(depth, batch, seq, N_HEADS, HEAD_DIM),
        jnp.stack(ks_rows).reshape(depth, n_streams, n_new, N_HEADS, 2, HALF_DIM),
        jnp.stack(vs_rows).reshape(depth, n_streams, n_new, N_HEADS, HEAD_DIM),
        jnp.stack(sgu_rows).reshape(depth, n_streams, n_new, WIDTH),
    )
```

```python
import functools
import math

import jax
import jax.numpy as jnp
from jax import lax
from jax.experimental import pallas as pl
from jax.experimental.pallas import tpu as pltpu

F32 = jnp.float32
BF16 = jnp.bfloat16

D_MODEL = 1024
N_HEADS = 4
HEAD_DIM = 128
HALF_DIM = 64
WIDTH = N_HEADS * HEAD_DIM
CHUNK = 64
SGU_CHUNK = 128
NORM_EPS = 1e-6
NEG_INF = -1e30
QK_SCALE = HALF_DIM ** -0.5
ALIBI_SLOPES = tuple(2.0 ** (-8.0 * (h + 1) / N_HEADS) for h in range(N_HEADS))

COL_U, COL_VA, COL_GA, COL_Q, COL_K, COL_V, COL_GB = (i * WIDTH for i in range(7))

VMEM_LIMIT_BYTES = 56 * 1024 * 1024

ATTN_TILE = 512
TOKEN_TILE = ATTN_TILE
LANE_CHUNK = 256
KEY_WIDTH = 2 * HEAD_DIM


def _lam_init(layer_idx):
    return 0.8 - 0.6 * math.exp(-0.3 * layer_idx)


def _compiler_params(n_axes):
    return pltpu.CompilerParams(
        dimension_semantics=("arbitrary",) * n_axes,
        vmem_limit_bytes=VMEM_LIMIT_BYTES,
    )


def _dot(a, b):
    return jnp.dot(a, b, preferred_element_type=F32)


def _dot_nt(a, b):
    return lax.dot_general(a, b, (((1,), (1,)), ((), ())), preferred_element_type=F32)


def _lam_value(lq1_ref, lk1_ref, lq2_ref, lk2_ref, lam_init):
    d1 = jnp.sum(lq1_ref[...] * lk1_ref[...], axis=-1, keepdims=True)
    d2 = jnp.sum(lq2_ref[...] * lk2_ref[...], axis=-1, keepdims=True)
    return jnp.exp(d1) - jnp.exp(d2) + lam_init


def _head_cols(h):
    return slice(h * HEAD_DIM, (h + 1) * HEAD_DIM)


def _inproj_kernel(*refs, prompt, n_aliased):
    (x_ref, ng_ref, w_ref, sgug_ref, sguw_ref, sgub_ref, gq_ref, gk_ref,
     gm128_ref, gm64_ref) = refs[:10]
    outs = refs[10 + n_aliased:]
    x = x_ref[...]
    tile = x.shape[0]
    ms = jnp.mean(x * x, axis=-1, keepdims=True)
    hb = ((x * lax.rsqrt(ms + NORM_EPS)) * ng_ref[...]).astype(BF16)

    def proj(col):
        return _dot(hb, w_ref[:, col:col + WIDTH])

    def group_rms_scale(z, gm_ref):
        return lax.rsqrt(_dot((z * z).astype(BF16), gm_ref[...]) + NORM_EPS)

    zq = proj(COL_Q)
    qn = ((zq * group_rms_scale(zq, gm64_ref)) * gq_ref[...] * QK_SCALE).astype(BF16)
    zk = proj(COL_K)
    kn = (zk * group_rms_scale(zk, gm64_ref)) * gk_ref[...]
    zv = proj(COL_V)
    gb = jax.nn.silu(proj(COL_GB)).astype(BF16)

    if prompt:
        ya_ref, q_ref, key_ref, kt_ref, v_ref, vt_ref, gb_ref = outs
        pos = (lax.broadcasted_iota(jnp.int32, (tile, HEAD_DIM), 0)
               + pl.program_id(1) * tile)
        lane = lax.broadcasted_iota(jnp.int32, (tile, HEAD_DIM), 1)
        lo = jnp.bitwise_and(pos, CHUNK - 1)
        lo_f = lo.astype(F32)
        hi_f = (pos - lo).astype(F32)
        kt_ref[...] = kn.T
        for h in range(N_HEADS):
            cols = _head_cols(h)
            q_ref[h] = qn[:, cols]
            gb_ref[h] = gb[:, cols]
            key_ref[h, :, :HEAD_DIM] = kn[:, cols].astype(BF16)
            key_ref[h, :, HEAD_DIM:] = jnp.where(
                lane == 0, ALIBI_SLOPES[h] * hi_f,
                jnp.where(lane == 1, ALIBI_SLOPES[h] * lo_f, 0.0)).astype(BF16)
            vt_ref[h] = zv[:, cols].T.astype(BF16)
            v_ref[pl.ds(h, tile, stride=N_HEADS), :] = zv[:, cols]
    else:
        ya_ref, q_ref, k_ref, v_ref, gb_ref, va_ref = outs
        q_ref[...] = qn
        k_ref[...] = kn
        v_ref[...] = zv
        gb_ref[...] = gb

    va = proj(COL_VA)
    if not prompt:
        va_ref[...] = va
    vn = ((va * group_rms_scale(va, gm128_ref)) * sgug_ref[...]).astype(BF16)
    gate = proj(COL_U) * jax.nn.silu(proj(COL_GA))
    for c in range(tile // SGU_CHUNK):
        rows = slice(c * SGU_CHUNK, (c + 1) * SGU_CHUNK)
        for g in range(N_HEADS):
            cols = _head_cols(g)
            mixed = _dot(sguw_ref[g], vn[rows, cols]) + sgub_ref[g]
            ya_ref[rows, cols] = (gate[rows, cols] * mixed).astype(BF16)


def _inproj_prompt(x, params, k_stack, v_stack, *, layer, depth, batch, seq):
    tile = TOKEN_TILE
    n_t = seq // tile
    n = batch * seq
    full = lambda a: pl.BlockSpec(a.shape, lambda b, i: (0,) * a.ndim)
    head_major = lambda width: pl.BlockSpec((None, N_HEADS, tile, width),
                                            lambda b, i: (b, 0, i, 0))
    in_specs = [pl.BlockSpec((tile, D_MODEL), lambda b, i: (b * n_t + i, 0))]
    in_specs += [full(a) for a in params]
    operands = [x, *params]
    aliases = {}
    if layer > 0:
        in_specs += [pl.BlockSpec(memory_space=pl.ANY)] * 2
        aliases = {len(operands): 3, len(operands) + 1: 4}
        operands += [k_stack, v_stack]
    out_shape = [
        jax.ShapeDtypeStruct((n, WIDTH), BF16),
        jax.ShapeDtypeStruct((batch, N_HEADS, seq, HEAD_DIM), BF16),
        jax.ShapeDtypeStruct((batch, N_HEADS, n_t, tile, KEY_WIDTH), BF16),
        jax.ShapeDtypeStruct((depth, batch, WIDTH, seq), F32),
        jax.ShapeDtypeStruct((depth, n * N_HEADS, HEAD_DIM), F32),
        jax.ShapeDtypeStruct((batch, N_HEADS, n_t, HEAD_DIM, tile), BF16),
        jax.ShapeDtypeStruct((batch, N_HEADS, seq, HEAD_DIM), BF16),
    ]
    out_specs = [
        pl.BlockSpec((tile, WIDTH), lambda b, i: (b * n_t + i, 0)),
        head_major(HEAD_DIM),
        pl.BlockSpec((None, N_HEADS, None, tile, KEY_WIDTH), lambda b, i: (b, 0, i, 0, 0)),
        pl.BlockSpec((None, None, WIDTH, tile), lambda b, i: (layer, b, 0, i)),
        pl.BlockSpec((None, tile * N_HEADS, HEAD_DIM), lambda b, i: (layer, b * n_t + i, 0)),
        pl.BlockSpec((None, N_HEADS, None, HEAD_DIM, tile), lambda b, i: (b, 0, i, 0, 0)),
        head_major(HEAD_DIM),
    ]
    return pl.pallas_call(
        functools.partial(_inproj_kernel, prompt=True, n_aliased=len(aliases)),
        grid=(batch, n_t),
        in_specs=in_specs,
        out_specs=out_specs,
        out_shape=out_shape,
        input_output_aliases=aliases,
        compiler_params=_compiler_params(2),
        name="inproj_prompt",
    )(*operands)


def _inproj_sample(x, params, *, tile):
    n = x.shape[0]
    row_spec = lambda width: pl.BlockSpec((tile, width), lambda i: (i, 0))
    full = lambda a: pl.BlockSpec(a.shape, lambda i: (0,) * a.ndim)
    dtypes = (BF16, BF16, F32, F32, BF16, F32)
    return pl.pallas_call(
        functools.partial(_inproj_kernel, prompt=False, n_aliased=0),
        grid=(n // tile,),
        in_specs=[row_spec(D_MODEL)] + [full(a) for a in params],
        out_specs=[row_spec(WIDTH)] * len(dtypes),
        out_shape=[jax.ShapeDtypeStruct((n, WIDTH), d) for d in dtypes],
        compiler_params=_compiler_params(1),
        name="inproj_sample",
    )(x, *params)


def _outproj_kernel(x_ref, ya_ref, yb_ref, w_ref, o_ref, *, head_major):
    if head_major:
        yb = jnp.concatenate([yb_ref[h] for h in range(N_HEADS)], axis=1)
    else:
        yb = yb_ref[...]
    y = _dot(ya_ref[...], w_ref[:WIDTH, :]) + _dot(yb, w_ref[WIDTH:, :])
    o_ref[...] = x_ref[...] + y


def _outproj(x, ya, yb, w_bf, *, tile):
    n = x.shape[0]
    head_major = yb.ndim == 4
    if head_major:
        n_t = yb.shape[2] // tile
        yb_spec = pl.BlockSpec((None, N_HEADS, tile, HEAD_DIM),
                               lambda i: (i // n_t, 0, i % n_t, 0))
    else:
        yb_spec = pl.BlockSpec((tile, WIDTH), lambda i: (i, 0))
    return pl.pallas_call(
        functools.partial(_outproj_kernel, head_major=head_major),
        grid=(n // tile,),
        in_specs=[
            pl.BlockSpec((tile, D_MODEL), lambda i: (i, 0)),
            pl.BlockSpec((tile, WIDTH), lambda i: (i, 0)),
            yb_spec,
            pl.BlockSpec(w_bf.shape, lambda i: (0, 0)),
        ],
        out_specs=pl.BlockSpec((tile, D_MODEL), lambda i: (i, 0)),
        out_shape=jax.ShapeDtypeStruct((n, D_MODEL), F32),
        compiler_params=_compiler_params(1),
        name="outproj",
    )(x, ya, yb, w_bf)


def _attn_kernel(slopes_ref, q_ref, key_ref, vt_ref, gb_ref, sg_ref, lq1_ref, lk1_ref,
                 lq2_ref, lk2_ref, o_ref, dt_ref, qbd_ref, s_ref, m_ref, l_ref, acc_ref,
                 *, lam_init):
    t = ATTN_TILE
    qi = pl.program_id(2)

    @pl.when(qi == 0)
    def _prepare_head():
        slope = slopes_ref[pl.program_id(1)]
        key = lax.broadcasted_iota(jnp.int32, (t, t), 0)
        qry = lax.broadcasted_iota(jnp.int32, (t, t), 1)
        allowed = jnp.right_shift(key, 6) <= jnp.right_shift(qry, 6)
        ahead = jnp.maximum(key - qry, 0).astype(F32)
        dt_ref[...] = jnp.where(allowed, (-2.0 * slope) * ahead, NEG_INF)

    q = q_ref[...]
    lane = lax.broadcasted_iota(jnp.int32, q.shape, 1)
    zero = jnp.zeros_like(q)
    ones_aug = jnp.where(lane < 2, 1.0, 0.0).astype(BF16)
    qbd_ref[:t, :HEAD_DIM] = jnp.where(lane < HALF_DIM, q, zero)
    qbd_ref[t:, :HEAD_DIM] = jnp.where(lane >= HALF_DIM, q, zero)
    qbd_ref[:t, HEAD_DIM:] = ones_aug
    qbd_ref[t:, HEAD_DIM:] = ones_aug

    m_ref[...] = jnp.full(m_ref.shape, NEG_INF, F32)
    l_ref[...] = jnp.zeros(l_ref.shape, F32)
    acc_ref[...] = jnp.zeros(acc_ref.shape, F32)

    n_chunks = 2 * t // LANE_CHUNK
    ahead_chunks = s_ref.shape[0]

    def scores(j, c):
        return _dot_nt(key_ref[j], qbd_ref[c * LANE_CHUNK:(c + 1) * LANE_CHUNK, :])

    def softmax_pv(j, c, st, diagonal):
        lanes = slice(c * LANE_CHUNK, (c + 1) * LANE_CHUNK)
        if diagonal:
            q0 = (c * LANE_CHUNK) % t
            st = st + dt_ref[:, q0:q0 + LANE_CHUNK]
        m_old = m_ref[:, lanes]
        m_new = jnp.maximum(m_old, jnp.max(st, axis=0, keepdims=True))
        alpha = jnp.exp(m_old - m_new)
        p = jnp.exp(st - m_new)
        l_ref[:, lanes] = alpha * l_ref[:, lanes] + jnp.sum(p, axis=0, keepdims=True)
        acc_ref[:, lanes] = alpha * acc_ref[:, lanes] + _dot(vt_ref[j], p.astype(BF16))
        m_ref[:, lanes] = m_new

    def tile_step(j, diagonal):
        pending = {}
        for c in range(n_chunks):
            nxt = c + ahead_chunks
            if nxt < n_chunks:
                pending[nxt] = scores(j, nxt)
            elif not diagonal:
                s_next = scores(j + 1, nxt - n_chunks)
            st = pending.pop(c) if c in pending else s_ref[c]
            softmax_pv(j, c, st, diagonal)
            if nxt >= n_chunks and not diagonal:
                s_ref[nxt - n_chunks] = s_next

    for c in range(ahead_chunks):
        s_ref[c] = scores(0, c)

    def body(j, carry):
        tile_step(j, False)
        return carry

    lax.fori_loop(0, qi, body, 0)
    tile_step(qi, True)

    lam = _lam_value(lq1_ref, lk1_ref, lq2_ref, lk2_ref, lam_init)
    acc = acc_ref[...] * (1.0 / l_ref[...])
    ot = acc[:, :t] - lam * acc[:, t:]
    ot = ot * lax.rsqrt(jnp.mean(ot * ot, axis=0, keepdims=True) + NORM_EPS)
    o = (ot.T * sg_ref[...]) * (1.0 - lam_init)
    o_ref[...] = (o * gb_ref[...].astype(F32)).astype(BF16)


def _prompt_attention(q, keys, vt, gb, slopes, sg, lq1, lk1, lq2, lk2, *, lam_init):
    batch, _, seq, _ = q.shape
    t = ATTN_TILE
    n_tiles = seq // t
    tile_spec = pl.BlockSpec((None, None, t, HEAD_DIM), lambda b, h, i: (b, h, i, 0))
    vec = lambda a: pl.BlockSpec(a.shape, lambda b, h, i: (0, 0))
    return pl.pallas_call(
        functools.partial(_attn_kernel, lam_init=lam_init),
        grid=(batch, N_HEADS, n_tiles),
        in_specs=[
            pl.BlockSpec(memory_space=pltpu.SMEM),
            tile_spec,
            pl.BlockSpec((None, None, n_tiles, t, KEY_WIDTH), lambda b, h, i: (b, h, 0, 0, 0)),
            pl.BlockSpec((None, None, n_tiles, HEAD_DIM, t), lambda b, h, i: (b, h, 0, 0, 0)),
            tile_spec, vec(sg), vec(lq1), vec(lk1), vec(lq2), vec(lk2)],
        out_specs=tile_spec,
        out_shape=jax.ShapeDtypeStruct((batch, N_HEADS, seq, HEAD_DIM), BF16),
        scratch_shapes=[
            pltpu.VMEM((t, t), F32),
            pltpu.VMEM((2 * t, KEY_WIDTH), BF16),
            pltpu.VMEM((2, t, LANE_CHUNK), F32),
            pltpu.VMEM((1, 2 * t), F32),
            pltpu.VMEM((1, 2 * t), F32),
            pltpu.VMEM((HEAD_DIM, 2 * t), F32),
        ],
        compiler_params=_compiler_params(3),
        name="prompt_attn",
    )(slopes, q, keys, vt, gb, sg, lq1, lk1, lq2, lk2)


def _sample_attn_kernel(slopes_ref, q_ref, kn_ref, vn_ref, ckt_ref, cv_ref, gb_ref, sg_ref,
                        lq1_ref, lk1_ref, lq2_ref, lk2_ref, o_ref, *, lam_init, past_len):
    nq = q_ref.shape[0]
    lam = _lam_value(lq1_ref, lk1_ref, lq2_ref, lk2_ref, lam_init)
    qrow = lax.broadcasted_iota(jnp.int32, (2 * nq, past_len), 0)
    qpos_past = past_len + jnp.where(qrow >= nq, qrow - nq, qrow)
    dist_past = (qpos_past - lax.broadcasted_iota(jnp.int32, (2 * nq, past_len), 1)).astype(F32)
    pad = HEAD_DIM - nq
    qrow_n = lax.broadcasted_iota(jnp.int32, (2 * nq, HEAD_DIM), 0)
    qidx_n = jnp.where(qrow_n >= nq, qrow_n - nq, qrow_n)
    kidx_n = lax.broadcasted_iota(jnp.int32, (2 * nq, HEAD_DIM), 1)
    dist_new = jnp.abs(qidx_n - kidx_n).astype(F32)
    real_new = kidx_n < nq
    zpad = jnp.zeros((pad, HEAD_DIM), BF16)
    for h in range(N_HEADS):
        cols = _head_cols(h)
        slope = slopes_ref[h]
        q = q_ref[:, cols]
        lane = lax.broadcasted_iota(jnp.int32, q.shape, 1)
        zero = jnp.zeros_like(q)
        qbd = jnp.concatenate([jnp.where(lane < HALF_DIM, q, zero),
                               jnp.where(lane >= HALF_DIM, q, zero)], axis=0)
        s_past = _dot(qbd, ckt_ref[cols, :].astype(BF16)) - slope * dist_past
        kn = jnp.concatenate([kn_ref[:, cols].astype(BF16), zpad], axis=0)
        vn = jnp.concatenate([vn_ref[:, cols].astype(BF16), zpad], axis=0)
        s_new = _dot_nt(qbd, kn) - slope * dist_new
        s_new = jnp.where(real_new, s_new, NEG_INF)
        m = jnp.maximum(jnp.max(s_past, axis=-1, keepdims=True),
                        jnp.max(s_new, axis=-1, keepdims=True))
        p_past = jnp.exp(s_past - m)
        p_new = jnp.exp(s_new - m)
        l = jnp.sum(p_past, axis=-1, keepdims=True) + jnp.sum(p_new, axis=-1, keepdims=True)
        v_past = cv_ref[pl.ds(h, past_len, stride=N_HEADS), :].astype(BF16)
        acc = _dot(p_past.astype(BF16), v_past) + _dot(p_new.astype(BF16), vn)
        acc = acc * (1.0 / l)
        o = acc[:nq] - lam * acc[nq:]
        o = o * lax.rsqrt(jnp.mean(o * o, axis=-1, keepdims=True) + NORM_EPS)
        o = (o * sg_ref[...]) * (1.0 - lam_init)
        o_ref[:, cols] = (o * gb_ref[:, cols].astype(F32)).astype(BF16)


def _sample_attention(q, k_new, v_new, cache_kt, cache_v, gb, slopes, sg, lq1, lk1, lq2, lk2,
                      *, layer, n_streams, n_new, lam_init):
    past_len = cache_kt.shape[3]
    new_spec = pl.BlockSpec((None, n_new, WIDTH), lambda b: (b, 0, 0))
    vec = lambda a: pl.BlockSpec(a.shape, lambda b: (0, 0))
    r3 = lambda a: a.reshape(n_streams, n_new, WIDTH)
    out = pl.pallas_call(
        functools.partial(_sample_attn_kernel, lam_init=lam_init, past_len=past_len),
        grid=(n_streams,),
        in_specs=[
            pl.BlockSpec(memory_space=pltpu.SMEM), new_spec, new_spec, new_spec,
            pl.BlockSpec((None, None, WIDTH, past_len), lambda b: (layer, b, 0, 0)),
            pl.BlockSpec((None, None, past_len * N_HEADS, HEAD_DIM), lambda b: (layer, b, 0, 0)),
            new_spec, vec(sg), vec(lq1), vec(lk1), vec(lq2), vec(lk2)],
        out_specs=new_spec,
        out_shape=jax.ShapeDtypeStruct((n_streams, n_new, WIDTH), BF16),
        compiler_params=_compiler_params(1),
        name="sample_attn",
    )(slopes, r3(q), r3(k_new), r3(v_new), cache_kt, cache_v, r3(gb), sg, lq1, lk1, lq2, lk2)
    return out.reshape(n_streams * n_new, WIDTH)


def _group_mean_matrix(group):
    idx = jnp.arange(WIDTH) // group
    return jnp.where(idx[:, None] == idx[None, :], 1.0 / group, 0.0).astype(BF16)


def kernel(x_prompt, x_sample, cache_k, cache_v, norm_g, w_in, sgu_norm_g, sgu_w, sgu_b,
           q_norm_g, k_norm_g, lambda_q1, lambda_k1, lambda_q2, lambda_k2, subln_g, w_out):
    depth = w_in.shape[0]
    batch, seq, _ = x_prompt.shape
    n_streams, n_new, _ = x_sample.shape
    past_len = cache_k.shape[2]
    assert seq % ATTN_TILE == 0
    assert SGU_CHUNK % n_new == 0 and past_len % CHUNK == 0 and n_new <= CHUNK
    sample_tile = min(n_streams * n_new, TOKEN_TILE)
    assert sample_tile % SGU_CHUNK == 0 and (n_streams * n_new) % sample_tile == 0

    gm128 = _group_mean_matrix(HEAD_DIM)
    gm64 = _group_mean_matrix(HALF_DIM)
    slopes = jnp.asarray(ALIBI_SLOPES, F32)
    tril = jnp.tril(jnp.ones((SGU_CHUNK, SGU_CHUNK), F32))
    tril_new = jnp.tril(jnp.ones((n_new, n_new), F32))
    streams_per_chunk = SGU_CHUNK // n_new
    eye = jnp.eye(streams_per_chunk, dtype=F32)
    cache_kt = jnp.transpose(cache_k, (0, 1, 3, 4, 5, 2)).reshape(depth, n_streams, WIDTH, past_len)
    cache_vr = cache_v.reshape(depth, n_streams, past_len * N_HEADS, HEAD_DIM)

    xp = x_prompt.reshape(batch * seq, D_MODEL)
    xs = x_sample.reshape(n_streams * n_new, D_MODEL)
    k_stack = v_stack = None
    ks_rows, vs_rows, sgu_rows = [], [], []
    for i in range(depth):
        lam_init = _lam_init(i)
        w_in_bf = w_in[i].astype(BF16)
        w_out_bf = w_out[i].astype(BF16)
        row = lambda a: a.reshape(1, -1).astype(F32)
        ng = row(norm_g[i])
        sgug = row(sgu_norm_g[i])
        gq = row(jnp.tile(q_norm_g[i], WIDTH // HALF_DIM))
        gk = row(jnp.tile(k_norm_g[i], WIDTH // HALF_DIM))
        sg = row(subln_g[i])
        lams = tuple(row(a[i]) for a in (lambda_q1, lambda_k1, lambda_q2, lambda_k2))
        sguw_p = (sgu_w[i] * tril).astype(BF16)
        sgub_p = jnp.broadcast_to(sgu_b[i][:, :, None], (N_HEADS, SGU_CHUNK, HEAD_DIM))
        w_new = sgu_w[i][:, :n_new, :n_new] * tril_new
        sguw_s = jnp.einsum("ab,hts->hatbs", eye, w_new).reshape(
            N_HEADS, SGU_CHUNK, SGU_CHUNK).astype(BF16)
        sgub_s = jnp.broadcast_to(
            jnp.tile(sgu_b[i][:, :n_new], (1, streams_per_chunk))[:, :, None],
            (N_HEADS, SGU_CHUNK, HEAD_DIM))
        params_p = (ng, w_in_bf, sgug, sguw_p, sgub_p, gq, gk, gm128, gm64)
        params_s = (ng, w_in_bf, sgug, sguw_s, sgub_s, gq, gk, gm128, gm64)

        ya, q, keys, k_stack, v_stack, vt, gb = _inproj_prompt(
            xp, params_p, k_stack, v_stack, layer=i, depth=depth, batch=batch, seq=seq)
        yb = _prompt_attention(q, keys, vt, gb, slopes, sg, *lams, lam_init=lam_init)
        xp = _outproj(xp, ya, yb, w_out_bf, tile=TOKEN_TILE)

        ya, q, k, v, gb, va = _inproj_sample(xs, params_s, tile=sample_tile)
        yb = _sample_attention(q, k, v, cache_kt, cache_vr, gb, slopes, sg, *lams, layer=i,
                               n_streams=n_streams, n_new=n_new, lam_init=lam_init)
        xs = _outproj(xs, ya, yb, w_out_bf, tile=sample_tile)
        ks_rows.append(k)
        vs_rows.append(v)
        sgu_rows.append(va)

    new_k_prompt = jnp.transpose(
        k_stack.reshape(depth, batch, N_HEADS, 2, HALF_DIM, seq), (0, 1, 5, 2, 3, 4))
    return (
        xp.reshape(batch, seq, D_MODEL),
        xs.reshape(n_streams, n_new, D_MODEL),
        new_k_prompt,
        v_stack.reshape(depth, batch, seq, N_HEADS, HEAD_DIM),
        jnp.stack(ks_rows).reshape(depth, n_streams, n_new, N_HEADS, 2, HALF_DIM),
        jnp.stack(vs_rows).reshape(depth, n_streams, n_new, N_HEADS, HEAD_DIM),
        jnp.stack(sgu_rows).reshape(depth, n_streams, n_new, WIDTH),
    )
```

```python
import functools
import math

import jax
import jax.numpy as jnp
from jax import lax
from jax.experimental import pallas as pl
from jax.experimental.pallas import tpu as pltpu

F32 = jnp.float32
BF16 = jnp.bfloat16

D_MODEL = 1024
N_HEADS = 4
HEAD_DIM = 128
HALF_DIM = 64
WIDTH = N_HEADS * HEAD_DIM
CHUNK = 64
SGU_CHUNK = 128
NORM_EPS = 1e-6
NEG_INF = -1e30
QK_SCALE = HALF_DIM ** -0.5
ALIBI_SLOPES = tuple(2.0 ** (-8.0 * (h + 1) / N_HEADS) for h in range(N_HEADS))

COL_U, COL_VA, COL_GA, COL_Q, COL_K, COL_V, COL_GB = (i * WIDTH for i in range(7))

VMEM_LIMIT_BYTES = 56 * 1024 * 1024

ATTN_TILE = 512
TOKEN_TILE = ATTN_TILE
LANE_CHUNK = 256
KEY_WIDTH = 2 * HEAD_DIM


def _lam_init(layer_idx):
    return 0.8 - 0.6 * math.exp(-0.3 * layer_idx)


def _compiler_params(n_axes):
    return pltpu.CompilerParams(
        dimension_semantics=("arbitrary",) * n_axes,
        vmem_limit_bytes=VMEM_LIMIT_BYTES,
    )


def _dot(a, b):
    return jnp.dot(a, b, preferred_element_type=F32)


def _dot_nt(a, b):
    return lax.dot_general(a, b, (((1,), (1,)), ((), ())), preferred_element_type=F32)


def _lam_value(lq1_ref, lk1_ref, lq2_ref, lk2_ref, lam_init):
    d1 = jnp.sum(lq1_ref[...] * lk1_ref[...], axis=-1, keepdims=True)
    d2 = jnp.sum(lq2_ref[...] * lk2_ref[...], axis=-1, keepdims=True)
    return jnp.exp(d1) - jnp.exp(d2) + lam_init


def _head_cols(h):
    return slice(h * HEAD_DIM, (h + 1) * HEAD_DIM)


def _inproj_kernel(*refs, prompt, n_aliased):
    (x_ref, ng_ref, w_ref, sgug_ref, sguw_ref, sgub_ref, gq_ref, gk_ref,
     gm128_ref, gm64_ref) = refs[:10]
    outs = refs[10 + n_aliased:]
    x = x_ref[...]
    tile = x.shape[0]
    ms = jnp.mean(x * x, axis=-1, keepdims=True)
    hb = ((x * lax.rsqrt(ms + NORM_EPS)) * ng_ref[...]).astype(BF16)

    def proj(col):
        return _dot(hb, w_ref[:, col:col + WIDTH])

    def group_rms_scale(z, gm_ref):
        return lax.rsqrt(_dot((z * z).astype(BF16), gm_ref[...]) + NORM_EPS)

    zq = proj(COL_Q)
    qn = ((zq * group_rms_scale(zq, gm64_ref)) * gq_ref[...] * QK_SCALE).astype(BF16)
    zk = proj(COL_K)
    kn = (zk * group_rms_scale(zk, gm64_ref)) * gk_ref[...]
    zv = proj(COL_V)
    gb = jax.nn.silu(proj(COL_GB)).astype(BF16)

    if prompt:
        ya_ref, q_ref, key_ref, kt_ref, v_ref, vt_ref, gb_ref = outs
        pos = (lax.broadcasted_iota(jnp.int32, (tile, HEAD_DIM), 0)
               + pl.program_id(1) * tile)
        lane = lax.broadcasted_iota(jnp.int32, (tile, HEAD_DIM), 1)
        lo = jnp.bitwise_and(pos, CHUNK - 1)
        lo_f = lo.astype(F32)
        hi_f = (pos - lo).astype(F32)
        kt_ref[...] = kn.T
        for h in range(N_HEADS):
            cols = _head_cols(h)
            q_ref[h] = qn[:, cols]
            gb_ref[h] = gb[:, cols]
            key_ref[h, :, :HEAD_DIM] = kn[:, cols].astype(BF16)
            key_ref[h, :, HEAD_DIM:] = jnp.where(
                lane == 0, ALIBI_SLOPES[h] * hi_f,
                jnp.where(lane == 1, ALIBI_SLOPES[h] * lo_f, 0.0)).astype(BF16)
            vt_ref[h] = zv[:, cols].T.astype(BF16)
            v_ref[pl.ds(h, tile, stride=N_HEADS), :] = zv[:, cols]
    else:
        ya_ref, q_ref, k_ref, v_ref, gb_ref, va_ref = outs
        q_ref[...] = qn
        k_ref[...] = kn
        v_ref[...] = zv
        gb_ref[...] = gb

    va = proj(COL_VA)
    if not prompt:
        va_ref[...] = va
    vn = ((va * group_rms_scale(va, gm128_ref)) * sgug_ref[...]).astype(BF16)
    gate = proj(COL_U) * jax.nn.silu(proj(COL_GA))
    for c in range(tile // SGU_CHUNK):
        rows = slice(c * SGU_CHUNK, (c + 1) * SGU_CHUNK)
        for g in range(N_HEADS):
            cols = _head_cols(g)
            mixed = _dot(sguw_ref[g], vn[rows, cols]) + sgub_ref[g]
            ya_ref[rows, cols] = (gate[rows, cols] * mixed).astype(BF16)


def _inproj_prompt(x, params, k_stack, v_stack, *, layer, depth, batch, seq):
    tile = TOKEN_TILE
    n_t = seq // tile
    n = batch * seq
    full = lambda a: pl.BlockSpec(a.shape, lambda b, i: (0,) * a.ndim)
    head_major = lambda width: pl.BlockSpec((None, N_HEADS, tile, width),
                                            lambda b, i: (b, 0, i, 0))
    in_specs = [pl.BlockSpec((tile, D_MODEL), lambda b, i: (b * n_t + i, 0))]
    in_specs += [full(a) for a in params]
    operands = [x, *params]
    aliases = {}
    if layer > 0:
        in_specs += [pl.BlockSpec(memory_space=pl.ANY)] * 2
        aliases = {len(operands): 3, len(operands) + 1: 4}
        operands += [k_stack, v_stack]
    out_shape = [
        jax.ShapeDtypeStruct((n, WIDTH), BF16),
        jax.ShapeDtypeStruct((batch, N_HEADS, seq, HEAD_DIM), BF16),
        jax.ShapeDtypeStruct((batch, N_HEADS, n_t, tile, KEY_WIDTH), BF16),
        jax.ShapeDtypeStruct((depth, batch, WIDTH, seq), F32),
        jax.ShapeDtypeStruct((depth, n * N_HEADS, HEAD_DIM), F32),
        jax.ShapeDtypeStruct((batch, N_HEADS, n_t, HEAD_DIM, tile), BF16),
        jax.ShapeDtypeStruct((batch, N_HEADS, seq, HEAD_DIM), BF16),
    ]
    out_specs = [
        pl.BlockSpec((tile, WIDTH), lambda b, i: (b * n_t + i, 0)),
        head_major(HEAD_DIM),
        pl.BlockSpec((None, N_HEADS, None, tile, KEY_WIDTH), lambda b, i: (b, 0, i, 0, 0)),
        pl.BlockSpec((None, None, WIDTH, tile), lambda b, i: (layer, b, 0, i)),
        pl.BlockSpec((None, tile * N_HEADS, HEAD_DIM), lambda b, i: (layer, b * n_t + i, 0)),
        pl.BlockSpec((None, N_HEADS, None, HEAD_DIM, tile), lambda b, i: (b, 0, i, 0, 0)),
        head_major(HEAD_DIM),
    ]
    return pl.pallas_call(
        functools.partial(_inproj_kernel, prompt=True, n_aliased=len(aliases)),
        grid=(batch, n_t),
        in_specs=in_specs,
        out_specs=out_specs,
        out_shape=out_shape,
        input_output_aliases=aliases,
        compiler_params=_compiler_params(2),
        name="inproj_prompt",
    )(*operands)


def _inproj_sample(x, params, *, tile):
    n = x.shape[0]
    row_spec = lambda width: pl.BlockSpec((tile, width), lambda i: (i, 0))
    full = lambda a: pl.BlockSpec(a.shape, lambda i: (0,) * a.ndim)
    dtypes = (BF16, BF16, F32, F32, BF16, F32)
    return pl.pallas_call(
        functools.partial(_inproj_kernel, prompt=False, n_aliased=0),
        grid=(n // tile,),
        in_specs=[row_spec(D_MODEL)] + [full(a) for a in params],
        out_specs=[row_spec(WIDTH)] * len(dtypes),
        out_shape=[jax.ShapeDtypeStruct((n, WIDTH), d) for d in dtypes],
        compiler_params=_compiler_params(1),
        name="inproj_sample",
    )(x, *params)


def _outproj_kernel(x_ref, ya_ref, yb_ref, w_ref, o_ref, *, head_major):
    if head_major:
        yb = jnp.concatenate([yb_ref[h] for h in range(N_HEADS)], axis=1)
    else:
        yb = yb_ref[...]
    y = _dot(ya_ref[...], w_ref[:WIDTH, :]) + _dot(yb, w_ref[WIDTH:, :])
    o_ref[...] = x_ref[...] + y


def _outproj(x, ya, yb, w_bf, *, tile):
    n = x.shape[0]
    head_major = yb.ndim == 4
    if head_major:
        n_t = yb.shape[2] // tile
        yb_spec = pl.BlockSpec((None, N_HEADS, tile, HEAD_DIM),
                               lambda i: (i // n_t, 0, i % n_t, 0))
    else:
        yb_spec = pl.BlockSpec((tile, WIDTH), lambda i: (i, 0))
    return pl.pallas_call(
        functools.partial(_outproj_kernel, head_major=head_major),
        grid=(n // tile,),
        in_specs=[
            pl.BlockSpec((tile, D_MODEL), lambda i: (i, 0)),
            pl.BlockSpec((tile, WIDTH), lambda i: (i, 0)),
            yb_spec,
            pl.BlockSpec(w_bf.shape, lambda i: (0, 0)),
        ],
        out_specs=pl.BlockSpec((tile, D_MODEL), lambda i: (i, 0)),
        out_shape=jax.ShapeDtypeStruct((n, D_MODEL), F32),
        compiler_params=_compiler_params(1),
        name="outproj",
    )(x, ya, yb, w_bf)


SCORES_AHEAD = 2


def _attn_units(seq):
    n_qc = seq // LANE_CHUNK
    qc_per_tile = ATTN_TILE // LANE_CHUNK
    return [(j, qc, half)
            for j in range(seq // ATTN_TILE)
            for qc in range(j * qc_per_tile, n_qc)
            for half in range(2)]


def _attn_kernel(slopes_ref, q_ref, key_ref, vt_ref, gb_ref, sg_ref, lq1_ref, lk1_ref,
                 lq2_ref, lk2_ref, o_ref, qbd_ref, m_ref, l_ref, acc_ref, *, lam_init):
    seq = q_ref.shape[0]
    t, w = ATTN_TILE, LANE_CHUNK
    assert t == 2 * w
    n_qc = seq // w
    slope = slopes_ref[pl.program_id(1)]

    key = lax.broadcasted_iota(jnp.int32, (w, w), 0)
    qry = lax.broadcasted_iota(jnp.int32, (w, w), 1)
    allowed = jnp.right_shift(key, 6) <= jnp.right_shift(qry, 6)
    ahead = jnp.maximum(key - qry, 0).astype(F32)
    dt = jnp.where(allowed, (-2.0 * slope) * ahead, NEG_INF)

    q = q_ref[...]
    lane = lax.broadcasted_iota(jnp.int32, q.shape, 1)
    zero = jnp.zeros_like(q)
    ones_aug = jnp.where(lane < 2, 1.0, 0.0).astype(BF16)
    qbd_ref[:seq, :HEAD_DIM] = jnp.where(lane < HALF_DIM, q, zero)
    qbd_ref[seq:, :HEAD_DIM] = jnp.where(lane >= HALF_DIM, q, zero)
    qbd_ref[:seq, HEAD_DIM:] = ones_aug
    qbd_ref[seq:, HEAD_DIM:] = ones_aug

    lam = _lam_value(lq1_ref, lk1_ref, lq2_ref, lk2_ref, lam_init)

    def n_keys(j, qc):
        return w if qc * w == j * t else t

    def scores(unit):
        j, qc, half = unit
        c = half * n_qc + qc
        return _dot_nt(key_ref[j, :n_keys(j, qc), :], qbd_ref[c * w:(c + 1) * w, :])

    def softmax(unit, st):
        j, qc, half = unit
        lanes = slice((half * n_qc + qc) * w, (half * n_qc + qc + 1) * w)
        nk = n_keys(j, qc)
        if qc * w < (j + 1) * t:
            st = st + dt if nk == w else jnp.concatenate([st[:w], st[w:] + dt], axis=0)
        m_cur = jnp.max(st, axis=0, keepdims=True)
        if j == 0:
            m_new, alpha = m_cur, None
        else:
            m_old = m_ref[:, lanes]
            m_new = jnp.maximum(m_old, m_cur)
            alpha = jnp.exp(m_old - m_new)
        p = jnp.exp(st - m_new)
        l_cur = jnp.sum(p, axis=0, keepdims=True)
        l_ref[:, lanes] = l_cur if j == 0 else alpha * l_ref[:, lanes] + l_cur
        m_ref[:, lanes] = m_new
        return p.astype(BF16), alpha

    def values(unit, p, alpha):
        j, qc, half = unit
        lanes = slice((half * n_qc + qc) * w, (half * n_qc + qc + 1) * w)
        pv = _dot(vt_ref[j, :, :n_keys(j, qc)], p)
        acc_ref[:, lanes] = pv if j == 0 else alpha * acc_ref[:, lanes] + pv

    def finalize(qc):
        rows = slice(qc * w, (qc + 1) * w)
        lanes1 = slice(qc * w, (qc + 1) * w)
        lanes2 = slice((n_qc + qc) * w, (n_qc + qc + 1) * w)
        ot = (acc_ref[:, lanes1] * (1.0 / l_ref[:, lanes1])
              - lam * (acc_ref[:, lanes2] * (1.0 / l_ref[:, lanes2])))
        ot = ot * lax.rsqrt(jnp.mean(ot * ot, axis=0, keepdims=True) + NORM_EPS)
        o = (ot.T * sg_ref[...]) * (1.0 - lam_init)
        o_ref[rows, :] = (o * gb_ref[rows, :].astype(F32)).astype(BF16)

    units = _attn_units(seq)
    pending_scores = {u: scores(units[u]) for u in range(min(SCORES_AHEAD, len(units)))}
    pending_values = None
    for u, unit in enumerate(units):
        if u + SCORES_AHEAD < len(units):
            pending_scores[u + SCORES_AHEAD] = scores(units[u + SCORES_AHEAD])
        p, alpha = softmax(unit, pending_scores.pop(u))
        if pending_values is not None:
            values(*pending_values)
            j_done, qc_done, half_done = pending_values[0]
            if half_done == 1 and qc_done * w < (j_done + 1) * t:
                finalize(qc_done)
        pending_values = (unit, p, alpha)
    values(*pending_values)
    finalize(pending_values[0][1])


def _prompt_attention(q, keys, vt, gb, slopes, sg, lq1, lk1, lq2, lk2, *, lam_init):
    batch, _, seq, _ = q.shape
    t = ATTN_TILE
    n_tiles = seq // t
    seq_spec = pl.BlockSpec((None, None, seq, HEAD_DIM), lambda b, h: (b, h, 0, 0))
    vec = lambda a: pl.BlockSpec(a.shape, lambda b, h: (0, 0))
    return pl.pallas_call(
        functools.partial(_attn_kernel, lam_init=lam_init),
        grid=(batch, N_HEADS),
        in_specs=[
            pl.BlockSpec(memory_space=pltpu.SMEM),
            seq_spec,
            pl.BlockSpec((None, None, n_tiles, t, KEY_WIDTH), lambda b, h: (b, h, 0, 0, 0)),
            pl.BlockSpec((None, None, n_tiles, HEAD_DIM, t), lambda b, h: (b, h, 0, 0, 0)),
            seq_spec, vec(sg), vec(lq1), vec(lk1), vec(lq2), vec(lk2)],
        out_specs=seq_spec,
        out_shape=jax.ShapeDtypeStruct((batch, N_HEADS, seq, HEAD_DIM), BF16),
        scratch_shapes=[
            pltpu.VMEM((2 * seq, KEY_WIDTH), BF16),
            pltpu.VMEM((1, 2 * seq), F32),
            pltpu.VMEM((1, 2 * seq), F32),
            pltpu.VMEM((HEAD_DIM, 2 * seq), F32),
        ],
        compiler_params=_compiler_params(2),
        name="prompt_attn",
    )(slopes, q, keys, vt, gb, sg, lq1, lk1, lq2, lk2)


def _sample_attn_kernel(slopes_ref, q_ref, kn_ref, vn_ref, ckt_ref, cv_ref, gb_ref, sg_ref,
                        lq1_ref, lk1_ref, lq2_ref, lk2_ref, o_ref, *, lam_init, past_len):
    nq = q_ref.shape[0]
    lam = _lam_value(lq1_ref, lk1_ref, lq2_ref, lk2_ref, lam_init)
    qrow = lax.broadcasted_iota(jnp.int32, (2 * nq, past_len), 0)
    qpos_past = past_len + jnp.where(qrow >= nq, qrow - nq, qrow)
    dist_past = (qpos_past - lax.broadcasted_iota(jnp.int32, (2 * nq, past_len), 1)).astype(F32)
    pad = HEAD_DIM - nq
    qrow_n = lax.broadcasted_iota(jnp.int32, (2 * nq, HEAD_DIM), 0)
    qidx_n = jnp.where(qrow_n >= nq, qrow_n - nq, qrow_n)
    kidx_n = lax.broadcasted_iota(jnp.int32, (2 * nq, HEAD_DIM), 1)
    dist_new = jnp.abs(qidx_n - kidx_n).astype(F32)
    real_new = kidx_n < nq
    zpad = jnp.zeros((pad, HEAD_DIM), BF16)
    for h in range(N_HEADS):
        cols = _head_cols(h)
        slope = slopes_ref[h]
        q = q_ref[:, cols]
        lane = lax.broadcasted_iota(jnp.int32, q.shape, 1)
        zero = jnp.zeros_like(q)
        qbd = jnp.concatenate([jnp.where(lane < HALF_DIM, q, zero),
                               jnp.where(lane >= HALF_DIM, q, zero)], axis=0)
        s_past = _dot(qbd, ckt_ref[cols, :].astype(BF16)) - slope * dist_past
        kn = jnp.concatenate([kn_ref[:, cols].astype(BF16), zpad], axis=0)
        vn = jnp.concatenate([vn_ref[:, cols].astype(BF16), zpad], axis=0)
        s_new = _dot_nt(qbd, kn) - slope * dist_new
        s_new = jnp.where(real_new, s_new, NEG_INF)
        m = jnp.maximum(jnp.max(s_past, axis=-1, keepdims=True),
                        jnp.max(s_new, axis=-1, keepdims=True))
        p_past = jnp.exp(s_past - m)
        p_new = jnp.exp(s_new - m)
        l = jnp.sum(p_past, axis=-1, keepdims=True) + jnp.sum(p_new, axis=-1, keepdims=True)
        v_past = cv_ref[pl.ds(h, past_len, stride=N_HEADS), :].astype(BF16)
        acc = _dot(p_past.astype(BF16), v_past) + _dot(p_new.astype(BF16), vn)
        acc = acc * (1.0 / l)
        o = acc[:nq] - lam * acc[nq:]
        o = o * lax.rsqrt(jnp.mean(o * o, axis=-1, keepdims=True) + NORM_EPS)
        o = (o * sg_ref[...]) * (1.0 - lam_init)
        o_ref[:, cols] = (o * gb_ref[:, cols].astype(F32)).astype(BF16)


def _sample_attention(q, k_new, v_new, cache_kt, cache_v, gb, slopes, sg, lq1, lk1, lq2, lk2,
                      *, layer, n_streams, n_new, lam_init):
    past_len = cache_kt.shape[3]
    new_spec = pl.BlockSpec((None, n_new, WIDTH), lambda b: (b, 0, 0))
    vec = lambda a: pl.BlockSpec(a.shape, lambda b: (0, 0))
    r3 = lambda a: a.reshape(n_streams, n_new, WIDTH)
    out = pl.pallas_call(
        functools.partial(_sample_attn_kernel, lam_init=lam_init, past_len=past_len),
        grid=(n_streams,),
        in_specs=[
            pl.BlockSpec(memory_space=pltpu.SMEM), new_spec, new_spec, new_spec,
            pl.BlockSpec((None, None, WIDTH, past_len), lambda b: (layer, b, 0, 0)),
            pl.BlockSpec((None, None, past_len * N_HEADS, HEAD_DIM), lambda b: (layer, b, 0, 0)),
            new_spec, vec(sg), vec(lq1), vec(lk1), vec(lq2), vec(lk2)],
        out_specs=new_spec,
        out_shape=jax.ShapeDtypeStruct((n_streams, n_new, WIDTH), BF16),
        compiler_params=_compiler_params(1),
        name="sample_attn",
    )(slopes, r3(q), r3(k_new), r3(v_new), cache_kt, cache_v, r3(gb), sg, lq1, lk1, lq2, lk2)
    return out.reshape(n_streams * n_new, WIDTH)


def _group_mean_matrix(group):
    idx = jnp.arange(WIDTH) // group
    return jnp.where(idx[:, None] == idx[None, :], 1.0 / group, 0.0).astype(BF16)


def kernel(x_prompt, x_sample, cache_k, cache_v, norm_g, w_in, sgu_norm_g, sgu_w, sgu_b,
           q_norm_g, k_norm_g, lambda_q1, lambda_k1, lambda_q2, lambda_k2, subln_g, w_out):
    depth = w_in.shape[0]
    batch, seq, _ = x_prompt.shape
    n_streams, n_new, _ = x_sample.shape
    past_len = cache_k.shape[2]
    assert seq % ATTN_TILE == 0
    assert SGU_CHUNK % n_new == 0 and past_len % CHUNK == 0 and n_new <= CHUNK
    sample_tile = min(n_streams * n_new, TOKEN_TILE)
    assert sample_tile % SGU_CHUNK == 0 and (n_streams * n_new) % sample_tile == 0

    gm128 = _group_mean_matrix(HEAD_DIM)
    gm64 = _group_mean_matrix(HALF_DIM)
    slopes = jnp.asarray(ALIBI_SLOPES, F32)
    tril = jnp.tril(jnp.ones((SGU_CHUNK, SGU_CHUNK), F32))
    tril_new = jnp.tril(jnp.ones((n_new, n_new), F32))
    streams_per_chunk = SGU_CHUNK // n_new
    eye = jnp.eye(streams_per_chunk, dtype=F32)
    cache_kt = jnp.transpose(cache_k, (0, 1, 3, 4, 5, 2)).reshape(depth, n_streams, WIDTH, past_len)
    cache_vr = cache_v.reshape(depth, n_streams, past_len * N_HEADS, HEAD_DIM)

    xp = x_prompt.reshape(batch * seq, D_MODEL)
    xs = x_sample.reshape(n_streams * n_new, D_MODEL)
    k_stack = v_stack = None
    ks_rows, vs_rows, sgu_rows = [], [], []
    for i in range(depth):
        lam_init = _lam_init(i)
        w_in_bf = w_in[i].astype(BF16)
        w_out_bf = w_out[i].astype(BF16)
        row = lambda a: a.reshape(1, -1).astype(F32)
        ng = row(norm_g[i])
        sgug = row(sgu_norm_g[i])
        gq = row(jnp.tile(q_norm_g[i], WIDTH // HALF_DIM))
        gk = row(jnp.tile(k_norm_g[i], WIDTH // HALF_DIM))
        sg = row(subln_g[i])
        lams = tuple(row(a[i]) for a in (lambda_q1, lambda_k1, lambda_q2, lambda_k2))
        sguw_p = (sgu_w[i] * tril).astype(BF16)
        sgub_p = jnp.broadcast_to(sgu_b[i][:, :, None], (N_HEADS, SGU_CHUNK, HEAD_DIM))
        w_new = sgu_w[i][:, :n_new, :n_new] * tril_new
        sguw_s = jnp.einsum("ab,hts->hatbs", eye, w_new).reshape(
            N_HEADS, SGU_CHUNK, SGU_CHUNK).astype(BF16)
        sgub_s = jnp.broadcast_to(
            jnp.tile(sgu_b[i][:, :n_new], (1, streams_per_chunk))[:, :, None],
            (N_HEADS, SGU_CHUNK, HEAD_DIM))
        params_p = (ng, w_in_bf, sgug, sguw_p, sgub_p, gq, gk, gm128, gm64)
        params_s = (ng, w_in_bf, sgug, sguw_s, sgub_s, gq, gk, gm128, gm64)

        ya, q, keys, k_stack, v_stack, vt, gb = _inproj_prompt(
            xp, params_p, k_stack, v_stack, layer=i, depth=depth, batch=batch, seq=seq)
        yb = _prompt_attention(q, keys, vt, gb, slopes, sg, *lams, lam_init=lam_init)
        xp = _outproj(xp, ya, yb, w_out_bf, tile=TOKEN_TILE)

        ya, q, k, v, gb, va = _inproj_sample(xs, params_s, tile=sample_tile)
        yb = _sample_attention(q, k, v, cache_kt, cache_vr, gb, slopes, sg, *lams, layer=i,
                               n_streams=n_streams, n_new=n_new, lam_init=lam_init)
        xs = _outproj(xs, ya, yb, w_out_bf, tile=sample_tile)
        ks_rows.append(k)
        vs_rows.append(v)
        sgu_rows.append(va)

    new_k_prompt = jnp.transpose(
        k_stack.reshape(depth, batch, N_HEADS, 2, HALF_DIM, seq), (0, 1, 5, 2, 3, 4))
    return (
        xp.reshape(batch, seq, D_MODEL),
        xs.reshape(n_streams, n_new, D_MODEL),
        new_k_prompt,
        v_stack.reshape(depth, batch, seq, N_HEADS, HEAD_DIM),
        jnp.stack(ks_rows).reshape(depth, n_streams, n_new, N_HEADS, 2, HALF_DIM),
        jnp.stack(vs_rows).reshape(depth, n_streams, n_new, N_HEADS, HEAD_DIM),
        jnp.stack(sgu_rows).reshape(depth, n_streams, n_new, WIDTH),
    )
```

```python
import functools
import math

import jax
import jax.numpy as jnp
from jax import lax
from jax.experimental import pallas as pl
from jax.experimental.pallas import tpu as pltpu

F32 = jnp.float32
BF16 = jnp.bfloat16

D_MODEL = 1024
N_HEADS = 4
HEAD_DIM = 128
HALF_DIM = 64
WIDTH = N_HEADS * HEAD_DIM
CHUNK = 64
SGU_CHUNK = 128
NORM_EPS = 1e-6
NEG_INF = -1e30
QK_SCALE = HALF_DIM ** -0.5
LOG2E = math.log2(math.e)
ALIBI_SLOPES = tuple(2.0 ** (-8.0 * (h + 1) / N_HEADS) for h in range(N_HEADS))

COL_U, COL_VA, COL_GA, COL_Q, COL_K, COL_V, COL_GB = (i * WIDTH for i in range(7))

VMEM_LIMIT_BYTES = 56 * 1024 * 1024

ATTN_TILE = 512
TOKEN_TILE = ATTN_TILE
LANE_CHUNK = 256
KEY_WIDTH = 2 * HEAD_DIM
GROUP_MEAN_WIDTH = 256
POS_SPLIT = 3
VT_ROWS = HEAD_DIM + 16


def _lam_init(layer_idx):
    return 0.8 - 0.6 * math.exp(-0.3 * layer_idx)


def _compiler_params(n_axes):
    return pltpu.CompilerParams(
        dimension_semantics=("arbitrary",) * n_axes,
        vmem_limit_bytes=VMEM_LIMIT_BYTES,
    )


def _dot(a, b):
    return jnp.dot(a, b, preferred_element_type=F32)


def _dot_nt(a, b):
    return lax.dot_general(a, b, (((1,), (1,)), ((), ())), preferred_element_type=F32)


def _lam_value(lq1_ref, lk1_ref, lq2_ref, lk2_ref, lam_init):
    d1 = jnp.sum(lq1_ref[...] * lk1_ref[...], axis=-1, keepdims=True)
    d2 = jnp.sum(lq2_ref[...] * lk2_ref[...], axis=-1, keepdims=True)
    return jnp.exp(d1) - jnp.exp(d2) + lam_init


def _head_cols(h):
    return slice(h * HEAD_DIM, (h + 1) * HEAD_DIM)


def _inproj_kernel(*refs, prompt, n_prev):
    (x_ref, ng_ref, w_ref, sgug_ref, sguw_ref, sgub_ref, gq_ref, gk_ref,
     gm128_ref, gm64_ref) = refs[:10]
    prev = refs[10:10 + n_prev]
    outs = refs[10 + n_prev:]
    x = x_ref[...]
    tile = x.shape[0]
    ms = jnp.mean(x * x, axis=-1, keepdims=True)
    hb = ((x * lax.rsqrt(ms + NORM_EPS)) * ng_ref[...]).astype(BF16)

    def proj(col):
        return _dot(hb, w_ref[:, col:col + WIDTH])

    def group_rms_scale(z, gm_ref):
        sq = (z * z).astype(BF16)
        gw = gm_ref.shape[0]
        mean_sq = jnp.concatenate(
            [_dot(sq[:, c:c + gw], gm_ref[...]) for c in range(0, WIDTH, gw)], axis=1)
        return lax.rsqrt(mean_sq + NORM_EPS)

    zq = proj(COL_Q)
    qn = ((zq * group_rms_scale(zq, gm64_ref)) * gq_ref[...] * (QK_SCALE * LOG2E)).astype(BF16)
    zk = proj(COL_K)
    kn = (zk * group_rms_scale(zk, gm64_ref)) * gk_ref[...]
    zv = proj(COL_V)
    gb = jax.nn.silu(proj(COL_GB)).astype(BF16)

    if prompt:
        ya_ref, q_ref, key_ref, kt_ref, v_ref, vt_ref, gb_ref = outs
        last = kt_ref.shape[0] - 1
        if n_prev:
            kt_prev_ref, v_prev_ref = prev
            kt_ref[:last] = kt_prev_ref[...]
            v_ref[:last] = v_prev_ref[...]
        pos = (lax.broadcasted_iota(jnp.int32, (tile, HEAD_DIM), 0)
               + pl.program_id(1) * tile)
        lane = lax.broadcasted_iota(jnp.int32, (tile, HEAD_DIM), 1)
        lo = jnp.bitwise_and(pos, CHUNK - 1)
        pos_cols = jnp.where(lane < POS_SPLIT, (pos - lo).astype(F32),
                             jnp.where(lane < 2 * POS_SPLIT, lo.astype(F32), 0.0)).astype(BF16)
        ones_rows = jnp.ones((VT_ROWS - HEAD_DIM, tile), BF16)
        kt_ref[last] = kn.T
        for h in range(N_HEADS):
            cols = _head_cols(h)
            q_ref[h] = qn[:, cols]
            gb_ref[h] = gb[:, cols]
            key_ref[h, :, :HEAD_DIM] = kn[:, cols].astype(BF16)
            key_ref[h, :, HEAD_DIM:] = pos_cols
            vt_ref[h, :HEAD_DIM, :] = zv[:, cols].T.astype(BF16)
            vt_ref[h, HEAD_DIM:, :] = ones_rows
            v_ref[last, pl.ds(h, tile, stride=N_HEADS), :] = zv[:, cols]
    else:
        ya_ref, q_ref, k_ref, v_ref, gb_ref, va_ref = outs
        q_ref[...] = qn
        k_ref[...] = kn
        v_ref[...] = zv
        gb_ref[...] = gb

    va = proj(COL_VA)
    if not prompt:
        va_ref[...] = va
    vn = ((va * group_rms_scale(va, gm128_ref)) * sgug_ref[...]).astype(BF16)
    gate = proj(COL_U) * jax.nn.silu(proj(COL_GA))
    for c in range(tile // SGU_CHUNK):
        rows = slice(c * SGU_CHUNK, (c + 1) * SGU_CHUNK)
        for g in range(N_HEADS):
            cols = _head_cols(g)
            mixed = _dot(sguw_ref[g], vn[rows, cols]) + sgub_ref[g]
            ya_ref[rows, cols] = (gate[rows, cols] * mixed).astype(BF16)


def _inproj_prompt(x, params, k_stack, v_stack, *, batch, seq):
    tile = TOKEN_TILE
    n_t = seq // tile
    n = batch * seq
    n_layers = 1 if k_stack is None else k_stack.shape[0] + 1
    full = lambda a: pl.BlockSpec(a.shape, lambda b, i: (0,) * a.ndim)
    head_major = lambda width: pl.BlockSpec((None, N_HEADS, tile, width),
                                            lambda b, i: (b, 0, i, 0))
    kt_spec = lambda layers: pl.BlockSpec((layers, None, WIDTH, tile),
                                          lambda b, i: (0, b, 0, i))
    v_spec = lambda layers: pl.BlockSpec((layers, tile * N_HEADS, HEAD_DIM),
                                         lambda b, i: (0, b * n_t + i, 0))
    in_specs = [pl.BlockSpec((tile, D_MODEL), lambda b, i: (b * n_t + i, 0))]
    in_specs += [full(a) for a in params]
    operands = [x, *params]
    if n_layers > 1:
        in_specs += [kt_spec(n_layers - 1), v_spec(n_layers - 1)]
        operands += [k_stack, v_stack]
    out_shape = [
        jax.ShapeDtypeStruct((n, WIDTH), BF16),
        jax.ShapeDtypeStruct((batch, N_HEADS, seq, HEAD_DIM), BF16),
        jax.ShapeDtypeStruct((batch, N_HEADS, n_t, tile, KEY_WIDTH), BF16),
        jax.ShapeDtypeStruct((n_layers, batch, WIDTH, seq), F32),
        jax.ShapeDtypeStruct((n_layers, n * N_HEADS, HEAD_DIM), F32),
        jax.ShapeDtypeStruct((batch, N_HEADS, n_t, VT_ROWS, tile), BF16),
        jax.ShapeDtypeStruct((batch, N_HEADS, seq, HEAD_DIM), BF16),
    ]
    out_specs = [
        pl.BlockSpec((tile, WIDTH), lambda b, i: (b * n_t + i, 0)),
        head_major(HEAD_DIM),
        pl.BlockSpec((None, N_HEADS, None, tile, KEY_WIDTH), lambda b, i: (b, 0, i, 0, 0)),
        kt_spec(n_layers),
        v_spec(n_layers),
        pl.BlockSpec((None, N_HEADS, None, VT_ROWS, tile), lambda b, i: (b, 0, i, 0, 0)),
        head_major(HEAD_DIM),
    ]
    return pl.pallas_call(
        functools.partial(_inproj_kernel, prompt=True, n_prev=len(operands) - 1 - len(params)),
        grid=(batch, n_t),
        in_specs=in_specs,
        out_specs=out_specs,
        out_shape=out_shape,
        compiler_params=_compiler_params(2),
        name="inproj_prompt",
    )(*operands)


def _inproj_sample(x, params, *, tile):
    n = x.shape[0]
    row_spec = lambda width: pl.BlockSpec((tile, width), lambda i: (i, 0))
    full = lambda a: pl.BlockSpec(a.shape, lambda i: (0,) * a.ndim)
    dtypes = (BF16, BF16, F32, F32, BF16, F32)
    return pl.pallas_call(
        functools.partial(_inproj_kernel, prompt=False, n_prev=0),
        grid=(n // tile,),
        in_specs=[row_spec(D_MODEL)] + [full(a) for a in params],
        out_specs=[row_spec(WIDTH)] * len(dtypes),
        out_shape=[jax.ShapeDtypeStruct((n, WIDTH), d) for d in dtypes],
        compiler_params=_compiler_params(1),
        name="inproj_sample",
    )(x, *params)


def _outproj_kernel(x_ref, ya_ref, yb_ref, w_ref, o_ref, *, head_major):
    if head_major:
        yb = jnp.concatenate([yb_ref[h] for h in range(N_HEADS)], axis=1)
    else:
        yb = yb_ref[...]
    y = _dot(ya_ref[...], w_ref[:WIDTH, :]) + _dot(yb, w_ref[WIDTH:, :])
    o_ref[...] = x_ref[...] + y


def _outproj(x, ya, yb, w_bf, *, tile):
    n = x.shape[0]
    head_major = yb.ndim == 4
    if head_major:
        n_t = yb.shape[2] // tile
        yb_spec = pl.BlockSpec((None, N_HEADS, tile, HEAD_DIM),
                               lambda i: (i // n_t, 0, i % n_t, 0))
    else:
        yb_spec = pl.BlockSpec((tile, WIDTH), lambda i: (i, 0))
    return pl.pallas_call(
        functools.partial(_outproj_kernel, head_major=head_major),
        grid=(n // tile,),
        in_specs=[
            pl.BlockSpec((tile, D_MODEL), lambda i: (i, 0)),
            pl.BlockSpec((tile, WIDTH), lambda i: (i, 0)),
            yb_spec,
            pl.BlockSpec(w_bf.shape, lambda i: (0, 0)),
        ],
        out_specs=pl.BlockSpec((tile, D_MODEL), lambda i: (i, 0)),
        out_shape=jax.ShapeDtypeStruct((n, D_MODEL), F32),
        compiler_params=_compiler_params(1),
        name="outproj",
    )(x, ya, yb, w_bf)


SCORES_AHEAD = 2


def _attn_units(seq):
    n_qc = seq // LANE_CHUNK
    qc_per_tile = ATTN_TILE // LANE_CHUNK
    return [(j, qc, half)
            for j in range(seq // ATTN_TILE)
            for qc in range(j * qc_per_tile, n_qc)
            for half in range(2)]


def _attn_kernel(slopes_ref, q_ref, key_ref, vt_ref, gb_ref, sg_ref, lq1_ref, lk1_ref,
                 lq2_ref, lk2_ref, o_ref, qbd_ref, m_ref, acc_ref, *, lam_init):
    seq = q_ref.shape[0]
    t, w = ATTN_TILE, LANE_CHUNK
    assert t == 2 * w
    n_qc = seq // w
    slope2 = slopes_ref[pl.program_id(1)] * LOG2E

    key = lax.broadcasted_iota(jnp.int32, (w, w), 0)
    qry = lax.broadcasted_iota(jnp.int32, (w, w), 1)
    allowed = jnp.right_shift(key, 6) <= jnp.right_shift(qry, 6)
    ahead = jnp.maximum(key - qry, 0).astype(F32)
    dt = jnp.where(allowed, (-2.0 * slope2) * ahead, NEG_INF)

    sub_tile = 16
    lane = lax.broadcasted_iota(jnp.int32, (sub_tile, HEAD_DIM), 1)
    rest = jnp.full((sub_tile, HEAD_DIM), slope2, F32)
    slope_cols = jnp.zeros((sub_tile, HEAD_DIM), F32)
    for piece in range(POS_SPLIT):
        part = rest.astype(BF16).astype(F32)
        slope_cols = jnp.where((lane == piece) | (lane == piece + POS_SPLIT), part, slope_cols)
        rest = rest - part
    slope_cols = jnp.tile(slope_cols.astype(BF16), (seq // sub_tile, 1))

    q = q_ref[...]
    lane = lax.broadcasted_iota(jnp.int32, q.shape, 1)
    zero = jnp.zeros_like(q)
    qbd_ref[:seq, :HEAD_DIM] = jnp.where(lane < HALF_DIM, q, zero)
    qbd_ref[seq:, :HEAD_DIM] = jnp.where(lane >= HALF_DIM, q, zero)
    qbd_ref[:seq, HEAD_DIM:] = slope_cols
    qbd_ref[seq:, HEAD_DIM:] = slope_cols

    lam = _lam_value(lq1_ref, lk1_ref, lq2_ref, lk2_ref, lam_init)

    def n_keys(j, qc):
        return w if qc * w == j * t else t

    def scores(unit):
        j, qc, half = unit
        c = half * n_qc + qc
        return _dot_nt(key_ref[j, :n_keys(j, qc), :], qbd_ref[c * w:(c + 1) * w, :])

    def softmax(unit, st):
        j, qc, half = unit
        lanes = slice((half * n_qc + qc) * w, (half * n_qc + qc + 1) * w)
        nk = n_keys(j, qc)
        if qc * w < (j + 1) * t:
            st = st + dt if nk == w else jnp.concatenate([st[:w], st[w:] + dt], axis=0)
        m_cur = jnp.max(st, axis=0, keepdims=True)
        if j == 0:
            m_new, alpha = m_cur, None
        else:
            m_old = m_ref[:, lanes]
            m_new = jnp.maximum(m_old, m_cur)
            alpha = jnp.exp2(m_old - m_new)
        m_ref[:, lanes] = m_new
        return jnp.exp2(st - m_new).astype(BF16), alpha

    def values(unit, p, alpha):
        j, qc, half = unit
        lanes = slice((half * n_qc + qc) * w, (half * n_qc + qc + 1) * w)
        pv = _dot(vt_ref[j, :, :n_keys(j, qc)], p)
        acc_ref[:, lanes] = pv if j == 0 else alpha * acc_ref[:, lanes] + pv

    def normalized(lanes):
        return acc_ref[:HEAD_DIM, lanes] * (1.0 / acc_ref[HEAD_DIM:HEAD_DIM + 1, lanes])

    def finalize(qc):
        rows = slice(qc * w, (qc + 1) * w)
        lanes1 = slice(qc * w, (qc + 1) * w)
        lanes2 = slice((n_qc + qc) * w, (n_qc + qc + 1) * w)
        ot = normalized(lanes1) - lam * normalized(lanes2)
        ot = ot * lax.rsqrt(jnp.mean(ot * ot, axis=0, keepdims=True) + NORM_EPS)
        o = (ot.T * sg_ref[...]) * (1.0 - lam_init)
        o_ref[rows, :] = (o * gb_ref[rows, :].astype(F32)).astype(BF16)

    units = _attn_units(seq)
    pending_scores = {u: scores(units[u]) for u in range(min(SCORES_AHEAD, len(units)))}
    pending_values = None
    for u, unit in enumerate(units):
        if u + SCORES_AHEAD < len(units):
            pending_scores[u + SCORES_AHEAD] = scores(units[u + SCORES_AHEAD])
        p, alpha = softmax(unit, pending_scores.pop(u))
        if pending_values is not None:
            values(*pending_values)
            j_done, qc_done, half_done = pending_values[0]
            if half_done == 1 and qc_done * w < (j_done + 1) * t:
                finalize(qc_done)
        pending_values = (unit, p, alpha)
    values(*pending_values)
    finalize(pending_values[0][1])


def _prompt_attention(q, keys, vt, gb, slopes, sg, lq1, lk1, lq2, lk2, *, lam_init):
    batch, _, seq, _ = q.shape
    t = ATTN_TILE
    n_tiles = seq // t
    seq_spec = pl.BlockSpec((None, None, seq, HEAD_DIM), lambda b, h: (b, h, 0, 0))
    vec = lambda a: pl.BlockSpec(a.shape, lambda b, h: (0, 0))
    return pl.pallas_call(
        functools.partial(_attn_kernel, lam_init=lam_init),
        grid=(batch, N_HEADS),
        in_specs=[
            pl.BlockSpec(memory_space=pltpu.SMEM),
            seq_spec,
            pl.BlockSpec((None, None, n_tiles, t, KEY_WIDTH), lambda b, h: (b, h, 0, 0, 0)),
            pl.BlockSpec((None, None, n_tiles, VT_ROWS, t), lambda b, h: (b, h, 0, 0, 0)),
            seq_spec, vec(sg), vec(lq1), vec(lk1), vec(lq2), vec(lk2)],
        out_specs=seq_spec,
        out_shape=jax.ShapeDtypeStruct((batch, N_HEADS, seq, HEAD_DIM), BF16),
        scratch_shapes=[
            pltpu.VMEM((2 * seq, KEY_WIDTH), BF16),
            pltpu.VMEM((1, 2 * seq), F32),
            pltpu.VMEM((VT_ROWS, 2 * seq), F32),
        ],
        compiler_params=_compiler_params(2),
        name="prompt_attn",
    )(slopes, q, keys, vt, gb, sg, lq1, lk1, lq2, lk2)


def _sample_attn_kernel(slopes_ref, q_ref, kn_ref, vn_ref, ckt_ref, cv_ref, gb_ref, sg_ref,
                        lq1_ref, lk1_ref, lq2_ref, lk2_ref, o_ref, *, lam_init, past_len):
    nq = q_ref.shape[0]
    lam = _lam_value(lq1_ref, lk1_ref, lq2_ref, lk2_ref, lam_init)
    qrow = lax.broadcasted_iota(jnp.int32, (2 * nq, past_len), 0)
    qpos_past = past_len + jnp.where(qrow >= nq, qrow - nq, qrow)
    dist_past = (qpos_past - lax.broadcasted_iota(jnp.int32, (2 * nq, past_len), 1)).astype(F32)
    pad = HEAD_DIM - nq
    qrow_n = lax.broadcasted_iota(jnp.int32, (2 * nq, HEAD_DIM), 0)
    qidx_n = jnp.where(qrow_n >= nq, qrow_n - nq, qrow_n)
    kidx_n = lax.broadcasted_iota(jnp.int32, (2 * nq, HEAD_DIM), 1)
    dist_new = jnp.abs(qidx_n - kidx_n).astype(F32)
    real_new = kidx_n < nq
    zpad = jnp.zeros((pad, HEAD_DIM), BF16)
    for h in range(N_HEADS):
        cols = _head_cols(h)
        slope = slopes_ref[h] * LOG2E
        q = q_ref[:, cols]
        lane = lax.broadcasted_iota(jnp.int32, q.shape, 1)
        zero = jnp.zeros_like(q)
        qbd = jnp.concatenate([jnp.where(lane < HALF_DIM, q, zero),
                               jnp.where(lane >= HALF_DIM, q, zero)], axis=0)
        s_past = _dot(qbd, ckt_ref[cols, :].astype(BF16)) - slope * dist_past
        kn = jnp.concatenate([kn_ref[:, cols].astype(BF16), zpad], axis=0)
        vn = jnp.concatenate([vn_ref[:, cols].astype(BF16), zpad], axis=0)
        s_new = _dot_nt(qbd, kn) - slope * dist_new
        s_new = jnp.where(real_new, s_new, NEG_INF)
        m = jnp.maximum(jnp.max(s_past, axis=-1, keepdims=True),
                        jnp.max(s_new, axis=-1, keepdims=True))
        p_past = jnp.exp2(s_past - m)
        p_new = jnp.exp2(s_new - m)
        l = jnp.sum(p_past, axis=-1, keepdims=True) + jnp.sum(p_new, axis=-1, keepdims=True)
        v_past = cv_ref[pl.ds(h, past_len, stride=N_HEADS), :].astype(BF16)
        acc = _dot(p_past.astype(BF16), v_past) + _dot(p_new.astype(BF16), vn)
        acc = acc * (1.0 / l)
        o = acc[:nq] - lam * acc[nq:]
        o = o * lax.rsqrt(jnp.mean(o * o, axis=-1, keepdims=True) + NORM_EPS)
        o = (o * sg_ref[...]) * (1.0 - lam_init)
        o_ref[:, cols] = (o * gb_ref[:, cols].astype(F32)).astype(BF16)


def _sample_attention(q, k_new, v_new, cache_kt, cache_v, gb, slopes, sg, lq1, lk1, lq2, lk2,
                      *, layer, n_streams, n_new, lam_init):
    past_len = cache_kt.shape[3]
    new_spec = pl.BlockSpec((None, n_new, WIDTH), lambda b: (b, 0, 0))
    vec = lambda a: pl.BlockSpec(a.shape, lambda b: (0, 0))
    r3 = lambda a: a.reshape(n_streams, n_new, WIDTH)
    out = pl.pallas_call(
        functools.partial(_sample_attn_kernel, lam_init=lam_init, past_len=past_len),
        grid=(n_streams,),
        in_specs=[
            pl.BlockSpec(memory_space=pltpu.SMEM), new_spec, new_spec, new_spec,
            pl.BlockSpec((None, None, WIDTH, past_len), lambda b: (layer, b, 0, 0)),
            pl.BlockSpec((None, None, past_len * N_HEADS, HEAD_DIM), lambda b: (layer, b, 0, 0)),
            new_spec, vec(sg), vec(lq1), vec(lk1), vec(lq2), vec(lk2)],
        out_specs=new_spec,
        out_shape=jax.ShapeDtypeStruct((n_streams, n_new, WIDTH), BF16),
        compiler_params=_compiler_params(1),
        name="sample_attn",
    )(slopes, r3(q), r3(k_new), r3(v_new), cache_kt, cache_v, r3(gb), sg, lq1, lk1, lq2, lk2)
    return out.reshape(n_streams * n_new, WIDTH)


def _group_mean_matrix(group):
    idx = jnp.arange(GROUP_MEAN_WIDTH) // group
    return jnp.where(idx[:, None] == idx[None, :], 1.0 / group, 0.0).astype(BF16)


def kernel(x_prompt, x_sample, cache_k, cache_v, norm_g, w_in, sgu_norm_g, sgu_w, sgu_b,
           q_norm_g, k_norm_g, lambda_q1, lambda_k1, lambda_q2, lambda_k2, subln_g, w_out):
    depth = w_in.shape[0]
    batch, seq, _ = x_prompt.shape
    n_streams, n_new, _ = x_sample.shape
    past_len = cache_k.shape[2]
    assert seq % ATTN_TILE == 0
    assert SGU_CHUNK % n_new == 0 and past_len % CHUNK == 0 and n_new <= CHUNK
    sample_tile = min(n_streams * n_new, TOKEN_TILE)
    assert sample_tile % SGU_CHUNK == 0 and (n_streams * n_new) % sample_tile == 0

    gm128 = _group_mean_matrix(HEAD_DIM)
    gm64 = _group_mean_matrix(HALF_DIM)
    slopes = jnp.asarray(ALIBI_SLOPES, F32)
    tril = jnp.tril(jnp.ones((SGU_CHUNK, SGU_CHUNK), F32))
    tril_new = jnp.tril(jnp.ones((n_new, n_new), F32))
    streams_per_chunk = SGU_CHUNK // n_new
    eye = jnp.eye(streams_per_chunk, dtype=F32)
    cache_kt = jnp.transpose(cache_k, (0, 1, 3, 4, 5, 2)).reshape(depth, n_streams, WIDTH, past_len)
    cache_vr = cache_v.reshape(depth, n_streams, past_len * N_HEADS, HEAD_DIM)

    xp = x_prompt.reshape(batch * seq, D_MODEL)
    xs = x_sample.reshape(n_streams * n_new, D_MODEL)
    k_stack = v_stack = None
    ks_rows, vs_rows, sgu_rows = [], [], []
    for i in range(depth):
        lam_init = _lam_init(i)
        w_in_bf = w_in[i].astype(BF16)
        w_out_bf = w_out[i].astype(BF16)
        row = lambda a: a.reshape(1, -1).astype(F32)
        ng = row(norm_g[i])
        sgug = row(sgu_norm_g[i])
        gq = row(jnp.tile(q_norm_g[i], WIDTH // HALF_DIM))
        gk = row(jnp.tile(k_norm_g[i], WIDTH // HALF_DIM))
        sg = row(subln_g[i])
        lams = tuple(row(a[i]) for a in (lambda_q1, lambda_k1, lambda_q2, lambda_k2))
        sguw_p = (sgu_w[i] * tril).astype(BF16)
        sgub_p = jnp.broadcast_to(sgu_b[i][:, :, None], (N_HEADS, SGU_CHUNK, HEAD_DIM))
        w_new = sgu_w[i][:, :n_new, :n_new] * tril_new
        sguw_s = jnp.einsum("ab,hts->hatbs", eye, w_new).reshape(
            N_HEADS, SGU_CHUNK, SGU_CHUNK).astype(BF16)
        sgub_s = jnp.broadcast_to(
            jnp.tile(sgu_b[i][:, :n_new], (1, streams_per_chunk))[:, :, None],
            (N_HEADS, SGU_CHUNK, HEAD_DIM))
        params_p = (ng, w_in_bf, sgug, sguw_p, sgub_p, gq, gk, gm128, gm64)
        params_s = (ng, w_in_bf, sgug, sguw_s, sgub_s, gq, gk, gm128, gm64)

        ya, q, keys, k_stack, v_stack, vt, gb = _inproj_prompt(
            xp, params_p, k_stack, v_stack, batch=batch, seq=seq)
        yb = _prompt_attention(q, keys, vt, gb, slopes, sg, *lams, lam_init=lam_init)
        xp = _outproj(xp, ya, yb, w_out_bf, tile=TOKEN_TILE)

        ya, q, k, v, gb, va = _inproj_sample(xs, params_s, tile=sample_tile)
        yb = _sample_attention(q, k, v, cache_kt, cache_vr, gb, slopes, sg, *lams, layer=i,
                               n_streams=n_streams, n_new=n_new, lam_init=lam_init)
        xs = _outproj(xs, ya, yb, w_out_bf, tile=sample_tile)
        ks_rows.append(k)
        vs_rows.append(v)
        sgu_rows.append(va)

    new_k_prompt = jnp.transpose(
        k_stack.reshape(depth, batch, N_HEADS, 2, HALF_DIM, seq), (0, 1, 5, 2, 3, 4))
    return (
        xp.reshape(batch, seq, D_MODEL),
        xs.reshape(n_streams, n_new, D_MODEL),
        new_k_prompt,
        v_stack.reshape(depth, batch, seq, N_HEADS, HEAD_DIM),
        jnp.stack(ks_rows).reshape(depth, n_streams, n_new, N_HEADS, 2, HALF_DIM),
        jnp.stack(vs_rows).reshape(depth, n_streams, n_new, N_HEADS, HEAD_DIM),
        jnp.stack(sgu_rows).reshape(depth, n_streams, n_new, WIDTH),
    )
```

```python
import functools
import math

import jax
import jax.numpy as jnp
from jax import lax
from jax.experimental import pallas as pl
from jax.experimental.pallas import tpu as pltpu

F32 = jnp.float32
BF16 = jnp.bfloat16

D_MODEL = 1024
N_HEADS = 4
HEAD_DIM = 128
HALF_DIM = 64
WIDTH = N_HEADS * HEAD_DIM
CHUNK = 64
SGU_CHUNK = 128
NORM_EPS = 1e-6
NEG_INF = -1e30
QK_SCALE = HALF_DIM ** -0.5
LOG2E = math.log2(math.e)
ALIBI_SLOPES = tuple(2.0 ** (-8.0 * (h + 1) / N_HEADS) for h in range(N_HEADS))

COL_U, COL_VA, COL_GA, COL_Q, COL_K, COL_V, COL_GB = (i * WIDTH for i in range(7))

VMEM_LIMIT_BYTES = 56 * 1024 * 1024

ATTN_TILE = 512
TOKEN_TILE = ATTN_TILE
PROMPT_SUB_TILES = 1
LANE_CHUNK = 256
KEY_WIDTH = 2 * HEAD_DIM
GROUP_MEAN_WIDTH = 256
POS_SPLIT = 3
VT_ROWS = HEAD_DIM + 16


def _lam_init(layer_idx):
    return 0.8 - 0.6 * math.exp(-0.3 * layer_idx)


def _compiler_params(n_axes):
    return pltpu.CompilerParams(
        dimension_semantics=("arbitrary",) * n_axes,
        vmem_limit_bytes=VMEM_LIMIT_BYTES,
    )


def _dot(a, b):
    return jnp.dot(a, b, preferred_element_type=F32)


def _dot_nt(a, b):
    return lax.dot_general(a, b, (((1,), (1,)), ((), ())), preferred_element_type=F32)


def _lam_value(lq1_ref, lk1_ref, lq2_ref, lk2_ref, lam_init):
    d1 = jnp.sum(lq1_ref[...] * lk1_ref[...], axis=-1, keepdims=True)
    d2 = jnp.sum(lq2_ref[...] * lk2_ref[...], axis=-1, keepdims=True)
    return jnp.exp(d1) - jnp.exp(d2) + lam_init


def _head_cols(h):
    return slice(h * HEAD_DIM, (h + 1) * HEAD_DIM)


def _inproj_kernel(*refs, prompt, n_prev, fused, n_sub):
    (x_ref, ng_ref, w_ref, sgug_ref, sguw_ref, sgub_ref, gq_ref, gk_ref,
     gm128_ref, gm64_ref) = refs[:10]
    n_in = 10
    if fused:
        ya_prev_ref, yb_prev_ref, wo_ref = refs[n_in:n_in + 3]
        n_in += 3
    prev = refs[n_in:n_in + n_prev]
    outs = refs[n_in + n_prev:]
    if fused:
        xo_ref, outs = outs[0], outs[1:]
    tile = x_ref.shape[0]
    sub = tile // n_sub

    def group_rms_scale(z, gm_ref):
        sq = (z * z).astype(BF16)
        gw = gm_ref.shape[0]
        mean_sq = jnp.concatenate(
            [_dot(sq[:, c:c + gw], gm_ref[...]) for c in range(0, WIDTH, gw)], axis=1)
        return lax.rsqrt(mean_sq + NORM_EPS)

    if prompt:
        ya_ref, q_ref, key_ref, kt_ref, v_ref, vt_ref, gb_ref = outs
        last = kt_ref.shape[0] - 1
        if n_prev:
            kt_prev_ref, v_prev_ref = prev
            kt_ref[:last] = kt_prev_ref[...]
            v_ref[:last] = v_prev_ref[...]
        for h in range(N_HEADS):
            vt_ref[h, HEAD_DIM:, :] = jnp.ones((VT_ROWS - HEAD_DIM, tile), BF16)
    else:
        ya_ref, q_ref, k_ref, v_ref, gb_ref, va_ref = outs

    for s in range(n_sub):
        rows = slice(s * sub, (s + 1) * sub)
        x = x_ref[rows, :]
        if fused:
            yb_prev = jnp.concatenate([yb_prev_ref[h, rows, :] for h in range(N_HEADS)], axis=1)
            x = x + (_dot(ya_prev_ref[rows, :], wo_ref[:WIDTH, :])
                     + _dot(yb_prev, wo_ref[WIDTH:, :]))
            xo_ref[rows, :] = x
        ms = jnp.mean(x * x, axis=-1, keepdims=True)
        hb = ((x * lax.rsqrt(ms + NORM_EPS)) * ng_ref[...]).astype(BF16)

        def proj(col):
            return _dot(hb, w_ref[:, col:col + WIDTH])

        va = proj(COL_VA)
        if not prompt:
            va_ref[rows, :] = va
        vn = ((va * group_rms_scale(va, gm128_ref)) * sgug_ref[...]).astype(BF16)
        gate = proj(COL_U) * jax.nn.silu(proj(COL_GA))
        for c in range(sub // SGU_CHUNK):
            crows = slice(c * SGU_CHUNK, (c + 1) * SGU_CHUNK)
            orows = slice(s * sub + c * SGU_CHUNK, s * sub + (c + 1) * SGU_CHUNK)
            for g in range(N_HEADS):
                cols = _head_cols(g)
                mixed = _dot(sguw_ref[g], vn[crows, cols]) + sgub_ref[g]
                ya_ref[orows, cols] = (gate[crows, cols] * mixed).astype(BF16)

        zq = proj(COL_Q)
        qn = ((zq * group_rms_scale(zq, gm64_ref)) * gq_ref[...]
              * (QK_SCALE * LOG2E)).astype(BF16)
        zk = proj(COL_K)
        kn = (zk * group_rms_scale(zk, gm64_ref)) * gk_ref[...]
        zv = proj(COL_V)
        gb = jax.nn.silu(proj(COL_GB)).astype(BF16)

        if prompt:
            pos = (lax.broadcasted_iota(jnp.int32, (sub, HEAD_DIM), 0)
                   + (pl.program_id(1) * tile + s * sub))
            lane = lax.broadcasted_iota(jnp.int32, (sub, HEAD_DIM), 1)
            lo = jnp.bitwise_and(pos, CHUNK - 1)
            pos_cols = jnp.where(
                lane < POS_SPLIT, (pos - lo).astype(F32),
                jnp.where(lane < 2 * POS_SPLIT, lo.astype(F32), 0.0)).astype(BF16)
            kt_ref[last, :, rows] = kn.T
            for h in range(N_HEADS):
                cols = _head_cols(h)
                q_ref[h, rows, :] = qn[:, cols]
                gb_ref[h, rows, :] = gb[:, cols]
                key_ref[h, rows, :HEAD_DIM] = kn[:, cols].astype(BF16)
                key_ref[h, rows, HEAD_DIM:] = pos_cols
                vt_ref[h, :HEAD_DIM, rows] = zv[:, cols].T.astype(BF16)
                v_ref[last, pl.ds(s * sub * N_HEADS + h, sub, stride=N_HEADS), :] = zv[:, cols]
        else:
            q_ref[rows, :] = qn
            k_ref[rows, :] = kn
            v_ref[rows, :] = zv
            gb_ref[rows, :] = gb


def _inproj_prompt(x, params, prev_out, k_stack, v_stack, *, batch, seq):
    tile = TOKEN_TILE
    n_t = seq // tile
    n = batch * seq
    n_layers = 1 if k_stack is None else k_stack.shape[0] + 1
    full = lambda a: pl.BlockSpec(a.shape, lambda b, i: (0,) * a.ndim,
                                  pipeline_mode=pl.Buffered(1))
    head_major = lambda width: pl.BlockSpec((None, N_HEADS, tile, width),
                                            lambda b, i: (b, 0, i, 0))
    kt_spec = lambda layers: pl.BlockSpec((layers, None, WIDTH, tile),
                                          lambda b, i: (0, b, 0, i))
    v_spec = lambda layers: pl.BlockSpec((layers, tile * N_HEADS, HEAD_DIM),
                                         lambda b, i: (0, b * n_t + i, 0))
    x_spec = pl.BlockSpec((tile, D_MODEL), lambda b, i: (b * n_t + i, 0))
    ya_spec = pl.BlockSpec((tile, WIDTH), lambda b, i: (b * n_t + i, 0))
    in_specs = [x_spec] + [full(a) for a in params]
    operands = [x, *params]
    fused = prev_out is not None
    if fused:
        ya_prev, yb_prev, w_out_prev = prev_out
        in_specs += [ya_spec, head_major(HEAD_DIM), full(w_out_prev)]
        operands += [ya_prev, yb_prev, w_out_prev]
    if n_layers > 1:
        in_specs += [kt_spec(n_layers - 1), v_spec(n_layers - 1)]
        operands += [k_stack, v_stack]
    out_shape = [
        jax.ShapeDtypeStruct((n, WIDTH), BF16),
        jax.ShapeDtypeStruct((batch, N_HEADS, seq, HEAD_DIM), BF16),
        jax.ShapeDtypeStruct((batch, N_HEADS, n_t, tile, KEY_WIDTH), BF16),
        jax.ShapeDtypeStruct((n_layers, batch, WIDTH, seq), F32),
        jax.ShapeDtypeStruct((n_layers, n * N_HEADS, HEAD_DIM), F32),
        jax.ShapeDtypeStruct((batch, N_HEADS, n_t, VT_ROWS, tile), BF16),
        jax.ShapeDtypeStruct((batch, N_HEADS, seq, HEAD_DIM), BF16),
    ]
    out_specs = [
        ya_spec,
        head_major(HEAD_DIM),
        pl.BlockSpec((None, N_HEADS, None, tile, KEY_WIDTH), lambda b, i: (b, 0, i, 0, 0)),
        kt_spec(n_layers),
        v_spec(n_layers),
        pl.BlockSpec((None, N_HEADS, None, VT_ROWS, tile), lambda b, i: (b, 0, i, 0, 0)),
        head_major(HEAD_DIM),
    ]
    if fused:
        out_shape.insert(0, jax.ShapeDtypeStruct((n, D_MODEL), F32))
        out_specs.insert(0, x_spec)
    outs = pl.pallas_call(
        functools.partial(_inproj_kernel, prompt=True, n_prev=2 * (n_layers > 1), fused=fused,
                          n_sub=PROMPT_SUB_TILES),
        grid=(batch, n_t),
        in_specs=in_specs,
        out_specs=out_specs,
        out_shape=out_shape,
        compiler_params=_compiler_params(2),
        name="inproj_prompt",
    )(*operands)
    return outs if fused else (x, *outs)


def _inproj_sample(x, params, *, tile):
    n = x.shape[0]
    row_spec = lambda width: pl.BlockSpec((tile, width), lambda i: (i, 0))
    full = lambda a: pl.BlockSpec(a.shape, lambda i: (0,) * a.ndim)
    dtypes = (BF16, BF16, F32, F32, BF16, F32)
    return pl.pallas_call(
        functools.partial(_inproj_kernel, prompt=False, n_prev=0, fused=False, n_sub=1),
        grid=(n // tile,),
        in_specs=[row_spec(D_MODEL)] + [full(a) for a in params],
        out_specs=[row_spec(WIDTH)] * len(dtypes),
        out_shape=[jax.ShapeDtypeStruct((n, WIDTH), d) for d in dtypes],
        compiler_params=_compiler_params(1),
        name="inproj_sample",
    )(x, *params)


def _outproj_kernel(x_ref, ya_ref, yb_ref, w_ref, o_ref, *, head_major):
    if head_major:
        yb = jnp.concatenate([yb_ref[h] for h in range(N_HEADS)], axis=1)
    else:
        yb = yb_ref[...]
    y = _dot(ya_ref[...], w_ref[:WIDTH, :]) + _dot(yb, w_ref[WIDTH:, :])
    o_ref[...] = x_ref[...] + y


def _outproj(x, ya, yb, w_bf, *, tile):
    n = x.shape[0]
    head_major = yb.ndim == 4
    if head_major:
        n_t = yb.shape[2] // tile
        yb_spec = pl.BlockSpec((None, N_HEADS, tile, HEAD_DIM),
                               lambda i: (i // n_t, 0, i % n_t, 0))
    else:
        yb_spec = pl.BlockSpec((tile, WIDTH), lambda i: (i, 0))
    return pl.pallas_call(
        functools.partial(_outproj_kernel, head_major=head_major),
        grid=(n // tile,),
        in_specs=[
            pl.BlockSpec((tile, D_MODEL), lambda i: (i, 0)),
            pl.BlockSpec((tile, WIDTH), lambda i: (i, 0)),
            yb_spec,
            pl.BlockSpec(w_bf.shape, lambda i: (0, 0)),
        ],
        out_specs=pl.BlockSpec((tile, D_MODEL), lambda i: (i, 0)),
        out_shape=jax.ShapeDtypeStruct((n, D_MODEL), F32),
        compiler_params=_compiler_params(1),
        name="outproj",
    )(x, ya, yb, w_bf)


SCORES_AHEAD = 2


def _attn_units(seq):
    n_qc = seq // LANE_CHUNK
    qc_per_tile = ATTN_TILE // LANE_CHUNK
    return [(j, qc, half)
            for j in range(seq // ATTN_TILE)
            for qc in range(j * qc_per_tile, n_qc)
            for half in range(2)]


def _attn_kernel(slopes_ref, q_ref, key_ref, vt_ref, gb_ref, sg_ref, lq1_ref, lk1_ref,
                 lq2_ref, lk2_ref, o_ref, qbd_ref, m_ref, acc_ref, *, lam_init):
    seq = q_ref.shape[0]
    t, w = ATTN_TILE, LANE_CHUNK
    assert t == 2 * w
    n_qc = seq // w
    slope2 = slopes_ref[pl.program_id(1)] * LOG2E

    key = lax.broadcasted_iota(jnp.int32, (w, w), 0)
    qry = lax.broadcasted_iota(jnp.int32, (w, w), 1)
    allowed = jnp.right_shift(key, 6) <= jnp.right_shift(qry, 6)
    ahead = jnp.maximum(key - qry, 0).astype(F32)
    dt = jnp.where(allowed, (-2.0 * slope2) * ahead, NEG_INF)

    sub_tile = 16
    lane = lax.broadcasted_iota(jnp.int32, (sub_tile, HEAD_DIM), 1)
    rest = jnp.full((sub_tile, HEAD_DIM), slope2, F32)
    slope_cols = jnp.zeros((sub_tile, HEAD_DIM), F32)
    for piece in range(POS_SPLIT):
        part = rest.astype(BF16).astype(F32)
        slope_cols = jnp.where((lane == piece) | (lane == piece + POS_SPLIT), part, slope_cols)
        rest = rest - part
    slope_cols = jnp.tile(slope_cols.astype(BF16), (seq // sub_tile, 1))

    q = q_ref[...]
    lane = lax.broadcasted_iota(jnp.int32, q.shape, 1)
    zero = jnp.zeros_like(q)
    qbd_ref[:seq, :HEAD_DIM] = jnp.where(lane < HALF_DIM, q, zero)
    qbd_ref[seq:, :HEAD_DIM] = jnp.where(lane >= HALF_DIM, q, zero)
    qbd_ref[:seq, HEAD_DIM:] = slope_cols
    qbd_ref[seq:, HEAD_DIM:] = slope_cols

    lam = _lam_value(lq1_ref, lk1_ref, lq2_ref, lk2_ref, lam_init)

    def n_keys(j, qc):
        return w if qc * w == j * t else t

    def scores(unit):
        j, qc, half = unit
        c = half * n_qc + qc
        return _dot_nt(key_ref[j, :n_keys(j, qc), :], qbd_ref[c * w:(c + 1) * w, :])

    def softmax(unit, st):
        j, qc, half = unit
        lanes = slice((half * n_qc + qc) * w, (half * n_qc + qc + 1) * w)
        nk = n_keys(j, qc)
        if qc * w < (j + 1) * t:
            st = st + dt if nk == w else jnp.concatenate([st[:w], st[w:] + dt], axis=0)
        m_cur = jnp.max(st, axis=0, keepdims=True)
        if j == 0:
            m_new, alpha = m_cur, None
        else:
            m_old = m_ref[:, lanes]
            m_new = jnp.maximum(m_old, m_cur)
            alpha = jnp.exp2(m_old - m_new)
        m_ref[:, lanes] = m_new
        return jnp.exp2(st - m_new).astype(BF16), alpha

    def values(unit, p, alpha):
        j, qc, half = unit
        lanes = slice((half * n_qc + qc) * w, (half * n_qc + qc + 1) * w)
        pv = _dot(vt_ref[j, :, :n_keys(j, qc)], p)
        acc_ref[:, lanes] = pv if j == 0 else alpha * acc_ref[:, lanes] + pv

    def normalized(lanes):
        return acc_ref[:HEAD_DIM, lanes] * (1.0 / acc_ref[HEAD_DIM:HEAD_DIM + 1, lanes])

    def finalize(qc):
        rows = slice(qc * w, (qc + 1) * w)
        lanes1 = slice(qc * w, (qc + 1) * w)
        lanes2 = slice((n_qc + qc) * w, (n_qc + qc + 1) * w)
        ot = normalized(lanes1) - lam * normalized(lanes2)
        ot = ot * lax.rsqrt(jnp.mean(ot * ot, axis=0, keepdims=True) + NORM_EPS)
        o = (ot.T * sg_ref[...]) * (1.0 - lam_init)
        o_ref[rows, :] = (o * gb_ref[rows, :].astype(F32)).astype(BF16)

    units = _attn_units(seq)
    pending_scores = {u: scores(units[u]) for u in range(min(SCORES_AHEAD, len(units)))}
    pending_values = None
    for u, unit in enumerate(units):
        if u + SCORES_AHEAD < len(units):
            pending_scores[u + SCORES_AHEAD] = scores(units[u + SCORES_AHEAD])
        p, alpha = softmax(unit, pending_scores.pop(u))
        if pending_values is not None:
            values(*pending_values)
            j_done, qc_done, half_done = pending_values[0]
            if half_done == 1 and qc_done * w < (j_done + 1) * t:
                finalize(qc_done)
        pending_values = (unit, p, alpha)
    values(*pending_values)
    finalize(pending_values[0][1])


def _prompt_attention(q, keys, vt, gb, slopes, sg, lq1, lk1, lq2, lk2, *, lam_init):
    batch, _, seq, _ = q.shape
    t = ATTN_TILE
    n_tiles = seq // t
    seq_spec = pl.BlockSpec((None, None, seq, HEAD_DIM), lambda b, h: (b, h, 0, 0))
    vec = lambda a: pl.BlockSpec(a.shape, lambda b, h: (0, 0))
    return pl.pallas_call(
        functools.partial(_attn_kernel, lam_init=lam_init),
        grid=(batch, N_HEADS),
        in_specs=[
            pl.BlockSpec(memory_space=pltpu.SMEM),
            seq_spec,
            pl.BlockSpec((None, None, n_tiles, t, KEY_WIDTH), lambda b, h: (b, h, 0, 0, 0)),
            pl.BlockSpec((None, None, n_tiles, VT_ROWS, t), lambda b, h: (b, h, 0, 0, 0)),
            seq_spec, vec(sg), vec(lq1), vec(lk1), vec(lq2), vec(lk2)],
        out_specs=seq_spec,
        out_shape=jax.ShapeDtypeStruct((batch, N_HEADS, seq, HEAD_DIM), BF16),
        scratch_shapes=[
            pltpu.VMEM((2 * seq, KEY_WIDTH), BF16),
            pltpu.VMEM((1, 2 * seq), F32),
            pltpu.VMEM((VT_ROWS, 2 * seq), F32),
        ],
        compiler_params=_compiler_params(2),
        name="prompt_attn",
    )(slopes, q, keys, vt, gb, sg, lq1, lk1, lq2, lk2)


def _sample_attn_kernel(slopes_ref, q_ref, kn_ref, vn_ref, ckt_ref, cv_ref, gb_ref, sg_ref,
                        lq1_ref, lk1_ref, lq2_ref, lk2_ref, o_ref, *, lam_init, past_len):
    nq = q_ref.shape[0]
    lam = _lam_value(lq1_ref, lk1_ref, lq2_ref, lk2_ref, lam_init)
    qrow = lax.broadcasted_iota(jnp.int32, (2 * nq, past_len), 0)
    qpos_past = past_len + jnp.where(qrow >= nq, qrow - nq, qrow)
    dist_past = (qpos_past - lax.broadcasted_iota(jnp.int32, (2 * nq, past_len), 1)).astype(F32)
    pad = HEAD_DIM - nq
    qrow_n = lax.broadcasted_iota(jnp.int32, (2 * nq, HEAD_DIM), 0)
    qidx_n = jnp.where(qrow_n >= nq, qrow_n - nq, qrow_n)
    kidx_n = lax.broadcasted_iota(jnp.int32, (2 * nq, HEAD_DIM), 1)
    dist_new = jnp.abs(qidx_n - kidx_n).astype(F32)
    real_new = kidx_n < nq
    zpad = jnp.zeros((pad, HEAD_DIM), BF16)
    for h in range(N_HEADS):
        cols = _head_cols(h)
        slope = slopes_ref[h] * LOG2E
        q = q_ref[:, cols]
        lane = lax.broadcasted_iota(jnp.int32, q.shape, 1)
        zero = jnp.zeros_like(q)
        qbd = jnp.concatenate([jnp.where(lane < HALF_DIM, q, zero),
                               jnp.where(lane >= HALF_DIM, q, zero)], axis=0)
        s_past = _dot(qbd, ckt_ref[cols, :].astype(BF16)) - slope * dist_past
        kn = jnp.concatenate([kn_ref[:, cols].astype(BF16), zpad], axis=0)
        vn = jnp.concatenate([vn_ref[:, cols].astype(BF16), zpad], axis=0)
        s_new = _dot_nt(qbd, kn) - slope * dist_new
        s_new = jnp.where(real_new, s_new, NEG_INF)
        m = jnp.maximum(jnp.max(s_past, axis=-1, keepdims=True),
                        jnp.max(s_new, axis=-1, keepdims=True))
        p_past = jnp.exp2(s_past - m)
        p_new = jnp.exp2(s_new - m)
        l = jnp.sum(p_past, axis=-1, keepdims=True) + jnp.sum(p_new, axis=-1, keepdims=True)
        v_past = cv_ref[pl.ds(h, past_len, stride=N_HEADS), :].astype(BF16)
        acc = _dot(p_past.astype(BF16), v_past) + _dot(p_new.astype(BF16), vn)
        acc = acc * (1.0 / l)
        o = acc[:nq] - lam * acc[nq:]
        o = o * lax.rsqrt(jnp.mean(o * o, axis=-1, keepdims=True) + NORM_EPS)
        o = (o * sg_ref[...]) * (1.0 - lam_init)
        o_ref[:, cols] = (o * gb_ref[:, cols].astype(F32)).astype(BF16)


def _sample_attention(q, k_new, v_new, cache_kt, cache_v, gb, slopes, sg, lq1, lk1, lq2, lk2,
                      *, layer, n_streams, n_new, lam_init):
    past_len = cache_kt.shape[3]
    new_spec = pl.BlockSpec((None, n_new, WIDTH), lambda b: (b, 0, 0))
    vec = lambda a: pl.BlockSpec(a.shape, lambda b: (0, 0))
    r3 = lambda a: a.reshape(n_streams, n_new, WIDTH)
    out = pl.pallas_call(
        functools.partial(_sample_attn_kernel, lam_init=lam_init, past_len=past_len),
        grid=(n_streams,),
        in_specs=[
            pl.BlockSpec(memory_space=pltpu.SMEM), new_spec, new_spec, new_spec,
            pl.BlockSpec((None, None, WIDTH, past_len), lambda b: (layer, b, 0, 0)),
            pl.BlockSpec((None, None, past_len * N_HEADS, HEAD_DIM), lambda b: (layer, b, 0, 0)),
            new_spec, vec(sg), vec(lq1), vec(lk1), vec(lq2), vec(lk2)],
        out_specs=new_spec,
        out_shape=jax.ShapeDtypeStruct((n_streams, n_new, WIDTH), BF16),
        compiler_params=_compiler_params(1),
        name="sample_attn",
    )(slopes, r3(q), r3(k_new), r3(v_new), cache_kt, cache_v, r3(gb), sg, lq1, lk1, lq2, lk2)
    return out.reshape(n_streams * n_new, WIDTH)


def _group_mean_matrix(group):
    idx = jnp.arange(GROUP_MEAN_WIDTH) // group
    return jnp.where(idx[:, None] == idx[None, :], 1.0 / group, 0.0).astype(BF16)


def kernel(x_prompt, x_sample, cache_k, cache_v, norm_g, w_in, sgu_norm_g, sgu_w, sgu_b,
           q_norm_g, k_norm_g, lambda_q1, lambda_k1, lambda_q2, lambda_k2, subln_g, w_out):
    depth = w_in.shape[0]
    batch, seq, _ = x_prompt.shape
    n_streams, n_new, _ = x_sample.shape
    past_len = cache_k.shape[2]
    assert seq % ATTN_TILE == 0
    assert SGU_CHUNK % n_new == 0 and past_len % CHUNK == 0 and n_new <= CHUNK
    sample_tile = min(n_streams * n_new, TOKEN_TILE)
    assert sample_tile % SGU_CHUNK == 0 and (n_streams * n_new) % sample_tile == 0

    gm128 = _group_mean_matrix(HEAD_DIM)
    gm64 = _group_mean_matrix(HALF_DIM)
    slopes = jnp.asarray(ALIBI_SLOPES, F32)
    tril = jnp.tril(jnp.ones((SGU_CHUNK, SGU_CHUNK), F32))
    tril_new = jnp.tril(jnp.ones((n_new, n_new), F32))
    streams_per_chunk = SGU_CHUNK // n_new
    eye = jnp.eye(streams_per_chunk, dtype=F32)
    cache_kt = jnp.transpose(cache_k, (0, 1, 3, 4, 5, 2)).reshape(depth, n_streams, WIDTH, past_len)
    cache_vr = cache_v.reshape(depth, n_streams, past_len * N_HEADS, HEAD_DIM)

    xp = x_prompt.reshape(batch * seq, D_MODEL)
    xs = x_sample.reshape(n_streams * n_new, D_MODEL)
    k_stack = v_stack = prompt_out = None
    ks_rows, vs_rows, sgu_rows = [], [], []
    for i in range(depth):
        lam_init = _lam_init(i)
        w_in_bf = w_in[i].astype(BF16)
        w_out_bf = w_out[i].astype(BF16)
        row = lambda a: a.reshape(1, -1).astype(F32)
        ng = row(norm_g[i])
        sgug = row(sgu_norm_g[i])
        gq = row(jnp.tile(q_norm_g[i], WIDTH // HALF_DIM))
        gk = row(jnp.tile(k_norm_g[i], WIDTH // HALF_DIM))
        sg = row(subln_g[i])
        lams = tuple(row(a[i]) for a in (lambda_q1, lambda_k1, lambda_q2, lambda_k2))
        sguw_p = (sgu_w[i] * tril).astype(BF16)
        sgub_p = jnp.broadcast_to(sgu_b[i][:, :, None], (N_HEADS, SGU_CHUNK, HEAD_DIM))
        w_new = sgu_w[i][:, :n_new, :n_new] * tril_new
        sguw_s = jnp.einsum("ab,hts->hatbs", eye, w_new).reshape(
            N_HEADS, SGU_CHUNK, SGU_CHUNK).astype(BF16)
        sgub_s = jnp.broadcast_to(
            jnp.tile(sgu_b[i][:, :n_new], (1, streams_per_chunk))[:, :, None],
            (N_HEADS, SGU_CHUNK, HEAD_DIM))
        params_p = (ng, w_in_bf, sgug, sguw_p, sgub_p, gq, gk, gm128, gm64)
        params_s = (ng, w_in_bf, sgug, sguw_s, sgub_s, gq, gk, gm128, gm64)

        xp, ya, q, keys, k_stack, v_stack, vt, gb = _inproj_prompt(
            xp, params_p, prompt_out, k_stack, v_stack, batch=batch, seq=seq)
        yb = _prompt_attention(q, keys, vt, gb, slopes, sg, *lams, lam_init=lam_init)
        prompt_out = (ya, yb, w_out_bf)

        ya, q, k, v, gb, va = _inproj_sample(xs, params_s, tile=sample_tile)
        yb = _sample_attention(q, k, v, cache_kt, cache_vr, gb, slopes, sg, *lams, layer=i,
                               n_streams=n_streams, n_new=n_new, lam_init=lam_init)
        xs = _outproj(xs, ya, yb, w_out_bf, tile=sample_tile)
        ks_rows.append(k)
        vs_rows.append(v)
        sgu_rows.append(va)

    xp = _outproj(xp, *prompt_out, tile=TOKEN_TILE)
    new_k_prompt = jnp.transpose(
        k_stack.reshape(depth, batch, N_HEADS, 2, HALF_DIM, seq), (0, 1, 5, 2, 3, 4))
    return (
        xp.reshape(batch, seq, D_MODEL),
        xs.reshape(n_streams, n_new, D_MODEL),
        new_k_prompt,
        v_stack.reshape(depth, batch, seq, N_HEADS, HEAD_DIM),
        jnp.stack(ks_rows).reshape(depth, n_streams, n_new, N_HEADS, 2, HALF_DIM),
        jnp.stack(vs_rows).reshape(depth, n_streams, n_new, N_HEADS, HEAD_DIM),
        jnp.stack(sgu_rows).reshape(depth, n_streams, n_new, WIDTH),
    )
```

```python
import functools
import math
from typing import NamedTuple

import jax
import jax.numpy as jnp
from jax import lax
from jax.experimental import pallas as pl
from jax.experimental.pallas import tpu as pltpu

F32 = jnp.float32
BF16 = jnp.bfloat16

D_MODEL = 1024
N_HEADS = 4
HEAD_DIM = 128
HALF_DIM = 64
WIDTH = N_HEADS * HEAD_DIM
CHUNK = 64
SGU_CHUNK = 128
NORM_EPS = 1e-6
NEG_INF = -1e30
QK_SCALE = HALF_DIM ** -0.5
LOG2E = math.log2(math.e)
ALIBI_SLOPES = tuple(2.0 ** (-8.0 * (h + 1) / N_HEADS) for h in range(N_HEADS))

COL_U, COL_VA, COL_GA, COL_Q, COL_K, COL_V, COL_GB = (i * WIDTH for i in range(7))

VMEM_LIMIT_BYTES = 56 * 1024 * 1024

ATTN_TILE = 512
TOKEN_TILE = ATTN_TILE
PROMPT_SUB_TILES = 1
OUTPROJ_TILE = 1024
LANE_CHUNK = 256
KEY_WIDTH = 2 * HEAD_DIM
GROUP_MEAN_WIDTH = 256
POS_SPLIT = 3
VT_ROWS = HEAD_DIM + 16


def _lam_init(layer_idx):
    return 0.8 - 0.6 * math.exp(-0.3 * layer_idx)


def _compiler_params(n_axes):
    return pltpu.CompilerParams(
        dimension_semantics=("arbitrary",) * n_axes,
        vmem_limit_bytes=VMEM_LIMIT_BYTES,
    )


class _LayerParam(NamedTuple):
    array: jax.Array
    layer: int


def _array(p):
    return p.array if isinstance(p, _LayerParam) else p


def _const_spec(p, single_buffer=False):
    if isinstance(p, _LayerParam):
        block = (None, *p.array.shape[1:])
        index = (p.layer,) + (0,) * (p.array.ndim - 1)
    else:
        block, index = p.shape, (0,) * p.ndim
    kwargs = dict(pipeline_mode=pl.Buffered(1)) if single_buffer else {}
    return pl.BlockSpec(block, lambda *grid_idx: index, **kwargs)


def _dot(a, b):
    return jnp.dot(a, b, preferred_element_type=F32)


def _dot_nt(a, b):
    return lax.dot_general(a, b, (((1,), (1,)), ((), ())), preferred_element_type=F32)


def _lam_value(lq1_ref, lk1_ref, lq2_ref, lk2_ref, lam_init):
    d1 = jnp.sum(lq1_ref[...] * lk1_ref[...], axis=-1, keepdims=True)
    d2 = jnp.sum(lq2_ref[...] * lk2_ref[...], axis=-1, keepdims=True)
    return jnp.exp(d1) - jnp.exp(d2) + lam_init


def _head_cols(h):
    return slice(h * HEAD_DIM, (h + 1) * HEAD_DIM)


def _inproj_kernel(*refs, prompt, n_prev, fused, n_sub):
    (x_ref, ng_ref, w_ref, sgug_ref, sguw_ref, sgub_ref, gq_ref, gk_ref,
     gm128_ref, gm64_ref) = refs[:10]
    n_in = 10
    if fused:
        ya_prev_ref, yb_prev_ref, wo_ref = refs[n_in:n_in + 3]
        n_in += 3
    prev = refs[n_in:n_in + n_prev]
    outs = refs[n_in + n_prev:]
    if fused:
        xo_ref, outs = outs[0], outs[1:]
    tile = x_ref.shape[0]
    sub = tile // n_sub

    def group_rms_scale(z, gm_ref):
        sq = (z * z).astype(BF16)
        gw = gm_ref.shape[0]
        mean_sq = jnp.concatenate(
            [_dot(sq[:, c:c + gw], gm_ref[...]) for c in range(0, WIDTH, gw)], axis=1)
        return lax.rsqrt(mean_sq + NORM_EPS)

    if prompt:
        ya_ref, q_ref, key_ref, kt_ref, v_ref, vt_ref, gb_ref = outs
        last = kt_ref.shape[0] - 1
        if n_prev:
            kt_prev_ref, v_prev_ref = prev
            kt_ref[:last] = kt_prev_ref[...]
            v_ref[:last] = v_prev_ref[...]
        for h in range(N_HEADS):
            vt_ref[h, HEAD_DIM:, :] = jnp.ones((VT_ROWS - HEAD_DIM, tile), BF16)
    else:
        ya_ref, q_ref, k_ref, v_ref, gb_ref, va_ref = outs

    for s in range(n_sub):
        rows = slice(s * sub, (s + 1) * sub)
        x = x_ref[rows, :]
        if fused:
            yb_prev = jnp.concatenate([yb_prev_ref[h, rows, :] for h in range(N_HEADS)], axis=1)
            x = x + (_dot(ya_prev_ref[rows, :], wo_ref[:WIDTH, :])
                     + _dot(yb_prev, wo_ref[WIDTH:, :]))
            xo_ref[rows, :] = x
        ms = jnp.mean(x * x, axis=-1, keepdims=True)
        hb = ((x * lax.rsqrt(ms + NORM_EPS)) * ng_ref[...]).astype(BF16)

        def proj(col):
            return _dot(hb, w_ref[:, col:col + WIDTH])

        va = proj(COL_VA)
        if not prompt:
            va_ref[rows, :] = va
        vn = ((va * group_rms_scale(va, gm128_ref)) * sgug_ref[...]).astype(BF16)
        gate = proj(COL_U) * jax.nn.silu(proj(COL_GA))
        for c in range(sub // SGU_CHUNK):
            crows = slice(c * SGU_CHUNK, (c + 1) * SGU_CHUNK)
            orows = slice(s * sub + c * SGU_CHUNK, s * sub + (c + 1) * SGU_CHUNK)
            for g in range(N_HEADS):
                cols = _head_cols(g)
                mixed = _dot(sguw_ref[g], vn[crows, cols]) + sgub_ref[g]
                ya_ref[orows, cols] = (gate[crows, cols] * mixed).astype(BF16)

        zq = proj(COL_Q)
        qn = ((zq * group_rms_scale(zq, gm64_ref)) * gq_ref[...]
              * (QK_SCALE * LOG2E)).astype(BF16)
        zk = proj(COL_K)
        kn = (zk * group_rms_scale(zk, gm64_ref)) * gk_ref[...]
        zv = proj(COL_V)
        gb = jax.nn.silu(proj(COL_GB)).astype(BF16)

        if prompt:
            pos = (lax.broadcasted_iota(jnp.int32, (sub, HEAD_DIM), 0)
                   + (pl.program_id(1) * tile + s * sub))
            lane = lax.broadcasted_iota(jnp.int32, (sub, HEAD_DIM), 1)
            lo = jnp.bitwise_and(pos, CHUNK - 1)
            pos_cols = jnp.where(
                lane < POS_SPLIT, (pos - lo).astype(F32),
                jnp.where(lane < 2 * POS_SPLIT, lo.astype(F32), 0.0)).astype(BF16)
            kt_ref[last, :, rows] = kn.T
            for h in range(N_HEADS):
                cols = _head_cols(h)
                q_ref[h, rows, :] = qn[:, cols]
                gb_ref[h, rows, :] = gb[:, cols]
                key_ref[h, rows, :HEAD_DIM] = kn[:, cols].astype(BF16)
                key_ref[h, rows, HEAD_DIM:] = pos_cols
                vt_ref[h, :HEAD_DIM, rows] = zv[:, cols].T.astype(BF16)
                v_ref[last, pl.ds(s * sub * N_HEADS + h, sub, stride=N_HEADS), :] = zv[:, cols]
        else:
            q_ref[rows, :] = qn
            k_ref[rows, :] = kn
            v_ref[rows, :] = zv
            gb_ref[rows, :] = gb


def _inproj_prompt(x, params, prev_out, k_stack, v_stack, *, batch, seq):
    tile = TOKEN_TILE
    n_t = seq // tile
    n = batch * seq
    n_layers = 1 if k_stack is None else k_stack.shape[0] + 1
    full = lambda p: _const_spec(p, single_buffer=True)
    head_major = lambda width: pl.BlockSpec((None, N_HEADS, tile, width),
                                            lambda b, i: (b, 0, i, 0))
    kt_spec = lambda layers: pl.BlockSpec((layers, None, WIDTH, tile),
                                          lambda b, i: (0, b, 0, i))
    v_spec = lambda layers: pl.BlockSpec((layers, tile * N_HEADS, HEAD_DIM),
                                         lambda b, i: (0, b * n_t + i, 0))
    x_spec = pl.BlockSpec((tile, D_MODEL), lambda b, i: (b * n_t + i, 0))
    ya_spec = pl.BlockSpec((tile, WIDTH), lambda b, i: (b * n_t + i, 0))
    in_specs = [x_spec] + [full(p) for p in params]
    operands = [x] + [_array(p) for p in params]
    fused = prev_out is not None
    if fused:
        ya_prev, yb_prev, w_out_prev = prev_out
        in_specs += [ya_spec, head_major(HEAD_DIM), full(w_out_prev)]
        operands += [ya_prev, yb_prev, _array(w_out_prev)]
    if n_layers > 1:
        in_specs += [kt_spec(n_layers - 1), v_spec(n_layers - 1)]
        operands += [k_stack, v_stack]
    out_shape = [
        jax.ShapeDtypeStruct((n, WIDTH), BF16),
        jax.ShapeDtypeStruct((batch, N_HEADS, seq, HEAD_DIM), BF16),
        jax.ShapeDtypeStruct((batch, N_HEADS, n_t, tile, KEY_WIDTH), BF16),
        jax.ShapeDtypeStruct((n_layers, batch, WIDTH, seq), F32),
        jax.ShapeDtypeStruct((n_layers, n * N_HEADS, HEAD_DIM), F32),
        jax.ShapeDtypeStruct((batch, N_HEADS, n_t, VT_ROWS, tile), BF16),
        jax.ShapeDtypeStruct((batch, N_HEADS, seq, HEAD_DIM), BF16),
    ]
    out_specs = [
        ya_spec,
        head_major(HEAD_DIM),
        pl.BlockSpec((None, N_HEADS, None, tile, KEY_WIDTH), lambda b, i: (b, 0, i, 0, 0)),
        kt_spec(n_layers),
        v_spec(n_layers),
        pl.BlockSpec((None, N_HEADS, None, VT_ROWS, tile), lambda b, i: (b, 0, i, 0, 0)),
        head_major(HEAD_DIM),
    ]
    if fused:
        out_shape.insert(0, jax.ShapeDtypeStruct((n, D_MODEL), F32))
        out_specs.insert(0, x_spec)
    outs = pl.pallas_call(
        functools.partial(_inproj_kernel, prompt=True, n_prev=2 * (n_layers > 1), fused=fused,
                          n_sub=PROMPT_SUB_TILES),
        grid=(batch, n_t),
        in_specs=in_specs,
        out_specs=out_specs,
        out_shape=out_shape,
        compiler_params=_compiler_params(2),
        name="inproj_prompt",
    )(*operands)
    return outs if fused else (x, *outs)


def _inproj_sample(x, params, *, tile):
    n = x.shape[0]
    row_spec = lambda width: pl.BlockSpec((tile, width), lambda i: (i, 0))
    dtypes = (BF16, BF16, F32, F32, BF16, F32)
    return pl.pallas_call(
        functools.partial(_inproj_kernel, prompt=False, n_prev=0, fused=False, n_sub=1),
        grid=(n // tile,),
        in_specs=[row_spec(D_MODEL)] + [_const_spec(p) for p in params],
        out_specs=[row_spec(WIDTH)] * len(dtypes),
        out_shape=[jax.ShapeDtypeStruct((n, WIDTH), d) for d in dtypes],
        compiler_params=_compiler_params(1),
        name="inproj_sample",
    )(x, *[_array(p) for p in params])


def _outproj_kernel(x_ref, ya_ref, yb_ref, w_ref, o_ref, *, head_major):
    if head_major:
        yb = jnp.concatenate([yb_ref[h] for h in range(N_HEADS)], axis=1)
    else:
        yb = yb_ref[...]
    y = _dot(ya_ref[...], w_ref[:WIDTH, :]) + _dot(yb, w_ref[WIDTH:, :])
    o_ref[...] = x_ref[...] + y


def _outproj(x, ya, yb, w_bf, *, tile):
    n = x.shape[0]
    head_major = yb.ndim == 4
    if head_major:
        n_t = yb.shape[2] // tile
        yb_spec = pl.BlockSpec((None, N_HEADS, tile, HEAD_DIM),
                               lambda i: (i // n_t, 0, i % n_t, 0))
    else:
        yb_spec = pl.BlockSpec((tile, WIDTH), lambda i: (i, 0))
    return pl.pallas_call(
        functools.partial(_outproj_kernel, head_major=head_major),
        grid=(n // tile,),
        in_specs=[
            pl.BlockSpec((tile, D_MODEL), lambda i: (i, 0)),
            pl.BlockSpec((tile, WIDTH), lambda i: (i, 0)),
            yb_spec,
            _const_spec(w_bf),
        ],
        out_specs=pl.BlockSpec((tile, D_MODEL), lambda i: (i, 0)),
        out_shape=jax.ShapeDtypeStruct((n, D_MODEL), F32),
        compiler_params=_compiler_params(1),
        name="outproj",
    )(x, ya, yb, _array(w_bf))


SCORES_AHEAD = 2


def _attn_units(seq):
    n_qc = seq // LANE_CHUNK
    qc_per_tile = ATTN_TILE // LANE_CHUNK
    return [(j, qc, half)
            for j in range(seq // ATTN_TILE)
            for qc in range(j * qc_per_tile, n_qc)
            for half in range(2)]


def _attn_kernel(slopes_ref, q_ref, key_ref, vt_ref, gb_ref, sg_ref, lq1_ref, lk1_ref,
                 lq2_ref, lk2_ref, o_ref, qbd_ref, m_ref, acc_ref, *, lam_init):
    seq = q_ref.shape[0]
    t, w = ATTN_TILE, LANE_CHUNK
    assert t == 2 * w
    n_qc = seq // w
    slope2 = slopes_ref[pl.program_id(1)] * LOG2E

    key = lax.broadcasted_iota(jnp.int32, (w, w), 0)
    qry = lax.broadcasted_iota(jnp.int32, (w, w), 1)
    allowed = jnp.right_shift(key, 6) <= jnp.right_shift(qry, 6)
    ahead = jnp.maximum(key - qry, 0).astype(F32)
    dt = jnp.where(allowed, (-2.0 * slope2) * ahead, NEG_INF)

    sub_tile = 16
    lane = lax.broadcasted_iota(jnp.int32, (sub_tile, HEAD_DIM), 1)
    rest = jnp.full((sub_tile, HEAD_DIM), slope2, F32)
    slope_cols = jnp.zeros((sub_tile, HEAD_DIM), F32)
    for piece in range(POS_SPLIT):
        part = rest.astype(BF16).astype(F32)
        slope_cols = jnp.where((lane == piece) | (lane == piece + POS_SPLIT), part, slope_cols)
        rest = rest - part
    slope_cols = jnp.tile(slope_cols.astype(BF16), (seq // sub_tile, 1))

    q = q_ref[...]
    lane = lax.broadcasted_iota(jnp.int32, q.shape, 1)
    zero = jnp.zeros_like(q)
    qbd_ref[:seq, :HEAD_DIM] = jnp.where(lane < HALF_DIM, q, zero)
    qbd_ref[seq:, :HEAD_DIM] = jnp.where(lane >= HALF_DIM, q, zero)
    qbd_ref[:seq, HEAD_DIM:] = slope_cols
    qbd_ref[seq:, HEAD_DIM:] = slope_cols

    lam = _lam_value(lq1_ref, lk1_ref, lq2_ref, lk2_ref, lam_init)

    def n_keys(j, qc):
        return w if qc * w == j * t else t

    def scores(unit):
        j, qc, half = unit
        c = half * n_qc + qc
        return _dot_nt(key_ref[j, :n_keys(j, qc), :], qbd_ref[c * w:(c + 1) * w, :])

    def softmax(unit, st):
        j, qc, half = unit
        lanes = slice((half * n_qc + qc) * w, (half * n_qc + qc + 1) * w)
        nk = n_keys(j, qc)
        if qc * w < (j + 1) * t:
            st = st + dt if nk == w else jnp.concatenate([st[:w], st[w:] + dt], axis=0)
        m_cur = jnp.max(st, axis=0, keepdims=True)
        if j == 0:
            m_new, alpha = m_cur, None
        else:
            m_old = m_ref[:, lanes]
            m_new = jnp.maximum(m_old, m_cur)
            alpha = jnp.exp2(m_old - m_new)
        m_ref[:, lanes] = m_new
        return jnp.exp2(st - m_new).astype(BF16), alpha

    def values(unit, p, alpha):
        j, qc, half = unit
        lanes = slice((half * n_qc + qc) * w, (half * n_qc + qc + 1) * w)
        pv = _dot(vt_ref[j, :, :n_keys(j, qc)], p)
        acc_ref[:, lanes] = pv if j == 0 else alpha * acc_ref[:, lanes] + pv

    def normalized(lanes):
        return acc_ref[:HEAD_DIM, lanes] * (1.0 / acc_ref[HEAD_DIM:HEAD_DIM + 1, lanes])

    def finalize(qc):
        rows = slice(qc * w, (qc + 1) * w)
        lanes1 = slice(qc * w, (qc + 1) * w)
        lanes2 = slice((n_qc + qc) * w, (n_qc + qc + 1) * w)
        ot = normalized(lanes1) - lam * normalized(lanes2)
        ot = ot * lax.rsqrt(jnp.mean(ot * ot, axis=0, keepdims=True) + NORM_EPS)
        o = (ot.T * sg_ref[...]) * (1.0 - lam_init)
        o_ref[rows, :] = (o * gb_ref[rows, :].astype(F32)).astype(BF16)

    units = _attn_units(seq)
    pending_scores = {u: scores(units[u]) for u in range(min(SCORES_AHEAD, len(units)))}
    pending_values = None
    for u, unit in enumerate(units):
        if u + SCORES_AHEAD < len(units):
            pending_scores[u + SCORES_AHEAD] = scores(units[u + SCORES_AHEAD])
        p, alpha = softmax(unit, pending_scores.pop(u))
        if pending_values is not None:
            values(*pending_values)
            j_done, qc_done, half_done = pending_values[0]
            if half_done == 1 and qc_done * w < (j_done + 1) * t:
                finalize(qc_done)
        pending_values = (unit, p, alpha)
    values(*pending_values)
    finalize(pending_values[0][1])


def _prompt_attention(q, keys, vt, gb, slopes, sg, lq1, lk1, lq2, lk2, *, lam_init):
    batch, _, seq, _ = q.shape
    t = ATTN_TILE
    n_tiles = seq // t
    seq_spec = pl.BlockSpec((None, None, seq, HEAD_DIM), lambda b, h: (b, h, 0, 0))
    vec = _const_spec
    return pl.pallas_call(
        functools.partial(_attn_kernel, lam_init=lam_init),
        grid=(batch, N_HEADS),
        in_specs=[
            pl.BlockSpec(memory_space=pltpu.SMEM),
            seq_spec,
            pl.BlockSpec((None, None, n_tiles, t, KEY_WIDTH), lambda b, h: (b, h, 0, 0, 0)),
            pl.BlockSpec((None, None, n_tiles, VT_ROWS, t), lambda b, h: (b, h, 0, 0, 0)),
            seq_spec, vec(sg), vec(lq1), vec(lk1), vec(lq2), vec(lk2)],
        out_specs=seq_spec,
        out_shape=jax.ShapeDtypeStruct((batch, N_HEADS, seq, HEAD_DIM), BF16),
        scratch_shapes=[
            pltpu.VMEM((2 * seq, KEY_WIDTH), BF16),
            pltpu.VMEM((1, 2 * seq), F32),
            pltpu.VMEM((VT_ROWS, 2 * seq), F32),
        ],
        compiler_params=_compiler_params(2),
        name="prompt_attn",
    )(slopes, q, keys, vt, gb, *[_array(p) for p in (sg, lq1, lk1, lq2, lk2)])


def _sample_attn_kernel(q_ref, kn_ref, vn_ref, ckt_ref, cv_ref, gb_ref, sg_ref,
                        lq1_ref, lk1_ref, lq2_ref, lk2_ref, o_ref, *, lam_init, past_len):
    nq = q_ref.shape[0]
    per_head = 2 * nq
    n_rows = N_HEADS * per_head
    lam = _lam_value(lq1_ref, lk1_ref, lq2_ref, lk2_ref, lam_init)

    def alibi(n_cols, key_pos0):
        row = lax.broadcasted_iota(jnp.int32, (n_rows, n_cols), 0)
        col = lax.broadcasted_iota(jnp.int32, (n_rows, n_cols), 1)
        dist = jnp.abs(past_len + lax.rem(row, nq) - (key_pos0 + col)).astype(F32)
        slope = jnp.zeros((n_rows, n_cols), F32)
        for h in range(N_HEADS):
            slope = jnp.where(lax.div(row, per_head) == h, ALIBI_SLOPES[h] * LOG2E, slope)
        return slope * dist

    zpad = jnp.zeros((HEAD_DIM - nq, HEAD_DIM), BF16)
    s_past, s_new, v_new = [], [], []
    for h in range(N_HEADS):
        cols = _head_cols(h)
        q = q_ref[:, cols]
        lane = lax.broadcasted_iota(jnp.int32, q.shape, 1)
        zero = jnp.zeros_like(q)
        qbd = jnp.concatenate([jnp.where(lane < HALF_DIM, q, zero),
                               jnp.where(lane >= HALF_DIM, q, zero)], axis=0)
        s_past.append(_dot(qbd, ckt_ref[cols, :].astype(BF16)))
        s_new.append(_dot_nt(qbd, jnp.concatenate([kn_ref[:, cols].astype(BF16), zpad], axis=0)))
        v_new.append(jnp.concatenate([vn_ref[:, cols].astype(BF16), zpad], axis=0))
    s_past = jnp.concatenate(s_past, axis=0) - alibi(past_len, 0)
    s_new = jnp.concatenate(s_new, axis=0) - alibi(HEAD_DIM, past_len)
    real_new = lax.broadcasted_iota(jnp.int32, s_new.shape, 1) < nq
    s_new = jnp.where(real_new, s_new, NEG_INF)
    m = jnp.maximum(jnp.max(s_past, axis=-1, keepdims=True),
                    jnp.max(s_new, axis=-1, keepdims=True))
    p_past = jnp.exp2(s_past - m)
    p_new = jnp.exp2(s_new - m)
    inv_l = 1.0 / (jnp.sum(p_past, axis=-1, keepdims=True)
                   + jnp.sum(p_new, axis=-1, keepdims=True))
    p_past = p_past.astype(BF16)
    p_new = p_new.astype(BF16)
    outs = []
    for h in range(N_HEADS):
        rows = slice(h * per_head, (h + 1) * per_head)
        v_past = cv_ref[pl.ds(h, past_len, stride=N_HEADS), :].astype(BF16)
        acc = (_dot(p_past[rows], v_past) + _dot(p_new[rows], v_new[h])) * inv_l[rows]
        o = acc[:nq] - lam * acc[nq:]
        outs.append(o * lax.rsqrt(jnp.mean(o * o, axis=-1, keepdims=True) + NORM_EPS))
    o = jnp.concatenate(outs, axis=1) * jnp.tile(sg_ref[...], (1, N_HEADS)) * (1.0 - lam_init)
    o_ref[...] = (o * gb_ref[...].astype(F32)).astype(BF16)


def _sample_attention(q, k_new, v_new, cache_kt, cache_v, gb, sg, lq1, lk1, lq2, lk2,
                      *, layer, n_streams, n_new, lam_init):
    past_len = cache_kt.shape[3]
    new_spec = pl.BlockSpec((None, n_new, WIDTH), lambda b: (b, 0, 0))
    vec = _const_spec
    r3 = lambda a: a.reshape(n_streams, n_new, WIDTH)
    out = pl.pallas_call(
        functools.partial(_sample_attn_kernel, lam_init=lam_init, past_len=past_len),
        grid=(n_streams,),
        in_specs=[
            new_spec, new_spec, new_spec,
            pl.BlockSpec((None, None, WIDTH, past_len), lambda b: (layer, b, 0, 0)),
            pl.BlockSpec((None, None, past_len * N_HEADS, HEAD_DIM), lambda b: (layer, b, 0, 0)),
            new_spec, vec(sg), vec(lq1), vec(lk1), vec(lq2), vec(lk2)],
        out_specs=new_spec,
        out_shape=jax.ShapeDtypeStruct((n_streams, n_new, WIDTH), BF16),
        compiler_params=_compiler_params(1),
        name="sample_attn",
    )(r3(q), r3(k_new), r3(v_new), cache_kt, cache_v, r3(gb),
      *[_array(p) for p in (sg, lq1, lk1, lq2, lk2)])
    return out.reshape(n_streams * n_new, WIDTH)


def _group_mean_matrix(group):
    idx = jnp.arange(GROUP_MEAN_WIDTH) // group
    return jnp.where(idx[:, None] == idx[None, :], 1.0 / group, 0.0).astype(BF16)


def kernel(x_prompt, x_sample, cache_k, cache_v, norm_g, w_in, sgu_norm_g, sgu_w, sgu_b,
           q_norm_g, k_norm_g, lambda_q1, lambda_k1, lambda_q2, lambda_k2, subln_g, w_out):
    depth = w_in.shape[0]
    batch, seq, _ = x_prompt.shape
    n_streams, n_new, _ = x_sample.shape
    past_len = cache_k.shape[2]
    assert seq % ATTN_TILE == 0
    assert SGU_CHUNK % n_new == 0 and past_len % CHUNK == 0 and n_new <= CHUNK
    sample_tile = min(n_streams * n_new, TOKEN_TILE)
    assert sample_tile % SGU_CHUNK == 0 and (n_streams * n_new) % sample_tile == 0

    gm128 = _group_mean_matrix(HEAD_DIM)
    gm64 = _group_mean_matrix(HALF_DIM)
    slopes = jnp.asarray(ALIBI_SLOPES, F32)
    tril = jnp.tril(jnp.ones((SGU_CHUNK, SGU_CHUNK), F32))
    tril_new = jnp.tril(jnp.ones((n_new, n_new), F32))
    streams_per_chunk = SGU_CHUNK // n_new
    eye = jnp.eye(streams_per_chunk, dtype=F32)
    cache_kt = jnp.transpose(cache_k, (0, 1, 3, 4, 5, 2)).reshape(depth, n_streams, WIDTH, past_len)
    cache_vr = cache_v.reshape(depth, n_streams, past_len * N_HEADS, HEAD_DIM)

    rows = lambda a: a.reshape(depth, 1, -1).astype(F32)
    w_in_bf = w_in.astype(BF16)
    w_out_bf = w_out.astype(BF16)
    ng = rows(norm_g)
    sgug = rows(sgu_norm_g)
    gq = rows(jnp.tile(q_norm_g, (1, WIDTH // HALF_DIM)))
    gk = rows(jnp.tile(k_norm_g, (1, WIDTH // HALF_DIM)))
    sg = rows(subln_g)
    lam_vecs = tuple(rows(a) for a in (lambda_q1, lambda_k1, lambda_q2, lambda_k2))
    sguw_p = (sgu_w * tril).astype(BF16)
    sgub_p = jnp.broadcast_to(sgu_b[..., None], (depth, N_HEADS, SGU_CHUNK, HEAD_DIM))
    w_new = sgu_w[:, :, :n_new, :n_new] * tril_new
    sguw_s = jnp.einsum("ab,lhts->lhatbs", eye, w_new).reshape(
        depth, N_HEADS, SGU_CHUNK, SGU_CHUNK).astype(BF16)
    sgub_s = jnp.broadcast_to(
        jnp.tile(sgu_b[:, :, :n_new], (1, 1, streams_per_chunk))[..., None],
        (depth, N_HEADS, SGU_CHUNK, HEAD_DIM))

    xp = x_prompt.reshape(batch * seq, D_MODEL)
    xs = x_sample.reshape(n_streams * n_new, D_MODEL)
    k_stack = v_stack = prompt_out = None
    ks_rows, vs_rows, sgu_rows = [], [], []
    for i in range(depth):
        lam_init = _lam_init(i)
        of_layer = lambda a: _LayerParam(a, i)
        params_p = (*map(of_layer, (ng, w_in_bf, sgug, sguw_p, sgub_p, gq, gk)), gm128, gm64)
        params_s = (*map(of_layer, (ng, w_in_bf, sgug, sguw_s, sgub_s, gq, gk)), gm128, gm64)
        attn_params = tuple(map(of_layer, (sg, *lam_vecs)))

        xp, ya, q, keys, k_stack, v_stack, vt, gb = _inproj_prompt(
            xp, params_p, prompt_out, k_stack, v_stack, batch=batch, seq=seq)
        yb = _prompt_attention(q, keys, vt, gb, slopes, *attn_params, lam_init=lam_init)
        prompt_out = (ya, yb, of_layer(w_out_bf))

        ya, q, k, v, gb, va = _inproj_sample(xs, params_s, tile=sample_tile)
        yb = _sample_attention(q, k, v, cache_kt, cache_vr, gb, *attn_params, layer=i,
                               n_streams=n_streams, n_new=n_new, lam_init=lam_init)
        xs = _outproj(xs, ya, yb, of_layer(w_out_bf), tile=sample_tile)
        ks_rows.append(k)
        vs_rows.append(v)
        sgu_rows.append(va)

    xp = _outproj(xp, *prompt_out, tile=OUTPROJ_TILE)
    new_k_prompt = jnp.transpose(
        k_stack.reshape(depth, batch, N_HEADS, 2, HALF_DIM, seq), (0, 1, 5, 2, 3, 4))
    return (
        xp.reshape(batch, seq, D_MODEL),
        xs.reshape(n_streams, n_new, D_MODEL),
        new_k_prompt,
        v_stack.reshape(depth, batch, seq, N_HEADS, HEAD_DIM),
        jnp.stack(ks_rows).reshape(depth, n_streams, n_new, N_HEADS, 2, HALF_DIM),
        jnp.stack(vs_rows).reshape(depth, n_streams, n_new, N_HEADS, HEAD_DIM),
        jnp.stack(sgu_rows).reshape(depth, n_streams, n_new, WIDTH),
    )
```

```python
import collections
import functools
import math
from typing import NamedTuple

import jax
import jax.numpy as jnp
from jax import lax
from jax.experimental import pallas as pl
from jax.experimental.pallas import tpu as pltpu

F32 = jnp.float32
BF16 = jnp.bfloat16

D_MODEL = 1024
N_HEADS = 4
HEAD_DIM = 128
HALF_DIM = 64
WIDTH = N_HEADS * HEAD_DIM
CHUNK = 64
SGU_CHUNK = 128
NORM_EPS = 1e-6
NEG_INF = -1e30
QK_SCALE = HALF_DIM ** -0.5
LOG2E = math.log2(math.e)
ALIBI_SLOPES = tuple(2.0 ** (-8.0 * (h + 1) / N_HEADS) for h in range(N_HEADS))

COL_U, COL_VA, COL_GA, COL_Q, COL_K, COL_V, COL_GB = (i * WIDTH for i in range(7))

VMEM_LIMIT_BYTES = 56 * 1024 * 1024

ATTN_TILE = 512
TOKEN_TILE = ATTN_TILE
PROMPT_SUB_TILES = 1
OUTPROJ_TILE = 1024
LANE_CHUNK = 256
KEY_WIDTH = 2 * HEAD_DIM
GROUP_MEAN_WIDTH = 256
POS_SPLIT = 3
VT_ROWS = HEAD_DIM + 16


def _lam_init(layer_idx):
    return 0.8 - 0.6 * math.exp(-0.3 * layer_idx)


def _compiler_params(n_axes):
    return pltpu.CompilerParams(
        dimension_semantics=("arbitrary",) * n_axes,
        vmem_limit_bytes=VMEM_LIMIT_BYTES,
    )


class _LayerParam(NamedTuple):
    array: jax.Array
    layer: int


def _array(p):
    return p.array if isinstance(p, _LayerParam) else p


def _const_spec(p, single_buffer=False):
    if isinstance(p, _LayerParam):
        block = (None, *p.array.shape[1:])
        index = (p.layer,) + (0,) * (p.array.ndim - 1)
    else:
        block, index = p.shape, (0,) * p.ndim
    kwargs = dict(pipeline_mode=pl.Buffered(1)) if single_buffer else {}
    return pl.BlockSpec(block, lambda *grid_idx: index, **kwargs)


def _dot(a, b):
    return jnp.dot(a, b, preferred_element_type=F32)


def _dot_nt(a, b):
    return lax.dot_general(a, b, (((1,), (1,)), ((), ())), preferred_element_type=F32)


def _lam_value(lq1_ref, lk1_ref, lq2_ref, lk2_ref, lam_init):
    d1 = jnp.sum(lq1_ref[...] * lk1_ref[...], axis=-1, keepdims=True)
    d2 = jnp.sum(lq2_ref[...] * lk2_ref[...], axis=-1, keepdims=True)
    return jnp.exp(d1) - jnp.exp(d2) + lam_init


def _head_cols(h):
    return slice(h * HEAD_DIM, (h + 1) * HEAD_DIM)


def _inproj_kernel(*refs, prompt, n_prev, fused, n_sub):
    (x_ref, ng_ref, w_ref, sgug_ref, sguw_ref, sgub_ref, gq_ref, gk_ref,
     gm128_ref, gm64_ref) = refs[:10]
    n_in = 10
    if fused:
        ya_prev_ref, yb_prev_ref, wo_ref = refs[n_in:n_in + 3]
        n_in += 3
    prev = refs[n_in:n_in + n_prev]
    outs = refs[n_in + n_prev:]
    if fused:
        xo_ref, outs = outs[0], outs[1:]
    tile = x_ref.shape[0]
    sub = tile // n_sub

    def group_rms_scale(z, gm_ref):
        sq = (z * z).astype(BF16)
        gw = gm_ref.shape[0]
        mean_sq = jnp.concatenate(
            [_dot(sq[:, c:c + gw], gm_ref[...]) for c in range(0, WIDTH, gw)], axis=1)
        return lax.rsqrt(mean_sq + NORM_EPS)

    if prompt:
        ya_ref, q_ref, key_ref, kt_ref, v_ref, vt_ref, gb_ref = outs
        last = kt_ref.shape[0] - 1
        if n_prev:
            kt_prev_ref, v_prev_ref = prev
            kt_ref[:last] = kt_prev_ref[...]
            v_ref[:last] = v_prev_ref[...]
        for h in range(N_HEADS):
            vt_ref[h, HEAD_DIM:, :] = jnp.ones((VT_ROWS - HEAD_DIM, tile), BF16)
    else:
        ya_ref, q_ref, k_ref, v_ref, gb_ref, va_ref = outs

    for s in range(n_sub):
        rows = slice(s * sub, (s + 1) * sub)
        x = x_ref[rows, :]
        if fused:
            yb_prev = jnp.concatenate([yb_prev_ref[h, rows, :] for h in range(N_HEADS)], axis=1)
            x = x + (_dot(ya_prev_ref[rows, :], wo_ref[:WIDTH, :])
                     + _dot(yb_prev, wo_ref[WIDTH:, :]))
            xo_ref[rows, :] = x
        hb = (x * ng_ref[...]).astype(BF16)
        row_scale = jnp.broadcast_to(
            lax.rsqrt(jnp.mean(x * x, axis=-1, keepdims=True) + NORM_EPS), (sub, WIDTH))

        def proj(col):
            return _dot(hb, w_ref[:, col:col + WIDTH]) * row_scale

        va = proj(COL_VA)
        if not prompt:
            va_ref[rows, :] = va
        vn = ((va * group_rms_scale(va, gm128_ref)) * sgug_ref[...]).astype(BF16)
        gate = proj(COL_U) * jax.nn.silu(proj(COL_GA))
        for c in range(sub // SGU_CHUNK):
            crows = slice(c * SGU_CHUNK, (c + 1) * SGU_CHUNK)
            orows = slice(s * sub + c * SGU_CHUNK, s * sub + (c + 1) * SGU_CHUNK)
            for g in range(N_HEADS):
                cols = _head_cols(g)
                mixed = _dot(sguw_ref[g], vn[crows, cols]) + sgub_ref[g]
                ya_ref[orows, cols] = (gate[crows, cols] * mixed).astype(BF16)

        zq = proj(COL_Q)
        qn = ((zq * group_rms_scale(zq, gm64_ref)) * gq_ref[...]
              * (QK_SCALE * LOG2E)).astype(BF16)
        zk = proj(COL_K)
        kn = (zk * group_rms_scale(zk, gm64_ref)) * gk_ref[...]
        gb = jax.nn.silu(proj(COL_GB)).astype(BF16)
        zv = proj(COL_V)

        if prompt:
            pos = (lax.broadcasted_iota(jnp.int32, (sub, HEAD_DIM), 0)
                   + (pl.program_id(1) * tile + s * sub))
            lane = lax.broadcasted_iota(jnp.int32, (sub, HEAD_DIM), 1)
            lo = jnp.bitwise_and(pos, CHUNK - 1)
            pos_cols = jnp.where(
                lane < POS_SPLIT, (pos - lo).astype(F32),
                jnp.where(lane < 2 * POS_SPLIT, lo.astype(F32), 0.0)).astype(BF16)
            kt_ref[last, :, rows] = kn.T
            for h in range(N_HEADS):
                cols = _head_cols(h)
                q_ref[h, rows, :] = qn[:, cols]
                gb_ref[h, rows, :] = gb[:, cols]
                key_ref[h, rows, :HEAD_DIM] = kn[:, cols].astype(BF16)
                key_ref[h, rows, HEAD_DIM:] = pos_cols
                vt_ref[h, :HEAD_DIM, rows] = zv[:, cols].T.astype(BF16)
                v_ref[last, pl.ds(s * sub * N_HEADS + h, sub, stride=N_HEADS), :] = zv[:, cols]
        else:
            q_ref[rows, :] = qn
            k_ref[rows, :] = kn
            v_ref[rows, :] = zv
            gb_ref[rows, :] = gb


def _inproj_prompt(x, params, prev_out, k_stack, v_stack, *, batch, seq):
    tile = TOKEN_TILE
    n_t = seq // tile
    n = batch * seq
    n_layers = 1 if k_stack is None else k_stack.shape[0] + 1
    full = lambda p: _const_spec(p, single_buffer=True)
    head_major = lambda width: pl.BlockSpec((None, N_HEADS, tile, width),
                                            lambda b, i: (b, 0, i, 0))
    kt_spec = lambda layers: pl.BlockSpec((layers, None, WIDTH, tile),
                                          lambda b, i: (0, b, 0, i))
    v_spec = lambda layers: pl.BlockSpec((layers, tile * N_HEADS, HEAD_DIM),
                                         lambda b, i: (0, b * n_t + i, 0))
    x_spec = pl.BlockSpec((tile, D_MODEL), lambda b, i: (b * n_t + i, 0))
    ya_spec = pl.BlockSpec((tile, WIDTH), lambda b, i: (b * n_t + i, 0))
    in_specs = [x_spec] + [full(p) for p in params]
    operands = [x] + [_array(p) for p in params]
    fused = prev_out is not None
    if fused:
        ya_prev, yb_prev, w_out_prev = prev_out
        in_specs += [ya_spec, head_major(HEAD_DIM), full(w_out_prev)]
        operands += [ya_prev, yb_prev, _array(w_out_prev)]
    if n_layers > 1:
        in_specs += [kt_spec(n_layers - 1), v_spec(n_layers - 1)]
        operands += [k_stack, v_stack]
    out_shape = [
        jax.ShapeDtypeStruct((n, WIDTH), BF16),
        jax.ShapeDtypeStruct((batch, N_HEADS, seq, HEAD_DIM), BF16),
        jax.ShapeDtypeStruct((batch, N_HEADS, n_t, tile, KEY_WIDTH), BF16),
        jax.ShapeDtypeStruct((n_layers, batch, WIDTH, seq), F32),
        jax.ShapeDtypeStruct((n_layers, n * N_HEADS, HEAD_DIM), F32),
        jax.ShapeDtypeStruct((batch, N_HEADS, n_t, VT_ROWS, tile), BF16),
        jax.ShapeDtypeStruct((batch, N_HEADS, seq, HEAD_DIM), BF16),
    ]
    out_specs = [
        ya_spec,
        head_major(HEAD_DIM),
        pl.BlockSpec((None, N_HEADS, None, tile, KEY_WIDTH), lambda b, i: (b, 0, i, 0, 0)),
        kt_spec(n_layers),
        v_spec(n_layers),
        pl.BlockSpec((None, N_HEADS, None, VT_ROWS, tile), lambda b, i: (b, 0, i, 0, 0)),
        head_major(HEAD_DIM),
    ]
    if fused:
        out_shape.insert(0, jax.ShapeDtypeStruct((n, D_MODEL), F32))
        out_specs.insert(0, x_spec)
    outs = pl.pallas_call(
        functools.partial(_inproj_kernel, prompt=True, n_prev=2 * (n_layers > 1), fused=fused,
                          n_sub=PROMPT_SUB_TILES),
        grid=(batch, n_t),
        in_specs=in_specs,
        out_specs=out_specs,
        out_shape=out_shape,
        compiler_params=_compiler_params(2),
        name="inproj_prompt",
    )(*operands)
    return outs if fused else (x, *outs)


def _inproj_sample(x, params, *, tile):
    n = x.shape[0]
    row_spec = lambda width: pl.BlockSpec((tile, width), lambda i: (i, 0))
    dtypes = (BF16, BF16, F32, F32, BF16, F32)
    return pl.pallas_call(
        functools.partial(_inproj_kernel, prompt=False, n_prev=0, fused=False, n_sub=1),
        grid=(n // tile,),
        in_specs=[row_spec(D_MODEL)] + [_const_spec(p) for p in params],
        out_specs=[row_spec(WIDTH)] * len(dtypes),
        out_shape=[jax.ShapeDtypeStruct((n, WIDTH), d) for d in dtypes],
        compiler_params=_compiler_params(1),
        name="inproj_sample",
    )(x, *[_array(p) for p in params])


def _outproj_kernel(x_ref, ya_ref, yb_ref, w_ref, o_ref, *, head_major):
    if head_major:
        yb = jnp.concatenate([yb_ref[h] for h in range(N_HEADS)], axis=1)
    else:
        yb = yb_ref[...]
    y = _dot(ya_ref[...], w_ref[:WIDTH, :]) + _dot(yb, w_ref[WIDTH:, :])
    o_ref[...] = x_ref[...] + y


def _outproj(x, ya, yb, w_bf, *, tile):
    n = x.shape[0]
    head_major = yb.ndim == 4
    if head_major:
        n_t = yb.shape[2] // tile
        yb_spec = pl.BlockSpec((None, N_HEADS, tile, HEAD_DIM),
                               lambda i: (i // n_t, 0, i % n_t, 0))
    else:
        yb_spec = pl.BlockSpec((tile, WIDTH), lambda i: (i, 0))
    return pl.pallas_call(
        functools.partial(_outproj_kernel, head_major=head_major),
        grid=(n // tile,),
        in_specs=[
            pl.BlockSpec((tile, D_MODEL), lambda i: (i, 0)),
            pl.BlockSpec((tile, WIDTH), lambda i: (i, 0)),
            yb_spec,
            _const_spec(w_bf),
        ],
        out_specs=pl.BlockSpec((tile, D_MODEL), lambda i: (i, 0)),
        out_shape=jax.ShapeDtypeStruct((n, D_MODEL), F32),
        compiler_params=_compiler_params(1),
        name="outproj",
    )(x, ya, yb, _array(w_bf))


SCORES_AHEAD = 3
VALUES_BEHIND = 2


def _attn_units(seq):
    n_qc = seq // LANE_CHUNK
    qc_per_tile = ATTN_TILE // LANE_CHUNK
    return [(j, qc, half)
            for j in range(seq // ATTN_TILE)
            for qc in range(j * qc_per_tile, n_qc)
            for half in range(2)]


def _attn_kernel(slopes_ref, q_ref, key_ref, vt_ref, gb_ref, sg_ref, lq1_ref, lk1_ref,
                 lq2_ref, lk2_ref, o_ref, qbd_ref, m_ref, acc_ref, *, lam_init):
    seq = q_ref.shape[0]
    t, w = ATTN_TILE, LANE_CHUNK
    assert t == 2 * w
    n_qc = seq // w
    slope2 = slopes_ref[pl.program_id(1)] * LOG2E

    key = lax.broadcasted_iota(jnp.int32, (w, w), 0)
    qry = lax.broadcasted_iota(jnp.int32, (w, w), 1)
    allowed = jnp.right_shift(key, 6) <= jnp.right_shift(qry, 6)
    ahead = jnp.maximum(key - qry, 0).astype(F32)
    dt = jnp.where(allowed, (-2.0 * slope2) * ahead, NEG_INF)

    sub_tile = 16
    lane = lax.broadcasted_iota(jnp.int32, (sub_tile, HEAD_DIM), 1)
    rest = jnp.full((sub_tile, HEAD_DIM), slope2, F32)
    slope_cols = jnp.zeros((sub_tile, HEAD_DIM), F32)
    for piece in range(POS_SPLIT):
        part = rest.astype(BF16).astype(F32)
        slope_cols = jnp.where((lane == piece) | (lane == piece + POS_SPLIT), part, slope_cols)
        rest = rest - part
    slope_cols = jnp.tile(slope_cols.astype(BF16), (seq // sub_tile, 1))

    q = q_ref[...]
    lane = lax.broadcasted_iota(jnp.int32, q.shape, 1)
    zero = jnp.zeros_like(q)
    qbd_ref[:seq, :HEAD_DIM] = jnp.where(lane < HALF_DIM, q, zero)
    qbd_ref[seq:, :HEAD_DIM] = jnp.where(lane >= HALF_DIM, q, zero)
    qbd_ref[:seq, HEAD_DIM:] = slope_cols
    qbd_ref[seq:, HEAD_DIM:] = slope_cols

    lam = _lam_value(lq1_ref, lk1_ref, lq2_ref, lk2_ref, lam_init)

    def n_keys(j, qc):
        return w if qc * w == j * t else t

    def scores(unit):
        j, qc, half = unit
        c = half * n_qc + qc
        return _dot_nt(key_ref[j, :n_keys(j, qc), :], qbd_ref[c * w:(c + 1) * w, :])

    def softmax(unit, st):
        j, qc, half = unit
        lanes = slice((half * n_qc + qc) * w, (half * n_qc + qc + 1) * w)
        nk = n_keys(j, qc)
        if qc * w < (j + 1) * t:
            st = st + dt if nk == w else jnp.concatenate([st[:w], st[w:] + dt], axis=0)
        m_cur = jnp.max(st, axis=0, keepdims=True)
        if j == 0:
            m_new, alpha = m_cur, None
        else:
            m_old = m_ref[:, lanes]
            m_new = jnp.maximum(m_old, m_cur)
            alpha = jnp.exp2(m_old - m_new)
        m_ref[:, lanes] = m_new
        return jnp.exp2(st - m_new).astype(BF16), alpha

    def values(unit, p, alpha):
        j, qc, half = unit
        lanes = slice((half * n_qc + qc) * w, (half * n_qc + qc + 1) * w)
        pv = _dot(vt_ref[j, :, :n_keys(j, qc)], p)
        acc_ref[:, lanes] = pv if j == 0 else alpha * acc_ref[:, lanes] + pv

    def normalized(lanes):
        return acc_ref[:HEAD_DIM, lanes] * (1.0 / acc_ref[HEAD_DIM:HEAD_DIM + 1, lanes])

    def finalize(qc):
        rows = slice(qc * w, (qc + 1) * w)
        lanes1 = slice(qc * w, (qc + 1) * w)
        lanes2 = slice((n_qc + qc) * w, (n_qc + qc + 1) * w)
        ot = normalized(lanes1) - lam * normalized(lanes2)
        ot = ot * lax.rsqrt(jnp.mean(ot * ot, axis=0, keepdims=True) + NORM_EPS)
        o = (ot.T * sg_ref[...]) * (1.0 - lam_init)
        o_ref[rows, :] = (o * gb_ref[rows, :].astype(F32)).astype(BF16)

    units = _attn_units(seq)
    pending_scores = {u: scores(units[u]) for u in range(min(SCORES_AHEAD, len(units)))}
    pending_values = collections.deque()

    def run_oldest_values():
        unit_done, p, alpha = pending_values.popleft()
        values(unit_done, p, alpha)
        j_done, qc_done, half_done = unit_done
        if half_done == 1 and qc_done * w < (j_done + 1) * t:
            finalize(qc_done)

    for u, unit in enumerate(units):
        if u + SCORES_AHEAD < len(units):
            pending_scores[u + SCORES_AHEAD] = scores(units[u + SCORES_AHEAD])
        pending_values.append((unit, *softmax(unit, pending_scores.pop(u))))
        if len(pending_values) > VALUES_BEHIND:
            run_oldest_values()
    while pending_values:
        run_oldest_values()


def _prompt_attention(q, keys, vt, gb, slopes, sg, lq1, lk1, lq2, lk2, *, lam_init):
    batch, _, seq, _ = q.shape
    t = ATTN_TILE
    n_tiles = seq // t
    seq_spec = pl.BlockSpec((None, None, seq, HEAD_DIM), lambda b, h: (b, h, 0, 0))
    vec = _const_spec
    return pl.pallas_call(
        functools.partial(_attn_kernel, lam_init=lam_init),
        grid=(batch, N_HEADS),
        in_specs=[
            pl.BlockSpec(memory_space=pltpu.SMEM),
            seq_spec,
            pl.BlockSpec((None, None, n_tiles, t, KEY_WIDTH), lambda b, h: (b, h, 0, 0, 0)),
            pl.BlockSpec((None, None, n_tiles, VT_ROWS, t), lambda b, h: (b, h, 0, 0, 0)),
            seq_spec, vec(sg), vec(lq1), vec(lk1), vec(lq2), vec(lk2)],
        out_specs=seq_spec,
        out_shape=jax.ShapeDtypeStruct((batch, N_HEADS, seq, HEAD_DIM), BF16),
        scratch_shapes=[
            pltpu.VMEM((2 * seq, KEY_WIDTH), BF16),
            pltpu.VMEM((1, 2 * seq), F32),
            pltpu.VMEM((VT_ROWS, 2 * seq), F32),
        ],
        compiler_params=_compiler_params(2),
        name="prompt_attn",
    )(slopes, q, keys, vt, gb, *[_array(p) for p in (sg, lq1, lk1, lq2, lk2)])


def _sample_attn_kernel(q_ref, kn_ref, vn_ref, ckt_ref, cv_ref, gb_ref, sg_ref,
                        lq1_ref, lk1_ref, lq2_ref, lk2_ref, o_ref, *, lam_init, past_len):
    nq = q_ref.shape[0]
    per_head = 2 * nq
    n_rows = N_HEADS * per_head
    lam = _lam_value(lq1_ref, lk1_ref, lq2_ref, lk2_ref, lam_init)

    def alibi(n_cols, key_pos0):
        row = lax.broadcasted_iota(jnp.int32, (n_rows, n_cols), 0)
        col = lax.broadcasted_iota(jnp.int32, (n_rows, n_cols), 1)
        dist = jnp.abs(past_len + lax.rem(row, nq) - (key_pos0 + col)).astype(F32)
        slope = jnp.zeros((n_rows, n_cols), F32)
        for h in range(N_HEADS):
            slope = jnp.where(lax.div(row, per_head) == h, ALIBI_SLOPES[h] * LOG2E, slope)
        return slope * dist

    zpad = jnp.zeros((HEAD_DIM - nq, HEAD_DIM), BF16)
    s_past, s_new, v_new = [], [], []
    for h in range(N_HEADS):
        cols = _head_cols(h)
        q = q_ref[:, cols]
        lane = lax.broadcasted_iota(jnp.int32, q.shape, 1)
        zero = jnp.zeros_like(q)
        qbd = jnp.concatenate([jnp.where(lane < HALF_DIM, q, zero),
                               jnp.where(lane >= HALF_DIM, q, zero)], axis=0)
        s_past.append(_dot(qbd, ckt_ref[cols, :].astype(BF16)))
        s_new.append(_dot_nt(qbd, jnp.concatenate([kn_ref[:, cols].astype(BF16), zpad], axis=0)))
        v_new.append(jnp.concatenate([vn_ref[:, cols].astype(BF16), zpad], axis=0))
    s_past = jnp.concatenate(s_past, axis=0) - alibi(past_len, 0)
    s_new = jnp.concatenate(s_new, axis=0) - alibi(HEAD_DIM, past_len)
    real_new = lax.broadcasted_iota(jnp.int32, s_new.shape, 1) < nq
    s_new = jnp.where(real_new, s_new, NEG_INF)
    m = jnp.maximum(jnp.max(s_past, axis=-1, keepdims=True),
                    jnp.max(s_new, axis=-1, keepdims=True))
    p_past = jnp.exp2(s_past - m)
    p_new = jnp.exp2(s_new - m)
    inv_l = 1.0 / (jnp.sum(p_past, axis=-1, keepdims=True)
                   + jnp.sum(p_new, axis=-1, keepdims=True))
    p_past = p_past.astype(BF16)
    p_new = p_new.astype(BF16)
    outs = []
    for h in range(N_HEADS):
        rows = slice(h * per_head, (h + 1) * per_head)
        v_past = cv_ref[pl.ds(h, past_len, stride=N_HEADS), :].astype(BF16)
        acc = (_dot(p_past[rows], v_past) + _dot(p_new[rows], v_new[h])) * inv_l[rows]
        o = acc[:nq] - lam * acc[nq:]
        outs.append(o * lax.rsqrt(jnp.mean(o * o, axis=-1, keepdims=True) + NORM_EPS))
    o = jnp.concatenate(outs, axis=1) * jnp.tile(sg_ref[...], (1, N_HEADS)) * (1.0 - lam_init)
    o_ref[...] = (o * gb_ref[...].astype(F32)).astype(BF16)


def _sample_attention(q, k_new, v_new, cache_kt, cache_v, gb, sg, lq1, lk1, lq2, lk2,
                      *, layer, n_streams, n_new, lam_init):
    past_len = cache_kt.shape[3]
    new_spec = pl.BlockSpec((None, n_new, WIDTH), lambda b: (b, 0, 0))
    vec = _const_spec
    r3 = lambda a: a.reshape(n_streams, n_new, WIDTH)
    out = pl.pallas_call(
        functools.partial(_sample_attn_kernel, lam_init=lam_init, past_len=past_len),
        grid=(n_streams,),
        in_specs=[
            new_spec, new_spec, new_spec,
            pl.BlockSpec((None, None, WIDTH, past_len), lambda b: (layer, b, 0, 0)),
            pl.BlockSpec((None, None, past_len * N_HEADS, HEAD_DIM), lambda b: (layer, b, 0, 0)),
            new_spec, vec(sg), vec(lq1), vec(lk1), vec(lq2), vec(lk2)],
        out_specs=new_spec,
        out_shape=jax.ShapeDtypeStruct((n_streams, n_new, WIDTH), BF16),
        compiler_params=_compiler_params(1),
        name="sample_attn",
    )(r3(q), r3(k_new), r3(v_new), cache_kt, cache_v, r3(gb),
      *[_array(p) for p in (sg, lq1, lk1, lq2, lk2)])
    return out.reshape(n_streams * n_new, WIDTH)


def _group_mean_matrix(group):
    idx = jnp.arange(GROUP_MEAN_WIDTH) // group
    return jnp.where(idx[:, None] == idx[None, :], 1.0 / group, 0.0).astype(BF16)


def kernel(x_prompt, x_sample, cache_k, cache_v, norm_g, w_in, sgu_norm_g, sgu_w, sgu_b,
           q_norm_g, k_norm_g, lambda_q1, lambda_k1, lambda_q2, lambda_k2, subln_g, w_out):
    depth = w_in.shape[0]
    batch, seq, _ = x_prompt.shape
    n_streams, n_new, _ = x_sample.shape
    past_len = cache_k.shape[2]
    assert seq % ATTN_TILE == 0
    assert SGU_CHUNK % n_new == 0 and past_len % CHUNK == 0 and n_new <= CHUNK
    sample_tile = min(n_streams * n_new, TOKEN_TILE)
    assert sample_tile % SGU_CHUNK == 0 and (n_streams * n_new) % sample_tile == 0

    gm128 = _group_mean_matrix(HEAD_DIM)
    gm64 = _group_mean_matrix(HALF_DIM)
    slopes = jnp.asarray(ALIBI_SLOPES, F32)
    tril = jnp.tril(jnp.ones((SGU_CHUNK, SGU_CHUNK), F32))
    tril_new = jnp.tril(jnp.ones((n_new, n_new), F32))
    streams_per_chunk = SGU_CHUNK // n_new
    eye = jnp.eye(streams_per_chunk, dtype=F32)
    cache_kt = jnp.transpose(cache_k, (0, 1, 3, 4, 5, 2)).reshape(depth, n_streams, WIDTH, past_len)
    cache_vr = cache_v.reshape(depth, n_streams, past_len * N_HEADS, HEAD_DIM)

    rows = lambda a: a.reshape(depth, 1, -1).astype(F32)
    w_in_bf = w_in.astype(BF16)
    w_out_bf = w_out.astype(BF16)
    ng = rows(norm_g)
    sgug = rows(sgu_norm_g)
    gq = rows(jnp.tile(q_norm_g, (1, WIDTH // HALF_DIM)))
    gk = rows(jnp.tile(k_norm_g, (1, WIDTH // HALF_DIM)))
    sg = rows(subln_g)
    lam_vecs = tuple(rows(a) for a in (lambda_q1, lambda_k1, lambda_q2, lambda_k2))
    sguw_p = (sgu_w * tril).astype(BF16)
    sgub_p = jnp.broadcast_to(sgu_b[..., None], (depth, N_HEADS, SGU_CHUNK, HEAD_DIM))
    w_new = sgu_w[:, :, :n_new, :n_new] * tril_new
    sguw_s = jnp.einsum("ab,lhts->lhatbs", eye, w_new).reshape(
        depth, N_HEADS, SGU_CHUNK, SGU_CHUNK).astype(BF16)
    sgub_s = jnp.broadcast_to(
        jnp.tile(sgu_b[:, :, :n_new], (1, 1, streams_per_chunk))[..., None],
        (depth, N_HEADS, SGU_CHUNK, HEAD_DIM))

    xp = x_prompt.reshape(batch * seq, D_MODEL)
    xs = x_sample.reshape(n_streams * n_new, D_MODEL)
    k_stack = v_stack = prompt_out = None
    ks_rows, vs_rows, sgu_rows = [], [], []
    for i in range(depth):
        lam_init = _lam_init(i)
        of_layer = lambda a: _LayerParam(a, i)
        params_p = (*map(of_layer, (ng, w_in_bf, sgug, sguw_p, sgub_p, gq, gk)), gm128, gm64)
        params_s = (*map(of_layer, (ng, w_in_bf, sgug, sguw_s, sgub_s, gq, gk)), gm128, gm64)
        attn_params = tuple(map(of_layer, (sg, *lam_vecs)))

        xp, ya, q, keys, k_stack, v_stack, vt, gb = _inproj_prompt(
            xp, params_p, prompt_out, k_stack, v_stack, batch=batch, seq=seq)
        yb = _prompt_attention(q, keys, vt, gb, slopes, *attn_params, lam_init=lam_init)
        prompt_out = (ya, yb, of_layer(w_out_bf))

        ya, q, k, v, gb, va = _inproj_sample(xs, params_s, tile=sample_tile)
        yb = _sample_attention(q, k, v, cache_kt, cache_vr, gb, *attn_params, layer=i,
                               n_streams=n_streams, n_new=n_new, lam_init=lam_init)
        xs = _outproj(xs, ya, yb, of_layer(w_out_bf), tile=sample_tile)
        ks_rows.append(k)
        vs_rows.append(v)
        sgu_rows.append(va)

    xp = _outproj(xp, *prompt_out, tile=OUTPROJ_TILE)
    new_k_prompt = jnp.transpose(
        k_stack.reshape(depth, batch, N_HEADS, 2, HALF_DIM, seq), (0, 1, 5, 2, 3, 4))
    return (
        xp.reshape(batch, seq, D_MODEL),
        xs.reshape(n_streams, n_new, D_MODEL),
        new_k_prompt,
        v_stack.reshape(depth, batch, seq, N_HEADS, HEAD_DIM),
        jnp.stack(ks_rows).reshape(depth, n_streams, n_new, N_HEADS, 2, HALF_DIM),
        jnp.stack(vs_rows).reshape(depth, n_streams, n_new, N_HEADS, HEAD_DIM),
        jnp.stack(sgu_rows).reshape(depth, n_streams, n_new, WIDTH),
    )
```

```python
import collections
import functools
import math
from typing import NamedTuple

import jax
import jax.numpy as jnp
from jax import lax
from jax.experimental import pallas as pl
from jax.experimental.pallas import tpu as pltpu

F32 = jnp.float32
BF16 = jnp.bfloat16

D_MODEL = 1024
N_HEADS = 4
HEAD_DIM = 128
HALF_DIM = 64
WIDTH = N_HEADS * HEAD_DIM
CHUNK = 64
SGU_CHUNK = 128
NORM_EPS = 1e-6
NEG_INF = -1e30
QK_SCALE = HALF_DIM ** -0.5
LOG2E = math.log2(math.e)
ALIBI_SLOPES = tuple(2.0 ** (-8.0 * (h + 1) / N_HEADS) for h in range(N_HEADS))

COL_U, COL_VA, COL_GA, COL_Q, COL_K, COL_V, COL_GB = (i * WIDTH for i in range(7))

VMEM_LIMIT_BYTES = 56 * 1024 * 1024

ATTN_TILE = 512
TOKEN_TILE = ATTN_TILE
PROMPT_SUB_TILES = 1
SAMPLE_STREAMS_PER_STEP = 2
OUTPROJ_TILE = 2048
LANE_CHUNK = 256
KEY_WIDTH = 2 * HEAD_DIM
GROUP_MEAN_WIDTH = 256
POS_SPLIT = 3
VT_ROWS = HEAD_DIM + 16


def _lam_init(layer_idx):
    return 0.8 - 0.6 * math.exp(-0.3 * layer_idx)


def _compiler_params(n_axes):
    return pltpu.CompilerParams(
        dimension_semantics=("arbitrary",) * n_axes,
        vmem_limit_bytes=VMEM_LIMIT_BYTES,
    )


class _LayerParam(NamedTuple):
    array: jax.Array
    layer: int


def _array(p):
    return p.array if isinstance(p, _LayerParam) else p


def _const_spec(p, single_buffer=False):
    if isinstance(p, _LayerParam):
        block = (None, *p.array.shape[1:])
        index = (p.layer,) + (0,) * (p.array.ndim - 1)
    else:
        block, index = p.shape, (0,) * p.ndim
    kwargs = dict(pipeline_mode=pl.Buffered(1)) if single_buffer else {}
    return pl.BlockSpec(block, lambda *grid_idx: index, **kwargs)


def _dot(a, b):
    return jnp.dot(a, b, preferred_element_type=F32)


def _dot_nt(a, b):
    return lax.dot_general(a, b, (((1,), (1,)), ((), ())), preferred_element_type=F32)


def _lam_value(lq1_ref, lk1_ref, lq2_ref, lk2_ref, lam_init):
    d1 = jnp.sum(lq1_ref[...] * lk1_ref[...], axis=-1, keepdims=True)
    d2 = jnp.sum(lq2_ref[...] * lk2_ref[...], axis=-1, keepdims=True)
    return jnp.exp(d1) - jnp.exp(d2) + lam_init


def _head_cols(h):
    return slice(h * HEAD_DIM, (h + 1) * HEAD_DIM)


def _inproj_kernel(*refs, prompt, n_prev, fused, n_sub):
    (x_ref, ng_ref, w_ref, sgug_ref, sguw_ref, sgub_ref, gq_ref, gk_ref,
     gm128_ref, gm64_ref) = refs[:10]
    n_in = 10
    if fused:
        ya_prev_ref, yb_prev_ref, wo_ref = refs[n_in:n_in + 3]
        n_in += 3
    prev = refs[n_in:n_in + n_prev]
    outs = refs[n_in + n_prev:]
    if fused:
        xo_ref, outs = outs[0], outs[1:]
    tile = x_ref.shape[0]
    sub = tile // n_sub

    def group_rms_scale(z, gm_ref):
        sq = (z * z).astype(BF16)
        gw = gm_ref.shape[0]
        mean_sq = jnp.concatenate(
            [_dot(sq[:, c:c + gw], gm_ref[...]) for c in range(0, WIDTH, gw)], axis=1)
        return lax.rsqrt(mean_sq + NORM_EPS)

    if prompt:
        ya_ref, q_ref, key_ref, kt_ref, v_ref, vt_ref, gb_ref = outs
        last = kt_ref.shape[0] - 1
        if n_prev:
            kt_prev_ref, v_prev_ref = prev
            kt_ref[:last] = kt_prev_ref[...]
            v_ref[:last] = v_prev_ref[...]
        for h in range(N_HEADS):
            vt_ref[h, HEAD_DIM:, :] = jnp.ones((VT_ROWS - HEAD_DIM, tile), BF16)
    else:
        ya_ref, q_ref, k_ref, v_ref, gb_ref, va_ref = outs

    for s in range(n_sub):
        rows = slice(s * sub, (s + 1) * sub)
        x = x_ref[rows, :]
        if fused:
            yb_prev = jnp.concatenate([yb_prev_ref[h, rows, :] for h in range(N_HEADS)], axis=1)
            x = x + (_dot(ya_prev_ref[rows, :], wo_ref[:WIDTH, :])
                     + _dot(yb_prev, wo_ref[WIDTH:, :]))
            xo_ref[rows, :] = x
        hb = (x * ng_ref[...]).astype(BF16)
        row_scale = jnp.broadcast_to(
            lax.rsqrt(jnp.mean(x * x, axis=-1, keepdims=True) + NORM_EPS), (sub, WIDTH))

        def proj(col):
            return _dot(hb, w_ref[:, col:col + WIDTH]) * row_scale

        va = proj(COL_VA)
        if not prompt:
            va_ref[rows, :] = va
        vn = ((va * group_rms_scale(va, gm128_ref)) * sgug_ref[...]).astype(BF16)
        gate = proj(COL_U) * jax.nn.silu(proj(COL_GA))
        for c in range(sub // SGU_CHUNK):
            crows = slice(c * SGU_CHUNK, (c + 1) * SGU_CHUNK)
            orows = slice(s * sub + c * SGU_CHUNK, s * sub + (c + 1) * SGU_CHUNK)
            for g in range(N_HEADS):
                cols = _head_cols(g)
                mixed = _dot(sguw_ref[g], vn[crows, cols]) + sgub_ref[g]
                ya_ref[orows, cols] = (gate[crows, cols] * mixed).astype(BF16)

        zq = proj(COL_Q)
        qn = ((zq * group_rms_scale(zq, gm64_ref)) * gq_ref[...]
              * (QK_SCALE * LOG2E)).astype(BF16)
        zk = proj(COL_K)
        kn = (zk * group_rms_scale(zk, gm64_ref)) * gk_ref[...]
        zv = proj(COL_V)
        gb = jax.nn.silu(proj(COL_GB)).astype(BF16)

        if prompt:
            pos = (lax.broadcasted_iota(jnp.int32, (sub, HEAD_DIM), 0)
                   + (pl.program_id(1) * tile + s * sub))
            lane = lax.broadcasted_iota(jnp.int32, (sub, HEAD_DIM), 1)
            lo = jnp.bitwise_and(pos, CHUNK - 1)
            pos_cols = jnp.where(
                lane < POS_SPLIT, (pos - lo).astype(F32),
                jnp.where(lane < 2 * POS_SPLIT, lo.astype(F32), 0.0)).astype(BF16)
            kt_ref[last, :, rows] = kn.T
            for h in range(N_HEADS):
                cols = _head_cols(h)
                q_ref[h, rows, :] = qn[:, cols]
                gb_ref[h, rows, :] = gb[:, cols]
                key_ref[h, rows, :HEAD_DIM] = kn[:, cols].astype(BF16)
                key_ref[h, rows, HEAD_DIM:] = pos_cols
                vt_ref[h, :HEAD_DIM, rows] = zv[:, cols].T.astype(BF16)
                v_ref[last, pl.ds(s * sub * N_HEADS + h, sub, stride=N_HEADS), :] = zv[:, cols]
        else:
            q_ref[rows, :] = qn
            k_ref[rows, :] = kn
            v_ref[rows, :] = zv
            gb_ref[rows, :] = gb


def _inproj_prompt(x, params, prev_out, k_stack, v_stack, *, batch, seq):
    tile = TOKEN_TILE
    n_t = seq // tile
    n = batch * seq
    n_layers = 1 if k_stack is None else k_stack.shape[0] + 1
    full = lambda p: _const_spec(p, single_buffer=True)
    head_major = lambda width: pl.BlockSpec((None, N_HEADS, tile, width),
                                            lambda b, i: (b, 0, i, 0))
    kt_spec = lambda layers: pl.BlockSpec((layers, None, WIDTH, tile),
                                          lambda b, i: (0, b, 0, i))
    v_spec = lambda layers: pl.BlockSpec((layers, tile * N_HEADS, HEAD_DIM),
                                         lambda b, i: (0, b * n_t + i, 0))
    x_spec = pl.BlockSpec((tile, D_MODEL), lambda b, i: (b * n_t + i, 0))
    ya_spec = pl.BlockSpec((tile, WIDTH), lambda b, i: (b * n_t + i, 0))
    in_specs = [x_spec] + [full(p) for p in params]
    operands = [x] + [_array(p) for p in params]
    fused = prev_out is not None
    if fused:
        ya_prev, yb_prev, w_out_prev = prev_out
        in_specs += [ya_spec, head_major(HEAD_DIM), full(w_out_prev)]
        operands += [ya_prev, yb_prev, _array(w_out_prev)]
    if n_layers > 1:
        in_specs += [kt_spec(n_layers - 1), v_spec(n_layers - 1)]
        operands += [k_stack, v_stack]
    out_shape = [
        jax.ShapeDtypeStruct((n, WIDTH), BF16),
        jax.ShapeDtypeStruct((batch, N_HEADS, seq, HEAD_DIM), BF16),
        jax.ShapeDtypeStruct((batch, N_HEADS, n_t, tile, KEY_WIDTH), BF16),
        jax.ShapeDtypeStruct((n_layers, batch, WIDTH, seq), F32),
        jax.ShapeDtypeStruct((n_layers, n * N_HEADS, HEAD_DIM), F32),
        jax.ShapeDtypeStruct((batch, N_HEADS, n_t, VT_ROWS, tile), BF16),
        jax.ShapeDtypeStruct((batch, N_HEADS, seq, HEAD_DIM), BF16),
    ]
    out_specs = [
        ya_spec,
        head_major(HEAD_DIM),
        pl.BlockSpec((None, N_HEADS, None, tile, KEY_WIDTH), lambda b, i: (b, 0, i, 0, 0)),
        kt_spec(n_layers),
        v_spec(n_layers),
        pl.BlockSpec((None, N_HEADS, None, VT_ROWS, tile), lambda b, i: (b, 0, i, 0, 0)),
        head_major(HEAD_DIM),
    ]
    if fused:
        out_shape.insert(0, jax.ShapeDtypeStruct((n, D_MODEL), F32))
        out_specs.insert(0, x_spec)
    outs = pl.pallas_call(
        functools.partial(_inproj_kernel, prompt=True, n_prev=2 * (n_layers > 1), fused=fused,
                          n_sub=PROMPT_SUB_TILES),
        grid=(batch, n_t),
        in_specs=in_specs,
        out_specs=out_specs,
        out_shape=out_shape,
        compiler_params=_compiler_params(2),
        name="inproj_prompt",
    )(*operands)
    return outs if fused else (x, *outs)


def _inproj_sample(x, params, *, tile):
    n = x.shape[0]
    row_spec = lambda width: pl.BlockSpec((tile, width), lambda i: (i, 0))
    dtypes = (BF16, BF16, F32, F32, BF16, F32)
    return pl.pallas_call(
        functools.partial(_inproj_kernel, prompt=False, n_prev=0, fused=False, n_sub=1),
        grid=(n // tile,),
        in_specs=[row_spec(D_MODEL)] + [_const_spec(p) for p in params],
        out_specs=[row_spec(WIDTH)] * len(dtypes),
        out_shape=[jax.ShapeDtypeStruct((n, WIDTH), d) for d in dtypes],
        compiler_params=_compiler_params(1),
        name="inproj_sample",
    )(x, *[_array(p) for p in params])


def _outproj_kernel(x_ref, ya_ref, yb_ref, w_ref, o_ref, *, head_major):
    if head_major:
        yb = jnp.concatenate([yb_ref[h] for h in range(N_HEADS)], axis=1)
    else:
        yb = yb_ref[...]
    y = _dot(ya_ref[...], w_ref[:WIDTH, :]) + _dot(yb, w_ref[WIDTH:, :])
    o_ref[...] = x_ref[...] + y


def _outproj(x, ya, yb, w_bf, *, tile):
    n = x.shape[0]
    head_major = yb.ndim == 4
    if head_major:
        n_t = yb.shape[2] // tile
        yb_spec = pl.BlockSpec((None, N_HEADS, tile, HEAD_DIM),
                               lambda i: (i // n_t, 0, i % n_t, 0))
    else:
        yb_spec = pl.BlockSpec((tile, WIDTH), lambda i: (i, 0))
    return pl.pallas_call(
        functools.partial(_outproj_kernel, head_major=head_major),
        grid=(n // tile,),
        in_specs=[
            pl.BlockSpec((tile, D_MODEL), lambda i: (i, 0)),
            pl.BlockSpec((tile, WIDTH), lambda i: (i, 0)),
            yb_spec,
            _const_spec(w_bf),
        ],
        out_specs=pl.BlockSpec((tile, D_MODEL), lambda i: (i, 0)),
        out_shape=jax.ShapeDtypeStruct((n, D_MODEL), F32),
        compiler_params=_compiler_params(1),
        name="outproj",
    )(x, ya, yb, _array(w_bf))


SCORES_AHEAD = 3
VALUES_BEHIND = 2


def _attn_units(seq):
    n_qc = seq // LANE_CHUNK
    qc_per_tile = ATTN_TILE // LANE_CHUNK
    return [(j, qc, half)
            for j in range(seq // ATTN_TILE)
            for qc in range(j * qc_per_tile, n_qc)
            for half in range(2)]


def _attn_kernel(slopes_ref, q_ref, key_ref, vt_ref, gb_ref, sg_ref, lq1_ref, lk1_ref,
                 lq2_ref, lk2_ref, o_ref, qbd_ref, s_ref, m_ref, acc_ref, *, lam_init):
    seq = q_ref.shape[0]
    t, w = ATTN_TILE, LANE_CHUNK
    assert t == 2 * w
    n_qc = seq // w
    slope2 = slopes_ref[pl.program_id(1)] * LOG2E

    key = lax.broadcasted_iota(jnp.int32, (w, w), 0)
    qry = lax.broadcasted_iota(jnp.int32, (w, w), 1)
    allowed = jnp.right_shift(key, 6) <= jnp.right_shift(qry, 6)
    ahead = jnp.maximum(key - qry, 0).astype(F32)
    dt = jnp.where(allowed, (-2.0 * slope2) * ahead, NEG_INF)

    sub_tile = 16
    lane = lax.broadcasted_iota(jnp.int32, (sub_tile, HEAD_DIM), 1)
    rest = jnp.full((sub_tile, HEAD_DIM), slope2, F32)
    slope_cols = jnp.zeros((sub_tile, HEAD_DIM), F32)
    for piece in range(POS_SPLIT):
        part = rest.astype(BF16).astype(F32)
        slope_cols = jnp.where((lane == piece) | (lane == piece + POS_SPLIT), part, slope_cols)
        rest = rest - part
    slope_cols = jnp.tile(slope_cols.astype(BF16), (w // sub_tile, 1))
    q_lane = lax.broadcasted_iota(jnp.int32, (w, HEAD_DIM), 1)

    lam = _lam_value(lq1_ref, lk1_ref, lq2_ref, lk2_ref, lam_init)

    def n_keys(j, qc):
        return w if qc * w == j * t else t

    def stacked_queries(qc, half):
        rows = slice((half * n_qc + qc) * w, (half * n_qc + qc + 1) * w)
        q = q_ref[qc * w:(qc + 1) * w, :]
        in_half = (q_lane >= HALF_DIM) if half else (q_lane < HALF_DIM)
        qbd_ref[rows, :HEAD_DIM] = jnp.where(in_half, q, jnp.zeros_like(q))
        qbd_ref[rows, HEAD_DIM:] = slope_cols
        return rows

    def scores(unit):
        j, qc, half = unit
        if j == 0:
            rows = stacked_queries(qc, half)
        else:
            rows = slice((half * n_qc + qc) * w, (half * n_qc + qc + 1) * w)
        return _dot_nt(key_ref[j, :n_keys(j, qc), :], qbd_ref[rows, :])

    def softmax(unit, st):
        j, qc, half = unit
        lanes = slice((half * n_qc + qc) * w, (half * n_qc + qc + 1) * w)
        nk = n_keys(j, qc)
        if qc * w < (j + 1) * t:
            st = st + dt if nk == w else jnp.concatenate([st[:w], st[w:] + dt], axis=0)
        m_cur = jnp.max(st, axis=0, keepdims=True)
        if j == 0:
            m_new, alpha = m_cur, None
        else:
            m_old = m_ref[:, lanes]
            m_new = jnp.maximum(m_old, m_cur)
            alpha = jnp.exp2(m_old - m_new)
        m_ref[:, lanes] = m_new
        return jnp.exp2(st - m_new).astype(BF16), alpha

    def values(unit, p, alpha):
        j, qc, half = unit
        lanes = slice((half * n_qc + qc) * w, (half * n_qc + qc + 1) * w)
        pv = _dot(vt_ref[j, :, :n_keys(j, qc)], p)
        acc_ref[:, lanes] = pv if j == 0 else alpha * acc_ref[:, lanes] + pv

    def normalized(lanes):
        return acc_ref[:HEAD_DIM, lanes] * (1.0 / acc_ref[HEAD_DIM:HEAD_DIM + 1, lanes])

    def finalize(qc):
        rows = slice(qc * w, (qc + 1) * w)
        lanes1 = slice(qc * w, (qc + 1) * w)
        lanes2 = slice((n_qc + qc) * w, (n_qc + qc + 1) * w)
        ot = normalized(lanes1) - lam * normalized(lanes2)
        ot = ot * lax.rsqrt(jnp.mean(ot * ot, axis=0, keepdims=True) + NORM_EPS)
        o = (ot.T * sg_ref[...]) * (1.0 - lam_init)
        o_ref[rows, :] = (o * gb_ref[rows, :].astype(F32)).astype(BF16)

    units = _attn_units(seq)
    n_slots = s_ref.shape[0]

    def emit_scores(u):
        nk = n_keys(*units[u][:2])
        s_ref[u % n_slots, :nk, :] = scores(units[u])

    def load_scores(u):
        return s_ref[u % n_slots, :n_keys(*units[u][:2]), :]

    for u in range(min(SCORES_AHEAD, len(units))):
        emit_scores(u)
    pending_values = collections.deque()

    def run_oldest_values():
        unit_done, p, alpha = pending_values.popleft()
        values(unit_done, p, alpha)
        j_done, qc_done, half_done = unit_done
        if half_done == 1 and qc_done * w < (j_done + 1) * t:
            finalize(qc_done)

    for u, unit in enumerate(units):
        if u + SCORES_AHEAD < len(units):
            emit_scores(u + SCORES_AHEAD)
        pending_values.append((unit, *softmax(unit, load_scores(u))))
        if len(pending_values) > VALUES_BEHIND:
            run_oldest_values()
    while pending_values:
        run_oldest_values()


def _prompt_attention(q, keys, vt, gb, slopes, sg, lq1, lk1, lq2, lk2, *, lam_init):
    batch, _, seq, _ = q.shape
    t = ATTN_TILE
    n_tiles = seq // t
    seq_spec = pl.BlockSpec((None, None, seq, HEAD_DIM), lambda b, h: (b, h, 0, 0))
    vec = _const_spec
    return pl.pallas_call(
        functools.partial(_attn_kernel, lam_init=lam_init),
        grid=(batch, N_HEADS),
        in_specs=[
            pl.BlockSpec(memory_space=pltpu.SMEM),
            seq_spec,
            pl.BlockSpec((None, None, n_tiles, t, KEY_WIDTH), lambda b, h: (b, h, 0, 0, 0)),
            pl.BlockSpec((None, None, n_tiles, VT_ROWS, t), lambda b, h: (b, h, 0, 0, 0)),
            seq_spec, vec(sg), vec(lq1), vec(lk1), vec(lq2), vec(lk2)],
        out_specs=seq_spec,
        out_shape=jax.ShapeDtypeStruct((batch, N_HEADS, seq, HEAD_DIM), BF16),
        scratch_shapes=[
            pltpu.VMEM((2 * seq, KEY_WIDTH), BF16),
            pltpu.VMEM((SCORES_AHEAD + 1, t, LANE_CHUNK), F32),
            pltpu.VMEM((1, 2 * seq), F32),
            pltpu.VMEM((VT_ROWS, 2 * seq), F32),
        ],
        compiler_params=_compiler_params(2),
        name="prompt_attn",
    )(slopes, q, keys, vt, gb, *[_array(p) for p in (sg, lq1, lk1, lq2, lk2)])


def _sample_attn_kernel(q_ref, kn_ref, vn_ref, ckt_ref, cv_ref, gb_ref, *rest, **static):
    for s in range(q_ref.shape[0]):
        _sample_attn_stream(q_ref.at[s], kn_ref.at[s], vn_ref.at[s], ckt_ref.at[s], cv_ref.at[s],
                            gb_ref.at[s], *rest[:-1], rest[-1].at[s], **static)


def _sample_attn_stream(q_ref, kn_ref, vn_ref, ckt_ref, cv_ref, gb_ref, sg_ref,
                        lq1_ref, lk1_ref, lq2_ref, lk2_ref, o_ref, *, lam_init, past_len):
    nq = q_ref.shape[0]
    per_head = 2 * nq
    n_rows = N_HEADS * per_head
    lam = _lam_value(lq1_ref, lk1_ref, lq2_ref, lk2_ref, lam_init)

    def alibi(n_cols, key_pos0):
        row = lax.broadcasted_iota(jnp.int32, (n_rows, n_cols), 0)
        col = lax.broadcasted_iota(jnp.int32, (n_rows, n_cols), 1)
        dist = jnp.abs(past_len + lax.rem(row, nq) - (key_pos0 + col)).astype(F32)
        slope = jnp.zeros((n_rows, n_cols), F32)
        for h in range(N_HEADS):
            slope = jnp.where(lax.div(row, per_head) == h, ALIBI_SLOPES[h] * LOG2E, slope)
        return slope * dist

    zpad = jnp.zeros((HEAD_DIM - nq, HEAD_DIM), BF16)
    s_past, s_new, v_new = [], [], []
    for h in range(N_HEADS):
        cols = _head_cols(h)
        q = q_ref[:, cols]
        lane = lax.broadcasted_iota(jnp.int32, q.shape, 1)
        zero = jnp.zeros_like(q)
        qbd = jnp.concatenate([jnp.where(lane < HALF_DIM, q, zero),
                               jnp.where(lane >= HALF_DIM, q, zero)], axis=0)
        s_past.append(_dot(qbd, ckt_ref[cols, :].astype(BF16)))
        s_new.append(_dot_nt(qbd, jnp.concatenate([kn_ref[:, cols].astype(BF16), zpad], axis=0)))
        v_new.append(jnp.concatenate([vn_ref[:, cols].astype(BF16), zpad], axis=0))
    s_past = jnp.concatenate(s_past, axis=0) - alibi(past_len, 0)
    s_new = jnp.concatenate(s_new, axis=0) - alibi(HEAD_DIM, past_len)
    real_new = lax.broadcasted_iota(jnp.int32, s_new.shape, 1) < nq
    s_new = jnp.where(real_new, s_new, NEG_INF)
    m = jnp.maximum(jnp.max(s_past, axis=-1, keepdims=True),
                    jnp.max(s_new, axis=-1, keepdims=True))
    p_past = jnp.exp2(s_past - m)
    p_new = jnp.exp2(s_new - m)
    inv_l = 1.0 / (jnp.sum(p_past, axis=-1, keepdims=True)
                   + jnp.sum(p_new, axis=-1, keepdims=True))
    p_past = p_past.astype(BF16)
    p_new = p_new.astype(BF16)
    outs = []
    for h in range(N_HEADS):
        rows = slice(h * per_head, (h + 1) * per_head)
        v_past = cv_ref[pl.ds(h, past_len, stride=N_HEADS), :].astype(BF16)
        acc = (_dot(p_past[rows], v_past) + _dot(p_new[rows], v_new[h])) * inv_l[rows]
        o = acc[:nq] - lam * acc[nq:]
        outs.append(o * lax.rsqrt(jnp.mean(o * o, axis=-1, keepdims=True) + NORM_EPS))
    o = jnp.concatenate(outs, axis=1) * jnp.tile(sg_ref[...], (1, N_HEADS)) * (1.0 - lam_init)
    o_ref[...] = (o * gb_ref[...].astype(F32)).astype(BF16)


def _sample_attention(q, k_new, v_new, cache_kt, cache_v, gb, sg, lq1, lk1, lq2, lk2,
                      *, layer, n_streams, n_new, lam_init):
    past_len = cache_kt.shape[3]
    per_step = SAMPLE_STREAMS_PER_STEP
    assert n_streams % per_step == 0
    new_spec = pl.BlockSpec((per_step, n_new, WIDTH), lambda b: (b, 0, 0))
    vec = _const_spec
    r3 = lambda a: a.reshape(n_streams, n_new, WIDTH)
    out = pl.pallas_call(
        functools.partial(_sample_attn_kernel, lam_init=lam_init, past_len=past_len),
        grid=(n_streams // per_step,),
        in_specs=[
            new_spec, new_spec, new_spec,
            pl.BlockSpec((None, per_step, WIDTH, past_len), lambda b: (layer, b, 0, 0)),
            pl.BlockSpec((None, per_step, past_len * N_HEADS, HEAD_DIM),
                         lambda b: (layer, b, 0, 0)),
            new_spec, vec(sg), vec(lq1), vec(lk1), vec(lq2), vec(lk2)],
        out_specs=new_spec,
        out_shape=jax.ShapeDtypeStruct((n_streams, n_new, WIDTH), BF16),
        compiler_params=_compiler_params(1),
        name="sample_attn",
    )(r3(q), r3(k_new), r3(v_new), cache_kt, cache_v, r3(gb),
      *[_array(p) for p in (sg, lq1, lk1, lq2, lk2)])
    return out.reshape(n_streams * n_new, WIDTH)


def _group_mean_matrix(group):
    idx = jnp.arange(GROUP_MEAN_WIDTH) // group
    return jnp.where(idx[:, None] == idx[None, :], 1.0 / group, 0.0).astype(BF16)


def kernel(x_prompt, x_sample, cache_k, cache_v, norm_g, w_in, sgu_norm_g, sgu_w, sgu_b,
           q_norm_g, k_norm_g, lambda_q1, lambda_k1, lambda_q2, lambda_k2, subln_g, w_out):
    depth = w_in.shape[0]
    batch, seq, _ = x_prompt.shape
    n_streams, n_new, _ = x_sample.shape
    past_len = cache_k.shape[2]
    assert seq % ATTN_TILE == 0
    assert SGU_CHUNK % n_new == 0 and past_len % CHUNK == 0 and n_new <= CHUNK
    sample_tile = min(n_streams * n_new, TOKEN_TILE)
    assert sample_tile % SGU_CHUNK == 0 and (n_streams * n_new) % sample_tile == 0

    gm128 = _group_mean_matrix(HEAD_DIM)
    gm64 = _group_mean_matrix(HALF_DIM)
    slopes = jnp.asarray(ALIBI_SLOPES, F32)
    tril = jnp.tril(jnp.ones((SGU_CHUNK, SGU_CHUNK), F32))
    tril_new = jnp.tril(jnp.ones((n_new, n_new), F32))
    streams_per_chunk = SGU_CHUNK // n_new
    eye = jnp.eye(streams_per_chunk, dtype=F32)
    cache_kt = jnp.transpose(cache_k, (0, 1, 3, 4, 5, 2)).reshape(depth, n_streams, WIDTH, past_len)
    cache_vr = cache_v.reshape(depth, n_streams, past_len * N_HEADS, HEAD_DIM)

    rows = lambda a: a.reshape(depth, 1, -1).astype(F32)
    w_in_bf = w_in.astype(BF16)
    w_out_bf = w_out.astype(BF16)
    ng = rows(norm_g)
    sgug = rows(sgu_norm_g)
    gq = rows(jnp.tile(q_norm_g, (1, WIDTH // HALF_DIM)))
    gk = rows(jnp.tile(k_norm_g, (1, WIDTH // HALF_DIM)))
    sg = rows(subln_g)
    lam_vecs = tuple(rows(a) for a in (lambda_q1, lambda_k1, lambda_q2, lambda_k2))
    sguw_p = (sgu_w * tril).astype(BF16)
    sgub_p = jnp.broadcast_to(sgu_b[..., None], (depth, N_HEADS, SGU_CHUNK, HEAD_DIM))
    w_new = sgu_w[:, :, :n_new, :n_new] * tril_new
    sguw_s = jnp.einsum("ab,lhts->lhatbs", eye, w_new).reshape(
        depth, N_HEADS, SGU_CHUNK, SGU_CHUNK).astype(BF16)
    sgub_s = jnp.broadcast_to(
        jnp.tile(sgu_b[:, :, :n_new], (1, 1, streams_per_chunk))[..., None],
        (depth, N_HEADS, SGU_CHUNK, HEAD_DIM))

    xp = x_prompt.reshape(batch * seq, D_MODEL)
    xs = x_sample.reshape(n_streams * n_new, D_MODEL)
    k_stack = v_stack = prompt_out = None
    ks_rows, vs_rows, sgu_rows = [], [], []
    for i in range(depth):
        lam_init = _lam_init(i)
        of_layer = lambda a: _LayerParam(a, i)
        params_p = (*map(of_layer, (ng, w_in_bf, sgug, sguw_p, sgub_p, gq, gk)), gm128, gm64)
        params_s = (*map(of_layer, (ng, w_in_bf, sgug, sguw_s, sgub_s, gq, gk)), gm128, gm64)
        attn_params = tuple(map(of_layer, (sg, *lam_vecs)))

        xp, ya, q, keys, k_stack, v_stack, vt, gb = _inproj_prompt(
            xp, params_p, prompt_out, k_stack, v_stack, batch=batch, seq=seq)
        yb = _prompt_attention(q, keys, vt, gb, slopes, *attn_params, lam_init=lam_init)
        prompt_out = (ya, yb, of_layer(w_out_bf))

        ya, q, k, v, gb, va = _inproj_sample(xs, params_s, tile=sample_tile)
        yb = _sample_attention(q, k, v, cache_kt, cache_vr, gb, *attn_params, layer=i,
                               n_streams=n_streams, n_new=n_new, lam_init=lam_init)
        xs = _outproj(xs, ya, yb, of_layer(w_out_bf), tile=sample_tile)
        ks_rows.append(k)
        vs_rows.append(v)
        sgu_rows.append(va)

    xp = _outproj(xp, *prompt_out, tile=OUTPROJ_TILE)
    new_k_prompt = jnp.transpose(
        k_stack.reshape(depth, batch, N_HEADS, 2, HALF_DIM, seq), (0, 1, 5, 2, 3, 4))
    return (
        xp.reshape(batch, seq, D_MODEL),
        xs.reshape(n_streams, n_new, D_MODEL),
        new_k_prompt,
        v_stack.reshape(depth, batch, seq, N_HEADS, HEAD_DIM),
        jnp.stack(ks_rows).reshape(depth, n_streams, n_new, N_HEADS, 2, HALF_DIM),
        jnp.stack(vs_rows).reshape(depth, n_streams, n_new, N_HEADS, HEAD_DIM),
        jnp.stack(sgu_rows).reshape(depth, n_streams, n_new, WIDTH),
    )
```

```python
import collections
import functools
import math
from typing import NamedTuple

import jax
import jax.numpy as jnp
from jax import lax
from jax.experimental import pallas as pl
from jax.experimental.pallas import tpu as pltpu

F32 = jnp.float32
BF16 = jnp.bfloat16

D_MODEL = 1024
N_HEADS = 4
HEAD_DIM = 128
HALF_DIM = 64
WIDTH = N_HEADS * HEAD_DIM
CHUNK = 64
SGU_CHUNK = 128
NORM_EPS = 1e-6
NEG_INF = -1e30
QK_SCALE = HALF_DIM ** -0.5
LOG2E = math.log2(math.e)
ALIBI_SLOPES = tuple(2.0 ** (-8.0 * (h + 1) / N_HEADS) for h in range(N_HEADS))

COL_U, COL_VA, COL_GA, COL_Q, COL_K, COL_V, COL_GB = (i * WIDTH for i in range(7))

VMEM_LIMIT_BYTES = 56 * 1024 * 1024

ATTN_TILE = 512
TOKEN_TILE = ATTN_TILE
PROMPT_SUB_TILES = 1
SAMPLE_STREAMS_PER_STEP = 2
OUTPROJ_TILE = 2048
LANE_CHUNK = 256
KEY_WIDTH = 2 * HEAD_DIM
GROUP_MEAN_WIDTH = 256
POS_SPLIT = 3
VT_ROWS = HEAD_DIM + 16


def _lam_init(layer_idx):
    return 0.8 - 0.6 * math.exp(-0.3 * layer_idx)


def _compiler_params(n_axes):
    return pltpu.CompilerParams(
        dimension_semantics=("arbitrary",) * n_axes,
        vmem_limit_bytes=VMEM_LIMIT_BYTES,
    )


class _LayerParam(NamedTuple):
    array: jax.Array
    layer: int


def _array(p):
    return p.array if isinstance(p, _LayerParam) else p


def _const_spec(p, single_buffer=False):
    if isinstance(p, _LayerParam):
        block = (None, *p.array.shape[1:])
        index = (p.layer,) + (0,) * (p.array.ndim - 1)
    else:
        block, index = p.shape, (0,) * p.ndim
    kwargs = dict(pipeline_mode=pl.Buffered(1)) if single_buffer else {}
    return pl.BlockSpec(block, lambda *grid_idx: index, **kwargs)


def _dot(a, b):
    return jnp.dot(a, b, preferred_element_type=F32)


def _dot_nt(a, b):
    return lax.dot_general(a, b, (((1,), (1,)), ((), ())), preferred_element_type=F32)


def _lam_value(lq1_ref, lk1_ref, lq2_ref, lk2_ref, lam_init):
    d1 = jnp.sum(lq1_ref[...] * lk1_ref[...], axis=-1, keepdims=True)
    d2 = jnp.sum(lq2_ref[...] * lk2_ref[...], axis=-1, keepdims=True)
    return jnp.exp(d1) - jnp.exp(d2) + lam_init


def _head_cols(h):
    return slice(h * HEAD_DIM, (h + 1) * HEAD_DIM)


def _inproj_kernel(*refs, prompt, n_prev, fused, n_sub):
    (x_ref, ng_ref, w_ref, sgug_ref, sguw_ref, sgub_ref, gq_ref, gk_ref,
     gm128_ref, gm64_ref) = refs[:10]
    n_in = 10
    if fused:
        ya_prev_ref, yb_prev_ref, wo_ref = refs[n_in:n_in + 3]
        n_in += 3
    prev = refs[n_in:n_in + n_prev]
    outs = refs[n_in + n_prev:]
    if fused:
        xo_ref, outs = outs[0], outs[1:]
    tile = x_ref.shape[0]
    sub = tile // n_sub

    def group_rms_scale(z, gm_ref):
        sq = (z * z).astype(BF16)
        gw = gm_ref.shape[0]
        mean_sq = jnp.concatenate(
            [_dot(sq[:, c:c + gw], gm_ref[...]) for c in range(0, WIDTH, gw)], axis=1)
        return lax.rsqrt(mean_sq + NORM_EPS)

    if prompt:
        ya_ref, q_ref, key_ref, kt_ref, v_ref, vt_ref, gb_ref = outs
        last = kt_ref.shape[0] - 1
        if n_prev:
            kt_prev_ref, v_prev_ref = prev
            kt_ref[:last] = kt_prev_ref[...]
            v_ref[:last] = v_prev_ref[...]
        for h in range(N_HEADS):
            vt_ref[h, HEAD_DIM:, :] = jnp.ones((VT_ROWS - HEAD_DIM, tile), BF16)
    else:
        ya_ref, q_ref, k_ref, v_ref, gb_ref, va_ref = outs

    for s in range(n_sub):
        rows = slice(s * sub, (s + 1) * sub)
        x = x_ref[rows, :]
        if fused:
            yb_prev = jnp.concatenate([yb_prev_ref[h, rows, :] for h in range(N_HEADS)], axis=1)
            x = x + (_dot(ya_prev_ref[rows, :], wo_ref[:WIDTH, :])
                     + _dot(yb_prev, wo_ref[WIDTH:, :]))
            xo_ref[rows, :] = x
        hb = (x * ng_ref[...]).astype(BF16)
        row_scale = jnp.broadcast_to(
            lax.rsqrt(jnp.mean(x * x, axis=-1, keepdims=True) + NORM_EPS), (sub, WIDTH))

        def proj(col):
            return _dot(hb, w_ref[:, col:col + WIDTH]) * row_scale

        va = proj(COL_VA)
        if not prompt:
            va_ref[rows, :] = va
        vn = ((va * group_rms_scale(va, gm128_ref)) * sgug_ref[...]).astype(BF16)
        gate = proj(COL_U) * jax.nn.silu(proj(COL_GA))
        for c in range(sub // SGU_CHUNK):
            crows = slice(c * SGU_CHUNK, (c + 1) * SGU_CHUNK)
            orows = slice(s * sub + c * SGU_CHUNK, s * sub + (c + 1) * SGU_CHUNK)
            for g in range(N_HEADS):
                cols = _head_cols(g)
                mixed = _dot(sguw_ref[g], vn[crows, cols]) + sgub_ref[g]
                ya_ref[orows, cols] = (gate[crows, cols] * mixed).astype(BF16)

        zq = proj(COL_Q)
        qn = ((zq * group_rms_scale(zq, gm64_ref)) * gq_ref[...]
              * (QK_SCALE * LOG2E)).astype(BF16)
        zk = proj(COL_K)
        kn = (zk * group_rms_scale(zk, gm64_ref)) * gk_ref[...]
        zv = proj(COL_V)
        gb = jax.nn.silu(proj(COL_GB)).astype(BF16)

        if prompt:
            pos = (lax.broadcasted_iota(jnp.int32, (sub, HEAD_DIM), 0)
                   + (pl.program_id(1) * tile + s * sub))
            lane = lax.broadcasted_iota(jnp.int32, (sub, HEAD_DIM), 1)
            lo = jnp.bitwise_and(pos, CHUNK - 1)
            pos_cols = jnp.where(
                lane < POS_SPLIT, (pos - lo).astype(F32),
                jnp.where(lane < 2 * POS_SPLIT, lo.astype(F32), 0.0)).astype(BF16)
            kt_ref[last, :, rows] = kn.T
            for h in range(N_HEADS):
                cols = _head_cols(h)
                q_ref[h, rows, :] = qn[:, cols]
                gb_ref[h, rows, :] = gb[:, cols]
                key_ref[h, rows, :HEAD_DIM] = kn[:, cols].astype(BF16)
                key_ref[h, rows, HEAD_DIM:] = pos_cols
                vt_ref[h, :HEAD_DIM, rows] = zv[:, cols].T.astype(BF16)
                v_ref[last, pl.ds(s * sub * N_HEADS + h, sub, stride=N_HEADS), :] = zv[:, cols]
        else:
            q_ref[rows, :] = qn
            k_ref[rows, :] = kn
            v_ref[rows, :] = zv
            gb_ref[rows, :] = gb


def _inproj_prompt(x, params, prev_out, k_stack, v_stack, *, batch, seq):
    tile = TOKEN_TILE
    n_t = seq // tile
    n = batch * seq
    n_layers = 1 if k_stack is None else k_stack.shape[0] + 1
    full = lambda p: _const_spec(p, single_buffer=True)
    head_major = lambda width: pl.BlockSpec((None, N_HEADS, tile, width),
                                            lambda b, i: (b, 0, i, 0))
    kt_spec = lambda layers: pl.BlockSpec((layers, None, WIDTH, tile),
                                          lambda b, i: (0, b, 0, i))
    v_spec = lambda layers: pl.BlockSpec((layers, tile * N_HEADS, HEAD_DIM),
                                         lambda b, i: (0, b * n_t + i, 0))
    x_spec = pl.BlockSpec((tile, D_MODEL), lambda b, i: (b * n_t + i, 0))
    ya_spec = pl.BlockSpec((tile, WIDTH), lambda b, i: (b * n_t + i, 0))
    in_specs = [x_spec] + [full(p) for p in params]
    operands = [x] + [_array(p) for p in params]
    fused = prev_out is not None
    if fused:
        ya_prev, yb_prev, w_out_prev = prev_out
        in_specs += [ya_spec, head_major(HEAD_DIM), full(w_out_prev)]
        operands += [ya_prev, yb_prev, _array(w_out_prev)]
    if n_layers > 1:
        in_specs += [kt_spec(n_layers - 1), v_spec(n_layers - 1)]
        operands += [k_stack, v_stack]
    out_shape = [
        jax.ShapeDtypeStruct((n, WIDTH), BF16),
        jax.ShapeDtypeStruct((batch, N_HEADS, seq, HEAD_DIM), BF16),
        jax.ShapeDtypeStruct((batch, N_HEADS, n_t, tile, KEY_WIDTH), BF16),
        jax.ShapeDtypeStruct((n_layers, batch, WIDTH, seq), F32),
        jax.ShapeDtypeStruct((n_layers, n * N_HEADS, HEAD_DIM), F32),
        jax.ShapeDtypeStruct((batch, N_HEADS, n_t, VT_ROWS, tile), BF16),
        jax.ShapeDtypeStruct((batch, N_HEADS, seq, HEAD_DIM), BF16),
    ]
    out_specs = [
        ya_spec,
        head_major(HEAD_DIM),
        pl.BlockSpec((None, N_HEADS, None, tile, KEY_WIDTH), lambda b, i: (b, 0, i, 0, 0)),
        kt_spec(n_layers),
        v_spec(n_layers),
        pl.BlockSpec((None, N_HEADS, None, VT_ROWS, tile), lambda b, i: (b, 0, i, 0, 0)),
        head_major(HEAD_DIM),
    ]
    if fused:
        out_shape.insert(0, jax.ShapeDtypeStruct((n, D_MODEL), F32))
        out_specs.insert(0, x_spec)
    outs = pl.pallas_call(
        functools.partial(_inproj_kernel, prompt=True, n_prev=2 * (n_layers > 1), fused=fused,
                          n_sub=PROMPT_SUB_TILES),
        grid=(batch, n_t),
        in_specs=in_specs,
        out_specs=out_specs,
        out_shape=out_shape,
        compiler_params=_compiler_params(2),
        name="inproj_prompt",
    )(*operands)
    return outs if fused else (x, *outs)


def _inproj_sample(x, params, *, tile):
    n = x.shape[0]
    row_spec = lambda width: pl.BlockSpec((tile, width), lambda i: (i, 0))
    dtypes = (BF16, BF16, F32, F32, BF16, F32)
    return pl.pallas_call(
        functools.partial(_inproj_kernel, prompt=False, n_prev=0, fused=False, n_sub=1),
        grid=(n // tile,),
        in_specs=[row_spec(D_MODEL)] + [_const_spec(p) for p in params],
        out_specs=[row_spec(WIDTH)] * len(dtypes),
        out_shape=[jax.ShapeDtypeStruct((n, WIDTH), d) for d in dtypes],
        compiler_params=_compiler_params(1),
        name="inproj_sample",
    )(x, *[_array(p) for p in params])


def _outproj_kernel(x_ref, ya_ref, yb_ref, w_ref, o_ref, *, head_major):
    if head_major:
        yb = jnp.concatenate([yb_ref[h] for h in range(N_HEADS)], axis=1)
    else:
        yb = yb_ref[...]
    y = _dot(ya_ref[...], w_ref[:WIDTH, :]) + _dot(yb, w_ref[WIDTH:, :])
    o_ref[...] = x_ref[...] + y


def _outproj(x, ya, yb, w_bf, *, tile):
    n = x.shape[0]
    head_major = yb.ndim == 4
    if head_major:
        n_t = yb.shape[2] // tile
        yb_spec = pl.BlockSpec((None, N_HEADS, tile, HEAD_DIM),
                               lambda i: (i // n_t, 0, i % n_t, 0))
    else:
        yb_spec = pl.BlockSpec((tile, WIDTH), lambda i: (i, 0))
    return pl.pallas_call(
        functools.partial(_outproj_kernel, head_major=head_major),
        grid=(n // tile,),
        in_specs=[
            pl.BlockSpec((tile, D_MODEL), lambda i: (i, 0)),
            pl.BlockSpec((tile, WIDTH), lambda i: (i, 0)),
            yb_spec,
            _const_spec(w_bf),
        ],
        out_specs=pl.BlockSpec((tile, D_MODEL), lambda i: (i, 0)),
        out_shape=jax.ShapeDtypeStruct((n, D_MODEL), F32),
        compiler_params=_compiler_params(1),
        name="outproj",
    )(x, ya, yb, _array(w_bf))


SCORES_AHEAD = 3
VALUES_BEHIND = 1
ATTN_HEADS_PER_STEP = 2


def _attn_units(seq):
    n_qc = seq // LANE_CHUNK
    qc_per_tile = ATTN_TILE // LANE_CHUNK
    return [(j, qc, half)
            for j in range(seq // ATTN_TILE)
            for qc in range(j * qc_per_tile, n_qc)
            for half in range(2)]


def _attn_kernel(slopes_ref, q_ref, key_ref, vt_ref, gb_ref, sg_ref, lq1_ref, lk1_ref,
                 lq2_ref, lk2_ref, o_ref, *scratch, lam_init):
    heads_per_step = q_ref.shape[0]

    def one_head(hh, carry):
        slope = slopes_ref[pl.program_id(1) * heads_per_step + hh]
        _attn_head(slope, q_ref.at[hh], key_ref.at[hh], vt_ref.at[hh], gb_ref.at[hh], sg_ref,
                   lq1_ref, lk1_ref, lq2_ref, lk2_ref, o_ref.at[hh], *scratch,
                   lam_init=lam_init)
        return carry

    lax.fori_loop(0, heads_per_step, one_head, 0)


def _attn_head(slope, q_ref, key_ref, vt_ref, gb_ref, sg_ref, lq1_ref, lk1_ref,
               lq2_ref, lk2_ref, o_ref, qbd_ref, s_ref, m_ref, acc_ref, *, lam_init):
    seq = q_ref.shape[0]
    t, w = ATTN_TILE, LANE_CHUNK
    assert t == 2 * w
    n_qc = seq // w
    slope2 = slope * LOG2E

    key = lax.broadcasted_iota(jnp.int32, (w, w), 0)
    qry = lax.broadcasted_iota(jnp.int32, (w, w), 1)
    allowed = jnp.right_shift(key, 6) <= jnp.right_shift(qry, 6)
    ahead = jnp.maximum(key - qry, 0).astype(F32)
    dt = jnp.where(allowed, (-2.0 * slope2) * ahead, NEG_INF)

    sub_tile = 16
    lane = lax.broadcasted_iota(jnp.int32, (sub_tile, HEAD_DIM), 1)
    rest = jnp.full((sub_tile, HEAD_DIM), slope2, F32)
    slope_cols = jnp.zeros((sub_tile, HEAD_DIM), F32)
    for piece in range(POS_SPLIT):
        part = rest.astype(BF16).astype(F32)
        slope_cols = jnp.where((lane == piece) | (lane == piece + POS_SPLIT), part, slope_cols)
        rest = rest - part
    slope_cols = jnp.tile(slope_cols.astype(BF16), (w // sub_tile, 1))
    q_lane = lax.broadcasted_iota(jnp.int32, (w, HEAD_DIM), 1)

    lam = _lam_value(lq1_ref, lk1_ref, lq2_ref, lk2_ref, lam_init)

    def n_keys(j, qc):
        return w if qc * w == j * t else t

    def stacked_queries(qc, half):
        rows = slice((half * n_qc + qc) * w, (half * n_qc + qc + 1) * w)
        q = q_ref[qc * w:(qc + 1) * w, :]
        in_half = (q_lane >= HALF_DIM) if half else (q_lane < HALF_DIM)
        qbd_ref[rows, :HEAD_DIM] = jnp.where(in_half, q, jnp.zeros_like(q))
        qbd_ref[rows, HEAD_DIM:] = slope_cols
        return rows

    def scores(unit):
        j, qc, half = unit
        if j == 0:
            rows = stacked_queries(qc, half)
        else:
            rows = slice((half * n_qc + qc) * w, (half * n_qc + qc + 1) * w)
        return _dot_nt(key_ref[j, :n_keys(j, qc), :], qbd_ref[rows, :])

    def softmax(unit, st):
        j, qc, half = unit
        lanes = slice((half * n_qc + qc) * w, (half * n_qc + qc + 1) * w)
        nk = n_keys(j, qc)
        if qc * w < (j + 1) * t:
            st = st + dt if nk == w else jnp.concatenate([st[:w], st[w:] + dt], axis=0)
        m_cur = jnp.max(st, axis=0, keepdims=True)
        if j == 0:
            m_new, alpha = m_cur, None
        else:
            m_old = m_ref[:, lanes]
            m_new = jnp.maximum(m_old, m_cur)
            alpha = jnp.exp2(m_old - m_new)
        m_ref[:, lanes] = m_new
        return jnp.exp2(st - m_new).astype(BF16), alpha

    def values(unit, p, alpha):
        j, qc, half = unit
        lanes = slice((half * n_qc + qc) * w, (half * n_qc + qc + 1) * w)
        pv = _dot(vt_ref[j, :, :n_keys(j, qc)], p)
        acc_ref[:, lanes] = pv if j == 0 else alpha * acc_ref[:, lanes] + pv

    def normalized(lanes):
        return acc_ref[:HEAD_DIM, lanes] * (1.0 / acc_ref[HEAD_DIM:HEAD_DIM + 1, lanes])

    def finalize(qc):
        rows = slice(qc * w, (qc + 1) * w)
        lanes1 = slice(qc * w, (qc + 1) * w)
        lanes2 = slice((n_qc + qc) * w, (n_qc + qc + 1) * w)
        ot = normalized(lanes1) - lam * normalized(lanes2)
        ot = ot * lax.rsqrt(jnp.mean(ot * ot, axis=0, keepdims=True) + NORM_EPS)
        o = (ot.T * sg_ref[...]) * (1.0 - lam_init)
        o_ref[rows, :] = (o * gb_ref[rows, :].astype(F32)).astype(BF16)

    units = _attn_units(seq)
    n_slots = s_ref.shape[0]

    def emit_scores(u):
        nk = n_keys(*units[u][:2])
        s_ref[u % n_slots, :nk, :] = scores(units[u])

    def load_scores(u):
        return s_ref[u % n_slots, :n_keys(*units[u][:2]), :]

    for u in range(min(SCORES_AHEAD, len(units))):
        emit_scores(u)
    pending_values = collections.deque()

    def run_oldest_values():
        unit_done, p, alpha = pending_values.popleft()
        values(unit_done, p, alpha)
        j_done, qc_done, half_done = unit_done
        if half_done == 1 and qc_done * w < (j_done + 1) * t:
            finalize(qc_done)

    for u, unit in enumerate(units):
        if u + SCORES_AHEAD < len(units):
            emit_scores(u + SCORES_AHEAD)
        pending_values.append((unit, *softmax(unit, load_scores(u))))
        if len(pending_values) > VALUES_BEHIND:
            run_oldest_values()
    while pending_values:
        run_oldest_values()


def _prompt_attention(q, keys, vt, gb, slopes, sg, lq1, lk1, lq2, lk2, *, lam_init):
    batch, _, seq, _ = q.shape
    t = ATTN_TILE
    n_tiles = seq // t
    hps = ATTN_HEADS_PER_STEP
    seq_spec = pl.BlockSpec((None, hps, seq, HEAD_DIM), lambda b, h: (b, h, 0, 0))
    vec = _const_spec
    return pl.pallas_call(
        functools.partial(_attn_kernel, lam_init=lam_init),
        grid=(batch, N_HEADS // hps),
        in_specs=[
            pl.BlockSpec(memory_space=pltpu.SMEM),
            seq_spec,
            pl.BlockSpec((None, hps, n_tiles, t, KEY_WIDTH), lambda b, h: (b, h, 0, 0, 0)),
            pl.BlockSpec((None, hps, n_tiles, VT_ROWS, t), lambda b, h: (b, h, 0, 0, 0)),
            seq_spec, vec(sg), vec(lq1), vec(lk1), vec(lq2), vec(lk2)],
        out_specs=seq_spec,
        out_shape=jax.ShapeDtypeStruct((batch, N_HEADS, seq, HEAD_DIM), BF16),
        scratch_shapes=[
            pltpu.VMEM((2 * seq, KEY_WIDTH), BF16),
            pltpu.VMEM((SCORES_AHEAD + 1, t, LANE_CHUNK), F32),
            pltpu.VMEM((1, 2 * seq), F32),
            pltpu.VMEM((VT_ROWS, 2 * seq), F32),
        ],
        compiler_params=_compiler_params(2),
        name="prompt_attn",
    )(slopes, q, keys, vt, gb, *[_array(p) for p in (sg, lq1, lk1, lq2, lk2)])


def _sample_attn_kernel(q_ref, kn_ref, vn_ref, ckt_ref, cv_ref, gb_ref, *rest, **static):
    for s in range(q_ref.shape[0]):
        _sample_attn_stream(q_ref.at[s], kn_ref.at[s], vn_ref.at[s], ckt_ref.at[s], cv_ref.at[s],
                            gb_ref.at[s], *rest[:-1], rest[-1].at[s], **static)


def _sample_attn_stream(q_ref, kn_ref, vn_ref, ckt_ref, cv_ref, gb_ref, sg_ref,
                        lq1_ref, lk1_ref, lq2_ref, lk2_ref, o_ref, *, lam_init, past_len):
    nq = q_ref.shape[0]
    per_head = 2 * nq
    n_rows = N_HEADS * per_head
    lam = _lam_value(lq1_ref, lk1_ref, lq2_ref, lk2_ref, lam_init)

    def alibi(n_cols, key_pos0):
        row = lax.broadcasted_iota(jnp.int32, (n_rows, n_cols), 0)
        col = lax.broadcasted_iota(jnp.int32, (n_rows, n_cols), 1)
        dist = jnp.abs(past_len + lax.rem(row, nq) - (key_pos0 + col)).astype(F32)
        slope = jnp.zeros((n_rows, n_cols), F32)
        for h in range(N_HEADS):
            slope = jnp.where(lax.div(row, per_head) == h, ALIBI_SLOPES[h] * LOG2E, slope)
        return slope * dist

    zpad = jnp.zeros((HEAD_DIM - nq, HEAD_DIM), BF16)
    s_past, s_new, v_new = [], [], []
    for h in range(N_HEADS):
        cols = _head_cols(h)
        q = q_ref[:, cols]
        lane = lax.broadcasted_iota(jnp.int32, q.shape, 1)
        zero = jnp.zeros_like(q)
        qbd = jnp.concatenate([jnp.where(lane < HALF_DIM, q, zero),
                               jnp.where(lane >= HALF_DIM, q, zero)], axis=0)
        s_past.append(_dot(qbd, ckt_ref[cols, :].astype(BF16)))
        s_new.append(_dot_nt(qbd, jnp.concatenate([kn_ref[:, cols].astype(BF16), zpad], axis=0)))
        v_new.append(jnp.concatenate([vn_ref[:, cols].astype(BF16), zpad], axis=0))
    s_past = jnp.concatenate(s_past, axis=0) - alibi(past_len, 0)
    s_new = jnp.concatenate(s_new, axis=0) - alibi(HEAD_DIM, past_len)
    real_new = lax.broadcasted_iota(jnp.int32, s_new.shape, 1) < nq
    s_new = jnp.where(real_new, s_new, NEG_INF)
    m = jnp.maximum(jnp.max(s_past, axis=-1, keepdims=True),
                    jnp.max(s_new, axis=-1, keepdims=True))
    p_past = jnp.exp2(s_past - m)
    p_new = jnp.exp2(s_new - m)
    inv_l = 1.0 / (jnp.sum(p_past, axis=-1, keepdims=True)
                   + jnp.sum(p_new, axis=-1, keepdims=True))
    p_past = p_past.astype(BF16)
    p_new = p_new.astype(BF16)
    outs = []
    for h in range(N_HEADS):
        rows = slice(h * per_head, (h + 1) * per_head)
        v_past = cv_ref[pl.ds(h, past_len, stride=N_HEADS), :].astype(BF16)
        acc = (_dot(p_past[rows], v_past) + _dot(p_new[rows], v_new[h])) * inv_l[rows]
        o = acc[:nq] - lam * acc[nq:]
        outs.append(o * lax.rsqrt(jnp.mean(o * o, axis=-1, keepdims=True) + NORM_EPS))
    o = jnp.concatenate(outs, axis=1) * jnp.tile(sg_ref[...], (1, N_HEADS)) * (1.0 - lam_init)
    o_ref[...] = (o * gb_ref[...].astype(F32)).astype(BF16)


def _sample_attention(q, k_new, v_new, cache_kt, cache_v, gb, sg, lq1, lk1, lq2, lk2,
                      *, layer, n_streams, n_new, lam_init):
    past_len = cache_kt.shape[3]
    per_step = SAMPLE_STREAMS_PER_STEP
    assert n_streams % per_step == 0
    new_spec = pl.BlockSpec((per_step, n_new, WIDTH), lambda b: (b, 0, 0))
    vec = _const_spec
    r3 = lambda a: a.reshape(n_streams, n_new, WIDTH)
    out = pl.pallas_call(
        functools.partial(_sample_attn_kernel, lam_init=lam_init, past_len=past_len),
        grid=(n_streams // per_step,),
        in_specs=[
            new_spec, new_spec, new_spec,
            pl.BlockSpec((None, per_step, WIDTH, past_len), lambda b: (layer, b, 0, 0)),
            pl.BlockSpec((None, per_step, past_len * N_HEADS, HEAD_DIM),
                         lambda b: (layer, b, 0, 0)),
            new_spec, vec(sg), vec(lq1), vec(lk1), vec(lq2), vec(lk2)],
        out_specs=new_spec,
        out_shape=jax.ShapeDtypeStruct((n_streams, n_new, WIDTH), BF16),
        compiler_params=_compiler_params(1),
        name="sample_attn",
    )(r3(q), r3(k_new), r3(v_new), cache_kt, cache_v, r3(gb),
      *[_array(p) for p in (sg, lq1, lk1, lq2, lk2)])
    return out.reshape(n_streams * n_new, WIDTH)


def _group_mean_matrix(group):
    idx = jnp.arange(GROUP_MEAN_WIDTH) // group
    return jnp.where(idx[:, None] == idx[None, :], 1.0 / group, 0.0).astype(BF16)


def kernel(x_prompt, x_sample, cache_k, cache_v, norm_g, w_in, sgu_norm_g, sgu_w, sgu_b,
           q_norm_g, k_norm_g, lambda_q1, lambda_k1, lambda_q2, lambda_k2, subln_g, w_out):
    depth = w_in.shape[0]
    batch, seq, _ = x_prompt.shape
    n_streams, n_new, _ = x_sample.shape
    past_len = cache_k.shape[2]
    assert seq % ATTN_TILE == 0
    assert SGU_CHUNK % n_new == 0 and past_len % CHUNK == 0 and n_new <= CHUNK
    sample_tile = min(n_streams * n_new, TOKEN_TILE)
    assert sample_tile % SGU_CHUNK == 0 and (n_streams * n_new) % sample_tile == 0

    gm128 = _group_mean_matrix(HEAD_DIM)
    gm64 = _group_mean_matrix(HALF_DIM)
    slopes = jnp.asarray(ALIBI_SLOPES, F32)
    tril = jnp.tril(jnp.ones((SGU_CHUNK, SGU_CHUNK), F32))
    tril_new = jnp.tril(jnp.ones((n_new, n_new), F32))
    streams_per_chunk = SGU_CHUNK // n_new
    eye = jnp.eye(streams_per_chunk, dtype=F32)
    cache_kt = jnp.transpose(cache_k, (0, 1, 3, 4, 5, 2)).reshape(depth, n_streams, WIDTH, past_len)
    cache_vr = cache_v.reshape(depth, n_streams, past_len * N_HEADS, HEAD_DIM)

    rows = lambda a: a.reshape(depth, 1, -1).astype(F32)
    w_in_bf = w_in.astype(BF16)
    w_out_bf = w_out.astype(BF16)
    ng = rows(norm_g)
    sgug = rows(sgu_norm_g)
    gq = rows(jnp.tile(q_norm_g, (1, WIDTH // HALF_DIM)))
    gk = rows(jnp.tile(k_norm_g, (1, WIDTH // HALF_DIM)))
    sg = rows(subln_g)
    lam_vecs = tuple(rows(a) for a in (lambda_q1, lambda_k1, lambda_q2, lambda_k2))
    sguw_p = (sgu_w * tril).astype(BF16)
    sgub_p = jnp.broadcast_to(sgu_b[..., None], (depth, N_HEADS, SGU_CHUNK, HEAD_DIM))
    w_new = sgu_w[:, :, :n_new, :n_new] * tril_new
    sguw_s = jnp.einsum("ab,lhts->lhatbs", eye, w_new).reshape(
        depth, N_HEADS, SGU_CHUNK, SGU_CHUNK).astype(BF16)
    sgub_s = jnp.broadcast_to(
        jnp.tile(sgu_b[:, :, :n_new], (1, 1, streams_per_chunk))[..., None],
        (depth, N_HEADS, SGU_CHUNK, HEAD_DIM))

    xp = x_prompt.reshape(batch * seq, D_MODEL)
    xs = x_sample.reshape(n_streams * n_new, D_MODEL)
    k_stack = v_stack = prompt_out = None
    ks_rows, vs_rows, sgu_rows = [], [], []
    for i in range(depth):
        lam_init = _lam_init(i)
        of_layer = lambda a: _LayerParam(a, i)
        params_p = (*map(of_layer, (ng, w_in_bf, sgug, sguw_p, sgub_p, gq, gk)), gm128, gm64)
        params_s = (*map(of_layer, (ng, w_in_bf, sgug, sguw_s, sgub_s, gq, gk)), gm128, gm64)
        attn_params = tuple(map(of_layer, (sg, *lam_vecs)))

        xp, ya, q, keys, k_stack, v_stack, vt, gb = _inproj_prompt(
            xp, params_p, prompt_out, k_stack, v_stack, batch=batch, seq=seq)
        yb = _prompt_attention(q, keys, vt, gb, slopes, *attn_params, lam_init=lam_init)
        prompt_out = (ya, yb, of_layer(w_out_bf))

        ya, q, k, v, gb, va = _inproj_sample(xs, params_s, tile=sample_tile)
        yb = _sample_attention(q, k, v, cache_kt, cache_vr, gb, *attn_params, layer=i,
                               n_streams=n_streams, n_new=n_new, lam_init=lam_init)
        xs = _outproj(xs, ya, yb, of_layer(w_out_bf), tile=sample_tile)
        ks_rows.append(k)
        vs_rows.append(v)
        sgu_rows.append(va)

    xp = _outproj(xp, *prompt_out, tile=OUTPROJ_TILE)
    new_k_prompt = jnp.transpose(
        k_stack.reshape(depth, batch, N_HEADS, 2, HALF_DIM, seq), (0, 1, 5, 2, 3, 4))
    return (
        xp.reshape(batch, seq, D_MODEL),
        xs.reshape(n_streams, n_new, D_MODEL),
        new_k_prompt,
        v_stack.reshape(depth, batch, seq, N_HEADS, HEAD_DIM),
        jnp.stack(ks_rows).reshape(depth, n_streams, n_new, N_HEADS, 2, HALF_DIM),
        jnp.stack(vs_rows).reshape(depth, n_streams, n_new, N_HEADS, HEAD_DIM),
        jnp.stack(sgu_rows).reshape(depth, n_streams, n_new, WIDTH),
    )
```

```python
import collections
import functools
import math
from typing import NamedTuple

import jax
import jax.numpy as jnp
from jax import lax
from jax.experimental import pallas as pl
from jax.experimental.pallas import tpu as pltpu

F32 = jnp.float32
BF16 = jnp.bfloat16

D_MODEL = 1024
N_HEADS = 4
HEAD_DIM = 128
HALF_DIM = 64
WIDTH = N_HEADS * HEAD_DIM
CHUNK = 64
SGU_CHUNK = 128
NORM_EPS = 1e-6
NEG_INF = -1e30
QK_SCALE = HALF_DIM ** -0.5
LOG2E = math.log2(math.e)
ALIBI_SLOPES = tuple(2.0 ** (-8.0 * (h + 1) / N_HEADS) for h in range(N_HEADS))

COL_U, COL_VA, COL_GA, COL_Q, COL_K, COL_V, COL_GB = (i * WIDTH for i in range(7))

VMEM_LIMIT_BYTES = 56 * 1024 * 1024

ATTN_TILE = 512
TOKEN_TILE = ATTN_TILE
PROMPT_SUB_TILES = 1
SAMPLE_STREAMS_PER_STEP = 2
OUTPROJ_TILE = 2048
LANE_CHUNK = 256
KEY_WIDTH = 2 * HEAD_DIM
GROUP_MEAN_WIDTH = 256
POS_SPLIT = 3
VT_ROWS = HEAD_DIM + 16


def _lam_init(layer_idx):
    return 0.8 - 0.6 * math.exp(-0.3 * layer_idx)


def _compiler_params(n_axes):
    return pltpu.CompilerParams(
        dimension_semantics=("arbitrary",) * n_axes,
        vmem_limit_bytes=VMEM_LIMIT_BYTES,
    )


class _LayerParam(NamedTuple):
    array: jax.Array
    layer: int


def _array(p):
    return p.array if isinstance(p, _LayerParam) else p


def _const_spec(p, single_buffer=False):
    if isinstance(p, _LayerParam):
        block = (None, *p.array.shape[1:])
        index = (p.layer,) + (0,) * (p.array.ndim - 1)
    else:
        block, index = p.shape, (0,) * p.ndim
    kwargs = dict(pipeline_mode=pl.Buffered(1)) if single_buffer else {}
    return pl.BlockSpec(block, lambda *grid_idx: index, **kwargs)


def _dot(a, b):
    return jnp.dot(a, b, preferred_element_type=F32)


def _dot_nt(a, b):
    return lax.dot_general(a, b, (((1,), (1,)), ((), ())), preferred_element_type=F32)


def _lam_value(lq1_ref, lk1_ref, lq2_ref, lk2_ref, lam_init):
    d1 = jnp.sum(lq1_ref[...] * lk1_ref[...], axis=-1, keepdims=True)
    d2 = jnp.sum(lq2_ref[...] * lk2_ref[...], axis=-1, keepdims=True)
    return jnp.exp(d1) - jnp.exp(d2) + lam_init


def _head_cols(h):
    return slice(h * HEAD_DIM, (h + 1) * HEAD_DIM)


def _inproj_kernel(*refs, prompt, n_prev, fused, n_sub):
    (x_ref, ng_ref, w_ref, sgug_ref, sguw_ref, sgub_ref, gq_ref, gk_ref,
     gm128_ref, gm64_ref) = refs[:10]
    n_in = 10
    if fused:
        ya_prev_ref, yb_prev_ref, wo_ref = refs[n_in:n_in + 3]
        n_in += 3
    prev = refs[n_in:n_in + n_prev]
    outs = refs[n_in + n_prev:]
    if fused:
        xo_ref, outs = outs[0], outs[1:]
    tile = x_ref.shape[0]
    sub = tile // n_sub

    def group_rms_scale(z, gm_ref):
        sq = (z * z).astype(BF16)
        gw = gm_ref.shape[0]
        mean_sq = jnp.concatenate(
            [_dot(sq[:, c:c + gw], gm_ref[...]) for c in range(0, WIDTH, gw)], axis=1)
        return lax.rsqrt(mean_sq + NORM_EPS)

    if prompt:
        ya_ref, q_ref, key_ref, kt_ref, v_ref, vt_ref, gb_ref = outs
        last = kt_ref.shape[0] - 1
        if n_prev:
            kt_prev_ref, v_prev_ref = prev
            kt_ref[:last] = kt_prev_ref[...]
            v_ref[:last] = v_prev_ref[...]
        for h in range(N_HEADS):
            vt_ref[h, HEAD_DIM:, :] = jnp.ones((VT_ROWS - HEAD_DIM, tile), BF16)
    else:
        ya_ref, q_ref, k_ref, v_ref, gb_ref, va_ref = outs

    for s in range(n_sub):
        rows = slice(s * sub, (s + 1) * sub)
        x = x_ref[rows, :]
        if fused:
            yb_prev = jnp.concatenate([yb_prev_ref[h, rows, :] for h in range(N_HEADS)], axis=1)
            x = x + (_dot(ya_prev_ref[rows, :], wo_ref[:WIDTH, :])
                     + _dot(yb_prev, wo_ref[WIDTH:, :]))
            xo_ref[rows, :] = x
        hb = (x * ng_ref[...]).astype(BF16)
        row_scale = jnp.broadcast_to(
            lax.rsqrt(jnp.mean(x * x, axis=-1, keepdims=True) + NORM_EPS), (sub, WIDTH))

        def proj(col):
            return _dot(hb, w_ref[:, col:col + WIDTH]) * row_scale

        va = proj(COL_VA)
        if not prompt:
            va_ref[rows, :] = va
        vn = ((va * group_rms_scale(va, gm128_ref)) * sgug_ref[...]).astype(BF16)
        gate = proj(COL_U) * jax.nn.silu(proj(COL_GA))
        for c in range(sub // SGU_CHUNK):
            crows = slice(c * SGU_CHUNK, (c + 1) * SGU_CHUNK)
            orows = slice(s * sub + c * SGU_CHUNK, s * sub + (c + 1) * SGU_CHUNK)
            for g in range(N_HEADS):
                cols = _head_cols(g)
                mixed = _dot(sguw_ref[g], vn[crows, cols]) + sgub_ref[g]
                ya_ref[orows, cols] = (gate[crows, cols] * mixed).astype(BF16)

        zq = proj(COL_Q)
        qn = (zq * group_rms_scale(zq, gm64_ref)) * gq_ref[...] * (QK_SCALE * LOG2E)
        zk = proj(COL_K)
        kn = (zk * group_rms_scale(zk, gm64_ref)) * gk_ref[...]
        zv = proj(COL_V)
        gb = jax.nn.silu(proj(COL_GB)).astype(BF16)

        if prompt:
            pos = (lax.broadcasted_iota(jnp.int32, (sub, HEAD_DIM), 0)
                   + (pl.program_id(1) * tile + s * sub))
            lane = lax.broadcasted_iota(jnp.int32, (sub, HEAD_DIM), 1)
            lo = jnp.bitwise_and(pos, CHUNK - 1)
            pos_cols = jnp.where(
                lane < POS_SPLIT, (pos - lo).astype(F32),
                jnp.where(lane < 2 * POS_SPLIT, lo.astype(F32), 0.0)).astype(BF16)
            kt_ref[last, :, rows] = kn.T
            for h in range(N_HEADS):
                cols = _head_cols(h)
                q_ref[h, :, rows] = qn[:, cols].T.astype(BF16)
                gb_ref[h, rows, :] = gb[:, cols]
                key_ref[h, rows, :HEAD_DIM] = kn[:, cols].astype(BF16)
                key_ref[h, rows, HEAD_DIM:] = pos_cols
                vt_ref[h, :HEAD_DIM, rows] = zv[:, cols].T.astype(BF16)
                v_ref[last, pl.ds(s * sub * N_HEADS + h, sub, stride=N_HEADS), :] = zv[:, cols]
        else:
            q_ref[rows, :] = qn.astype(BF16)
            k_ref[rows, :] = kn
            v_ref[rows, :] = zv
            gb_ref[rows, :] = gb


def _inproj_prompt(x, params, prev_out, k_stack, v_stack, *, batch, seq):
    tile = TOKEN_TILE
    n_t = seq // tile
    n = batch * seq
    n_layers = 1 if k_stack is None else k_stack.shape[0] + 1
    full = lambda p: _const_spec(p, single_buffer=True)
    head_major = lambda width: pl.BlockSpec((None, N_HEADS, tile, width),
                                            lambda b, i: (b, 0, i, 0))
    kt_spec = lambda layers: pl.BlockSpec((layers, None, WIDTH, tile),
                                          lambda b, i: (0, b, 0, i))
    v_spec = lambda layers: pl.BlockSpec((layers, tile * N_HEADS, HEAD_DIM),
                                         lambda b, i: (0, b * n_t + i, 0))
    x_spec = pl.BlockSpec((tile, D_MODEL), lambda b, i: (b * n_t + i, 0))
    ya_spec = pl.BlockSpec((tile, WIDTH), lambda b, i: (b * n_t + i, 0))
    in_specs = [x_spec] + [full(p) for p in params]
    operands = [x] + [_array(p) for p in params]
    fused = prev_out is not None
    if fused:
        ya_prev, yb_prev, w_out_prev = prev_out
        in_specs += [ya_spec, head_major(HEAD_DIM), full(w_out_prev)]
        operands += [ya_prev, yb_prev, _array(w_out_prev)]
    if n_layers > 1:
        in_specs += [kt_spec(n_layers - 1), v_spec(n_layers - 1)]
        operands += [k_stack, v_stack]
    out_shape = [
        jax.ShapeDtypeStruct((n, WIDTH), BF16),
        jax.ShapeDtypeStruct((batch, N_HEADS, HEAD_DIM, seq), BF16),
        jax.ShapeDtypeStruct((batch, N_HEADS, n_t, tile, KEY_WIDTH), BF16),
        jax.ShapeDtypeStruct((n_layers, batch, WIDTH, seq), F32),
        jax.ShapeDtypeStruct((n_layers, n * N_HEADS, HEAD_DIM), F32),
        jax.ShapeDtypeStruct((batch, N_HEADS, n_t, VT_ROWS, tile), BF16),
        jax.ShapeDtypeStruct((batch, N_HEADS, seq, HEAD_DIM), BF16),
    ]
    out_specs = [
        ya_spec,
        pl.BlockSpec((None, N_HEADS, HEAD_DIM, tile), lambda b, i: (b, 0, 0, i)),
        pl.BlockSpec((None, N_HEADS, None, tile, KEY_WIDTH), lambda b, i: (b, 0, i, 0, 0)),
        kt_spec(n_layers),
        v_spec(n_layers),
        pl.BlockSpec((None, N_HEADS, None, VT_ROWS, tile), lambda b, i: (b, 0, i, 0, 0)),
        head_major(HEAD_DIM),
    ]
    if fused:
        out_shape.insert(0, jax.ShapeDtypeStruct((n, D_MODEL), F32))
        out_specs.insert(0, x_spec)
    outs = pl.pallas_call(
        functools.partial(_inproj_kernel, prompt=True, n_prev=2 * (n_layers > 1), fused=fused,
                          n_sub=PROMPT_SUB_TILES),
        grid=(batch, n_t),
        in_specs=in_specs,
        out_specs=out_specs,
        out_shape=out_shape,
        compiler_params=_compiler_params(2),
        name="inproj_prompt",
    )(*operands)
    return outs if fused else (x, *outs)


def _inproj_sample(x, params, *, tile):
    n = x.shape[0]
    row_spec = lambda width: pl.BlockSpec((tile, width), lambda i: (i, 0))
    dtypes = (BF16, BF16, F32, F32, BF16, F32)
    return pl.pallas_call(
        functools.partial(_inproj_kernel, prompt=False, n_prev=0, fused=False, n_sub=1),
        grid=(n // tile,),
        in_specs=[row_spec(D_MODEL)] + [_const_spec(p) for p in params],
        out_specs=[row_spec(WIDTH)] * len(dtypes),
        out_shape=[jax.ShapeDtypeStruct((n, WIDTH), d) for d in dtypes],
        compiler_params=_compiler_params(1),
        name="inproj_sample",
    )(x, *[_array(p) for p in params])


def _outproj_kernel(x_ref, ya_ref, yb_ref, w_ref, o_ref, *, head_major):
    if head_major:
        yb = jnp.concatenate([yb_ref[h] for h in range(N_HEADS)], axis=1)
    else:
        yb = yb_ref[...]
    y = _dot(ya_ref[...], w_ref[:WIDTH, :]) + _dot(yb, w_ref[WIDTH:, :])
    o_ref[...] = x_ref[...] + y


def _outproj(x, ya, yb, w_bf, *, tile):
    n = x.shape[0]
    head_major = yb.ndim == 4
    if head_major:
        n_t = yb.shape[2] // tile
        yb_spec = pl.BlockSpec((None, N_HEADS, tile, HEAD_DIM),
                               lambda i: (i // n_t, 0, i % n_t, 0))
    else:
        yb_spec = pl.BlockSpec((tile, WIDTH), lambda i: (i, 0))
    return pl.pallas_call(
        functools.partial(_outproj_kernel, head_major=head_major),
        grid=(n // tile,),
        in_specs=[
            pl.BlockSpec((tile, D_MODEL), lambda i: (i, 0)),
            pl.BlockSpec((tile, WIDTH), lambda i: (i, 0)),
            yb_spec,
            _const_spec(w_bf),
        ],
        out_specs=pl.BlockSpec((tile, D_MODEL), lambda i: (i, 0)),
        out_shape=jax.ShapeDtypeStruct((n, D_MODEL), F32),
        compiler_params=_compiler_params(1),
        name="outproj",
    )(x, ya, yb, _array(w_bf))


SCORES_AHEAD = 3
VALUES_BEHIND = 1
ATTN_HEADS_PER_STEP = 2


def _attn_units(seq):
    n_qc = seq // LANE_CHUNK
    qc_per_tile = ATTN_TILE // LANE_CHUNK
    return [(j, qc, half)
            for j in range(seq // ATTN_TILE)
            for qc in range(j * qc_per_tile, n_qc)
            for half in range(2)]


def _attn_kernel(slopes_ref, q_ref, key_ref, vt_ref, gb_ref, sg_ref, lq1_ref, lk1_ref,
                 lq2_ref, lk2_ref, o_ref, *scratch, lam_init):
    heads_per_step = q_ref.shape[0]

    def one_head(hh, carry):
        slope = slopes_ref[pl.program_id(1) * heads_per_step + hh]
        _attn_head(slope, q_ref.at[hh], key_ref.at[hh], vt_ref.at[hh], gb_ref.at[hh], sg_ref,
                   lq1_ref, lk1_ref, lq2_ref, lk2_ref, o_ref.at[hh], *scratch,
                   lam_init=lam_init)
        return carry

    lax.fori_loop(0, heads_per_step, one_head, 0)


def _attn_head(slope, q_ref, key_ref, vt_ref, gb_ref, sg_ref, lq1_ref, lk1_ref,
               lq2_ref, lk2_ref, o_ref, qbd_ref, s_ref, m_ref, acc_ref, *, lam_init):
    seq = q_ref.shape[1]
    t, w = ATTN_TILE, LANE_CHUNK
    assert t == 2 * w
    n_qc = seq // w
    slope2 = slope * LOG2E

    key = lax.broadcasted_iota(jnp.int32, (w, w), 0)
    qry = lax.broadcasted_iota(jnp.int32, (w, w), 1)
    allowed = jnp.right_shift(key, 6) <= jnp.right_shift(qry, 6)
    ahead = jnp.maximum(key - qry, 0).astype(F32)
    dt = jnp.where(allowed, (-2.0 * slope2) * ahead, NEG_INF)

    row = lax.broadcasted_iota(jnp.int32, (HEAD_DIM, w), 0)
    rest = jnp.full((HEAD_DIM, w), slope2, F32)
    slope_rows = jnp.zeros((HEAD_DIM, w), F32)
    for piece in range(POS_SPLIT):
        part = rest.astype(BF16).astype(F32)
        slope_rows = jnp.where((row == piece) | (row == piece + POS_SPLIT), part, slope_rows)
        rest = rest - part
    slope_rows = slope_rows.astype(BF16)
    zero_half = jnp.zeros((HALF_DIM, w), BF16)

    lam = _lam_value(lq1_ref, lk1_ref, lq2_ref, lk2_ref, lam_init)

    def n_keys(j, qc):
        return w if qc * w == j * t else t

    def stacked_queries(qc, half):
        lanes = slice((half * n_qc + qc) * w, (half * n_qc + qc + 1) * w)
        own = slice(half * HALF_DIM, (half + 1) * HALF_DIM)
        other = slice((1 - half) * HALF_DIM, (2 - half) * HALF_DIM)
        qbd_ref[own, lanes] = q_ref[own, qc * w:(qc + 1) * w]
        qbd_ref[other, lanes] = zero_half
        qbd_ref[HEAD_DIM:, lanes] = slope_rows
        return lanes

    def scores(unit):
        j, qc, half = unit
        if j == 0:
            lanes = stacked_queries(qc, half)
        else:
            lanes = slice((half * n_qc + qc) * w, (half * n_qc + qc + 1) * w)
        return _dot(key_ref[j, :n_keys(j, qc), :], qbd_ref[:, lanes])

    def softmax(unit, st):
        j, qc, half = unit
        lanes = slice((half * n_qc + qc) * w, (half * n_qc + qc + 1) * w)
        nk = n_keys(j, qc)
        if qc * w < (j + 1) * t:
            st = st + dt if nk == w else jnp.concatenate([st[:w], st[w:] + dt], axis=0)
        m_cur = jnp.max(st, axis=0, keepdims=True)
        if j == 0:
            m_new, alpha = m_cur, None
        else:
            m_old = m_ref[:, lanes]
            m_new = jnp.maximum(m_old, m_cur)
            alpha = jnp.exp2(m_old - m_new)
        m_ref[:, lanes] = m_new
        return jnp.exp2(st - m_new).astype(BF16), alpha

    def values(unit, p, alpha):
        j, qc, half = unit
        lanes = slice((half * n_qc + qc) * w, (half * n_qc + qc + 1) * w)
        pv = _dot(vt_ref[j, :, :n_keys(j, qc)], p)
        acc_ref[:, lanes] = pv if j == 0 else alpha * acc_ref[:, lanes] + pv

    def normalized(lanes):
        return acc_ref[:HEAD_DIM, lanes] * (1.0 / acc_ref[HEAD_DIM:HEAD_DIM + 1, lanes])

    def finalize(qc):
        rows = slice(qc * w, (qc + 1) * w)
        lanes1 = slice(qc * w, (qc + 1) * w)
        lanes2 = slice((n_qc + qc) * w, (n_qc + qc + 1) * w)
        ot = normalized(lanes1) - lam * normalized(lanes2)
        ot = ot * lax.rsqrt(jnp.mean(ot * ot, axis=0, keepdims=True) + NORM_EPS)
        o = (ot.T * sg_ref[...]) * (1.0 - lam_init)
        o_ref[rows, :] = (o * gb_ref[rows, :].astype(F32)).astype(BF16)

    units = _attn_units(seq)
    n_slots = s_ref.shape[0]

    def emit_scores(u):
        nk = n_keys(*units[u][:2])
        s_ref[u % n_slots, :nk, :] = scores(units[u])

    def load_scores(u):
        return s_ref[u % n_slots, :n_keys(*units[u][:2]), :]

    for u in range(min(SCORES_AHEAD, len(units))):
        emit_scores(u)
    pending_values = collections.deque()

    def run_oldest_values():
        unit_done, p, alpha = pending_values.popleft()
        values(unit_done, p, alpha)
        j_done, qc_done, half_done = unit_done
        if half_done == 1 and qc_done * w < (j_done + 1) * t:
            finalize(qc_done)

    for u, unit in enumerate(units):
        if u + SCORES_AHEAD < len(units):
            emit_scores(u + SCORES_AHEAD)
        pending_values.append((unit, *softmax(unit, load_scores(u))))
        if len(pending_values) > VALUES_BEHIND:
            run_oldest_values()
    while pending_values:
        run_oldest_values()


def _prompt_attention(q, keys, vt, gb, slopes, sg, lq1, lk1, lq2, lk2, *, lam_init):
    batch, _, _, seq = q.shape
    t = ATTN_TILE
    n_tiles = seq // t
    hps = ATTN_HEADS_PER_STEP
    seq_spec = pl.BlockSpec((None, hps, seq, HEAD_DIM), lambda b, h: (b, h, 0, 0))
    vec = _const_spec
    return pl.pallas_call(
        functools.partial(_attn_kernel, lam_init=lam_init),
        grid=(batch, N_HEADS // hps),
        in_specs=[
            pl.BlockSpec(memory_space=pltpu.SMEM),
            pl.BlockSpec((None, hps, HEAD_DIM, seq), lambda b, h: (b, h, 0, 0)),
            pl.BlockSpec((None, hps, n_tiles, t, KEY_WIDTH), lambda b, h: (b, h, 0, 0, 0)),
            pl.BlockSpec((None, hps, n_tiles, VT_ROWS, t), lambda b, h: (b, h, 0, 0, 0)),
            seq_spec, vec(sg), vec(lq1), vec(lk1), vec(lq2), vec(lk2)],
        out_specs=seq_spec,
        out_shape=jax.ShapeDtypeStruct((batch, N_HEADS, seq, HEAD_DIM), BF16),
        scratch_shapes=[
            pltpu.VMEM((KEY_WIDTH, 2 * seq), BF16),
            pltpu.VMEM((SCORES_AHEAD + 1, t, LANE_CHUNK), F32),
            pltpu.VMEM((1, 2 * seq), F32),
            pltpu.VMEM((VT_ROWS, 2 * seq), F32),
        ],
        compiler_params=_compiler_params(2),
        name="prompt_attn",
    )(slopes, q, keys, vt, gb, *[_array(p) for p in (sg, lq1, lk1, lq2, lk2)])


def _sample_attn_kernel(q_ref, kn_ref, vn_ref, ckt_ref, cv_ref, gb_ref, *rest, **static):
    for s in range(q_ref.shape[0]):
        _sample_attn_stream(q_ref.at[s], kn_ref.at[s], vn_ref.at[s], ckt_ref.at[s], cv_ref.at[s],
                            gb_ref.at[s], *rest[:-1], rest[-1].at[s], **static)


def _sample_attn_stream(q_ref, kn_ref, vn_ref, ckt_ref, cv_ref, gb_ref, sg_ref,
                        lq1_ref, lk1_ref, lq2_ref, lk2_ref, o_ref, *, lam_init, past_len):
    nq = q_ref.shape[0]
    per_head = 2 * nq
    n_rows = N_HEADS * per_head
    lam = _lam_value(lq1_ref, lk1_ref, lq2_ref, lk2_ref, lam_init)

    def alibi(n_cols, key_pos0):
        row = lax.broadcasted_iota(jnp.int32, (n_rows, n_cols), 0)
        col = lax.broadcasted_iota(jnp.int32, (n_rows, n_cols), 1)
        dist = jnp.abs(past_len + lax.rem(row, nq) - (key_pos0 + col)).astype(F32)
        slope = jnp.zeros((n_rows, n_cols), F32)
        for h in range(N_HEADS):
            slope = jnp.where(lax.div(row, per_head) == h, ALIBI_SLOPES[h] * LOG2E, slope)
        return slope * dist

    zpad = jnp.zeros((HEAD_DIM - nq, HEAD_DIM), BF16)
    s_past, s_new, v_new = [], [], []
    for h in range(N_HEADS):
        cols = _head_cols(h)
        q = q_ref[:, cols]
        lane = lax.broadcasted_iota(jnp.int32, q.shape, 1)
        zero = jnp.zeros_like(q)
        qbd = jnp.concatenate([jnp.where(lane < HALF_DIM, q, zero),
                               jnp.where(lane >= HALF_DIM, q, zero)], axis=0)
        s_past.append(_dot(qbd, ckt_ref[cols, :].astype(BF16)))
        s_new.append(_dot_nt(qbd, jnp.concatenate([kn_ref[:, cols].astype(BF16), zpad], axis=0)))
        v_new.append(jnp.concatenate([vn_ref[:, cols].astype(BF16), zpad], axis=0))
    s_past = jnp.concatenate(s_past, axis=0) - alibi(past_len, 0)
    s_new = jnp.concatenate(s_new, axis=0) - alibi(HEAD_DIM, past_len)
    real_new = lax.broadcasted_iota(jnp.int32, s_new.shape, 1) < nq
    s_new = jnp.where(real_new, s_new, NEG_INF)
    m = jnp.maximum(jnp.max(s_past, axis=-1, keepdims=True),
                    jnp.max(s_new, axis=-1, keepdims=True))
    p_past = jnp.exp2(s_past - m)
    p_new = jnp.exp2(s_new - m)
    inv_l = 1.0 / (jnp.sum(p_past, axis=-1, keepdims=True)
                   + jnp.sum(p_new, axis=-1, keepdims=True))
    p_past = p_past.astype(BF16)
    p_new = p_new.astype(BF16)
    outs = []
    for h in range(N_HEADS):
        rows = slice(h * per_head, (h + 1) * per_head)
        v_past = cv_ref[pl.ds(h, past_len, stride=N_HEADS), :].astype(BF16)
        acc = (_dot(p_past[rows], v_past) + _dot(p_new[rows], v_new[h])) * inv_l[rows]
        o = acc[:nq] - lam * acc[nq:]
        outs.append(o * lax.rsqrt(jnp.mean(o * o, axis=-1, keepdims=True) + NORM_EPS))
    o = jnp.concatenate(outs, axis=1) * jnp.tile(sg_ref[...], (1, N_HEADS)) * (1.0 - lam_init)
    o_ref[...] = (o * gb_ref[...].astype(F32)).astype(BF16)


def _sample_attention(q, k_new, v_new, cache_kt, cache_v, gb, sg, lq1, lk1, lq2, lk2,
                      *, layer, n_streams, n_new, lam_init):
    past_len = cache_kt.shape[3]
    per_step = SAMPLE_STREAMS_PER_STEP
    assert n_streams % per_step == 0
    new_spec = pl.BlockSpec((per_step, n_new, WIDTH), lambda b: (b, 0, 0))
    vec = _const_spec
    r3 = lambda a: a.reshape(n_streams, n_new, WIDTH)
    out = pl.pallas_call(
        functools.partial(_sample_attn_kernel, lam_init=lam_init, past_len=past_len),
        grid=(n_streams // per_step,),
        in_specs=[
            new_spec, new_spec, new_spec,
            pl.BlockSpec((None, per_step, WIDTH, past_len), lambda b: (layer, b, 0, 0)),
            pl.BlockSpec((None, per_step, past_len * N_HEADS, HEAD_DIM),
                         lambda b: (layer, b, 0, 0)),
            new_spec, vec(sg), vec(lq1), vec(lk1), vec(lq2), vec(lk2)],
        out_specs=new_spec,
        out_shape=jax.ShapeDtypeStruct((n_streams, n_new, WIDTH), BF16),
        compiler_params=_compiler_params(1),
        name="sample_attn",
    )(r3(q), r3(k_new), r3(v_new), cache_kt, cache_v, r3(gb),
      *[_array(p) for p in (sg, lq1, lk1, lq2, lk2)])
    return out.reshape(n_streams * n_new, WIDTH)


def _group_mean_matrix(group):
    idx = jnp.arange(GROUP_MEAN_WIDTH) // group
    return jnp.where(idx[:, None] == idx[None, :], 1.0 / group, 0.0).astype(BF16)


def kernel(x_prompt, x_sample, cache_k, cache_v, norm_g, w_in, sgu_norm_g, sgu_w, sgu_b,
           q_norm_g, k_norm_g, lambda_q1, lambda_k1, lambda_q2, lambda_k2, subln_g, w_out):
    depth = w_in.shape[0]
    batch, seq, _ = x_prompt.shape
    n_streams, n_new, _ = x_sample.shape
    past_len = cache_k.shape[2]
    assert seq % ATTN_TILE == 0
    assert SGU_CHUNK % n_new == 0 and past_len % CHUNK == 0 and n_new <= CHUNK
    sample_tile = min(n_streams * n_new, TOKEN_TILE)
    assert sample_tile % SGU_CHUNK == 0 and (n_streams * n_new) % sample_tile == 0

    gm128 = _group_mean_matrix(HEAD_DIM)
    gm64 = _group_mean_matrix(HALF_DIM)
    slopes = jnp.asarray(ALIBI_SLOPES, F32)
    tril = jnp.tril(jnp.ones((SGU_CHUNK, SGU_CHUNK), F32))
    tril_new = jnp.tril(jnp.ones((n_new, n_new), F32))
    streams_per_chunk = SGU_CHUNK // n_new
    eye = jnp.eye(streams_per_chunk, dtype=F32)
    cache_kt = jnp.transpose(cache_k, (0, 1, 3, 4, 5, 2)).reshape(depth, n_streams, WIDTH, past_len)
    cache_vr = cache_v.reshape(depth, n_streams, past_len * N_HEADS, HEAD_DIM)

    rows = lambda a: a.reshape(depth, 1, -1).astype(F32)
    w_in_bf = w_in.astype(BF16)
    w_out_bf = w_out.astype(BF16)
    ng = rows(norm_g)
    sgug = rows(sgu_norm_g)
    gq = rows(jnp.tile(q_norm_g, (1, WIDTH // HALF_DIM)))
    gk = rows(jnp.tile(k_norm_g, (1, WIDTH // HALF_DIM)))
    sg = rows(subln_g)
    lam_vecs = tuple(rows(a) for a in (lambda_q1, lambda_k1, lambda_q2, lambda_k2))
    sguw_p = (sgu_w * tril).astype(BF16)
    sgub_p = jnp.broadcast_to(sgu_b[..., None], (depth, N_HEADS, SGU_CHUNK, HEAD_DIM))
    w_new = sgu_w[:, :, :n_new, :n_new] * tril_new
    sguw_s = jnp.einsum("ab,lhts->lhatbs", eye, w_new).reshape(
        depth, N_HEADS, SGU_CHUNK, SGU_CHUNK).astype(BF16)
    sgub_s = jnp.broadcast_to(
        jnp.tile(sgu_b[:, :, :n_new], (1, 1, streams_per_chunk))[..., None],
        (depth, N_HEADS, SGU_CHUNK, HEAD_DIM))

    xp = x_prompt.reshape(batch * seq, D_MODEL)
    xs = x_sample.reshape(n_streams * n_new, D_MODEL)
    k_stack = v_stack = prompt_out = None
    ks_rows, vs_rows, sgu_rows = [], [], []
    for i in range(depth):
        lam_init = _lam_init(i)
        of_layer = lambda a: _LayerParam(a, i)
        params_p = (*map(of_layer, (ng, w_in_bf, sgug, sguw_p, sgub_p, gq, gk)), gm128, gm64)
        params_s = (*map(of_layer, (ng, w_in_bf, sgug, sguw_s, sgub_s, gq, gk)), gm128, gm64)
        attn_params = tuple(map(of_layer, (sg, *lam_vecs)))

        xp, ya, q, keys, k_stack, v_stack, vt, gb = _inproj_prompt(
            xp, params_p, prompt_out, k_stack, v_stack, batch=batch, seq=seq)
        yb = _prompt_attention(q, keys, vt, gb, slopes, *attn_params, lam_init=lam_init)
        prompt_out = (ya, yb, of_layer(w_out_bf))

        ya, q, k, v, gb, va = _inproj_sample(xs, params_s, tile=sample_tile)
        yb = _sample_attention(q, k, v, cache_kt, cache_vr, gb, *attn_params, layer=i,
                               n_streams=n_streams, n_new=n_new, lam_init=lam_init)
        xs = _outproj(xs, ya, yb, of_layer(w_out_bf), tile=sample_tile)
        ks_rows.append(k)
        vs_rows.append(v)
        sgu_rows.append(va)

    xp = _outproj(xp, *prompt_out, tile=OUTPROJ_TILE)
    new_k_prompt = jnp.transpose(
        k_stack.reshape(depth, batch, N_HEADS, 2, HALF_DIM, seq), (0, 1, 5, 2, 3, 4))
    return (
        xp.reshape(batch, seq, D_MODEL),
        xs.reshape(n_streams, n_new, D_MODEL),
        new_k_prompt,
        v_stack.reshape(depth, batch, seq, N_HEADS, HEAD_DIM),
        jnp.stack(ks_rows).reshape(depth, n_streams, n_new, N_HEADS, 2, HALF_DIM),
        jnp.stack(vs_rows).reshape(depth, n_streams, n_new, N_HEADS, HEAD_DIM),
        jnp.stack(sgu_rows).reshape(depth, n_streams, n_new, WIDTH),
    )
```

```python
import collections
import functools
import math
from typing import NamedTuple

import jax
import jax.numpy as jnp
from jax import lax
from jax.experimental import pallas as pl
from jax.experimental.pallas import tpu as pltpu

F32 = jnp.float32
BF16 = jnp.bfloat16

D_MODEL = 1024
N_HEADS = 4
HEAD_DIM = 128
HALF_DIM = 64
WIDTH = N_HEADS * HEAD_DIM
CHUNK = 64
SGU_CHUNK = 128
NORM_EPS = 1e-6
NEG_INF = -1e30
QK_SCALE = HALF_DIM ** -0.5
LOG2E = math.log2(math.e)
ALIBI_SLOPES = tuple(2.0 ** (-8.0 * (h + 1) / N_HEADS) for h in range(N_HEADS))

COL_U, COL_VA, COL_GA, COL_Q, COL_K, COL_V, COL_GB = (i * WIDTH for i in range(7))

VMEM_LIMIT_BYTES = 56 * 1024 * 1024

ATTN_TILE = 512
TOKEN_TILE = ATTN_TILE
SAMPLE_STREAMS_PER_STEP = 2
OUTPROJ_TILE = 2048
LANE_CHUNK = 256
KEY_WIDTH = 2 * HEAD_DIM
GROUP_MEAN_WIDTH = 256
POS_SPLIT = 3
VT_ROWS = HEAD_DIM + 16


def _lam_init(layer_idx):
    return 0.8 - 0.6 * math.exp(-0.3 * layer_idx)


def _compiler_params(n_axes):
    return pltpu.CompilerParams(
        dimension_semantics=("arbitrary",) * n_axes,
        vmem_limit_bytes=VMEM_LIMIT_BYTES,
    )


class _LayerParam(NamedTuple):
    array: jax.Array
    layer: int


def _array(p):
    return p.array if isinstance(p, _LayerParam) else p


def _const_spec(p, single_buffer=False):
    if isinstance(p, _LayerParam):
        block = (None, *p.array.shape[1:])
        index = (p.layer,) + (0,) * (p.array.ndim - 1)
    else:
        block, index = p.shape, (0,) * p.ndim
    kwargs = dict(pipeline_mode=pl.Buffered(1)) if single_buffer else {}
    return pl.BlockSpec(block, lambda *grid_idx: index, **kwargs)


def _dot(a, b):
    return jnp.dot(a, b, preferred_element_type=F32)


def _dot_nt(a, b):
    return lax.dot_general(a, b, (((1,), (1,)), ((), ())), preferred_element_type=F32)


def _lam_value(lq1_ref, lk1_ref, lq2_ref, lk2_ref, lam_init):
    d1 = jnp.sum(lq1_ref[...] * lk1_ref[...], axis=-1, keepdims=True)
    d2 = jnp.sum(lq2_ref[...] * lk2_ref[...], axis=-1, keepdims=True)
    return jnp.exp(d1) - jnp.exp(d2) + lam_init


def _head_cols(h):
    return slice(h * HEAD_DIM, (h + 1) * HEAD_DIM)


def _inproj_kernel(*refs, prompt, n_prev, fused):
    (x_ref, ng_ref, w_ref, sgug_ref, sguw_ref, sgub_ref, gq_ref, gk_ref,
     gm128_ref, gm64_ref) = refs[:10]
    n_in = 10
    if fused:
        ya_prev_ref, yb_prev_ref, wo_ref = refs[n_in:n_in + 3]
        n_in += 3
    prev = refs[n_in:n_in + n_prev]
    outs = refs[n_in + n_prev:]
    if fused:
        xo_ref, outs = outs[0], outs[1:]
    tile = x_ref.shape[0]

    def group_rms_scale(z, gm_ref):
        sq = (z * z).astype(BF16)
        gw = gm_ref.shape[0]
        mean_sq = jnp.concatenate(
            [_dot(sq[:, c:c + gw], gm_ref[...]) for c in range(0, WIDTH, gw)], axis=1)
        return lax.rsqrt(mean_sq + NORM_EPS)

    if prompt:
        ya_ref, q_ref, key_ref, kt_ref, v_ref, vt_ref, gb_ref = outs
        last = kt_ref.shape[0] - 1
        if n_prev:
            kt_prev_ref, v_prev_ref = prev
            kt_ref[:last] = kt_prev_ref[...]
            v_ref[:last] = v_prev_ref[...]
        for h in range(N_HEADS):
            vt_ref[h, HEAD_DIM:, :] = jnp.ones((VT_ROWS - HEAD_DIM, tile), BF16)
    else:
        ya_ref, q_ref, k_ref, v_ref, gb_ref, va_ref = outs

    x = x_ref[...]
    if fused:
        yb_prev = jnp.concatenate([yb_prev_ref[h] for h in range(N_HEADS)], axis=1)
        x = x + (_dot(ya_prev_ref[...], wo_ref[:WIDTH, :]) + _dot(yb_prev, wo_ref[WIDTH:, :]))
        xo_ref[...] = x
    hb = (x * ng_ref[...]).astype(BF16)
    row_scale = jnp.broadcast_to(
        lax.rsqrt(jnp.mean(x * x, axis=-1, keepdims=True) + NORM_EPS), (tile, WIDTH))

    def proj(col):
        return _dot(hb, w_ref[:, col:col + WIDTH]) * row_scale

    va = proj(COL_VA)
    if not prompt:
        va_ref[...] = va
    vn = ((va * group_rms_scale(va, gm128_ref)) * sgug_ref[...]).astype(BF16)
    gate = proj(COL_U) * jax.nn.silu(proj(COL_GA))
    for c in range(tile // SGU_CHUNK):
        rows = slice(c * SGU_CHUNK, (c + 1) * SGU_CHUNK)
        for g in range(N_HEADS):
            cols = _head_cols(g)
            mixed = _dot(sguw_ref[g], vn[rows, cols]) + sgub_ref[g]
            ya_ref[rows, cols] = (gate[rows, cols] * mixed).astype(BF16)

    zq = proj(COL_Q)
    qn = (zq * group_rms_scale(zq, gm64_ref)) * gq_ref[...] * (QK_SCALE * LOG2E)
    zk = proj(COL_K)
    kn = (zk * group_rms_scale(zk, gm64_ref)) * gk_ref[...]
    zv = proj(COL_V)
    gb = jax.nn.silu(proj(COL_GB)).astype(BF16)

    if prompt:
        pos = (lax.broadcasted_iota(jnp.int32, (tile, HEAD_DIM), 0)
               + pl.program_id(1) * tile)
        lane = lax.broadcasted_iota(jnp.int32, (tile, HEAD_DIM), 1)
        lo = jnp.bitwise_and(pos, CHUNK - 1)
        pos_cols = jnp.where(
            lane < POS_SPLIT, (pos - lo).astype(F32),
            jnp.where(lane < 2 * POS_SPLIT, lo.astype(F32), 0.0)).astype(BF16)
        kt_ref[last] = kn.T
        for h in range(N_HEADS):
            cols = _head_cols(h)
            q_ref[h] = qn[:, cols].T.astype(BF16)
            gb_ref[h] = gb[:, cols]
            key_ref[h, :, :HEAD_DIM] = kn[:, cols].astype(BF16)
            key_ref[h, :, HEAD_DIM:] = pos_cols
            vt_ref[h, :HEAD_DIM, :] = zv[:, cols].T.astype(BF16)
            v_ref[last, pl.ds(h, tile, stride=N_HEADS), :] = zv[:, cols]
    else:
        q_ref[...] = qn.astype(BF16)
        k_ref[...] = kn
        v_ref[...] = zv
        gb_ref[...] = gb


def _inproj_prompt(x, params, prev_out, k_stack, v_stack, *, batch, seq):
    tile = TOKEN_TILE
    n_t = seq // tile
    n = batch * seq
    n_layers = 1 if k_stack is None else k_stack.shape[0] + 1
    full = lambda p: _const_spec(p, single_buffer=True)
    head_major = lambda width: pl.BlockSpec((None, N_HEADS, tile, width),
                                            lambda b, i: (b, 0, i, 0))
    kt_spec = lambda layers: pl.BlockSpec((layers, None, WIDTH, tile),
                                          lambda b, i: (0, b, 0, i))
    v_spec = lambda layers: pl.BlockSpec((layers, tile * N_HEADS, HEAD_DIM),
                                         lambda b, i: (0, b * n_t + i, 0))
    x_spec = pl.BlockSpec((tile, D_MODEL), lambda b, i: (b * n_t + i, 0))
    ya_spec = pl.BlockSpec((tile, WIDTH), lambda b, i: (b * n_t + i, 0))
    in_specs = [x_spec] + [full(p) for p in params]
    operands = [x] + [_array(p) for p in params]
    fused = prev_out is not None
    if fused:
        ya_prev, yb_prev, w_out_prev = prev_out
        in_specs += [ya_spec, head_major(HEAD_DIM), full(w_out_prev)]
        operands += [ya_prev, yb_prev, _array(w_out_prev)]
    if n_layers > 1:
        in_specs += [kt_spec(n_layers - 1), v_spec(n_layers - 1)]
        operands += [k_stack, v_stack]
    out_shape = [
        jax.ShapeDtypeStruct((n, WIDTH), BF16),
        jax.ShapeDtypeStruct((batch, N_HEADS, HEAD_DIM, seq), BF16),
        jax.ShapeDtypeStruct((batch, N_HEADS, n_t, tile, KEY_WIDTH), BF16),
        jax.ShapeDtypeStruct((n_layers, batch, WIDTH, seq), F32),
        jax.ShapeDtypeStruct((n_layers, n * N_HEADS, HEAD_DIM), F32),
        jax.ShapeDtypeStruct((batch, N_HEADS, n_t, VT_ROWS, tile), BF16),
        jax.ShapeDtypeStruct((batch, N_HEADS, seq, HEAD_DIM), BF16),
    ]
    out_specs = [
        ya_spec,
        pl.BlockSpec((None, N_HEADS, HEAD_DIM, tile), lambda b, i: (b, 0, 0, i)),
        pl.BlockSpec((None, N_HEADS, None, tile, KEY_WIDTH), lambda b, i: (b, 0, i, 0, 0)),
        kt_spec(n_layers),
        v_spec(n_layers),
        pl.BlockSpec((None, N_HEADS, None, VT_ROWS, tile), lambda b, i: (b, 0, i, 0, 0)),
        head_major(HEAD_DIM),
    ]
    if fused:
        out_shape.insert(0, jax.ShapeDtypeStruct((n, D_MODEL), F32))
        out_specs.insert(0, x_spec)
    outs = pl.pallas_call(
        functools.partial(_inproj_kernel, prompt=True, n_prev=2 * (n_layers > 1), fused=fused),
        grid=(batch, n_t),
        in_specs=in_specs,
        out_specs=out_specs,
        out_shape=out_shape,
        compiler_params=_compiler_params(2),
        name="inproj_prompt",
    )(*operands)
    return outs if fused else (x, *outs)


def _inproj_sample(x, params, *, tile):
    n = x.shape[0]
    row_spec = lambda width: pl.BlockSpec((tile, width), lambda i: (i, 0))
    dtypes = (BF16, BF16, F32, F32, BF16, F32)
    return pl.pallas_call(
        functools.partial(_inproj_kernel, prompt=False, n_prev=0, fused=False),
        grid=(n // tile,),
        in_specs=[row_spec(D_MODEL)] + [_const_spec(p) for p in params],
        out_specs=[row_spec(WIDTH)] * len(dtypes),
        out_shape=[jax.ShapeDtypeStruct((n, WIDTH), d) for d in dtypes],
        compiler_params=_compiler_params(1),
        name="inproj_sample",
    )(x, *[_array(p) for p in params])


def _outproj_kernel(x_ref, ya_ref, yb_ref, w_ref, o_ref, *, head_major):
    if head_major:
        yb = jnp.concatenate([yb_ref[h] for h in range(N_HEADS)], axis=1)
    else:
        yb = yb_ref[...]
    y = _dot(ya_ref[...], w_ref[:WIDTH, :]) + _dot(yb, w_ref[WIDTH:, :])
    o_ref[...] = x_ref[...] + y


def _outproj(x, ya, yb, w_bf, *, tile):
    n = x.shape[0]
    head_major = yb.ndim == 4
    if head_major:
        n_t = yb.shape[2] // tile
        yb_spec = pl.BlockSpec((None, N_HEADS, tile, HEAD_DIM),
                               lambda i: (i // n_t, 0, i % n_t, 0))
    else:
        yb_spec = pl.BlockSpec((tile, WIDTH), lambda i: (i, 0))
    return pl.pallas_call(
        functools.partial(_outproj_kernel, head_major=head_major),
        grid=(n // tile,),
        in_specs=[
            pl.BlockSpec((tile, D_MODEL), lambda i: (i, 0)),
            pl.BlockSpec((tile, WIDTH), lambda i: (i, 0)),
            yb_spec,
            _const_spec(w_bf),
        ],
        out_specs=pl.BlockSpec((tile, D_MODEL), lambda i: (i, 0)),
        out_shape=jax.ShapeDtypeStruct((n, D_MODEL), F32),
        compiler_params=_compiler_params(1),
        name="outproj",
    )(x, ya, yb, _array(w_bf))


SCORES_AHEAD = 3
VALUES_BEHIND = 1
ATTN_HEADS_PER_STEP = 4


def _attn_units(seq):
    n_qc = seq // LANE_CHUNK
    qc_per_tile = ATTN_TILE // LANE_CHUNK
    return [(j, qc, half)
            for j in range(seq // ATTN_TILE)
            for qc in range(j * qc_per_tile, n_qc)
            for half in range(2)]


def _attn_kernel(slopes_ref, q_ref, key_ref, vt_ref, gb_ref, sg_ref, lq1_ref, lk1_ref,
                 lq2_ref, lk2_ref, o_ref, *scratch, lam_init):
    heads_per_step = q_ref.shape[0]

    def one_head(hh, carry):
        slope = slopes_ref[pl.program_id(1) * heads_per_step + hh]
        _attn_head(slope, q_ref.at[hh], key_ref.at[hh], vt_ref.at[hh], gb_ref.at[hh], sg_ref,
                   lq1_ref, lk1_ref, lq2_ref, lk2_ref, o_ref.at[hh], *scratch,
                   lam_init=lam_init)
        return carry

    lax.fori_loop(0, heads_per_step, one_head, 0)


def _attn_head(slope, q_ref, key_ref, vt_ref, gb_ref, sg_ref, lq1_ref, lk1_ref,
               lq2_ref, lk2_ref, o_ref, qbd_ref, s_ref, m_ref, acc_ref, *, lam_init):
    seq = q_ref.shape[1]
    t, w = ATTN_TILE, LANE_CHUNK
    assert t == 2 * w
    n_qc = seq // w
    slope2 = slope * LOG2E

    key = lax.broadcasted_iota(jnp.int32, (w, w), 0)
    qry = lax.broadcasted_iota(jnp.int32, (w, w), 1)
    allowed = jnp.right_shift(key, 6) <= jnp.right_shift(qry, 6)
    ahead = jnp.maximum(key - qry, 0).astype(F32)
    dt = jnp.where(allowed, (-2.0 * slope2) * ahead, NEG_INF)

    row = lax.broadcasted_iota(jnp.int32, (HEAD_DIM, w), 0)
    rest = jnp.full((HEAD_DIM, w), slope2, F32)
    slope_rows = jnp.zeros((HEAD_DIM, w), F32)
    for piece in range(POS_SPLIT):
        part = rest.astype(BF16).astype(F32)
        slope_rows = jnp.where((row == piece) | (row == piece + POS_SPLIT), part, slope_rows)
        rest = rest - part
    slope_rows = slope_rows.astype(BF16)
    zero_half = jnp.zeros((HALF_DIM, w), BF16)

    lam = _lam_value(lq1_ref, lk1_ref, lq2_ref, lk2_ref, lam_init)

    def n_keys(j, qc):
        return w if qc * w == j * t else t

    def stacked_queries(qc, half):
        lanes = slice((half * n_qc + qc) * w, (half * n_qc + qc + 1) * w)
        own = slice(half * HALF_DIM, (half + 1) * HALF_DIM)
        other = slice((1 - half) * HALF_DIM, (2 - half) * HALF_DIM)
        qbd_ref[own, lanes] = q_ref[own, qc * w:(qc + 1) * w]
        qbd_ref[other, lanes] = zero_half
        qbd_ref[HEAD_DIM:, lanes] = slope_rows
        return lanes

    def scores(unit):
        j, qc, half = unit
        if j == 0:
            lanes = stacked_queries(qc, half)
        else:
            lanes = slice((half * n_qc + qc) * w, (half * n_qc + qc + 1) * w)
        return _dot(key_ref[j, :n_keys(j, qc), :], qbd_ref[:, lanes])

    def softmax(unit, st):
        j, qc, half = unit
        lanes = slice((half * n_qc + qc) * w, (half * n_qc + qc + 1) * w)
        nk = n_keys(j, qc)
        if qc * w < (j + 1) * t:
            st = st + dt if nk == w else jnp.concatenate([st[:w], st[w:] + dt], axis=0)
        m_cur = jnp.max(st, axis=0, keepdims=True)
        if j == 0:
            m_new, alpha = m_cur, None
        else:
            m_old = m_ref[:, lanes]
            m_new = jnp.maximum(m_old, m_cur)
            alpha = jnp.exp2(m_old - m_new)
        m_ref[:, lanes] = m_new
        return jnp.exp2(st - m_new).astype(BF16), alpha

    def values(unit, p, alpha):
        j, qc, half = unit
        lanes = slice((half * n_qc + qc) * w, (half * n_qc + qc + 1) * w)
        pv = _dot(vt_ref[j, :, :n_keys(j, qc)], p)
        acc_ref[:, lanes] = pv if j == 0 else alpha * acc_ref[:, lanes] + pv

    def normalized(lanes):
        return acc_ref[:HEAD_DIM, lanes] * (1.0 / acc_ref[HEAD_DIM:HEAD_DIM + 1, lanes])

    def finalize(qc):
        rows = slice(qc * w, (qc + 1) * w)
        lanes1 = slice(qc * w, (qc + 1) * w)
        lanes2 = slice((n_qc + qc) * w, (n_qc + qc + 1) * w)
        ot = normalized(lanes1) - lam * normalized(lanes2)
        ot = ot * lax.rsqrt(jnp.mean(ot * ot, axis=0, keepdims=True) + NORM_EPS)
        o = (ot.T * sg_ref[...]) * (1.0 - lam_init)
        o_ref[rows, :] = (o * gb_ref[rows, :].astype(F32)).astype(BF16)

    units = _attn_units(seq)
    n_slots = s_ref.shape[0]

    def emit_scores(u):
        nk = n_keys(*units[u][:2])
        s_ref[u % n_slots, :nk, :] = scores(units[u])

    def load_scores(u):
        return s_ref[u % n_slots, :n_keys(*units[u][:2]), :]

    for u in range(min(SCORES_AHEAD, len(units))):
        emit_scores(u)
    pending_values = collections.deque()

    def run_oldest_values():
        unit_done, p, alpha = pending_values.popleft()
        values(unit_done, p, alpha)
        j_done, qc_done, half_done = unit_done
        if half_done == 1 and qc_done * w < (j_done + 1) * t:
            finalize(qc_done)

    for u, unit in enumerate(units):
        if u + SCORES_AHEAD < len(units):
            emit_scores(u + SCORES_AHEAD)
        pending_values.append((unit, *softmax(unit, load_scores(u))))
        if len(pending_values) > VALUES_BEHIND:
            run_oldest_values()
    while pending_values:
        run_oldest_values()


def _prompt_attention(q, keys, vt, gb, slopes, sg, lq1, lk1, lq2, lk2, *, lam_init):
    batch, _, _, seq = q.shape
    t = ATTN_TILE
    n_tiles = seq // t
    hps = ATTN_HEADS_PER_STEP
    seq_spec = pl.BlockSpec((None, hps, seq, HEAD_DIM), lambda b, h: (b, h, 0, 0))
    vec = _const_spec
    return pl.pallas_call(
        functools.partial(_attn_kernel, lam_init=lam_init),
        grid=(batch, N_HEADS // hps),
        in_specs=[
            pl.BlockSpec(memory_space=pltpu.SMEM),
            pl.BlockSpec((None, hps, HEAD_DIM, seq), lambda b, h: (b, h, 0, 0)),
            pl.BlockSpec((None, hps, n_tiles, t, KEY_WIDTH), lambda b, h: (b, h, 0, 0, 0)),
            pl.BlockSpec((None, hps, n_tiles, VT_ROWS, t), lambda b, h: (b, h, 0, 0, 0)),
            seq_spec, vec(sg), vec(lq1), vec(lk1), vec(lq2), vec(lk2)],
        out_specs=seq_spec,
        out_shape=jax.ShapeDtypeStruct((batch, N_HEADS, seq, HEAD_DIM), BF16),
        scratch_shapes=[
            pltpu.VMEM((KEY_WIDTH, 2 * seq), BF16),
            pltpu.VMEM((SCORES_AHEAD + 1, t, LANE_CHUNK), F32),
            pltpu.VMEM((1, 2 * seq), F32),
            pltpu.VMEM((VT_ROWS, 2 * seq), F32),
        ],
        compiler_params=_compiler_params(2),
        name="prompt_attn",
    )(slopes, q, keys, vt, gb, *[_array(p) for p in (sg, lq1, lk1, lq2, lk2)])


def _sample_attn_kernel(q_ref, kn_ref, vn_ref, ckt_ref, cv_ref, gb_ref, *rest, **static):
    for s in range(q_ref.shape[0]):
        _sample_attn_stream(q_ref.at[s], kn_ref.at[s], vn_ref.at[s], ckt_ref.at[s], cv_ref.at[s],
                            gb_ref.at[s], *rest[:-1], rest[-1].at[s], **static)


def _sample_attn_stream(q_ref, kn_ref, vn_ref, ckt_ref, cv_ref, gb_ref, sg_ref,
                        lq1_ref, lk1_ref, lq2_ref, lk2_ref, o_ref, *, lam_init, past_len):
    nq = q_ref.shape[0]
    per_head = 2 * nq
    n_rows = N_HEADS * per_head
    lam = _lam_value(lq1_ref, lk1_ref, lq2_ref, lk2_ref, lam_init)

    def alibi(n_cols, key_pos0):
        row = lax.broadcasted_iota(jnp.int32, (n_rows, n_cols), 0)
        col = lax.broadcasted_iota(jnp.int32, (n_rows, n_cols), 1)
        dist = jnp.abs(past_len + lax.rem(row, nq) - (key_pos0 + col)).astype(F32)
        slope = jnp.zeros((n_rows, n_cols), F32)
        for h in range(N_HEADS):
            slope = jnp.where(lax.div(row, per_head) == h, ALIBI_SLOPES[h] * LOG2E, slope)
        return slope * dist

    zpad = jnp.zeros((HEAD_DIM - nq, HEAD_DIM), BF16)
    s_past, s_new, v_new = [], [], []
    for h in range(N_HEADS):
        cols = _head_cols(h)
        q = q_ref[:, cols]
        lane = lax.broadcasted_iota(jnp.int32, q.shape, 1)
        zero = jnp.zeros_like(q)
        qbd = jnp.concatenate([jnp.where(lane < HALF_DIM, q, zero),
                               jnp.where(lane >= HALF_DIM, q, zero)], axis=0)
        s_past.append(_dot(qbd, ckt_ref[cols, :].astype(BF16)))
        s_new.append(_dot_nt(qbd, jnp.concatenate([kn_ref[:, cols].astype(BF16), zpad], axis=0)))
        v_new.append(jnp.concatenate([vn_ref[:, cols].astype(BF16), zpad], axis=0))
    s_past = jnp.concatenate(s_past, axis=0) - alibi(past_len, 0)
    s_new = jnp.concatenate(s_new, axis=0) - alibi(HEAD_DIM, past_len)
    real_new = lax.broadcasted_iota(jnp.int32, s_new.shape, 1) < nq
    s_new = jnp.where(real_new, s_new, NEG_INF)
    m = jnp.maximum(jnp.max(s_past, axis=-1, keepdims=True),
                    jnp.max(s_new, axis=-1, keepdims=True))
    p_past = jnp.exp2(s_past - m)
    p_new = jnp.exp2(s_new - m)
    inv_l = 1.0 / (jnp.sum(p_past, axis=-1, keepdims=True)
                   + jnp.sum(p_new, axis=-1, keepdims=True))
    p_past = p_past.astype(BF16)
    p_new = p_new.astype(BF16)
    outs = []
    for h in range(N_HEADS):
        rows = slice(h * per_head, (h + 1) * per_head)
        v_past = cv_ref[pl.ds(h, past_len, stride=N_HEADS), :].astype(BF16)
        acc = (_dot(p_past[rows], v_past) + _dot(p_new[rows], v_new[h])) * inv_l[rows]
        o = acc[:nq] - lam * acc[nq:]
        outs.append(o * lax.rsqrt(jnp.mean(o * o, axis=-1, keepdims=True) + NORM_EPS))
    o = jnp.concatenate(outs, axis=1) * jnp.tile(sg_ref[...], (1, N_HEADS)) * (1.0 - lam_init)
    o_ref[...] = (o * gb_ref[...].astype(F32)).astype(BF16)


def _sample_attention(q, k_new, v_new, cache_kt, cache_v, gb, sg, lq1, lk1, lq2, lk2,
                      *, layer, n_streams, n_new, lam_init):
    past_len = cache_kt.shape[3]
    per_step = SAMPLE_STREAMS_PER_STEP
    assert n_streams % per_step == 0
    new_spec = pl.BlockSpec((per_step, n_new, WIDTH), lambda b: (b, 0, 0))
    vec = _const_spec
    r3 = lambda a: a.reshape(n_streams, n_new, WIDTH)
    out = pl.pallas_call(
        functools.partial(_sample_attn_kernel, lam_init=lam_init, past_len=past_len),
        grid=(n_streams // per_step,),
        in_specs=[
            new_spec, new_spec, new_spec,
            pl.BlockSpec((None, per_step, WIDTH, past_len), lambda b: (layer, b, 0, 0)),
            pl.BlockSpec((None, per_step, past_len * N_HEADS, HEAD_DIM),
                         lambda b: (layer, b, 0, 0)),
            new_spec, vec(sg), vec(lq1), vec(lk1), vec(lq2), vec(lk2)],
        out_specs=new_spec,
        out_shape=jax.ShapeDtypeStruct((n_streams, n_new, WIDTH), BF16),
        compiler_params=_compiler_params(1),
        name="sample_attn",
    )(r3(q), r3(k_new), r3(v_new), cache_kt, cache_v, r3(gb),
      *[_array(p) for p in (sg, lq1, lk1, lq2, lk2)])
    return out.reshape(n_streams * n_new, WIDTH)


def _group_mean_matrix(group):
    idx = jnp.arange(GROUP_MEAN_WIDTH) // group
    return jnp.where(idx[:, None] == idx[None, :], 1.0 / group, 0.0).astype(BF16)


def kernel(x_prompt, x_sample, cache_k, cache_v, norm_g, w_in, sgu_norm_g, sgu_w, sgu_b,
           q_norm_g, k_norm_g, lambda_q1, lambda_k1, lambda_q2, lambda_k2, subln_g, w_out):
    depth = w_in.shape[0]
    batch, seq, _ = x_prompt.shape
    n_streams, n_new, _ = x_sample.shape
    past_len = cache_k.shape[2]
    assert seq % ATTN_TILE == 0
    assert SGU_CHUNK % n_new == 0 and past_len % CHUNK == 0 and n_new <= CHUNK
    sample_tile = min(n_streams * n_new, TOKEN_TILE)
    assert sample_tile % SGU_CHUNK == 0 and (n_streams * n_new) % sample_tile == 0

    gm128 = _group_mean_matrix(HEAD_DIM)
    gm64 = _group_mean_matrix(HALF_DIM)
    slopes = jnp.asarray(ALIBI_SLOPES, F32)
    tril = jnp.tril(jnp.ones((SGU_CHUNK, SGU_CHUNK), F32))
    tril_new = jnp.tril(jnp.ones((n_new, n_new), F32))
    streams_per_chunk = SGU_CHUNK // n_new
    eye = jnp.eye(streams_per_chunk, dtype=F32)
    cache_kt = jnp.transpose(cache_k, (0, 1, 3, 4, 5, 2)).reshape(depth, n_streams, WIDTH, past_len)
    cache_vr = cache_v.reshape(depth, n_streams, past_len * N_HEADS, HEAD_DIM)

    rows = lambda a: a.reshape(depth, 1, -1).astype(F32)
    w_in_bf = w_in.astype(BF16)
    w_out_bf = w_out.astype(BF16)
    ng = rows(norm_g)
    sgug = rows(sgu_norm_g)
    gq = rows(jnp.tile(q_norm_g, (1, WIDTH // HALF_DIM)))
    gk = rows(jnp.tile(k_norm_g, (1, WIDTH // HALF_DIM)))
    sg = rows(subln_g)
    lam_vecs = tuple(rows(a) for a in (lambda_q1, lambda_k1, lambda_q2, lambda_k2))
    sguw_p = (sgu_w * tril).astype(BF16)
    sgub_p = jnp.broadcast_to(sgu_b[..., None], (depth, N_HEADS, SGU_CHUNK, HEAD_DIM))
    w_new = sgu_w[:, :, :n_new, :n_new] * tril_new
    sguw_s = jnp.einsum("ab,lhts->lhatbs", eye, w_new).reshape(
        depth, N_HEADS, SGU_CHUNK, SGU_CHUNK).astype(BF16)
    sgub_s = jnp.broadcast_to(
        jnp.tile(sgu_b[:, :, :n_new], (1, 1, streams_per_chunk))[..., None],
        (depth, N_HEADS, SGU_CHUNK, HEAD_DIM))

    xp = x_prompt.reshape(batch * seq, D_MODEL)
    xs = x_sample.reshape(n_streams * n_new, D_MODEL)
    k_stack = v_stack = prompt_out = None
    ks_rows, vs_rows, sgu_rows = [], [], []
    for i in range(depth):
        lam_init = _lam_init(i)
        of_layer = lambda a: _LayerParam(a, i)
        params_p = (*map(of_layer, (ng, w_in_bf, sgug, sguw_p, sgub_p, gq, gk)), gm128, gm64)
        params_s = (*map(of_layer, (ng, w_in_bf, sgug, sguw_s, sgub_s, gq, gk)), gm128, gm64)
        attn_params = tuple(map(of_layer, (sg, *lam_vecs)))

        xp, ya, q, keys, k_stack, v_stack, vt, gb = _inproj_prompt(
            xp, params_p, prompt_out, k_stack, v_stack, batch=batch, seq=seq)
        yb = _prompt_attention(q, keys, vt, gb, slopes, *attn_params, lam_init=lam_init)
        prompt_out = (ya, yb, of_layer(w_out_bf))

        ya, q, k, v, gb, va = _inproj_sample(xs, params_s, tile=sample_tile)
        yb = _sample_attention(q, k, v, cache_kt, cache_vr, gb, *attn_params, layer=i,
                               n_streams=n_streams, n_new=n_new, lam_init=lam_init)
        xs = _outproj(xs, ya, yb, of_layer(w_out_bf), tile=sample_tile)
        ks_rows.append(k)
        vs_rows.append(v)
        sgu_rows.append(va)

    xp = _outproj(xp, *prompt_out, tile=OUTPROJ_TILE)
    new_k_prompt = jnp.transpose(
        k_stack.reshape(depth, batch, N_HEADS, 2, HALF_DIM, seq), (0, 1, 5, 2, 3, 4))
    return (
        xp.reshape(batch, seq, D_MODEL),
        xs.reshape(n_streams, n_new, D_MODEL),
        new_k_prompt,
        v_stack.reshape(depth, batch, seq, N_HEADS, HEAD_DIM),
        jnp.stack(ks_rows).reshape(depth, n_streams, n_new, N_HEADS, 2, HALF_DIM),
        jnp.stack(vs_rows).reshape(depth, n_streams, n_new, N_HEADS, HEAD_DIM),
        jnp.stack(sgu_rows).reshape(depth, n_streams, n_new, WIDTH),
    )
```

```python
import collections
import functools
import math
from typing import NamedTuple

import jax
import jax.numpy as jnp
from jax import lax
from jax.experimental import pallas as pl
from jax.experimental.pallas import tpu as pltpu

F32 = jnp.float32
BF16 = jnp.bfloat16

D_MODEL = 1024
N_HEADS = 4
HEAD_DIM = 128
HALF_DIM = 64
WIDTH = N_HEADS * HEAD_DIM
CHUNK = 64
SGU_CHUNK = 128
NORM_EPS = 1e-6
NEG_INF = -1e30
QK_SCALE = HALF_DIM ** -0.5
LOG2E = math.log2(math.e)
ALIBI_SLOPES = tuple(2.0 ** (-8.0 * (h + 1) / N_HEADS) for h in range(N_HEADS))

COL_U, COL_VA, COL_GA, COL_Q, COL_K, COL_V, COL_GB = (i * WIDTH for i in range(7))

VMEM_LIMIT_BYTES = 56 * 1024 * 1024

ATTN_TILE = 512
TOKEN_TILE = ATTN_TILE
SAMPLE_STREAMS_PER_STEP = 2
OUTPROJ_TILE = 2048
LANE_CHUNK = 256
KEY_WIDTH = 2 * HEAD_DIM
GROUP_MEAN_WIDTH = 256
POS_SPLIT = 3
VT_ROWS = HEAD_DIM + 16


def _lam_init(layer_idx):
    return 0.8 - 0.6 * math.exp(-0.3 * layer_idx)


def _compiler_params(n_axes):
    return pltpu.CompilerParams(
        dimension_semantics=("arbitrary",) * n_axes,
        vmem_limit_bytes=VMEM_LIMIT_BYTES,
    )


class _LayerParam(NamedTuple):
    array: jax.Array
    layer: int


def _array(p):
    return p.array if isinstance(p, _LayerParam) else p


def _const_spec(p, single_buffer=False):
    if isinstance(p, _LayerParam):
        block = (None, *p.array.shape[1:])
        index = (p.layer,) + (0,) * (p.array.ndim - 1)
    else:
        block, index = p.shape, (0,) * p.ndim
    kwargs = dict(pipeline_mode=pl.Buffered(1)) if single_buffer else {}
    return pl.BlockSpec(block, lambda *grid_idx: index, **kwargs)


def _dot(a, b):
    return jnp.dot(a, b, preferred_element_type=F32)


def _dot_nt(a, b):
    return lax.dot_general(a, b, (((1,), (1,)), ((), ())), preferred_element_type=F32)


def _lam_value(lq1_ref, lk1_ref, lq2_ref, lk2_ref, lam_init):
    d1 = jnp.sum(lq1_ref[...] * lk1_ref[...], axis=-1, keepdims=True)
    d2 = jnp.sum(lq2_ref[...] * lk2_ref[...], axis=-1, keepdims=True)
    return jnp.exp(d1) - jnp.exp(d2) + lam_init


def _head_cols(h):
    return slice(h * HEAD_DIM, (h + 1) * HEAD_DIM)


def _inproj_kernel(*refs, prompt, n_prev, fused):
    (x_ref, ng_ref, w_ref, sgug_ref, sguw_ref, sgub_ref, gq_ref, gk_ref,
     gm128_ref, gm64_ref) = refs[:10]
    n_in = 10
    if fused:
        ya_prev_ref, yb_prev_ref, wo_ref = refs[n_in:n_in + 3]
        n_in += 3
    prev = refs[n_in:n_in + n_prev]
    outs = refs[n_in + n_prev:]
    if fused:
        xo_ref, outs = outs[0], outs[1:]
    tile = x_ref.shape[0]

    def group_rms_scale(z, gm_ref):
        sq = (z * z).astype(BF16)
        gw = gm_ref.shape[0]
        mean_sq = jnp.concatenate(
            [_dot(sq[:, c:c + gw], gm_ref[...]) for c in range(0, WIDTH, gw)], axis=1)
        return lax.rsqrt(mean_sq + NORM_EPS)

    if prompt:
        ya_ref, q_ref, key_ref, kt_ref, v_ref, vt_ref, gb_ref = outs
        last = kt_ref.shape[0] - 1
        if n_prev:
            kt_prev_ref, v_prev_ref = prev
            kt_ref[:last] = kt_prev_ref[...]
            v_ref[:last] = v_prev_ref[...]
        for h in range(N_HEADS):
            vt_ref[h, HEAD_DIM:, :] = jnp.ones((VT_ROWS - HEAD_DIM, tile), BF16)
    else:
        ya_ref, q_ref, k_ref, v_ref, gb_ref, va_ref = outs

    x = x_ref[...]
    if fused:
        yb_prev = jnp.concatenate([yb_prev_ref[h] for h in range(N_HEADS)], axis=1)
        x = x + (_dot(ya_prev_ref[...], wo_ref[:WIDTH, :]) + _dot(yb_prev, wo_ref[WIDTH:, :]))
        xo_ref[...] = x
    hb = (x * ng_ref[...]).astype(BF16)
    row_scale = jnp.broadcast_to(
        lax.rsqrt(jnp.mean(x * x, axis=-1, keepdims=True) + NORM_EPS), (tile, WIDTH))

    def proj(col):
        return _dot(hb, w_ref[:, col:col + WIDTH]) * row_scale

    va = proj(COL_VA)
    if not prompt:
        va_ref[...] = va
    vn = ((va * group_rms_scale(va, gm128_ref)) * sgug_ref[...]).astype(BF16)
    gate = proj(COL_U) * jax.nn.silu(proj(COL_GA))
    for c in range(tile // SGU_CHUNK):
        rows = slice(c * SGU_CHUNK, (c + 1) * SGU_CHUNK)
        for g in range(N_HEADS):
            cols = _head_cols(g)
            mixed = _dot(sguw_ref[g], vn[rows, cols]) + sgub_ref[g]
            ya_ref[rows, cols] = (gate[rows, cols] * mixed).astype(BF16)

    zq = proj(COL_Q)
    qn = (zq * group_rms_scale(zq, gm64_ref)) * gq_ref[...] * (QK_SCALE * LOG2E)
    zk = proj(COL_K)
    kn = (zk * group_rms_scale(zk, gm64_ref)) * gk_ref[...]
    zv = proj(COL_V)
    gb = jax.nn.silu(proj(COL_GB)).astype(BF16)

    if prompt:
        pos = (lax.broadcasted_iota(jnp.int32, (tile, HEAD_DIM), 0)
               + pl.program_id(1) * tile)
        lane = lax.broadcasted_iota(jnp.int32, (tile, HEAD_DIM), 1)
        lo = jnp.bitwise_and(pos, CHUNK - 1)
        pos_cols = jnp.where(
            lane < POS_SPLIT, (pos - lo).astype(F32),
            jnp.where(lane < 2 * POS_SPLIT, lo.astype(F32), 0.0)).astype(BF16)
        kt_ref[last] = kn.T
        for h in range(N_HEADS):
            cols = _head_cols(h)
            q_ref[h] = qn[:, cols].T.astype(BF16)
            gb_ref[h] = gb[:, cols]
            key_ref[h, :, :HEAD_DIM] = kn[:, cols].astype(BF16)
            key_ref[h, :, HEAD_DIM:] = pos_cols
            vt_ref[h, :HEAD_DIM, :] = zv[:, cols].T.astype(BF16)
            v_ref[last, pl.ds(h, tile, stride=N_HEADS), :] = zv[:, cols]
    else:
        q_ref[...] = qn.astype(BF16)
        k_ref[...] = kn
        v_ref[...] = zv
        gb_ref[...] = gb


def _inproj_prompt(x, params, prev_out, k_stack, v_stack, *, batch, seq):
    tile = TOKEN_TILE
    n_t = seq // tile
    n = batch * seq
    n_layers = 1 if k_stack is None else k_stack.shape[0] + 1
    full = lambda p: _const_spec(p, single_buffer=True)
    head_major = lambda width: pl.BlockSpec((None, N_HEADS, tile, width),
                                            lambda b, i: (b, 0, i, 0))
    kt_spec = lambda layers: pl.BlockSpec((layers, None, WIDTH, tile),
                                          lambda b, i: (0, b, 0, i))
    v_spec = lambda layers: pl.BlockSpec((layers, tile * N_HEADS, HEAD_DIM),
                                         lambda b, i: (0, b * n_t + i, 0))
    x_spec = pl.BlockSpec((tile, D_MODEL), lambda b, i: (b * n_t + i, 0))
    ya_spec = pl.BlockSpec((tile, WIDTH), lambda b, i: (b * n_t + i, 0))
    in_specs = [x_spec] + [full(p) for p in params]
    operands = [x] + [_array(p) for p in params]
    fused = prev_out is not None
    if fused:
        ya_prev, yb_prev, w_out_prev = prev_out
        in_specs += [ya_spec, head_major(HEAD_DIM), full(w_out_prev)]
        operands += [ya_prev, yb_prev, _array(w_out_prev)]
    if n_layers > 1:
        in_specs += [kt_spec(n_layers - 1), v_spec(n_layers - 1)]
        operands += [k_stack, v_stack]
    out_shape = [
        jax.ShapeDtypeStruct((n, WIDTH), BF16),
        jax.ShapeDtypeStruct((batch, N_HEADS, HEAD_DIM, seq), BF16),
        jax.ShapeDtypeStruct((batch, N_HEADS, n_t, tile, KEY_WIDTH), BF16),
        jax.ShapeDtypeStruct((n_layers, batch, WIDTH, seq), F32),
        jax.ShapeDtypeStruct((n_layers, n * N_HEADS, HEAD_DIM), F32),
        jax.ShapeDtypeStruct((batch, N_HEADS, n_t, VT_ROWS, tile), BF16),
        jax.ShapeDtypeStruct((batch, N_HEADS, seq, HEAD_DIM), BF16),
    ]
    out_specs = [
        ya_spec,
        pl.BlockSpec((None, N_HEADS, HEAD_DIM, tile), lambda b, i: (b, 0, 0, i)),
        pl.BlockSpec((None, N_HEADS, None, tile, KEY_WIDTH), lambda b, i: (b, 0, i, 0, 0)),
        kt_spec(n_layers),
        v_spec(n_layers),
        pl.BlockSpec((None, N_HEADS, None, VT_ROWS, tile), lambda b, i: (b, 0, i, 0, 0)),
        head_major(HEAD_DIM),
    ]
    if fused:
        out_shape.insert(0, jax.ShapeDtypeStruct((n, D_MODEL), F32))
        out_specs.insert(0, x_spec)
    outs = pl.pallas_call(
        functools.partial(_inproj_kernel, prompt=True, n_prev=2 * (n_layers > 1), fused=fused),
        grid=(batch, n_t),
        in_specs=in_specs,
        out_specs=out_specs,
        out_shape=out_shape,
        compiler_params=_compiler_params(2),
        name="inproj_prompt",
    )(*operands)
    return outs if fused else (x, *outs)


def _inproj_sample(x, params, *, tile):
    n = x.shape[0]
    row_spec = lambda width: pl.BlockSpec((tile, width), lambda i: (i, 0))
    dtypes = (BF16, BF16, F32, F32, BF16, F32)
    return pl.pallas_call(
        functools.partial(_inproj_kernel, prompt=False, n_prev=0, fused=False),
        grid=(n // tile,),
        in_specs=[row_spec(D_MODEL)] + [_const_spec(p) for p in params],
        out_specs=[row_spec(WIDTH)] * len(dtypes),
        out_shape=[jax.ShapeDtypeStruct((n, WIDTH), d) for d in dtypes],
        compiler_params=_compiler_params(1),
        name="inproj_sample",
    )(x, *[_array(p) for p in params])


def _outproj_kernel(x_ref, ya_ref, yb_ref, w_ref, o_ref, *, head_major):
    if head_major:
        yb = jnp.concatenate([yb_ref[h] for h in range(N_HEADS)], axis=1)
    else:
        yb = yb_ref[...]
    y = _dot(ya_ref[...], w_ref[:WIDTH, :]) + _dot(yb, w_ref[WIDTH:, :])
    o_ref[...] = x_ref[...] + y


def _outproj(x, ya, yb, w_bf, *, tile):
    n = x.shape[0]
    head_major = yb.ndim == 4
    if head_major:
        n_t = yb.shape[2] // tile
        yb_spec = pl.BlockSpec((None, N_HEADS, tile, HEAD_DIM),
                               lambda i: (i // n_t, 0, i % n_t, 0))
    else:
        yb_spec = pl.BlockSpec((tile, WIDTH), lambda i: (i, 0))
    return pl.pallas_call(
        functools.partial(_outproj_kernel, head_major=head_major),
        grid=(n // tile,),
        in_specs=[
            pl.BlockSpec((tile, D_MODEL), lambda i: (i, 0)),
            pl.BlockSpec((tile, WIDTH), lambda i: (i, 0)),
            yb_spec,
            _const_spec(w_bf),
        ],
        out_specs=pl.BlockSpec((tile, D_MODEL), lambda i: (i, 0)),
        out_shape=jax.ShapeDtypeStruct((n, D_MODEL), F32),
        compiler_params=_compiler_params(1),
        name="outproj",
    )(x, ya, yb, _array(w_bf))


SCORES_AHEAD = 3
VALUES_BEHIND = 1
ATTN_HEADS_PER_STEP = 2


def _attn_units(seq):
    n_qc = seq // LANE_CHUNK
    qc_per_tile = ATTN_TILE // LANE_CHUNK
    return [(j, qc, half)
            for j in range(seq // ATTN_TILE)
            for qc in range(j * qc_per_tile, n_qc)
            for half in range(2)]


def _attn_kernel(slopes_ref, q_ref, key_ref, vt_ref, gb_ref, sg_ref, lq1_ref, lk1_ref,
                 lq2_ref, lk2_ref, o_ref, *scratch, lam_init):
    heads_per_step = q_ref.shape[0]

    def one_head(hh, carry):
        slope = slopes_ref[pl.program_id(1) * heads_per_step + hh]
        _attn_head(slope, q_ref.at[hh], key_ref.at[hh], vt_ref.at[hh], gb_ref.at[hh], sg_ref,
                   lq1_ref, lk1_ref, lq2_ref, lk2_ref, o_ref.at[hh], *scratch,
                   lam_init=lam_init)
        return carry

    lax.fori_loop(0, heads_per_step, one_head, 0)


def _attn_head(slope, q_ref, key_ref, vt_ref, gb_ref, sg_ref, lq1_ref, lk1_ref,
               lq2_ref, lk2_ref, o_ref, qbd_ref, s_ref, m_ref, acc_ref, *, lam_init):
    seq = q_ref.shape[1]
    t, w = ATTN_TILE, LANE_CHUNK
    assert t == 2 * w
    n_qc = seq // w
    slope2 = slope * LOG2E

    key = lax.broadcasted_iota(jnp.int32, (w, w), 0)
    qry = lax.broadcasted_iota(jnp.int32, (w, w), 1)
    allowed = jnp.right_shift(key, 6) <= jnp.right_shift(qry, 6)
    ahead = jnp.maximum(key - qry, 0).astype(F32)
    dt = jnp.where(allowed, (-2.0 * slope2) * ahead, NEG_INF)

    row = lax.broadcasted_iota(jnp.int32, (HEAD_DIM, w), 0)
    rest = jnp.full((HEAD_DIM, w), slope2, F32)
    slope_rows = jnp.zeros((HEAD_DIM, w), F32)
    for piece in range(POS_SPLIT):
        part = rest.astype(BF16).astype(F32)
        slope_rows = jnp.where((row == piece) | (row == piece + POS_SPLIT), part, slope_rows)
        rest = rest - part
    slope_rows = slope_rows.astype(BF16)
    zero_half = jnp.zeros((HALF_DIM, w), BF16)

    lam = _lam_value(lq1_ref, lk1_ref, lq2_ref, lk2_ref, lam_init)

    def n_keys(j, qc):
        return w if qc * w == j * t else t

    def stacked_queries(qc, half):
        lanes = slice((half * n_qc + qc) * w, (half * n_qc + qc + 1) * w)
        own = slice(half * HALF_DIM, (half + 1) * HALF_DIM)
        other = slice((1 - half) * HALF_DIM, (2 - half) * HALF_DIM)
        qbd_ref[own, lanes] = q_ref[own, qc * w:(qc + 1) * w]
        qbd_ref[other, lanes] = zero_half
        qbd_ref[HEAD_DIM:, lanes] = slope_rows
        return lanes

    def scores(unit):
        j, qc, half = unit
        if j == 0:
            lanes = stacked_queries(qc, half)
        else:
            lanes = slice((half * n_qc + qc) * w, (half * n_qc + qc + 1) * w)
        return _dot(key_ref[j, :n_keys(j, qc), :], qbd_ref[:, lanes])

    def softmax(unit, st):
        j, qc, half = unit
        lanes = slice((half * n_qc + qc) * w, (half * n_qc + qc + 1) * w)
        nk = n_keys(j, qc)
        if qc * w < (j + 1) * t:
            st = st + dt if nk == w else jnp.concatenate([st[:w], st[w:] + dt], axis=0)
        m_cur = jnp.max(st, axis=0, keepdims=True)
        if j == 0:
            m_new, alpha = m_cur, None
        else:
            m_old = m_ref[:, lanes]
            m_new = jnp.maximum(m_old, m_cur)
            alpha = jnp.exp2(m_old - m_new)
        m_ref[:, lanes] = m_new
        return jnp.exp2(st - m_new).astype(BF16), alpha

    def values(unit, p, alpha):
        j, qc, half = unit
        lanes = slice((half * n_qc + qc) * w, (half * n_qc + qc + 1) * w)
        pv = _dot(vt_ref[j, :, :n_keys(j, qc)], p)
        acc_ref[:, lanes] = pv if j == 0 else alpha * acc_ref[:, lanes] + pv

    def normalized(lanes):
        return acc_ref[:HEAD_DIM, lanes] * (1.0 / acc_ref[HEAD_DIM:HEAD_DIM + 1, lanes])

    def finalize(qc):
        rows = slice(qc * w, (qc + 1) * w)
        lanes1 = slice(qc * w, (qc + 1) * w)
        lanes2 = slice((n_qc + qc) * w, (n_qc + qc + 1) * w)
        ot = normalized(lanes1) - lam * normalized(lanes2)
        ot = ot * lax.rsqrt(jnp.mean(ot * ot, axis=0, keepdims=True) + NORM_EPS)
        o = (ot.T * sg_ref[...]) * (1.0 - lam_init)
        o_ref[rows, :] = (o * gb_ref[rows, :].astype(F32)).astype(BF16)

    units = _attn_units(seq)
    n_slots = s_ref.shape[0]

    def emit_scores(u):
        nk = n_keys(*units[u][:2])
        s_ref[u % n_slots, :nk, :] = scores(units[u])

    def load_scores(u):
        return s_ref[u % n_slots, :n_keys(*units[u][:2]), :]

    for u in range(min(SCORES_AHEAD, len(units))):
        emit_scores(u)
    pending_values = collections.deque()

    def run_oldest_values():
        unit_done, p, alpha = pending_values.popleft()
        values(unit_done, p, alpha)
        j_done, qc_done, half_done = unit_done
        if half_done == 1 and qc_done * w < (j_done + 1) * t:
            finalize(qc_done)

    for u, unit in enumerate(units):
        if u + SCORES_AHEAD < len(units):
            emit_scores(u + SCORES_AHEAD)
        pending_values.append((unit, *softmax(unit, load_scores(u))))
        if len(pending_values) > VALUES_BEHIND:
            run_oldest_values()
    while pending_values:
        run_oldest_values()


def _prompt_attention(q, keys, vt, gb, slopes, sg, lq1, lk1, lq2, lk2, *, lam_init):
    batch, _, _, seq = q.shape
    t = ATTN_TILE
    n_tiles = seq // t
    hps = ATTN_HEADS_PER_STEP
    seq_spec = pl.BlockSpec((None, hps, seq, HEAD_DIM), lambda b, h: (b, h, 0, 0))
    vec = _const_spec
    return pl.pallas_call(
        functools.partial(_attn_kernel, lam_init=lam_init),
        grid=(batch, N_HEADS // hps),
        in_specs=[
            pl.BlockSpec(memory_space=pltpu.SMEM),
            pl.BlockSpec((None, hps, HEAD_DIM, seq), lambda b, h: (b, h, 0, 0)),
            pl.BlockSpec((None, hps, n_tiles, t, KEY_WIDTH), lambda b, h: (b, h, 0, 0, 0)),
            pl.BlockSpec((None, hps, n_tiles, VT_ROWS, t), lambda b, h: (b, h, 0, 0, 0)),
            seq_spec, vec(sg), vec(lq1), vec(lk1), vec(lq2), vec(lk2)],
        out_specs=seq_spec,
        out_shape=jax.ShapeDtypeStruct((batch, N_HEADS, seq, HEAD_DIM), BF16),
        scratch_shapes=[
            pltpu.VMEM((KEY_WIDTH, 2 * seq), BF16),
            pltpu.VMEM((SCORES_AHEAD + 1, t, LANE_CHUNK), F32),
            pltpu.VMEM((1, 2 * seq), F32),
            pltpu.VMEM((VT_ROWS, 2 * seq), F32),
        ],
        compiler_params=_compiler_params(2),
        name="prompt_attn",
    )(slopes, q, keys, vt, gb, *[_array(p) for p in (sg, lq1, lk1, lq2, lk2)])


def _sample_attn_kernel(q_ref, kn_ref, vn_ref, ckt_ref, cv_ref, gb_ref, *rest, **static):
    for s in range(q_ref.shape[0]):
        _sample_attn_stream(q_ref.at[s], kn_ref.at[s], vn_ref.at[s], ckt_ref.at[s], cv_ref.at[s],
                            gb_ref.at[s], *rest[:-1], rest[-1].at[s], **static)


def _sample_attn_stream(q_ref, kn_ref, vn_ref, ckt_ref, cv_ref, gb_ref, sg_ref,
                        lq1_ref, lk1_ref, lq2_ref, lk2_ref, o_ref, *, lam_init, past_len):
    nq = q_ref.shape[0]
    per_head = 2 * nq
    n_rows = N_HEADS * per_head
    lam = _lam_value(lq1_ref, lk1_ref, lq2_ref, lk2_ref, lam_init)

    def alibi(n_cols, key_pos0):
        row = lax.broadcasted_iota(jnp.int32, (n_rows, n_cols), 0)
        col = lax.broadcasted_iota(jnp.int32, (n_rows, n_cols), 1)
        dist = jnp.abs(past_len + lax.rem(row, nq) - (key_pos0 + col)).astype(F32)
        slope = jnp.zeros((n_rows, n_cols), F32)
        for h in range(N_HEADS):
            slope = jnp.where(lax.div(row, per_head) == h, ALIBI_SLOPES[h] * LOG2E, slope)
        return slope * dist

    zpad = jnp.zeros((HEAD_DIM - nq, HEAD_DIM), BF16)
    s_past, s_new, v_new = [], [], []
    for h in range(N_HEADS):
        cols = _head_cols(h)
        q = q_ref[:, cols]
        lane = lax.broadcasted_iota(jnp.int32, q.shape, 1)
        zero = jnp.zeros_like(q)
        qbd = jnp.concatenate([jnp.where(lane < HALF_DIM, q, zero),
                               jnp.where(lane >= HALF_DIM, q, zero)], axis=0)
        s_past.append(_dot(qbd, ckt_ref[cols, :].astype(BF16)))
        s_new.append(_dot_nt(qbd, jnp.concatenate([kn_ref[:, cols].astype(BF16), zpad], axis=0)))
        v_new.append(jnp.concatenate([vn_ref[:, cols].astype(BF16), zpad], axis=0))
    s_past = jnp.concatenate(s_past, axis=0) - alibi(past_len, 0)
    s_new = jnp.concatenate(s_new, axis=0) - alibi(HEAD_DIM, past_len)
    real_new = lax.broadcasted_iota(jnp.int32, s_new.shape, 1) < nq
    s_new = jnp.where(real_new, s_new, NEG_INF)
    m = jnp.maximum(jnp.max(s_past, axis=-1, keepdims=True),
                    jnp.max(s_new, axis=-1, keepdims=True))
    p_past = jnp.exp2(s_past - m)
    p_new = jnp.exp2(s_new - m)
    inv_l = 1.0 / (jnp.sum(p_past, axis=-1, keepdims=True)
                   + jnp.sum(p_new, axis=-1, keepdims=True))
    p_past = p_past.astype(BF16)
    p_new = p_new.astype(BF16)
    outs = []
    for h in range(N_HEADS):
        rows = slice(h * per_head, (h + 1) * per_head)
        v_past = cv_ref[pl.ds(h, past_len, stride=N_HEADS), :].astype(BF16)
        acc = (_dot(p_past[rows], v_past) + _dot(p_new[rows], v_new[h])) * inv_l[rows]
        o = acc[:nq] - lam * acc[nq:]
        outs.append(o * lax.rsqrt(jnp.mean(o * o, axis=-1, keepdims=True) + NORM_EPS))
    o = jnp.concatenate(outs, axis=1) * jnp.tile(sg_ref[...], (1, N_HEADS)) * (1.0 - lam_init)
    o_ref[...] = (o * gb_ref[...].astype(F32)).astype(BF16)


def _sample_attention(q, k_new, v_new, cache_kt, cache_v, gb, sg, lq1, lk1, lq2, lk2,
                      *, layer, n_streams, n_new, lam_init):
    past_len = cache_kt.shape[3]
    per_step = SAMPLE_STREAMS_PER_STEP
    assert n_streams % per_step == 0
    new_spec = pl.BlockSpec((per_step, n_new, WIDTH), lambda b: (b, 0, 0))
    vec = _const_spec
    r3 = lambda a: a.reshape(n_streams, n_new, WIDTH)
    out = pl.pallas_call(
        functools.partial(_sample_attn_kernel, lam_init=lam_init, past_len=past_len),
        grid=(n_streams // per_step,),
        in_specs=[
            new_spec, new_spec, new_spec,
            pl.BlockSpec((None, per_step, WIDTH, past_len), lambda b: (layer, b, 0, 0)),
            pl.BlockSpec((None, per_step, past_len * N_HEADS, HEAD_DIM),
                         lambda b: (layer, b, 0, 0)),
            new_spec, vec(sg), vec(lq1), vec(lk1), vec(lq2), vec(lk2)],
        out_specs=new_spec,
        out_shape=jax.ShapeDtypeStruct((n_streams, n_new, WIDTH), BF16),
        compiler_params=_compiler_params(1),
        name="sample_attn",
    )(r3(q), r3(k_new), r3(v_new), cache_kt, cache_v, r3(gb),
      *[_array(p) for p in (sg, lq1, lk1, lq2, lk2)])
    return out.reshape(n_streams * n_new, WIDTH)


def _group_mean_matrix(group):
    idx = jnp.arange(GROUP_MEAN_WIDTH) // group
    return jnp.where(idx[:, None] == idx[None, :], 1.0 / group, 0.0).astype(BF16)


def kernel(x_prompt, x_sample, cache_k, cache_v, norm_g, w_in, sgu_norm_g, sgu_w, sgu_b,
           q_norm_g, k_norm_g, lambda_q1, lambda_k1, lambda_q2, lambda_k2, subln_g, w_out):
    depth = w_in.shape[0]
    batch, seq, _ = x_prompt.shape
    n_streams, n_new, _ = x_sample.shape
    past_len = cache_k.shape[2]
    assert seq % ATTN_TILE == 0
    assert SGU_CHUNK % n_new == 0 and past_len % CHUNK == 0 and n_new <= CHUNK
    sample_tile = min(n_streams * n_new, TOKEN_TILE)
    assert sample_tile % SGU_CHUNK == 0 and (n_streams * n_new) % sample_tile == 0

    gm128 = _group_mean_matrix(HEAD_DIM)
    gm64 = _group_mean_matrix(HALF_DIM)
    slopes = jnp.asarray(ALIBI_SLOPES, F32)
    tril = jnp.tril(jnp.ones((SGU_CHUNK, SGU_CHUNK), F32))
    tril_new = jnp.tril(jnp.ones((n_new, n_new), F32))
    streams_per_chunk = SGU_CHUNK // n_new
    eye = jnp.eye(streams_per_chunk, dtype=F32)
    cache_kt = jnp.transpose(cache_k, (0, 1, 3, 4, 5, 2)).reshape(depth, n_streams, WIDTH, past_len)
    cache_vr = cache_v.reshape(depth, n_streams, past_len * N_HEADS, HEAD_DIM)

    rows = lambda a: a.reshape(depth, 1, -1).astype(F32)
    w_in_bf = w_in.astype(BF16)
    w_out_bf = w_out.astype(BF16)
    ng = rows(norm_g)
    sgug = rows(sgu_norm_g)
    gq = rows(jnp.tile(q_norm_g, (1, WIDTH // HALF_DIM)))
    gk = rows(jnp.tile(k_norm_g, (1, WIDTH // HALF_DIM)))
    sg = rows(subln_g)
    lam_vecs = tuple(rows(a) for a in (lambda_q1, lambda_k1, lambda_q2, lambda_k2))
    sguw_p = (sgu_w * tril).astype(BF16)
    sgub_p = jnp.broadcast_to(sgu_b[..., None], (depth, N_HEADS, SGU_CHUNK, HEAD_DIM))
    w_new = sgu_w[:, :, :n_new, :n_new] * tril_new
    sguw_s = jnp.einsum("ab,lhts->lhatbs", eye, w_new).reshape(
        depth, N_HEADS, SGU_CHUNK, SGU_CHUNK).astype(BF16)
    sgub_s = jnp.broadcast_to(
        jnp.tile(sgu_b[:, :, :n_new], (1, 1, streams_per_chunk))[..., None],
        (depth, N_HEADS, SGU_CHUNK, HEAD_DIM))

    xp = x_prompt.reshape(batch * seq, D_MODEL)
    xs = x_sample.reshape(n_streams * n_new, D_MODEL)
    k_stack = v_stack = prompt_out = None
    ks_rows, vs_rows, sgu_rows = [], [], []
    for i in range(depth):
        lam_init = _lam_init(i)
        of_layer = lambda a: _LayerParam(a, i)
        params_p = (*map(of_layer, (ng, w_in_bf, sgug, sguw_p, sgub_p, gq, gk)), gm128, gm64)
        params_s = (*map(of_layer, (ng, w_in_bf, sgug, sguw_s, sgub_s, gq, gk)), gm128, gm64)
        attn_params = tuple(map(of_layer, (sg, *lam_vecs)))

        xp, ya, q, keys, k_stack, v_stack, vt, gb = _inproj_prompt(
            xp, params_p, prompt_out, k_stack, v_stack, batch=batch, seq=seq)
        yb = _prompt_attention(q, keys, vt, gb, slopes, *attn_params, lam_init=lam_init)
        prompt_out = (ya, yb, of_layer(w_out_bf))

        ya, q, k, v, gb, va = _inproj_sample(xs, params_s, tile=sample_tile)
        yb = _sample_attention(q, k, v, cache_kt, cache_vr, gb, *attn_params, layer=i,
                               n_streams=n_streams, n_new=n_new, lam_init=lam_init)
        xs = _outproj(xs, ya, yb, of_layer(w_out_bf), tile=sample_tile)
        ks_rows.append(k)
        vs_rows.append(v)
        sgu_rows.append(va)

    xp = _outproj(xp, *prompt_out, tile=OUTPROJ_TILE)
    new_k_prompt = jnp.transpose(
        k_stack.reshape(depth, batch, N_HEADS, 2, HALF_DIM, seq), (0, 1, 5, 2, 3, 4))
    return (
        xp.reshape(batch, seq, D_MODEL),
        xs.reshape(n_streams, n_new, D_MODEL),
        new_k_prompt,
        v_stack.reshape(depth, batch, seq, N_HEADS, HEAD_DIM),
        jnp.stack(ks_rows).reshape(depth, n_streams, n_new, N_HEADS, 2, HALF_DIM),
        jnp.stack(vs_rows).reshape(depth, n_streams, n_new, N_HEADS, HEAD_DIM),
        jnp.stack(sgu_rows).reshape(depth, n_streams, n_new, WIDTH),
    )
```

```python
import collections
import functools
import math
from typing import NamedTuple

import jax
import jax.numpy as jnp
from jax import lax
from jax.experimental import pallas as pl
from jax.experimental.pallas import tpu as pltpu

F32 = jnp.float32
BF16 = jnp.bfloat16

D_MODEL = 1024
N_HEADS = 4
HEAD_DIM = 128
HALF_DIM = 64
WIDTH = N_HEADS * HEAD_DIM
CHUNK = 64
SGU_CHUNK = 128
NORM_EPS = 1e-6
NEG_INF = -1e30
QK_SCALE = HALF_DIM ** -0.5
LOG2E = math.log2(math.e)
ALIBI_SLOPES = tuple(2.0 ** (-8.0 * (h + 1) / N_HEADS) for h in range(N_HEADS))

COL_U, COL_VA, COL_GA, COL_Q, COL_K, COL_V, COL_GB = (i * WIDTH for i in range(7))

VMEM_LIMIT_BYTES = 56 * 1024 * 1024

ATTN_TILE = 512
TOKEN_TILE = ATTN_TILE
SAMPLE_STREAMS_PER_STEP = 2
OUTPROJ_TILE = 2048
LANE_CHUNK = 256
KEY_WIDTH = 2 * HEAD_DIM
POS_SPLIT = 3
VT_ROWS = HEAD_DIM + 16


def _lam_init(layer_idx):
    return 0.8 - 0.6 * math.exp(-0.3 * layer_idx)


def _compiler_params(n_axes):
    return pltpu.CompilerParams(
        dimension_semantics=("arbitrary",) * n_axes,
        vmem_limit_bytes=VMEM_LIMIT_BYTES,
    )


class _LayerParam(NamedTuple):
    array: jax.Array
    layer: int


def _array(p):
    return p.array if isinstance(p, _LayerParam) else p


def _const_spec(p, single_buffer=False):
    if isinstance(p, _LayerParam):
        block = (None, *p.array.shape[1:])
        index = (p.layer,) + (0,) * (p.array.ndim - 1)
    else:
        block, index = p.shape, (0,) * p.ndim
    kwargs = dict(pipeline_mode=pl.Buffered(1)) if single_buffer else {}
    return pl.BlockSpec(block, lambda *grid_idx: index, **kwargs)


def _dot(a, b):
    return jnp.dot(a, b, preferred_element_type=F32)


def _dot_nt(a, b):
    return lax.dot_general(a, b, (((1,), (1,)), ((), ())), preferred_element_type=F32)


def _lam_value(lq1_ref, lk1_ref, lq2_ref, lk2_ref, lam_init):
    d1 = jnp.sum(lq1_ref[...] * lk1_ref[...], axis=-1, keepdims=True)
    d2 = jnp.sum(lq2_ref[...] * lk2_ref[...], axis=-1, keepdims=True)
    return jnp.exp(d1) - jnp.exp(d2) + lam_init


def _head_cols(h):
    return slice(h * HEAD_DIM, (h + 1) * HEAD_DIM)


def _inproj_kernel(*refs, prompt, n_prev, fused):
    x_ref, ng_ref, w_ref, sgug_ref, sguw_ref, sgub_ref, gq_ref, gk_ref = refs[:8]
    n_in = 8
    if fused:
        ya_prev_ref, yb_prev_ref, wo_ref = refs[n_in:n_in + 3]
        n_in += 3
    prev = refs[n_in:n_in + n_prev]
    outs = refs[n_in + n_prev:]
    if fused:
        xo_ref, outs = outs[0], outs[1:]
    tile = x_ref.shape[0]

    first_half = lax.broadcasted_iota(jnp.int32, (tile, HEAD_DIM), 1) < HALF_DIM

    def group_rms_scale(z, group):
        scales = []
        for g in range(WIDTH // HEAD_DIM):
            sq = z[:, _head_cols(g)] ** 2
            if group == HEAD_DIM:
                total = jnp.sum(sq, axis=-1, keepdims=True)
            else:
                total = jnp.where(
                    first_half,
                    jnp.sum(jnp.where(first_half, sq, 0.0), axis=-1, keepdims=True),
                    jnp.sum(jnp.where(first_half, 0.0, sq), axis=-1, keepdims=True))
            scales.append(jnp.broadcast_to(lax.rsqrt(total * (1.0 / group) + NORM_EPS),
                                           (tile, HEAD_DIM)))
        return jnp.concatenate(scales, axis=1)

    if prompt:
        ya_ref, q_ref, key_ref, kt_ref, v_ref, vt_ref, gb_ref = outs
        last = kt_ref.shape[0] - 1
        if n_prev:
            kt_prev_ref, v_prev_ref = prev
            kt_ref[:last] = kt_prev_ref[...]
            v_ref[:last] = v_prev_ref[...]
        for h in range(N_HEADS):
            vt_ref[h, HEAD_DIM:, :] = jnp.ones((VT_ROWS - HEAD_DIM, tile), BF16)
    else:
        ya_ref, q_ref, k_ref, v_ref, gb_ref, va_ref = outs

    x = x_ref[...]
    if fused:
        yb_prev = jnp.concatenate([yb_prev_ref[h] for h in range(N_HEADS)], axis=1)
        x = x + (_dot(ya_prev_ref[...], wo_ref[:WIDTH, :]) + _dot(yb_prev, wo_ref[WIDTH:, :]))
        xo_ref[...] = x
    hb = (x * ng_ref[...]).astype(BF16)
    row_scale = jnp.broadcast_to(
        lax.rsqrt(jnp.mean(x * x, axis=-1, keepdims=True) + NORM_EPS), (tile, WIDTH))

    def proj(col):
        return _dot(hb, w_ref[:, col:col + WIDTH]) * row_scale

    va = proj(COL_VA)
    if not prompt:
        va_ref[...] = va
    vn = ((va * group_rms_scale(va, HEAD_DIM)) * sgug_ref[...]).astype(BF16)
    gate = proj(COL_U) * jax.nn.silu(proj(COL_GA))
    for c in range(tile // SGU_CHUNK):
        rows = slice(c * SGU_CHUNK, (c + 1) * SGU_CHUNK)
        for g in range(N_HEADS):
            cols = _head_cols(g)
            mixed = _dot(sguw_ref[g], vn[rows, cols]) + sgub_ref[g]
            ya_ref[rows, cols] = (gate[rows, cols] * mixed).astype(BF16)

    zq = proj(COL_Q)
    qn = (zq * group_rms_scale(zq, HALF_DIM)) * gq_ref[...] * (QK_SCALE * LOG2E)
    zk = proj(COL_K)
    kn = (zk * group_rms_scale(zk, HALF_DIM)) * gk_ref[...]
    zv = proj(COL_V)
    gb = jax.nn.silu(proj(COL_GB)).astype(BF16)

    if prompt:
        pos = (lax.broadcasted_iota(jnp.int32, (tile, HEAD_DIM), 0)
               + pl.program_id(1) * tile)
        lane = lax.broadcasted_iota(jnp.int32, (tile, HEAD_DIM), 1)
        lo = jnp.bitwise_and(pos, CHUNK - 1)
        pos_cols = jnp.where(
            lane < POS_SPLIT, (pos - lo).astype(F32),
            jnp.where(lane < 2 * POS_SPLIT, lo.astype(F32), 0.0)).astype(BF16)
        kt_ref[last] = kn.T
        for h in range(N_HEADS):
            cols = _head_cols(h)
            q_ref[h] = qn[:, cols].T.astype(BF16)
            gb_ref[h] = gb[:, cols]
            key_ref[h, :, :HEAD_DIM] = kn[:, cols].astype(BF16)
            key_ref[h, :, HEAD_DIM:] = pos_cols
            vt_ref[h, :HEAD_DIM, :] = zv[:, cols].T.astype(BF16)
            v_ref[last, pl.ds(h, tile, stride=N_HEADS), :] = zv[:, cols]
    else:
        q_ref[...] = qn.astype(BF16)
        k_ref[...] = kn
        v_ref[...] = zv
        gb_ref[...] = gb


def _inproj_prompt(x, params, prev_out, k_stack, v_stack, *, batch, seq):
    tile = TOKEN_TILE
    n_t = seq // tile
    n = batch * seq
    n_layers = 1 if k_stack is None else k_stack.shape[0] + 1
    full = lambda p: _const_spec(p, single_buffer=True)
    head_major = lambda width: pl.BlockSpec((None, N_HEADS, tile, width),
                                            lambda b, i: (b, 0, i, 0))
    kt_spec = lambda layers: pl.BlockSpec((layers, None, WIDTH, tile),
                                          lambda b, i: (0, b, 0, i))
    v_spec = lambda layers: pl.BlockSpec((layers, tile * N_HEADS, HEAD_DIM),
                                         lambda b, i: (0, b * n_t + i, 0))
    x_spec = pl.BlockSpec((tile, D_MODEL), lambda b, i: (b * n_t + i, 0))
    ya_spec = pl.BlockSpec((tile, WIDTH), lambda b, i: (b * n_t + i, 0))
    in_specs = [x_spec] + [full(p) for p in params]
    operands = [x] + [_array(p) for p in params]
    fused = prev_out is not None
    if fused:
        ya_prev, yb_prev, w_out_prev = prev_out
        in_specs += [ya_spec, head_major(HEAD_DIM), full(w_out_prev)]
        operands += [ya_prev, yb_prev, _array(w_out_prev)]
    if n_layers > 1:
        in_specs += [kt_spec(n_layers - 1), v_spec(n_layers - 1)]
        operands += [k_stack, v_stack]
    out_shape = [
        jax.ShapeDtypeStruct((n, WIDTH), BF16),
        jax.ShapeDtypeStruct((batch, N_HEADS, HEAD_DIM, seq), BF16),
        jax.ShapeDtypeStruct((batch, N_HEADS, n_t, tile, KEY_WIDTH), BF16),
        jax.ShapeDtypeStruct((n_layers, batch, WIDTH, seq), F32),
        jax.ShapeDtypeStruct((n_layers, n * N_HEADS, HEAD_DIM), F32),
        jax.ShapeDtypeStruct((batch, N_HEADS, n_t, VT_ROWS, tile), BF16),
        jax.ShapeDtypeStruct((batch, N_HEADS, seq, HEAD_DIM), BF16),
    ]
    out_specs = [
        ya_spec,
        pl.BlockSpec((None, N_HEADS, HEAD_DIM, tile), lambda b, i: (b, 0, 0, i)),
        pl.BlockSpec((None, N_HEADS, None, tile, KEY_WIDTH), lambda b, i: (b, 0, i, 0, 0)),
        kt_spec(n_layers),
        v_spec(n_layers),
        pl.BlockSpec((None, N_HEADS, None, VT_ROWS, tile), lambda b, i: (b, 0, i, 0, 0)),
        head_major(HEAD_DIM),
    ]
    if fused:
        out_shape.insert(0, jax.ShapeDtypeStruct((n, D_MODEL), F32))
        out_specs.insert(0, x_spec)
    outs = pl.pallas_call(
        functools.partial(_inproj_kernel, prompt=True, n_prev=2 * (n_layers > 1), fused=fused),
        grid=(batch, n_t),
        in_specs=in_specs,
        out_specs=out_specs,
        out_shape=out_shape,
        compiler_params=_compiler_params(2),
        name="inproj_prompt",
    )(*operands)
    return outs if fused else (x, *outs)


def _inproj_sample(x, params, *, tile):
    n = x.shape[0]
    row_spec = lambda width: pl.BlockSpec((tile, width), lambda i: (i, 0))
    dtypes = (BF16, BF16, F32, F32, BF16, F32)
    return pl.pallas_call(
        functools.partial(_inproj_kernel, prompt=False, n_prev=0, fused=False),
        grid=(n // tile,),
        in_specs=[row_spec(D_MODEL)] + [_const_spec(p) for p in params],
        out_specs=[row_spec(WIDTH)] * len(dtypes),
        out_shape=[jax.ShapeDtypeStruct((n, WIDTH), d) for d in dtypes],
        compiler_params=_compiler_params(1),
        name="inproj_sample",
    )(x, *[_array(p) for p in params])


def _outproj_kernel(x_ref, ya_ref, yb_ref, w_ref, o_ref, *, head_major):
    if head_major:
        yb = jnp.concatenate([yb_ref[h] for h in range(N_HEADS)], axis=1)
    else:
        yb = yb_ref[...]
    y = _dot(ya_ref[...], w_ref[:WIDTH, :]) + _dot(yb, w_ref[WIDTH:, :])
    o_ref[...] = x_ref[...] + y


def _outproj(x, ya, yb, w_bf, *, tile):
    n = x.shape[0]
    head_major = yb.ndim == 4
    if head_major:
        n_t = yb.shape[2] // tile
        yb_spec = pl.BlockSpec((None, N_HEADS, tile, HEAD_DIM),
                               lambda i: (i // n_t, 0, i % n_t, 0))
    else:
        yb_spec = pl.BlockSpec((tile, WIDTH), lambda i: (i, 0))
    return pl.pallas_call(
        functools.partial(_outproj_kernel, head_major=head_major),
        grid=(n // tile,),
        in_specs=[
            pl.BlockSpec((tile, D_MODEL), lambda i: (i, 0)),
            pl.BlockSpec((tile, WIDTH), lambda i: (i, 0)),
            yb_spec,
            _const_spec(w_bf),
        ],
        out_specs=pl.BlockSpec((tile, D_MODEL), lambda i: (i, 0)),
        out_shape=jax.ShapeDtypeStruct((n, D_MODEL), F32),
        compiler_params=_compiler_params(1),
        name="outproj",
    )(x, ya, yb, _array(w_bf))


SCORES_AHEAD = 3
VALUES_BEHIND = 1
ATTN_HEADS_PER_STEP = 2


def _attn_units(seq):
    n_qc = seq // LANE_CHUNK
    qc_per_tile = ATTN_TILE // LANE_CHUNK
    return [(j, qc, half)
            for j in range(seq // ATTN_TILE)
            for qc in range(j * qc_per_tile, n_qc)
            for half in range(2)]


def _attn_kernel(slopes_ref, q_ref, key_ref, vt_ref, gb_ref, sg_ref, lq1_ref, lk1_ref,
                 lq2_ref, lk2_ref, o_ref, *scratch, lam_init):
    heads_per_step = q_ref.shape[0]

    def one_head(hh, carry):
        slope = slopes_ref[pl.program_id(1) * heads_per_step + hh]
        _attn_head(slope, q_ref.at[hh], key_ref.at[hh], vt_ref.at[hh], gb_ref.at[hh], sg_ref,
                   lq1_ref, lk1_ref, lq2_ref, lk2_ref, o_ref.at[hh], *scratch,
                   lam_init=lam_init)
        return carry

    lax.fori_loop(0, heads_per_step, one_head, 0)


def _attn_head(slope, q_ref, key_ref, vt_ref, gb_ref, sg_ref, lq1_ref, lk1_ref,
               lq2_ref, lk2_ref, o_ref, qbd_ref, s_ref, m_ref, acc_ref, *, lam_init):
    seq = q_ref.shape[1]
    t, w = ATTN_TILE, LANE_CHUNK
    assert t == 2 * w
    n_qc = seq // w
    slope2 = slope * LOG2E

    key = lax.broadcasted_iota(jnp.int32, (w, w), 0)
    qry = lax.broadcasted_iota(jnp.int32, (w, w), 1)
    allowed = jnp.right_shift(key, 6) <= jnp.right_shift(qry, 6)
    ahead = jnp.maximum(key - qry, 0).astype(F32)
    dt = jnp.where(allowed, (-2.0 * slope2) * ahead, NEG_INF)

    row = lax.broadcasted_iota(jnp.int32, (HEAD_DIM, w), 0)
    rest = jnp.full((HEAD_DIM, w), slope2, F32)
    slope_rows = jnp.zeros((HEAD_DIM, w), F32)
    for piece in range(POS_SPLIT):
        part = rest.astype(BF16).astype(F32)
        slope_rows = jnp.where((row == piece) | (row == piece + POS_SPLIT), part, slope_rows)
        rest = rest - part
    slope_rows = slope_rows.astype(BF16)
    zero_half = jnp.zeros((HALF_DIM, w), BF16)

    lam = _lam_value(lq1_ref, lk1_ref, lq2_ref, lk2_ref, lam_init)

    def n_keys(j, qc):
        return w if qc * w == j * t else t

    def stacked_queries(qc, half):
        lanes = slice((half * n_qc + qc) * w, (half * n_qc + qc + 1) * w)
        own = slice(half * HALF_DIM, (half + 1) * HALF_DIM)
        other = slice((1 - half) * HALF_DIM, (2 - half) * HALF_DIM)
        qbd_ref[own, lanes] = q_ref[own, qc * w:(qc + 1) * w]
        qbd_ref[other, lanes] = zero_half
        qbd_ref[HEAD_DIM:, lanes] = slope_rows
        return lanes

    def scores(unit):
        j, qc, half = unit
        if j == 0:
            lanes = stacked_queries(qc, half)
        else:
            lanes = slice((half * n_qc + qc) * w, (half * n_qc + qc + 1) * w)
        return _dot(key_ref[j, :n_keys(j, qc), :], qbd_ref[:, lanes])

    def softmax(unit, st):
        j, qc, half = unit
        lanes = slice((half * n_qc + qc) * w, (half * n_qc + qc + 1) * w)
        nk = n_keys(j, qc)
        if qc * w < (j + 1) * t:
            st = st + dt if nk == w else jnp.concatenate([st[:w], st[w:] + dt], axis=0)
        m_cur = jnp.max(st, axis=0, keepdims=True)
        if j == 0:
            m_new, alpha = m_cur, None
        else:
            m_old = m_ref[:, lanes]
            m_new = jnp.maximum(m_old, m_cur)
            alpha = jnp.exp2(m_old - m_new)
        m_ref[:, lanes] = m_new
        return jnp.exp2(st - m_new).astype(BF16), alpha

    def values(unit, p, alpha):
        j, qc, half = unit
        lanes = slice((half * n_qc + qc) * w, (half * n_qc + qc + 1) * w)
        pv = _dot(vt_ref[j, :, :n_keys(j, qc)], p)
        acc_ref[:, lanes] = pv if j == 0 else alpha * acc_ref[:, lanes] + pv

    def normalized(lanes):
        return acc_ref[:HEAD_DIM, lanes] * (1.0 / acc_ref[HEAD_DIM:HEAD_DIM + 1, lanes])

    def finalize(qc):
        rows = slice(qc * w, (qc + 1) * w)
        lanes1 = slice(qc * w, (qc + 1) * w)
        lanes2 = slice((n_qc + qc) * w, (n_qc + qc + 1) * w)
        ot = normalized(lanes1) - lam * normalized(lanes2)
        ot = ot * lax.rsqrt(jnp.mean(ot * ot, axis=0, keepdims=True) + NORM_EPS)
        o = (ot.T * sg_ref[...]) * (1.0 - lam_init)
        o_ref[rows, :] = (o * gb_ref[rows, :].astype(F32)).astype(BF16)

    units = _attn_units(seq)
    n_slots = s_ref.shape[0]

    def emit_scores(u):
        nk = n_keys(*units[u][:2])
        s_ref[u % n_slots, :nk, :] = scores(units[u])

    def load_scores(u):
        return s_ref[u % n_slots, :n_keys(*units[u][:2]), :]

    for u in range(min(SCORES_AHEAD, len(units))):
        emit_scores(u)
    pending_values = collections.deque()

    def run_oldest_values():
        unit_done, p, alpha = pending_values.popleft()
        values(unit_done, p, alpha)
        j_done, qc_done, half_done = unit_done
        if half_done == 1 and qc_done * w < (j_done + 1) * t:
            finalize(qc_done)

    for u, unit in enumerate(units):
        if u + SCORES_AHEAD < len(units):
            emit_scores(u + SCORES_AHEAD)
        pending_values.append((unit, *softmax(unit, load_scores(u))))
        if len(pending_values) > VALUES_BEHIND:
            run_oldest_values()
    while pending_values:
        run_oldest_values()


def _prompt_attention(q, keys, vt, gb, slopes, sg, lq1, lk1, lq2, lk2, *, lam_init):
    batch, _, _, seq = q.shape
    t = ATTN_TILE
    n_tiles = seq // t
    hps = ATTN_HEADS_PER_STEP
    seq_spec = pl.BlockSpec((None, hps, seq, HEAD_DIM), lambda b, h: (b, h, 0, 0))
    vec = _const_spec
    return pl.pallas_call(
        functools.partial(_attn_kernel, lam_init=lam_init),
        grid=(batch, N_HEADS // hps),
        in_specs=[
            pl.BlockSpec(memory_space=pltpu.SMEM),
            pl.BlockSpec((None, hps, HEAD_DIM, seq), lambda b, h: (b, h, 0, 0)),
            pl.BlockSpec((None, hps, n_tiles, t, KEY_WIDTH), lambda b, h: (b, h, 0, 0, 0)),
            pl.BlockSpec((None, hps, n_tiles, VT_ROWS, t), lambda b, h: (b, h, 0, 0, 0)),
            seq_spec, vec(sg), vec(lq1), vec(lk1), vec(lq2), vec(lk2)],
        out_specs=seq_spec,
        out_shape=jax.ShapeDtypeStruct((batch, N_HEADS, seq, HEAD_DIM), BF16),
        scratch_shapes=[
            pltpu.VMEM((KEY_WIDTH, 2 * seq), BF16),
            pltpu.VMEM((SCORES_AHEAD + 1, t, LANE_CHUNK), F32),
            pltpu.VMEM((1, 2 * seq), F32),
            pltpu.VMEM((VT_ROWS, 2 * seq), F32),
        ],
        compiler_params=_compiler_params(2),
        name="prompt_attn",
    )(slopes, q, keys, vt, gb, *[_array(p) for p in (sg, lq1, lk1, lq2, lk2)])


def _sample_attn_kernel(q_ref, kn_ref, vn_ref, ckt_ref, cv_ref, gb_ref, *rest, **static):
    for s in range(q_ref.shape[0]):
        _sample_attn_stream(q_ref.at[s], kn_ref.at[s], vn_ref.at[s], ckt_ref.at[s], cv_ref.at[s],
                            gb_ref.at[s], *rest[:-1], rest[-1].at[s], **static)


def _sample_attn_stream(q_ref, kn_ref, vn_ref, ckt_ref, cv_ref, gb_ref, sg_ref,
                        lq1_ref, lk1_ref, lq2_ref, lk2_ref, o_ref, *, lam_init, past_len):
    nq = q_ref.shape[0]
    per_head = 2 * nq
    n_rows = N_HEADS * per_head
    lam = _lam_value(lq1_ref, lk1_ref, lq2_ref, lk2_ref, lam_init)

    def alibi(n_cols, key_pos0):
        row = lax.broadcasted_iota(jnp.int32, (n_rows, n_cols), 0)
        col = lax.broadcasted_iota(jnp.int32, (n_rows, n_cols), 1)
        dist = jnp.abs(past_len + lax.rem(row, nq) - (key_pos0 + col)).astype(F32)
        slope = jnp.zeros((n_rows, n_cols), F32)
        for h in range(N_HEADS):
            slope = jnp.where(lax.div(row, per_head) == h, ALIBI_SLOPES[h] * LOG2E, slope)
        return slope * dist

    zpad = jnp.zeros((HEAD_DIM - nq, HEAD_DIM), BF16)
    s_past, s_new, v_new = [], [], []
    for h in range(N_HEADS):
        cols = _head_cols(h)
        q = q_ref[:, cols]
        lane = lax.broadcasted_iota(jnp.int32, q.shape, 1)
        zero = jnp.zeros_like(q)
        qbd = jnp.concatenate([jnp.where(lane < HALF_DIM, q, zero),
                               jnp.where(lane >= HALF_DIM, q, zero)], axis=0)
        s_past.append(_dot(qbd, ckt_ref[cols, :].astype(BF16)))
        s_new.append(_dot_nt(qbd, jnp.concatenate([kn_ref[:, cols].astype(BF16), zpad], axis=0)))
        v_new.append(jnp.concatenate([vn_ref[:, cols].astype(BF16), zpad], axis=0))
    s_past = jnp.concatenate(s_past, axis=0) - alibi(past_len, 0)
    s_new = jnp.concatenate(s_new, axis=0) - alibi(HEAD_DIM, past_len)
    real_new = lax.broadcasted_iota(jnp.int32, s_new.shape, 1) < nq
    s_new = jnp.where(real_new, s_new, NEG_INF)
    m = jnp.maximum(jnp.max(s_past, axis=-1, keepdims=True),
                    jnp.max(s_new, axis=-1, keepdims=True))
    p_past = jnp.exp2(s_past - m)
    p_new = jnp.exp2(s_new - m)
    inv_l = 1.0 / (jnp.sum(p_past, axis=-1, keepdims=True)
                   + jnp.sum(p_new, axis=-1, keepdims=True))
    p_past = p_past.astype(BF16)
    p_new = p_new.astype(BF16)
    outs = []
    for h in range(N_HEADS):
        rows = slice(h * per_head, (h + 1) * per_head)
        v_past = cv_ref[pl.ds(h, past_len, stride=N_HEADS), :].astype(BF16)
        acc = (_dot(p_past[rows], v_past) + _dot(p_new[rows], v_new[h])) * inv_l[rows]
        o = acc[:nq] - lam * acc[nq:]
        outs.append(o * lax.rsqrt(jnp.mean(o * o, axis=-1, keepdims=True) + NORM_EPS))
    o = jnp.concatenate(outs, axis=1) * jnp.tile(sg_ref[...], (1, N_HEADS)) * (1.0 - lam_init)
    o_ref[...] = (o * gb_ref[...].astype(F32)).astype(BF16)


def _sample_attention(q, k_new, v_new, cache_kt, cache_v, gb, sg, lq1, lk1, lq2, lk2,
                      *, layer, n_streams, n_new, lam_init):
    past_len = cache_kt.shape[3]
    per_step = SAMPLE_STREAMS_PER_STEP
    assert n_streams % per_step == 0
    new_spec = pl.BlockSpec((per_step, n_new, WIDTH), lambda b: (b, 0, 0))
    vec = _const_spec
    r3 = lambda a: a.reshape(n_streams, n_new, WIDTH)
    out = pl.pallas_call(
        functools.partial(_sample_attn_kernel, lam_init=lam_init, past_len=past_len),
        grid=(n_streams // per_step,),
        in_specs=[
            new_spec, new_spec, new_spec,
            pl.BlockSpec((None, per_step, WIDTH, past_len), lambda b: (layer, b, 0, 0)),
            pl.BlockSpec((None, per_step, past_len * N_HEADS, HEAD_DIM),
                         lambda b: (layer, b, 0, 0)),
            new_spec, vec(sg), vec(lq1), vec(lk1), vec(lq2), vec(lk2)],
        out_specs=new_spec,
        out_shape=jax.ShapeDtypeStruct((n_streams, n_new, WIDTH), BF16),
        compiler_params=_compiler_params(1),
        name="sample_attn",
    )(r3(q), r3(k_new), r3(v_new), cache_kt, cache_v, r3(gb),
      *[_array(p) for p in (sg, lq1, lk1, lq2, lk2)])
    return out.reshape(n_streams * n_new, WIDTH)


def kernel(x_prompt, x_sample, cache_k, cache_v, norm_g, w_in, sgu_norm_g, sgu_w, sgu_b,
           q_norm_g, k_norm_g, lambda_q1, lambda_k1, lambda_q2, lambda_k2, subln_g, w_out):
    depth = w_in.shape[0]
    batch, seq, _ = x_prompt.shape
    n_streams, n_new, _ = x_sample.shape
    past_len = cache_k.shape[2]
    assert seq % ATTN_TILE == 0
    assert SGU_CHUNK % n_new == 0 and past_len % CHUNK == 0 and n_new <= CHUNK
    sample_tile = min(n_streams * n_new, TOKEN_TILE)
    assert sample_tile % SGU_CHUNK == 0 and (n_streams * n_new) % sample_tile == 0

    slopes = jnp.asarray(ALIBI_SLOPES, F32)
    tril = jnp.tril(jnp.ones((SGU_CHUNK, SGU_CHUNK), F32))
    tril_new = jnp.tril(jnp.ones((n_new, n_new), F32))
    streams_per_chunk = SGU_CHUNK // n_new
    eye = jnp.eye(streams_per_chunk, dtype=F32)
    cache_kt = jnp.transpose(cache_k, (0, 1, 3, 4, 5, 2)).reshape(depth, n_streams, WIDTH, past_len)
    cache_vr = cache_v.reshape(depth, n_streams, past_len * N_HEADS, HEAD_DIM)

    rows = lambda a: a.reshape(depth, 1, -1).astype(F32)
    w_in_bf = w_in.astype(BF16)
    w_out_bf = w_out.astype(BF16)
    ng = rows(norm_g)
    sgug = rows(sgu_norm_g)
    gq = rows(jnp.tile(q_norm_g, (1, WIDTH // HALF_DIM)))
    gk = rows(jnp.tile(k_norm_g, (1, WIDTH // HALF_DIM)))
    sg = rows(subln_g)
    lam_vecs = tuple(rows(a) for a in (lambda_q1, lambda_k1, lambda_q2, lambda_k2))
    sguw_p = (sgu_w * tril).astype(BF16)
    sgub_p = jnp.broadcast_to(sgu_b[..., None], (depth, N_HEADS, SGU_CHUNK, HEAD_DIM))
    w_new = sgu_w[:, :, :n_new, :n_new] * tril_new
    sguw_s = jnp.einsum("ab,lhts->lhatbs", eye, w_new).reshape(
        depth, N_HEADS, SGU_CHUNK, SGU_CHUNK).astype(BF16)
    sgub_s = jnp.broadcast_to(
        jnp.tile(sgu_b[:, :, :n_new], (1, 1, streams_per_chunk))[..., None],
        (depth, N_HEADS, SGU_CHUNK, HEAD_DIM))

    xp = x_prompt.reshape(batch * seq, D_MODEL)
    xs = x_sample.reshape(n_streams * n_new, D_MODEL)
    k_stack = v_stack = prompt_out = None
    ks_rows, vs_rows, sgu_rows = [], [], []
    for i in range(depth):
        lam_init = _lam_init(i)
        of_layer = lambda a: _LayerParam(a, i)
        params_p = tuple(map(of_layer, (ng, w_in_bf, sgug, sguw_p, sgub_p, gq, gk)))
        params_s = tuple(map(of_layer, (ng, w_in_bf, sgug, sguw_s, sgub_s, gq, gk)))
        attn_params = tuple(map(of_layer, (sg, *lam_vecs)))

        xp, ya, q, keys, k_stack, v_stack, vt, gb = _inproj_prompt(
            xp, params_p, prompt_out, k_stack, v_stack, batch=batch, seq=seq)
        yb = _prompt_attention(q, keys, vt, gb, slopes, *attn_params, lam_init=lam_init)
        prompt_out = (ya, yb, of_layer(w_out_bf))

        ya, q, k, v, gb, va = _inproj_sample(xs, params_s, tile=sample_tile)
        yb = _sample_attention(q, k, v, cache_kt, cache_vr, gb, *attn_params, layer=i,
                               n_streams=n_streams, n_new=n_new, lam_init=lam_init)
        xs = _outproj(xs, ya, yb, of_layer(w_out_bf), tile=sample_tile)
        ks_rows.append(k)
        vs_rows.append(v)
        sgu_rows.append(va)

    xp = _outproj(xp, *prompt_out, tile=OUTPROJ_TILE)
    new_k_prompt = jnp.transpose(
        k_stack.reshape(depth, batch, N_HEADS, 2, HALF_DIM, seq), (0, 1, 5, 2, 3, 4))
    return (
        xp.reshape(batch, seq, D_MODEL),
        xs.reshape(n_streams, n_new, D_MODEL),
        new_k_prompt,
        v_stack.reshape(depth, batch, seq, N_HEADS, HEAD_DIM),
        jnp.stack(ks_rows).reshape(depth, n_streams, n_new, N_HEADS, 2, HALF_DIM),
        jnp.stack(vs_rows).reshape(depth, n_streams, n_new, N_HEADS, HEAD_DIM),
        jnp.stack(sgu_rows).reshape(depth, n_streams, n_new, WIDTH),
    )
```

```python
import collections
import functools
import math
from typing import NamedTuple

import jax
import jax.numpy as jnp
from jax import lax
from jax.experimental import pallas as pl
from jax.experimental.pallas import tpu as pltpu

F32 = jnp.float32
BF16 = jnp.bfloat16

D_MODEL = 1024
N_HEADS = 4
HEAD_DIM = 128
HALF_DIM = 64
WIDTH = N_HEADS * HEAD_DIM
CHUNK = 64
SGU_CHUNK = 128
NORM_EPS = 1e-6
NEG_INF = -1e30
QK_SCALE = HALF_DIM ** -0.5
LOG2E = math.log2(math.e)
ALIBI_SLOPES = tuple(2.0 ** (-8.0 * (h + 1) / N_HEADS) for h in range(N_HEADS))

COL_U, COL_VA, COL_GA, COL_Q, COL_K, COL_V, COL_GB = (i * WIDTH for i in range(7))

VMEM_LIMIT_BYTES = 56 * 1024 * 1024

ATTN_TILE = 512
TOKEN_TILE = ATTN_TILE
SAMPLE_STREAMS_PER_STEP = 2
OUTPROJ_TILE = 2048
LANE_CHUNK = 256
KEY_WIDTH = 2 * HEAD_DIM
POS_SPLIT = 3
VT_ROWS = HEAD_DIM + 16


def _lam_init(layer_idx):
    return 0.8 - 0.6 * math.exp(-0.3 * layer_idx)


def _compiler_params(n_axes):
    return pltpu.CompilerParams(
        dimension_semantics=("arbitrary",) * n_axes,
        vmem_limit_bytes=VMEM_LIMIT_BYTES,
    )


class _LayerParam(NamedTuple):
    array: jax.Array
    layer: int


def _array(p):
    return p.array if isinstance(p, _LayerParam) else p


def _const_spec(p, single_buffer=False):
    if isinstance(p, _LayerParam):
        block = (None, *p.array.shape[1:])
        index = (p.layer,) + (0,) * (p.array.ndim - 1)
    else:
        block, index = p.shape, (0,) * p.ndim
    kwargs = dict(pipeline_mode=pl.Buffered(1)) if single_buffer else {}
    return pl.BlockSpec(block, lambda *grid_idx: index, **kwargs)


def _dot(a, b):
    return jnp.dot(a, b, preferred_element_type=F32)


def _dot_nt(a, b):
    return lax.dot_general(a, b, (((1,), (1,)), ((), ())), preferred_element_type=F32)


def _lam_value(lq1_ref, lk1_ref, lq2_ref, lk2_ref, lam_init):
    d1 = jnp.sum(lq1_ref[...] * lk1_ref[...], axis=-1, keepdims=True)
    d2 = jnp.sum(lq2_ref[...] * lk2_ref[...], axis=-1, keepdims=True)
    return jnp.exp(d1) - jnp.exp(d2) + lam_init


def _head_cols(h):
    return slice(h * HEAD_DIM, (h + 1) * HEAD_DIM)


def _inproj_kernel(*refs, prompt, n_prev, fused):
    x_ref, ng_ref, w_ref, sgug_ref, sguwt_ref, sgub_ref, gq_ref, gk_ref = refs[:8]
    n_in = 8
    if fused:
        ya_prev_ref, yb_prev_ref, wo_ref = refs[n_in:n_in + 3]
        n_in += 3
    prev = refs[n_in:n_in + n_prev]
    outs = refs[n_in + n_prev:]
    if fused:
        xo_ref, outs = outs[0], outs[1:]
    tile = x_ref.shape[0]

    first_half = lax.broadcasted_iota(jnp.int32, (tile, HEAD_DIM), 1) < HALF_DIM

    def group_rms_scale(z, group):
        scales = []
        for g in range(WIDTH // HEAD_DIM):
            sq = z[:, _head_cols(g)] ** 2
            if group == HEAD_DIM:
                total = jnp.sum(sq, axis=-1, keepdims=True)
            else:
                total = jnp.where(
                    first_half,
                    jnp.sum(jnp.where(first_half, sq, 0.0), axis=-1, keepdims=True),
                    jnp.sum(jnp.where(first_half, 0.0, sq), axis=-1, keepdims=True))
            scales.append(jnp.broadcast_to(lax.rsqrt(total * (1.0 / group) + NORM_EPS),
                                           (tile, HEAD_DIM)))
        return jnp.concatenate(scales, axis=1)

    if prompt:
        ya_ref, q_ref, key_ref, kt_ref, v_ref, vt_ref, gb_ref = outs
        last = kt_ref.shape[0] - 1
        if n_prev:
            kt_prev_ref, v_prev_ref = prev
            kt_ref[:last] = kt_prev_ref[...]
            v_ref[:last] = v_prev_ref[...]
        for h in range(N_HEADS):
            vt_ref[h, HEAD_DIM:, :] = jnp.ones((VT_ROWS - HEAD_DIM, tile), BF16)
    else:
        ya_ref, q_ref, k_ref, v_ref, gb_ref, va_ref = outs

    x = x_ref[...]
    if fused:
        yb_prev = jnp.concatenate([yb_prev_ref[h] for h in range(N_HEADS)], axis=1)
        x = x + (_dot(ya_prev_ref[...], wo_ref[:WIDTH, :]) + _dot(yb_prev, wo_ref[WIDTH:, :]))
        xo_ref[...] = x
    hb = (x * ng_ref[...]).astype(BF16)
    row_scale = jnp.broadcast_to(
        lax.rsqrt(jnp.mean(x * x, axis=-1, keepdims=True) + NORM_EPS), (tile, WIDTH))

    def proj(col):
        return _dot(hb, w_ref[:, col:col + WIDTH]) * row_scale

    va = proj(COL_VA)
    if not prompt:
        va_ref[...] = va
    vn = (va * group_rms_scale(va, HEAD_DIM)) * sgug_ref[...]
    gate = proj(COL_U) * jax.nn.silu(proj(COL_GA))
    chunk_rows = [slice(c * SGU_CHUNK, (c + 1) * SGU_CHUNK) for c in range(tile // SGU_CHUNK)]
    for g in range(N_HEADS):
        cols = _head_cols(g)
        vn_t = jnp.concatenate([vn[r, cols].T for r in chunk_rows], axis=0).astype(BF16)
        mixed_t = _dot(vn_t, sguwt_ref[g])
        for r in chunk_rows:
            mixed = mixed_t[r, :].T + sgub_ref[g]
            ya_ref[r, cols] = (gate[r, cols] * mixed).astype(BF16)

    zq = proj(COL_Q)
    qn = (zq * group_rms_scale(zq, HALF_DIM)) * gq_ref[...] * (QK_SCALE * LOG2E)
    zk = proj(COL_K)
    kn = (zk * group_rms_scale(zk, HALF_DIM)) * gk_ref[...]
    zv = proj(COL_V)
    gb = jax.nn.silu(proj(COL_GB)).astype(BF16)

    if prompt:
        pos = (lax.broadcasted_iota(jnp.int32, (tile, HEAD_DIM), 0)
               + pl.program_id(1) * tile)
        lane = lax.broadcasted_iota(jnp.int32, (tile, HEAD_DIM), 1)
        lo = jnp.bitwise_and(pos, CHUNK - 1)
        pos_cols = jnp.where(
            lane < POS_SPLIT, (pos - lo).astype(F32),
            jnp.where(lane < 2 * POS_SPLIT, lo.astype(F32), 0.0)).astype(BF16)
        kt_ref[last] = kn.T
        for h in range(N_HEADS):
            cols = _head_cols(h)
            q_ref[h] = qn[:, cols].T.astype(BF16)
            gb_ref[h] = gb[:, cols]
            key_ref[h, :, :HEAD_DIM] = kn[:, cols].astype(BF16)
            key_ref[h, :, HEAD_DIM:] = pos_cols
            vt_ref[h, :HEAD_DIM, :] = zv[:, cols].T.astype(BF16)
            v_ref[last, pl.ds(h, tile, stride=N_HEADS), :] = zv[:, cols]
    else:
        q_ref[...] = qn.astype(BF16)
        k_ref[...] = kn
        v_ref[...] = zv
        gb_ref[...] = gb


def _inproj_prompt(x, params, prev_out, k_stack, v_stack, *, batch, seq):
    tile = TOKEN_TILE
    n_t = seq // tile
    n = batch * seq
    n_layers = 1 if k_stack is None else k_stack.shape[0] + 1
    full = lambda p: _const_spec(p, single_buffer=True)
    head_major = lambda width: pl.BlockSpec((None, N_HEADS, tile, width),
                                            lambda b, i: (b, 0, i, 0))
    kt_spec = lambda layers: pl.BlockSpec((layers, None, WIDTH, tile),
                                          lambda b, i: (0, b, 0, i))
    v_spec = lambda layers: pl.BlockSpec((layers, tile * N_HEADS, HEAD_DIM),
                                         lambda b, i: (0, b * n_t + i, 0))
    x_spec = pl.BlockSpec((tile, D_MODEL), lambda b, i: (b * n_t + i, 0))
    ya_spec = pl.BlockSpec((tile, WIDTH), lambda b, i: (b * n_t + i, 0))
    in_specs = [x_spec] + [full(p) for p in params]
    operands = [x] + [_array(p) for p in params]
    fused = prev_out is not None
    if fused:
        ya_prev, yb_prev, w_out_prev = prev_out
        in_specs += [ya_spec, head_major(HEAD_DIM), full(w_out_prev)]
        operands += [ya_prev, yb_prev, _array(w_out_prev)]
    if n_layers > 1:
        in_specs += [kt_spec(n_layers - 1), v_spec(n_layers - 1)]
        operands += [k_stack, v_stack]
    out_shape = [
        jax.ShapeDtypeStruct((n, WIDTH), BF16),
        jax.ShapeDtypeStruct((batch, N_HEADS, HEAD_DIM, seq), BF16),
        jax.ShapeDtypeStruct((batch, N_HEADS, n_t, tile, KEY_WIDTH), BF16),
        jax.ShapeDtypeStruct((n_layers, batch, WIDTH, seq), F32),
        jax.ShapeDtypeStruct((n_layers, n * N_HEADS, HEAD_DIM), F32),
        jax.ShapeDtypeStruct((batch, N_HEADS, n_t, VT_ROWS, tile), BF16),
        jax.ShapeDtypeStruct((batch, N_HEADS, seq, HEAD_DIM), BF16),
    ]
    out_specs = [
        ya_spec,
        pl.BlockSpec((None, N_HEADS, HEAD_DIM, tile), lambda b, i: (b, 0, 0, i)),
        pl.BlockSpec((None, N_HEADS, None, tile, KEY_WIDTH), lambda b, i: (b, 0, i, 0, 0)),
        kt_spec(n_layers),
        v_spec(n_layers),
        pl.BlockSpec((None, N_HEADS, None, VT_ROWS, tile), lambda b, i: (b, 0, i, 0, 0)),
        head_major(HEAD_DIM),
    ]
    if fused:
        out_shape.insert(0, jax.ShapeDtypeStruct((n, D_MODEL), F32))
        out_specs.insert(0, x_spec)
    outs = pl.pallas_call(
        functools.partial(_inproj_kernel, prompt=True, n_prev=2 * (n_layers > 1), fused=fused),
        grid=(batch, n_t),
        in_specs=in_specs,
        out_specs=out_specs,
        out_shape=out_shape,
        compiler_params=_compiler_params(2),
        name="inproj_prompt",
    )(*operands)
    return outs if fused else (x, *outs)


def _inproj_sample(x, params, *, tile):
    n = x.shape[0]
    row_spec = lambda width: pl.BlockSpec((tile, width), lambda i: (i, 0))
    dtypes = (BF16, BF16, F32, F32, BF16, F32)
    return pl.pallas_call(
        functools.partial(_inproj_kernel, prompt=False, n_prev=0, fused=False),
        grid=(n // tile,),
        in_specs=[row_spec(D_MODEL)] + [_const_spec(p) for p in params],
        out_specs=[row_spec(WIDTH)] * len(dtypes),
        out_shape=[jax.ShapeDtypeStruct((n, WIDTH), d) for d in dtypes],
        compiler_params=_compiler_params(1),
        name="inproj_sample",
    )(x, *[_array(p) for p in params])


def _outproj_kernel(x_ref, ya_ref, yb_ref, w_ref, o_ref, *, head_major):
    if head_major:
        yb = jnp.concatenate([yb_ref[h] for h in range(N_HEADS)], axis=1)
    else:
        yb = yb_ref[...]
    y = _dot(ya_ref[...], w_ref[:WIDTH, :]) + _dot(yb, w_ref[WIDTH:, :])
    o_ref[...] = x_ref[...] + y


def _outproj(x, ya, yb, w_bf, *, tile):
    n = x.shape[0]
    head_major = yb.ndim == 4
    if head_major:
        n_t = yb.shape[2] // tile
        yb_spec = pl.BlockSpec((None, N_HEADS, tile, HEAD_DIM),
                               lambda i: (i // n_t, 0, i % n_t, 0))
    else:
        yb_spec = pl.BlockSpec((tile, WIDTH), lambda i: (i, 0))
    return pl.pallas_call(
        functools.partial(_outproj_kernel, head_major=head_major),
        grid=(n // tile,),
        in_specs=[
            pl.BlockSpec((tile, D_MODEL), lambda i: (i, 0)),
            pl.BlockSpec((tile, WIDTH), lambda i: (i, 0)),
            yb_spec,
            _const_spec(w_bf),
        ],
        out_specs=pl.BlockSpec((tile, D_MODEL), lambda i: (i, 0)),
        out_shape=jax.ShapeDtypeStruct((n, D_MODEL), F32),
        compiler_params=_compiler_params(1),
        name="outproj",
    )(x, ya, yb, _array(w_bf))


SCORES_AHEAD = 3
VALUES_BEHIND = 1
ATTN_HEADS_PER_STEP = 2


def _attn_units(seq):
    n_qc = seq // LANE_CHUNK
    qc_per_tile = ATTN_TILE // LANE_CHUNK
    return [(j, qc, half)
            for j in range(seq // ATTN_TILE)
            for qc in range(j * qc_per_tile, n_qc)
            for half in range(2)]


def _attn_kernel(slopes_ref, q_ref, key_ref, vt_ref, gb_ref, sg_ref, lq1_ref, lk1_ref,
                 lq2_ref, lk2_ref, o_ref, *scratch, lam_init):
    heads_per_step = q_ref.shape[0]

    def one_head(hh, carry):
        slope = slopes_ref[pl.program_id(1) * heads_per_step + hh]
        _attn_head(slope, q_ref.at[hh], key_ref.at[hh], vt_ref.at[hh], gb_ref.at[hh], sg_ref,
                   lq1_ref, lk1_ref, lq2_ref, lk2_ref, o_ref.at[hh], *scratch,
                   lam_init=lam_init)
        return carry

    lax.fori_loop(0, heads_per_step, one_head, 0)


def _attn_head(slope, q_ref, key_ref, vt_ref, gb_ref, sg_ref, lq1_ref, lk1_ref,
               lq2_ref, lk2_ref, o_ref, qbd_ref, s_ref, m_ref, acc_ref, *, lam_init):
    seq = q_ref.shape[1]
    t, w = ATTN_TILE, LANE_CHUNK
    assert t == 2 * w
    n_qc = seq // w
    slope2 = slope * LOG2E

    key = lax.broadcasted_iota(jnp.int32, (w, w), 0)
    qry = lax.broadcasted_iota(jnp.int32, (w, w), 1)
    allowed = jnp.right_shift(key, 6) <= jnp.right_shift(qry, 6)
    ahead = jnp.maximum(key - qry, 0).astype(F32)
    dt = jnp.where(allowed, (-2.0 * slope2) * ahead, NEG_INF)

    row = lax.broadcasted_iota(jnp.int32, (HEAD_DIM, w), 0)
    rest = jnp.full((HEAD_DIM, w), slope2, F32)
    slope_rows = jnp.zeros((HEAD_DIM, w), F32)
    for piece in range(POS_SPLIT):
        part = rest.astype(BF16).astype(F32)
        slope_rows = jnp.where((row == piece) | (row == piece + POS_SPLIT), part, slope_rows)
        rest = rest - part
    slope_rows = slope_rows.astype(BF16)
    zero_half = jnp.zeros((HALF_DIM, w), BF16)

    lam = _lam_value(lq1_ref, lk1_ref, lq2_ref, lk2_ref, lam_init)

    def n_keys(j, qc):
        return w if qc * w == j * t else t

    def stacked_queries(qc, half):
        lanes = slice((half * n_qc + qc) * w, (half * n_qc + qc + 1) * w)
        own = slice(half * HALF_DIM, (half + 1) * HALF_DIM)
        other = slice((1 - half) * HALF_DIM, (2 - half) * HALF_DIM)
        qbd_ref[own, lanes] = q_ref[own, qc * w:(qc + 1) * w]
        qbd_ref[other, lanes] = zero_half
        qbd_ref[HEAD_DIM:, lanes] = slope_rows
        return lanes

    def scores(unit):
        j, qc, half = unit
        if j == 0:
            lanes = stacked_queries(qc, half)
        else:
            lanes = slice((half * n_qc + qc) * w, (half * n_qc + qc + 1) * w)
        return _dot(key_ref[j, :n_keys(j, qc), :], qbd_ref[:, lanes])

    def softmax(unit, st):
        j, qc, half = unit
        lanes = slice((half * n_qc + qc) * w, (half * n_qc + qc + 1) * w)
        nk = n_keys(j, qc)
        if qc * w < (j + 1) * t:
            st = st + dt if nk == w else jnp.concatenate([st[:w], st[w:] + dt], axis=0)
        m_cur = jnp.max(st, axis=0, keepdims=True)
        if j == 0:
            m_new, alpha = m_cur, None
        else:
            m_old = m_ref[:, lanes]
            m_new = jnp.maximum(m_old, m_cur)
            alpha = jnp.exp2(m_old - m_new)
        m_ref[:, lanes] = m_new
        return jnp.exp2(st - m_new).astype(BF16), alpha

    def values(unit, p, alpha):
        j, qc, half = unit
        lanes = slice((half * n_qc + qc) * w, (half * n_qc + qc + 1) * w)
        pv = _dot(vt_ref[j, :, :n_keys(j, qc)], p)
        acc_ref[:, lanes] = pv if j == 0 else alpha * acc_ref[:, lanes] + pv

    def normalized(lanes):
        return acc_ref[:HEAD_DIM, lanes] * (1.0 / acc_ref[HEAD_DIM:HEAD_DIM + 1, lanes])

    def finalize(qc):
        rows = slice(qc * w, (qc + 1) * w)
        lanes1 = slice(qc * w, (qc + 1) * w)
        lanes2 = slice((n_qc + qc) * w, (n_qc + qc + 1) * w)
        ot = normalized(lanes1) - lam * normalized(lanes2)
        ot = ot * lax.rsqrt(jnp.mean(ot * ot, axis=0, keepdims=True) + NORM_EPS)
        o = (ot.T * sg_ref[...]) * (1.0 - lam_init)
        o_ref[rows, :] = (o * gb_ref[rows, :].astype(F32)).astype(BF16)

    units = _attn_units(seq)
    n_slots = s_ref.shape[0]

    def emit_scores(u):
        nk = n_keys(*units[u][:2])
        s_ref[u % n_slots, :nk, :] = scores(units[u])

    def load_scores(u):
        return s_ref[u % n_slots, :n_keys(*units[u][:2]), :]

    for u in range(min(SCORES_AHEAD, len(units))):
        emit_scores(u)
    pending_values = collections.deque()

    def run_oldest_values():
        unit_done, p, alpha = pending_values.popleft()
        values(unit_done, p, alpha)
        j_done, qc_done, half_done = unit_done
        if half_done == 1 and qc_done * w < (j_done + 1) * t:
            finalize(qc_done)

    for u, unit in enumerate(units):
        if u + SCORES_AHEAD < len(units):
            emit_scores(u + SCORES_AHEAD)
        pending_values.append((unit, *softmax(unit, load_scores(u))))
        if len(pending_values) > VALUES_BEHIND:
            run_oldest_values()
    while pending_values:
        run_oldest_values()


def _prompt_attention(q, keys, vt, gb, slopes, sg, lq1, lk1, lq2, lk2, *, lam_init):
    batch, _, _, seq = q.shape
    t = ATTN_TILE
    n_tiles = seq // t
    hps = ATTN_HEADS_PER_STEP
    seq_spec = pl.BlockSpec((None, hps, seq, HEAD_DIM), lambda b, h: (b, h, 0, 0))
    vec = _const_spec
    return pl.pallas_call(
        functools.partial(_attn_kernel, lam_init=lam_init),
        grid=(batch, N_HEADS // hps),
        in_specs=[
            pl.BlockSpec(memory_space=pltpu.SMEM),
            pl.BlockSpec((None, hps, HEAD_DIM, seq), lambda b, h: (b, h, 0, 0)),
            pl.BlockSpec((None, hps, n_tiles, t, KEY_WIDTH), lambda b, h: (b, h, 0, 0, 0)),
            pl.BlockSpec((None, hps, n_tiles, VT_ROWS, t), lambda b, h: (b, h, 0, 0, 0)),
            seq_spec, vec(sg), vec(lq1), vec(lk1), vec(lq2), vec(lk2)],
        out_specs=seq_spec,
        out_shape=jax.ShapeDtypeStruct((batch, N_HEADS, seq, HEAD_DIM), BF16),
        scratch_shapes=[
            pltpu.VMEM((KEY_WIDTH, 2 * seq), BF16),
            pltpu.VMEM((SCORES_AHEAD + 1, t, LANE_CHUNK), F32),
            pltpu.VMEM((1, 2 * seq), F32),
            pltpu.VMEM((VT_ROWS, 2 * seq), F32),
        ],
        compiler_params=_compiler_params(2),
        name="prompt_attn",
    )(slopes, q, keys, vt, gb, *[_array(p) for p in (sg, lq1, lk1, lq2, lk2)])


def _sample_attn_kernel(q_ref, kn_ref, vn_ref, ckt_ref, cv_ref, gb_ref, *rest, **static):
    for s in range(q_ref.shape[0]):
        _sample_attn_stream(q_ref.at[s], kn_ref.at[s], vn_ref.at[s], ckt_ref.at[s], cv_ref.at[s],
                            gb_ref.at[s], *rest[:-1], rest[-1].at[s], **static)


def _sample_attn_stream(q_ref, kn_ref, vn_ref, ckt_ref, cv_ref, gb_ref, sg_ref,
                        lq1_ref, lk1_ref, lq2_ref, lk2_ref, o_ref, *, lam_init, past_len):
    nq = q_ref.shape[0]
    per_head = 2 * nq
    n_rows = N_HEADS * per_head
    lam = _lam_value(lq1_ref, lk1_ref, lq2_ref, lk2_ref, lam_init)

    def alibi(n_cols, key_pos0):
        row = lax.broadcasted_iota(jnp.int32, (n_rows, n_cols), 0)
        col = lax.broadcasted_iota(jnp.int32, (n_rows, n_cols), 1)
        dist = jnp.abs(past_len + lax.rem(row, nq) - (key_pos0 + col)).astype(F32)
        slope = jnp.zeros((n_rows, n_cols), F32)
        for h in range(N_HEADS):
            slope = jnp.where(lax.div(row, per_head) == h, ALIBI_SLOPES[h] * LOG2E, slope)
        return slope * dist

    zpad = jnp.zeros((HEAD_DIM - nq, HEAD_DIM), BF16)
    s_past, s_new, v_new = [], [], []
    for h in range(N_HEADS):
        cols = _head_cols(h)
        q = q_ref[:, cols]
        lane = lax.broadcasted_iota(jnp.int32, q.shape, 1)
        zero = jnp.zeros_like(q)
        qbd = jnp.concatenate([jnp.where(lane < HALF_DIM, q, zero),
                               jnp.where(lane >= HALF_DIM, q, zero)], axis=0)
        s_past.append(_dot(qbd, ckt_ref[cols, :].astype(BF16)))
        s_new.append(_dot_nt(qbd, jnp.concatenate([kn_ref[:, cols].astype(BF16), zpad], axis=0)))
        v_new.append(jnp.concatenate([vn_ref[:, cols].astype(BF16), zpad], axis=0))
    s_past = jnp.concatenate(s_past, axis=0) - alibi(past_len, 0)
    s_new = jnp.concatenate(s_new, axis=0) - alibi(HEAD_DIM, past_len)
    real_new = lax.broadcasted_iota(jnp.int32, s_new.shape, 1) < nq
    s_new = jnp.where(real_new, s_new, NEG_INF)
    m = jnp.maximum(jnp.max(s_past, axis=-1, keepdims=True),
                    jnp.max(s_new, axis=-1, keepdims=True))
    p_past = jnp.exp2(s_past - m)
    p_new = jnp.exp2(s_new - m)
    inv_l = 1.0 / (jnp.sum(p_past, axis=-1, keepdims=True)
                   + jnp.sum(p_new, axis=-1, keepdims=True))
    p_past = p_past.astype(BF16)
    p_new = p_new.astype(BF16)
    outs = []
    for h in range(N_HEADS):
        rows = slice(h * per_head, (h + 1) * per_head)
        v_past = cv_ref[pl.ds(h, past_len, stride=N_HEADS), :].astype(BF16)
        acc = (_dot(p_past[rows], v_past) + _dot(p_new[rows], v_new[h])) * inv_l[rows]
        o = acc[:nq] - lam * acc[nq:]
        outs.append(o * lax.rsqrt(jnp.mean(o * o, axis=-1, keepdims=True) + NORM_EPS))
    o = jnp.concatenate(outs, axis=1) * jnp.tile(sg_ref[...], (1, N_HEADS)) * (1.0 - lam_init)
    o_ref[...] = (o * gb_ref[...].astype(F32)).astype(BF16)


def _sample_attention(q, k_new, v_new, cache_kt, cache_v, gb, sg, lq1, lk1, lq2, lk2,
                      *, layer, n_streams, n_new, lam_init):
    past_len = cache_kt.shape[3]
    per_step = SAMPLE_STREAMS_PER_STEP
    assert n_streams % per_step == 0
    new_spec = pl.BlockSpec((per_step, n_new, WIDTH), lambda b: (b, 0, 0))
    vec = _const_spec
    r3 = lambda a: a.reshape(n_streams, n_new, WIDTH)
    out = pl.pallas_call(
        functools.partial(_sample_attn_kernel, lam_init=lam_init, past_len=past_len),
        grid=(n_streams // per_step,),
        in_specs=[
            new_spec, new_spec, new_spec,
            pl.BlockSpec((None, per_step, WIDTH, past_len), lambda b: (layer, b, 0, 0)),
            pl.BlockSpec((None, per_step, past_len * N_HEADS, HEAD_DIM),
                         lambda b: (layer, b, 0, 0)),
            new_spec, vec(sg), vec(lq1), vec(lk1), vec(lq2), vec(lk2)],
        out_specs=new_spec,
        out_shape=jax.ShapeDtypeStruct((n_streams, n_new, WIDTH), BF16),
        compiler_params=_compiler_params(1),
        name="sample_attn",
    )(r3(q), r3(k_new), r3(v_new), cache_kt, cache_v, r3(gb),
      *[_array(p) for p in (sg, lq1, lk1, lq2, lk2)])
    return out.reshape(n_streams * n_new, WIDTH)


def kernel(x_prompt, x_sample, cache_k, cache_v, norm_g, w_in, sgu_norm_g, sgu_w, sgu_b,
           q_norm_g, k_norm_g, lambda_q1, lambda_k1, lambda_q2, lambda_k2, subln_g, w_out):
    depth = w_in.shape[0]
    batch, seq, _ = x_prompt.shape
    n_streams, n_new, _ = x_sample.shape
    past_len = cache_k.shape[2]
    assert seq % ATTN_TILE == 0
    assert SGU_CHUNK % n_new == 0 and past_len % CHUNK == 0 and n_new <= CHUNK
    sample_tile = min(n_streams * n_new, TOKEN_TILE)
    assert sample_tile % SGU_CHUNK == 0 and (n_streams * n_new) % sample_tile == 0

    slopes = jnp.asarray(ALIBI_SLOPES, F32)
    tril = jnp.tril(jnp.ones((SGU_CHUNK, SGU_CHUNK), F32))
    tril_new = jnp.tril(jnp.ones((n_new, n_new), F32))
    streams_per_chunk = SGU_CHUNK // n_new
    eye = jnp.eye(streams_per_chunk, dtype=F32)
    cache_kt = jnp.transpose(cache_k, (0, 1, 3, 4, 5, 2)).reshape(depth, n_streams, WIDTH, past_len)
    cache_vr = cache_v.reshape(depth, n_streams, past_len * N_HEADS, HEAD_DIM)

    rows = lambda a: a.reshape(depth, 1, -1).astype(F32)
    w_in_bf = w_in.astype(BF16)
    w_out_bf = w_out.astype(BF16)
    ng = rows(norm_g)
    sgug = rows(sgu_norm_g)
    gq = rows(jnp.tile(q_norm_g, (1, WIDTH // HALF_DIM)))
    gk = rows(jnp.tile(k_norm_g, (1, WIDTH // HALF_DIM)))
    sg = rows(subln_g)
    lam_vecs = tuple(rows(a) for a in (lambda_q1, lambda_k1, lambda_q2, lambda_k2))
    sguw_p = jnp.swapaxes(sgu_w * tril, -1, -2).astype(BF16)
    sgub_p = jnp.broadcast_to(sgu_b[..., None], (depth, N_HEADS, SGU_CHUNK, HEAD_DIM))
    w_new = sgu_w[:, :, :n_new, :n_new] * tril_new
    sguw_s = jnp.swapaxes(jnp.einsum("ab,lhts->lhatbs", eye, w_new).reshape(
        depth, N_HEADS, SGU_CHUNK, SGU_CHUNK), -1, -2).astype(BF16)
    sgub_s = jnp.broadcast_to(
        jnp.tile(sgu_b[:, :, :n_new], (1, 1, streams_per_chunk))[..., None],
        (depth, N_HEADS, SGU_CHUNK, HEAD_DIM))

    xp = x_prompt.reshape(batch * seq, D_MODEL)
    xs = x_sample.reshape(n_streams * n_new, D_MODEL)
    k_stack = v_stack = prompt_out = None
    ks_rows, vs_rows, sgu_rows = [], [], []
    for i in range(depth):
        lam_init = _lam_init(i)
        of_layer = lambda a: _LayerParam(a, i)
        params_p = tuple(map(of_layer, (ng, w_in_bf, sgug, sguw_p, sgub_p, gq, gk)))
        params_s = tuple(map(of_layer, (ng, w_in_bf, sgug, sguw_s, sgub_s, gq, gk)))
        attn_params = tuple(map(of_layer, (sg, *lam_vecs)))

        xp, ya, q, keys, k_stack, v_stack, vt, gb = _inproj_prompt(
            xp, params_p, prompt_out, k_stack, v_stack, batch=batch, seq=seq)
        yb = _prompt_attention(q, keys, vt, gb, slopes, *attn_params, lam_init=lam_init)
        prompt_out = (ya, yb, of_layer(w_out_bf))

        ya, q, k, v, gb, va = _inproj_sample(xs, params_s, tile=sample_tile)
        yb = _sample_attention(q, k, v, cache_kt, cache_vr, gb, *attn_params, layer=i,
                               n_streams=n_streams, n_new=n_new, lam_init=lam_init)
        xs = _outproj(xs, ya, yb, of_layer(w_out_bf), tile=sample_tile)
        ks_rows.append(k)
        vs_rows.append(v)
        sgu_rows.append(va)

    xp = _outproj(xp, *prompt_out, tile=OUTPROJ_TILE)
    new_k_prompt = jnp.transpose(
        k_stack.reshape(depth, batch, N_HEADS, 2, HALF_DIM, seq), (0, 1, 5, 2, 3, 4))
    return (
        xp.reshape(batch, seq, D_MODEL),
        xs.reshape(n_streams, n_new, D_MODEL),
        new_k_prompt,
        v_stack.reshape(depth, batch, seq, N_HEADS, HEAD_DIM),
        jnp.stack(ks_rows).reshape(depth, n_streams, n_new, N_HEADS, 2, HALF_DIM),
        jnp.stack(vs_rows).reshape(depth, n_streams, n_new, N_HEADS, HEAD_DIM),
        jnp.stack(sgu_rows).reshape(depth, n_streams, n_new, WIDTH),
    )
```

```python
import collections
import functools
import math
from typing import NamedTuple

import jax
import jax.numpy as jnp
from jax import lax
from jax.experimental import pallas as pl
from jax.experimental.pallas import tpu as pltpu

F32 = jnp.float32
BF16 = jnp.bfloat16

D_MODEL = 1024
N_HEADS = 4
HEAD_DIM = 128
HALF_DIM = 64
WIDTH = N_HEADS * HEAD_DIM
CHUNK = 64
SGU_CHUNK = 128
NORM_EPS = 1e-6
NEG_INF = -1e30
QK_SCALE = HALF_DIM ** -0.5
LOG2E = math.log2(math.e)
ALIBI_SLOPES = tuple(2.0 ** (-8.0 * (h + 1) / N_HEADS) for h in range(N_HEADS))

COL_U, COL_VA, COL_GA, COL_Q, COL_K, COL_V, COL_GB = (i * WIDTH for i in range(7))

VMEM_LIMIT_BYTES = 56 * 1024 * 1024

ATTN_TILE = 512
TOKEN_TILE = ATTN_TILE
SAMPLE_STREAMS_PER_STEP = 2
OUTPROJ_TILE = 2048
LANE_CHUNK = 256
KEY_WIDTH = 2 * HEAD_DIM
POS_SPLIT = 3
VT_ROWS = HEAD_DIM + 16


def _lam_init(layer_idx):
    return 0.8 - 0.6 * math.exp(-0.3 * layer_idx)


def _compiler_params(n_axes):
    return pltpu.CompilerParams(
        dimension_semantics=("arbitrary",) * n_axes,
        vmem_limit_bytes=VMEM_LIMIT_BYTES,
    )


class _LayerParam(NamedTuple):
    array: jax.Array
    layer: int


def _array(p):
    return p.array if isinstance(p, _LayerParam) else p


def _const_spec(p, single_buffer=False):
    if isinstance(p, _LayerParam):
        block = (None, *p.array.shape[1:])
        index = (p.layer,) + (0,) * (p.array.ndim - 1)
    else:
        block, index = p.shape, (0,) * p.ndim
    kwargs = dict(pipeline_mode=pl.Buffered(1)) if single_buffer else {}
    return pl.BlockSpec(block, lambda *grid_idx: index, **kwargs)


def _dot(a, b):
    return jnp.dot(a, b, preferred_element_type=F32)


def _dot_nt(a, b):
    return lax.dot_general(a, b, (((1,), (1,)), ((), ())), preferred_element_type=F32)


def _lam_value(lq1_ref, lk1_ref, lq2_ref, lk2_ref, lam_init):
    d1 = jnp.sum(lq1_ref[...] * lk1_ref[...], axis=-1, keepdims=True)
    d2 = jnp.sum(lq2_ref[...] * lk2_ref[...], axis=-1, keepdims=True)
    return jnp.exp(d1) - jnp.exp(d2) + lam_init


def _head_cols(h):
    return slice(h * HEAD_DIM, (h + 1) * HEAD_DIM)


def _inproj_kernel(*refs, prompt, n_prev, fused):
    x_ref, ng_ref, w_ref, sgug_ref, sguwt_ref, sgub_ref, gq_ref, gk_ref = refs[:8]
    n_in = 8
    if fused:
        ya_prev_ref, yb_prev_ref, wo_ref = refs[n_in:n_in + 3]
        n_in += 3
    prev = refs[n_in:n_in + n_prev]
    outs = refs[n_in + n_prev:]
    if fused:
        xo_ref, outs = outs[0], outs[1:]
    tile = x_ref.shape[0]

    first_half = lax.broadcasted_iota(jnp.int32, (tile, HEAD_DIM), 1) < HALF_DIM

    def group_rms_scale(z, group):
        scales = []
        for g in range(WIDTH // HEAD_DIM):
            sq = z[:, _head_cols(g)] ** 2
            if group == HEAD_DIM:
                total = jnp.sum(sq, axis=-1, keepdims=True)
            else:
                total = jnp.where(
                    first_half,
                    jnp.sum(jnp.where(first_half, sq, 0.0), axis=-1, keepdims=True),
                    jnp.sum(jnp.where(first_half, 0.0, sq), axis=-1, keepdims=True))
            scales.append(jnp.broadcast_to(lax.rsqrt(total * (1.0 / group) + NORM_EPS),
                                           (tile, HEAD_DIM)))
        return jnp.concatenate(scales, axis=1)

    if prompt:
        ya_ref, q_ref, key_ref, kt_ref, v_ref, vt_ref, gb_ref = outs
        last = kt_ref.shape[0] - 1
        if n_prev:
            kt_prev_ref, v_prev_ref = prev
            kt_ref[:last] = kt_prev_ref[...]
            v_ref[:last] = v_prev_ref[...]
        for h in range(N_HEADS):
            vt_ref[h, HEAD_DIM:, :] = jnp.ones((VT_ROWS - HEAD_DIM, tile), BF16)
    else:
        ya_ref, q_ref, k_ref, v_ref, gb_ref, va_ref = outs

    x = x_ref[...]
    if fused:
        yb_prev = jnp.concatenate([yb_prev_ref[h] for h in range(N_HEADS)], axis=1)
        x = x + (_dot(ya_prev_ref[...], wo_ref[:WIDTH, :]) + _dot(yb_prev, wo_ref[WIDTH:, :]))
        xo_ref[...] = x
    hb = (x * ng_ref[...]).astype(BF16)
    row_scale = jnp.broadcast_to(
        lax.rsqrt(jnp.mean(x * x, axis=-1, keepdims=True) + NORM_EPS), (tile, WIDTH))

    def proj(col):
        return _dot(hb, w_ref[:, col:col + WIDTH]) * row_scale

    va = proj(COL_VA)
    if not prompt:
        va_ref[...] = va
    vn = (va * group_rms_scale(va, HEAD_DIM)) * sgug_ref[...]
    gate = proj(COL_U) * jax.nn.silu(proj(COL_GA))
    chunk_rows = [slice(c * SGU_CHUNK, (c + 1) * SGU_CHUNK) for c in range(tile // SGU_CHUNK)]
    for g in range(N_HEADS):
        cols = _head_cols(g)
        vn_t = jnp.concatenate([vn[r, cols].T for r in chunk_rows], axis=0).astype(BF16)
        mixed_t = _dot(vn_t, sguwt_ref[g])
        for r in chunk_rows:
            mixed = mixed_t[r, :].T + sgub_ref[g]
            ya_ref[r, cols] = (gate[r, cols] * mixed).astype(BF16)

    zq = proj(COL_Q)
    qn = (zq * group_rms_scale(zq, HALF_DIM)) * gq_ref[...] * (QK_SCALE * LOG2E)
    zk = proj(COL_K)
    kn = (zk * group_rms_scale(zk, HALF_DIM)) * gk_ref[...]
    zv = proj(COL_V)
    gb = jax.nn.silu(proj(COL_GB)).astype(BF16)

    if prompt:
        kt_ref[last] = kn.T
        for h in range(N_HEADS):
            cols = _head_cols(h)
            q_ref[h] = qn[:, cols].T.astype(BF16)
            gb_ref[h] = gb[:, cols]
            key_ref[h] = kn[:, cols].astype(BF16)
            vt_ref[h, :HEAD_DIM, :] = zv[:, cols].T.astype(BF16)
            v_ref[last, pl.ds(h, tile, stride=N_HEADS), :] = zv[:, cols]
    else:
        q_ref[...] = qn.astype(BF16)
        k_ref[...] = kn
        v_ref[...] = zv
        gb_ref[...] = gb


def _inproj_prompt(x, params, prev_out, k_stack, v_stack, *, batch, seq):
    tile = TOKEN_TILE
    n_t = seq // tile
    n = batch * seq
    n_layers = 1 if k_stack is None else k_stack.shape[0] + 1
    full = lambda p: _const_spec(p, single_buffer=True)
    head_major = lambda width: pl.BlockSpec((None, N_HEADS, tile, width),
                                            lambda b, i: (b, 0, i, 0))
    kt_spec = lambda layers: pl.BlockSpec((layers, None, WIDTH, tile),
                                          lambda b, i: (0, b, 0, i))
    v_spec = lambda layers: pl.BlockSpec((layers, tile * N_HEADS, HEAD_DIM),
                                         lambda b, i: (0, b * n_t + i, 0))
    x_spec = pl.BlockSpec((tile, D_MODEL), lambda b, i: (b * n_t + i, 0))
    ya_spec = pl.BlockSpec((tile, WIDTH), lambda b, i: (b * n_t + i, 0))
    in_specs = [x_spec] + [full(p) for p in params]
    operands = [x] + [_array(p) for p in params]
    fused = prev_out is not None
    if fused:
        ya_prev, yb_prev, w_out_prev = prev_out
        in_specs += [ya_spec, head_major(HEAD_DIM), full(w_out_prev)]
        operands += [ya_prev, yb_prev, _array(w_out_prev)]
    if n_layers > 1:
        in_specs += [kt_spec(n_layers - 1), v_spec(n_layers - 1)]
        operands += [k_stack, v_stack]
    out_shape = [
        jax.ShapeDtypeStruct((n, WIDTH), BF16),
        jax.ShapeDtypeStruct((batch, N_HEADS, HEAD_DIM, seq), BF16),
        jax.ShapeDtypeStruct((batch, N_HEADS, n_t, tile, HEAD_DIM), BF16),
        jax.ShapeDtypeStruct((n_layers, batch, WIDTH, seq), F32),
        jax.ShapeDtypeStruct((n_layers, n * N_HEADS, HEAD_DIM), F32),
        jax.ShapeDtypeStruct((batch, N_HEADS, n_t, VT_ROWS, tile), BF16),
        jax.ShapeDtypeStruct((batch, N_HEADS, seq, HEAD_DIM), BF16),
    ]
    out_specs = [
        ya_spec,
        pl.BlockSpec((None, N_HEADS, HEAD_DIM, tile), lambda b, i: (b, 0, 0, i)),
        pl.BlockSpec((None, N_HEADS, None, tile, HEAD_DIM), lambda b, i: (b, 0, i, 0, 0)),
        kt_spec(n_layers),
        v_spec(n_layers),
        pl.BlockSpec((None, N_HEADS, None, VT_ROWS, tile), lambda b, i: (b, 0, i, 0, 0)),
        head_major(HEAD_DIM),
    ]
    if fused:
        out_shape.insert(0, jax.ShapeDtypeStruct((n, D_MODEL), F32))
        out_specs.insert(0, x_spec)
    outs = pl.pallas_call(
        functools.partial(_inproj_kernel, prompt=True, n_prev=2 * (n_layers > 1), fused=fused),
        grid=(batch, n_t),
        in_specs=in_specs,
        out_specs=out_specs,
        out_shape=out_shape,
        compiler_params=_compiler_params(2),
        name="inproj_prompt",
    )(*operands)
    return outs if fused else (x, *outs)


def _inproj_sample(x, params, *, tile):
    n = x.shape[0]
    row_spec = lambda width: pl.BlockSpec((tile, width), lambda i: (i, 0))
    dtypes = (BF16, BF16, F32, F32, BF16, F32)
    return pl.pallas_call(
        functools.partial(_inproj_kernel, prompt=False, n_prev=0, fused=False),
        grid=(n // tile,),
        in_specs=[row_spec(D_MODEL)] + [_const_spec(p) for p in params],
        out_specs=[row_spec(WIDTH)] * len(dtypes),
        out_shape=[jax.ShapeDtypeStruct((n, WIDTH), d) for d in dtypes],
        compiler_params=_compiler_params(1),
        name="inproj_sample",
    )(x, *[_array(p) for p in params])


def _outproj_kernel(x_ref, ya_ref, yb_ref, w_ref, o_ref, *, head_major):
    if head_major:
        yb = jnp.concatenate([yb_ref[h] for h in range(N_HEADS)], axis=1)
    else:
        yb = yb_ref[...]
    y = _dot(ya_ref[...], w_ref[:WIDTH, :]) + _dot(yb, w_ref[WIDTH:, :])
    o_ref[...] = x_ref[...] + y


def _outproj(x, ya, yb, w_bf, *, tile):
    n = x.shape[0]
    head_major = yb.ndim == 4
    if head_major:
        n_t = yb.shape[2] // tile
        yb_spec = pl.BlockSpec((None, N_HEADS, tile, HEAD_DIM),
                               lambda i: (i // n_t, 0, i % n_t, 0))
    else:
        yb_spec = pl.BlockSpec((tile, WIDTH), lambda i: (i, 0))
    return pl.pallas_call(
        functools.partial(_outproj_kernel, head_major=head_major),
        grid=(n // tile,),
        in_specs=[
            pl.BlockSpec((tile, D_MODEL), lambda i: (i, 0)),
            pl.BlockSpec((tile, WIDTH), lambda i: (i, 0)),
            yb_spec,
            _const_spec(w_bf),
        ],
        out_specs=pl.BlockSpec((tile, D_MODEL), lambda i: (i, 0)),
        out_shape=jax.ShapeDtypeStruct((n, D_MODEL), F32),
        compiler_params=_compiler_params(1),
        name="outproj",
    )(x, ya, yb, _array(w_bf))


SCORES_AHEAD = 3
VALUES_BEHIND = 1
ATTN_HEADS_PER_STEP = 2


def _attn_units(seq):
    n_qc = seq // LANE_CHUNK
    qc_per_tile = ATTN_TILE // LANE_CHUNK
    return [(j, qc, half)
            for j in range(seq // ATTN_TILE)
            for qc in range(j * qc_per_tile, n_qc)
            for half in range(2)]


def _attn_kernel(slopes_ref, q_ref, key_ref, pos_ref, vt_ref, gb_ref, sg_ref, lq1_ref, lk1_ref,
                 lq2_ref, lk2_ref, o_ref, *scratch, lam_init):
    heads_per_step = q_ref.shape[0]

    def one_head(hh, carry):
        slope = slopes_ref[pl.program_id(1) * heads_per_step + hh]
        _attn_head(slope, q_ref.at[hh], key_ref.at[hh], pos_ref, vt_ref.at[hh], gb_ref.at[hh],
                   sg_ref, lq1_ref, lk1_ref, lq2_ref, lk2_ref, o_ref.at[hh], *scratch,
                   lam_init=lam_init)
        return carry

    lax.fori_loop(0, heads_per_step, one_head, 0)


def _attn_head(slope, q_ref, key_ref, pos_ref, vt_ref, gb_ref, sg_ref, lq1_ref, lk1_ref,
               lq2_ref, lk2_ref, o_ref, qbd_ref, s_ref, m_ref, acc_ref, *, lam_init):
    seq = q_ref.shape[1]
    t, w = ATTN_TILE, LANE_CHUNK
    assert t == 2 * w
    n_qc = seq // w
    slope2 = slope * LOG2E

    key = lax.broadcasted_iota(jnp.int32, (w, w), 0)
    qry = lax.broadcasted_iota(jnp.int32, (w, w), 1)
    allowed = jnp.right_shift(key, 6) <= jnp.right_shift(qry, 6)
    ahead = jnp.maximum(key - qry, 0).astype(F32)
    dt = jnp.where(allowed, (-2.0 * slope2) * ahead, NEG_INF)

    row = lax.broadcasted_iota(jnp.int32, (HEAD_DIM, w), 0)
    rest = jnp.full((HEAD_DIM, w), slope2, F32)
    slope_rows = jnp.zeros((HEAD_DIM, w), F32)
    for piece in range(POS_SPLIT):
        part = rest.astype(BF16).astype(F32)
        slope_rows = jnp.where((row == piece) | (row == piece + POS_SPLIT), part, slope_rows)
        rest = rest - part
    slope_rows = slope_rows.astype(BF16)
    zero_half = jnp.zeros((HALF_DIM, w), BF16)

    lam = _lam_value(lq1_ref, lk1_ref, lq2_ref, lk2_ref, lam_init)

    def n_keys(j, qc):
        return w if qc * w == j * t else t

    def stacked_queries(qc, half):
        lanes = slice((half * n_qc + qc) * w, (half * n_qc + qc + 1) * w)
        own = slice(half * HALF_DIM, (half + 1) * HALF_DIM)
        other = slice((1 - half) * HALF_DIM, (2 - half) * HALF_DIM)
        qbd_ref[own, lanes] = q_ref[own, qc * w:(qc + 1) * w]
        qbd_ref[other, lanes] = zero_half
        qbd_ref[HEAD_DIM:, lanes] = slope_rows
        return lanes

    def scores(unit):
        j, qc, half = unit
        if j == 0:
            lanes = stacked_queries(qc, half)
        else:
            lanes = slice((half * n_qc + qc) * w, (half * n_qc + qc + 1) * w)
        nk = n_keys(j, qc)
        keys = jnp.concatenate([key_ref[j, :nk, :], pos_ref[j, :nk, :]], axis=1)
        return _dot(keys, qbd_ref[:, lanes])

    def softmax(unit, st):
        j, qc, half = unit
        lanes = slice((half * n_qc + qc) * w, (half * n_qc + qc + 1) * w)
        nk = n_keys(j, qc)
        if qc * w < (j + 1) * t:
            st = st + dt if nk == w else jnp.concatenate([st[:w], st[w:] + dt], axis=0)
        m_cur = jnp.max(st, axis=0, keepdims=True)
        if j == 0:
            m_new, alpha = m_cur, None
        else:
            m_old = m_ref[:, lanes]
            m_new = jnp.maximum(m_old, m_cur)
            alpha = jnp.exp2(m_old - m_new)
        m_ref[:, lanes] = m_new
        return jnp.exp2(st - m_new).astype(BF16), alpha

    def values(unit, p, alpha):
        j, qc, half = unit
        lanes = slice((half * n_qc + qc) * w, (half * n_qc + qc + 1) * w)
        pv = _dot(vt_ref[j, :, :n_keys(j, qc)], p)
        acc_ref[:, lanes] = pv if j == 0 else alpha * acc_ref[:, lanes] + pv

    def normalized(lanes):
        return acc_ref[:HEAD_DIM, lanes] * (1.0 / acc_ref[HEAD_DIM:HEAD_DIM + 1, lanes])

    def finalize(qc):
        rows = slice(qc * w, (qc + 1) * w)
        lanes1 = slice(qc * w, (qc + 1) * w)
        lanes2 = slice((n_qc + qc) * w, (n_qc + qc + 1) * w)
        ot = normalized(lanes1) - lam * normalized(lanes2)
        ot = ot * lax.rsqrt(jnp.mean(ot * ot, axis=0, keepdims=True) + NORM_EPS)
        o = (ot.T * sg_ref[...]) * (1.0 - lam_init)
        o_ref[rows, :] = (o * gb_ref[rows, :].astype(F32)).astype(BF16)

    units = _attn_units(seq)
    n_slots = s_ref.shape[0]

    def emit_scores(u):
        nk = n_keys(*units[u][:2])
        s_ref[u % n_slots, :nk, :] = scores(units[u])

    def load_scores(u):
        return s_ref[u % n_slots, :n_keys(*units[u][:2]), :]

    for u in range(min(SCORES_AHEAD, len(units))):
        emit_scores(u)
    pending_values = collections.deque()

    def run_oldest_values():
        unit_done, p, alpha = pending_values.popleft()
        values(unit_done, p, alpha)
        j_done, qc_done, half_done = unit_done
        if half_done == 1 and qc_done * w < (j_done + 1) * t:
            finalize(qc_done)

    for u, unit in enumerate(units):
        if u + SCORES_AHEAD < len(units):
            emit_scores(u + SCORES_AHEAD)
        pending_values.append((unit, *softmax(unit, load_scores(u))))
        if len(pending_values) > VALUES_BEHIND:
            run_oldest_values()
    while pending_values:
        run_oldest_values()


def _prompt_attention(q, keys, vt, gb, slopes, sg, lq1, lk1, lq2, lk2, *, lam_init):
    batch, _, _, seq = q.shape
    t = ATTN_TILE
    n_tiles = seq // t
    hps = ATTN_HEADS_PER_STEP
    seq_spec = pl.BlockSpec((None, hps, seq, HEAD_DIM), lambda b, h: (b, h, 0, 0))
    vec = _const_spec
    pos = jnp.arange(seq, dtype=jnp.int32)[:, None]
    lane = jnp.arange(HEAD_DIM, dtype=jnp.int32)[None, :]
    lo = jnp.bitwise_and(pos, CHUNK - 1)
    pos_cols = jnp.where(lane < POS_SPLIT, pos - lo, jnp.where(lane < 2 * POS_SPLIT, lo, 0))
    pos_cols = pos_cols.astype(BF16).reshape(n_tiles, t, HEAD_DIM)
    return pl.pallas_call(
        functools.partial(_attn_kernel, lam_init=lam_init),
        grid=(batch, N_HEADS // hps),
        in_specs=[
            pl.BlockSpec(memory_space=pltpu.SMEM),
            pl.BlockSpec((None, hps, HEAD_DIM, seq), lambda b, h: (b, h, 0, 0)),
            pl.BlockSpec((None, hps, n_tiles, t, HEAD_DIM), lambda b, h: (b, h, 0, 0, 0)),
            vec(pos_cols),
            pl.BlockSpec((None, hps, n_tiles, VT_ROWS, t), lambda b, h: (b, h, 0, 0, 0)),
            seq_spec, vec(sg), vec(lq1), vec(lk1), vec(lq2), vec(lk2)],
        out_specs=seq_spec,
        out_shape=jax.ShapeDtypeStruct((batch, N_HEADS, seq, HEAD_DIM), BF16),
        scratch_shapes=[
            pltpu.VMEM((KEY_WIDTH, 2 * seq), BF16),
            pltpu.VMEM((SCORES_AHEAD + 1, t, LANE_CHUNK), F32),
            pltpu.VMEM((1, 2 * seq), F32),
            pltpu.VMEM((VT_ROWS, 2 * seq), F32),
        ],
        compiler_params=_compiler_params(2),
        name="prompt_attn",
    )(slopes, q, keys, pos_cols, vt, gb, *[_array(p) for p in (sg, lq1, lk1, lq2, lk2)])


def _sample_attn_kernel(q_ref, kn_ref, vn_ref, ckt_ref, cv_ref, gb_ref, *rest, **static):
    for s in range(q_ref.shape[0]):
        _sample_attn_stream(q_ref.at[s], kn_ref.at[s], vn_ref.at[s], ckt_ref.at[s], cv_ref.at[s],
                            gb_ref.at[s], *rest[:-1], rest[-1].at[s], **static)


def _sample_attn_stream(q_ref, kn_ref, vn_ref, ckt_ref, cv_ref, gb_ref, sg_ref,
                        lq1_ref, lk1_ref, lq2_ref, lk2_ref, o_ref, *, lam_init, past_len):
    nq = q_ref.shape[0]
    per_head = 2 * nq
    n_rows = N_HEADS * per_head
    lam = _lam_value(lq1_ref, lk1_ref, lq2_ref, lk2_ref, lam_init)

    def alibi(n_cols, key_pos0):
        row = lax.broadcasted_iota(jnp.int32, (n_rows, n_cols), 0)
        col = lax.broadcasted_iota(jnp.int32, (n_rows, n_cols), 1)
        dist = jnp.abs(past_len + lax.rem(row, nq) - (key_pos0 + col)).astype(F32)
        slope = jnp.zeros((n_rows, n_cols), F32)
        for h in range(N_HEADS):
            slope = jnp.where(lax.div(row, per_head) == h, ALIBI_SLOPES[h] * LOG2E, slope)
        return slope * dist

    zpad = jnp.zeros((HEAD_DIM - nq, HEAD_DIM), BF16)
    s_past, s_new, v_new = [], [], []
    for h in range(N_HEADS):
        cols = _head_cols(h)
        q = q_ref[:, cols]
        lane = lax.broadcasted_iota(jnp.int32, q.shape, 1)
        zero = jnp.zeros_like(q)
        qbd = jnp.concatenate([jnp.where(lane < HALF_DIM, q, zero),
                               jnp.where(lane >= HALF_DIM, q, zero)], axis=0)
        s_past.append(_dot(qbd, ckt_ref[cols, :].astype(BF16)))
        s_new.append(_dot_nt(qbd, jnp.concatenate([kn_ref[:, cols].astype(BF16), zpad], axis=0)))
        v_new.append(jnp.concatenate([vn_ref[:, cols].astype(BF16), zpad], axis=0))
    s_past = jnp.concatenate(s_past, axis=0) - alibi(past_len, 0)
    s_new = jnp.concatenate(s_new, axis=0) - alibi(HEAD_DIM, past_len)
    real_new = lax.broadcasted_iota(jnp.int32, s_new.shape, 1) < nq
    s_new = jnp.where(real_new, s_new, NEG_INF)
    m = jnp.maximum(jnp.max(s_past, axis=-1, keepdims=True),
                    jnp.max(s_new, axis=-1, keepdims=True))
    p_past = jnp.exp2(s_past - m)
    p_new = jnp.exp2(s_new - m)
    inv_l = 1.0 / (jnp.sum(p_past, axis=-1, keepdims=True)
                   + jnp.sum(p_new, axis=-1, keepdims=True))
    p_past = p_past.astype(BF16)
    p_new = p_new.astype(BF16)
    outs = []
    for h in range(N_HEADS):
        rows = slice(h * per_head, (h + 1) * per_head)
        v_past = cv_ref[pl.ds(h, past_len, stride=N_HEADS), :].astype(BF16)
        acc = (_dot(p_past[rows], v_past) + _dot(p_new[rows], v_new[h])) * inv_l[rows]
        o = acc[:nq] - lam * acc[nq:]
        outs.append(o * lax.rsqrt(jnp.mean(o * o, axis=-1, keepdims=True) + NORM_EPS))
    o = jnp.concatenate(outs, axis=1) * jnp.tile(sg_ref[...], (1, N_HEADS)) * (1.0 - lam_init)
    o_ref[...] = (o * gb_ref[...].astype(F32)).astype(BF16)


def _sample_attention(q, k_new, v_new, cache_kt, cache_v, gb, sg, lq1, lk1, lq2, lk2,
                      *, layer, n_streams, n_new, lam_init):
    past_len = cache_kt.shape[3]
    per_step = SAMPLE_STREAMS_PER_STEP
    assert n_streams % per_step == 0
    new_spec = pl.BlockSpec((per_step, n_new, WIDTH), lambda b: (b, 0, 0))
    vec = _const_spec
    r3 = lambda a: a.reshape(n_streams, n_new, WIDTH)
    out = pl.pallas_call(
        functools.partial(_sample_attn_kernel, lam_init=lam_init, past_len=past_len),
        grid=(n_streams // per_step,),
        in_specs=[
            new_spec, new_spec, new_spec,
            pl.BlockSpec((None, per_step, WIDTH, past_len), lambda b: (layer, b, 0, 0)),
            pl.BlockSpec((None, per_step, past_len * N_HEADS, HEAD_DIM),
                         lambda b: (layer, b, 0, 0)),
            new_spec, vec(sg), vec(lq1), vec(lk1), vec(lq2), vec(lk2)],
        out_specs=new_spec,
        out_shape=jax.ShapeDtypeStruct((n_streams, n_new, WIDTH), BF16),
        compiler_params=_compiler_params(1),
        name="sample_attn",
    )(r3(q), r3(k_new), r3(v_new), cache_kt, cache_v, r3(gb),
      *[_array(p) for p in (sg, lq1, lk1, lq2, lk2)])
    return out.reshape(n_streams * n_new, WIDTH)


def kernel(x_prompt, x_sample, cache_k, cache_v, norm_g, w_in, sgu_norm_g, sgu_w, sgu_b,
           q_norm_g, k_norm_g, lambda_q1, lambda_k1, lambda_q2, lambda_k2, subln_g, w_out):
    depth = w_in.shape[0]
    batch, seq, _ = x_prompt.shape
    n_streams, n_new, _ = x_sample.shape
    past_len = cache_k.shape[2]
    assert seq % ATTN_TILE == 0
    assert SGU_CHUNK % n_new == 0 and past_len % CHUNK == 0 and n_new <= CHUNK
    sample_tile = min(n_streams * n_new, TOKEN_TILE)
    assert sample_tile % SGU_CHUNK == 0 and (n_streams * n_new) % sample_tile == 0

    slopes = jnp.asarray(ALIBI_SLOPES, F32)
    tril = jnp.tril(jnp.ones((SGU_CHUNK, SGU_CHUNK), F32))
    tril_new = jnp.tril(jnp.ones((n_new, n_new), F32))
    streams_per_chunk = SGU_CHUNK // n_new
    eye = jnp.eye(streams_per_chunk, dtype=F32)
    cache_kt = jnp.transpose(cache_k, (0, 1, 3, 4, 5, 2)).reshape(depth, n_streams, WIDTH, past_len)
    cache_vr = cache_v.reshape(depth, n_streams, past_len * N_HEADS, HEAD_DIM)

    rows = lambda a: a.reshape(depth, 1, -1).astype(F32)
    w_in_bf = w_in.astype(BF16)
    w_out_bf = w_out.astype(BF16)
    ng = rows(norm_g)
    sgug = rows(sgu_norm_g)
    gq = rows(jnp.tile(q_norm_g, (1, WIDTH // HALF_DIM)))
    gk = rows(jnp.tile(k_norm_g, (1, WIDTH // HALF_DIM)))
    sg = rows(subln_g)
    lam_vecs = tuple(rows(a) for a in (lambda_q1, lambda_k1, lambda_q2, lambda_k2))
    sguw_p = jnp.swapaxes(sgu_w * tril, -1, -2).astype(BF16)
    sgub_p = jnp.broadcast_to(sgu_b[..., None], (depth, N_HEADS, SGU_CHUNK, HEAD_DIM))
    w_new = sgu_w[:, :, :n_new, :n_new] * tril_new
    sguw_s = jnp.swapaxes(jnp.einsum("ab,lhts->lhatbs", eye, w_new).reshape(
        depth, N_HEADS, SGU_CHUNK, SGU_CHUNK), -1, -2).astype(BF16)
    sgub_s = jnp.broadcast_to(
        jnp.tile(sgu_b[:, :, :n_new], (1, 1, streams_per_chunk))[..., None],
        (depth, N_HEADS, SGU_CHUNK, HEAD_DIM))

    xp = x_prompt.reshape(batch * seq, D_MODEL)
    xs = x_sample.reshape(n_streams * n_new, D_MODEL)
    k_stack = v_stack = prompt_out = None
    ks_rows, vs_rows, sgu_rows = [], [], []
    for i in range(depth):
        lam_init = _lam_init(i)
        of_layer = lambda a: _LayerParam(a, i)
        params_p = tuple(map(of_layer, (ng, w_in_bf, sgug, sguw_p, sgub_p, gq, gk)))
        params_s = tuple(map(of_layer, (ng, w_in_bf, sgug, sguw_s, sgub_s, gq, gk)))
        attn_params = tuple(map(of_layer, (sg, *lam_vecs)))

        xp, ya, q, keys, k_stack, v_stack, vt, gb = _inproj_prompt(
            xp, params_p, prompt_out, k_stack, v_stack, batch=batch, seq=seq)
        yb = _prompt_attention(q, keys, vt, gb, slopes, *attn_params, lam_init=lam_init)
        prompt_out = (ya, yb, of_layer(w_out_bf))

        ya, q, k, v, gb, va = _inproj_sample(xs, params_s, tile=sample_tile)
        yb = _sample_attention(q, k, v, cache_kt, cache_vr, gb, *attn_params, layer=i,
                               n_streams=n_streams, n_new=n_new, lam_init=lam_init)
        xs = _outproj(xs, ya, yb, of_layer(w_out_bf), tile=sample_tile)
        ks_rows.append(k)
        vs_rows.append(v)
        sgu_rows.append(va)

    xp = _outproj(xp, *prompt_out, tile=OUTPROJ_TILE)
    new_k_prompt = jnp.transpose(
        k_stack.reshape(depth, batch, N_HEADS, 2, HALF_DIM, seq), (0, 1, 5, 2, 3, 4))
    return (
        xp.reshape(batch, seq, D_MODEL),
        xs.reshape(n_streams, n_new, D_MODEL),
        new_k_prompt,
        v_stack.reshape(depth, batch, seq, N_HEADS, HEAD_DIM),
        jnp.stack(ks_rows).reshape(depth, n_streams, n_new, N_HEADS, 2, HALF_DIM),
        jnp.stack(vs_rows).reshape(depth, n_streams, n_new, N_HEADS, HEAD_DIM),
        jnp.stack(sgu_rows).reshape(depth, n_streams, n_new, WIDTH),
    )
```

```python
import collections
import functools
import math
from typing import NamedTuple

import jax
import jax.numpy as jnp
from jax import lax
from jax.experimental import pallas as pl
from jax.experimental.pallas import tpu as pltpu

F32 = jnp.float32
BF16 = jnp.bfloat16

D_MODEL = 1024
N_HEADS = 4
HEAD_DIM = 128
HALF_DIM = 64
WIDTH = N_HEADS * HEAD_DIM
CHUNK = 64
SGU_CHUNK = 128
NORM_EPS = 1e-6
NEG_INF = -1e30
QK_SCALE = HALF_DIM ** -0.5
LOG2E = math.log2(math.e)
ALIBI_SLOPES = tuple(2.0 ** (-8.0 * (h + 1) / N_HEADS) for h in range(N_HEADS))

COL_U, COL_VA, COL_GA, COL_Q, COL_K, COL_V, COL_GB = (i * WIDTH for i in range(7))

VMEM_LIMIT_BYTES = 56 * 1024 * 1024

ATTN_TILE = 512
TOKEN_TILE = ATTN_TILE
SAMPLE_STREAMS_PER_STEP = 2
OUTPROJ_TILE = 2048
LANE_CHUNK = 256
KEY_WIDTH = 2 * HEAD_DIM
POS_SPLIT = 3
VT_ROWS = HEAD_DIM + 16


def _lam_init(layer_idx):
    return 0.8 - 0.6 * math.exp(-0.3 * layer_idx)


def _compiler_params(n_axes):
    return pltpu.CompilerParams(
        dimension_semantics=("arbitrary",) * n_axes,
        vmem_limit_bytes=VMEM_LIMIT_BYTES,
    )


class _LayerParam(NamedTuple):
    array: jax.Array
    layer: int


def _array(p):
    return p.array if isinstance(p, _LayerParam) else p


def _const_spec(p, single_buffer=False):
    if isinstance(p, _LayerParam):
        block = (None, *p.array.shape[1:])
        index = (p.layer,) + (0,) * (p.array.ndim - 1)
    else:
        block, index = p.shape, (0,) * p.ndim
    kwargs = dict(pipeline_mode=pl.Buffered(1)) if single_buffer else {}
    return pl.BlockSpec(block, lambda *grid_idx: index, **kwargs)


def _dot(a, b):
    return jnp.dot(a, b, preferred_element_type=F32)


def _dot_nt(a, b):
    return lax.dot_general(a, b, (((1,), (1,)), ((), ())), preferred_element_type=F32)


def _lam_value(lq1_ref, lk1_ref, lq2_ref, lk2_ref, lam_init):
    d1 = jnp.sum(lq1_ref[...] * lk1_ref[...], axis=-1, keepdims=True)
    d2 = jnp.sum(lq2_ref[...] * lk2_ref[...], axis=-1, keepdims=True)
    return jnp.exp(d1) - jnp.exp(d2) + lam_init


def _head_cols(h):
    return slice(h * HEAD_DIM, (h + 1) * HEAD_DIM)


def _inproj_kernel(*refs, prompt, n_prev, fused):
    x_ref, ng_ref, w_ref, sgug_ref, sguwt_ref, sgub_ref, gq_ref, gk_ref = refs[:8]
    n_in = 8
    if fused:
        ya_prev_ref, yb_prev_ref, wo_ref = refs[n_in:n_in + 3]
        n_in += 3
    prev = refs[n_in:n_in + n_prev]
    outs = refs[n_in + n_prev:]
    if fused:
        xo_ref, outs = outs[0], outs[1:]
    tile = x_ref.shape[0]

    first_half = lax.broadcasted_iota(jnp.int32, (tile, HEAD_DIM), 1) < HALF_DIM

    def group_rms_scale(z, group):
        scales = []
        for g in range(WIDTH // HEAD_DIM):
            sq = z[:, _head_cols(g)] ** 2
            if group == HEAD_DIM:
                total = jnp.sum(sq, axis=-1, keepdims=True)
            else:
                total = jnp.where(
                    first_half,
                    jnp.sum(jnp.where(first_half, sq, 0.0), axis=-1, keepdims=True),
                    jnp.sum(jnp.where(first_half, 0.0, sq), axis=-1, keepdims=True))
            scales.append(jnp.broadcast_to(lax.rsqrt(total * (1.0 / group) + NORM_EPS),
                                           (tile, HEAD_DIM)))
        return jnp.concatenate(scales, axis=1)

    if prompt:
        ya_ref, q_ref, key_ref, kt_ref, v_ref, vt_ref, gb_ref = outs
        last = kt_ref.shape[0] - 1
        if n_prev:
            kt_prev_ref, v_prev_ref = prev
            kt_ref[:last] = kt_prev_ref[...]
            v_ref[:last] = v_prev_ref[...]
        for h in range(N_HEADS):
            vt_ref[h, HEAD_DIM:, :] = jnp.ones((VT_ROWS - HEAD_DIM, tile), BF16)
    else:
        ya_ref, q_ref, k_ref, v_ref, gb_ref, va_ref = outs
        last = k_ref.shape[0] - 1
        for stacked_ref, prev_ref in zip((k_ref, v_ref, va_ref), prev):
            stacked_ref[:last] = prev_ref[...]

    x = x_ref[...]
    if fused:
        yb_prev = jnp.concatenate([yb_prev_ref[h] for h in range(N_HEADS)], axis=1)
        x = x + (_dot(ya_prev_ref[...], wo_ref[:WIDTH, :]) + _dot(yb_prev, wo_ref[WIDTH:, :]))
        xo_ref[...] = x
    hb = (x * ng_ref[...]).astype(BF16)
    row_scale = jnp.broadcast_to(
        lax.rsqrt(jnp.mean(x * x, axis=-1, keepdims=True) + NORM_EPS), (tile, WIDTH))

    def proj(col):
        return _dot(hb, w_ref[:, col:col + WIDTH]) * row_scale

    va = proj(COL_VA)
    if not prompt:
        va_ref[last] = va
    vn = (va * group_rms_scale(va, HEAD_DIM)) * sgug_ref[...]
    gate = proj(COL_U) * jax.nn.silu(proj(COL_GA))
    chunk_rows = [slice(c * SGU_CHUNK, (c + 1) * SGU_CHUNK) for c in range(tile // SGU_CHUNK)]
    for g in range(N_HEADS):
        cols = _head_cols(g)
        vn_t = jnp.concatenate([vn[r, cols].T for r in chunk_rows], axis=0).astype(BF16)
        mixed_t = _dot(vn_t, sguwt_ref[g])
        for r in chunk_rows:
            mixed = mixed_t[r, :].T + sgub_ref[g]
            ya_ref[r, cols] = (gate[r, cols] * mixed).astype(BF16)

    zq = proj(COL_Q)
    qn = (zq * group_rms_scale(zq, HALF_DIM)) * gq_ref[...] * (QK_SCALE * LOG2E)
    zk = proj(COL_K)
    kn = (zk * group_rms_scale(zk, HALF_DIM)) * gk_ref[...]
    zv = proj(COL_V)
    gb = jax.nn.silu(proj(COL_GB)).astype(BF16)

    if prompt:
        kt_ref[last] = kn.T
        for h in range(N_HEADS):
            cols = _head_cols(h)
            q_ref[h] = qn[:, cols].T.astype(BF16)
            gb_ref[h] = gb[:, cols]
            key_ref[h] = kn[:, cols].astype(BF16)
            vt_ref[h, :HEAD_DIM, :] = zv[:, cols].T.astype(BF16)
            v_ref[last, pl.ds(h, tile, stride=N_HEADS), :] = zv[:, cols]
    else:
        q_ref[...] = qn.astype(BF16)
        k_ref[last] = kn
        v_ref[last] = zv
        gb_ref[...] = gb


def _inproj_prompt(x, params, prev_out, k_stack, v_stack, *, batch, seq):
    tile = TOKEN_TILE
    n_t = seq // tile
    n = batch * seq
    n_layers = 1 if k_stack is None else k_stack.shape[0] + 1
    full = lambda p: _const_spec(p, single_buffer=True)
    head_major = lambda width: pl.BlockSpec((None, N_HEADS, tile, width),
                                            lambda b, i: (b, 0, i, 0))
    kt_spec = lambda layers: pl.BlockSpec((layers, None, WIDTH, tile),
                                          lambda b, i: (0, b, 0, i))
    v_spec = lambda layers: pl.BlockSpec((layers, tile * N_HEADS, HEAD_DIM),
                                         lambda b, i: (0, b * n_t + i, 0))
    x_spec = pl.BlockSpec((tile, D_MODEL), lambda b, i: (b * n_t + i, 0))
    ya_spec = pl.BlockSpec((tile, WIDTH), lambda b, i: (b * n_t + i, 0))
    in_specs = [x_spec] + [full(p) for p in params]
    operands = [x] + [_array(p) for p in params]
    fused = prev_out is not None
    if fused:
        ya_prev, yb_prev, w_out_prev = prev_out
        in_specs += [ya_spec, head_major(HEAD_DIM), full(w_out_prev)]
        operands += [ya_prev, yb_prev, _array(w_out_prev)]
    if n_layers > 1:
        in_specs += [kt_spec(n_layers - 1), v_spec(n_layers - 1)]
        operands += [k_stack, v_stack]
    out_shape = [
        jax.ShapeDtypeStruct((n, WIDTH), BF16),
        jax.ShapeDtypeStruct((batch, N_HEADS, HEAD_DIM, seq), BF16),
        jax.ShapeDtypeStruct((batch, N_HEADS, n_t, tile, HEAD_DIM), BF16),
        jax.ShapeDtypeStruct((n_layers, batch, WIDTH, seq), F32),
        jax.ShapeDtypeStruct((n_layers, n * N_HEADS, HEAD_DIM), F32),
        jax.ShapeDtypeStruct((batch, N_HEADS, n_t, VT_ROWS, tile), BF16),
        jax.ShapeDtypeStruct((batch, N_HEADS, seq, HEAD_DIM), BF16),
    ]
    out_specs = [
        ya_spec,
        pl.BlockSpec((None, N_HEADS, HEAD_DIM, tile), lambda b, i: (b, 0, 0, i)),
        pl.BlockSpec((None, N_HEADS, None, tile, HEAD_DIM), lambda b, i: (b, 0, i, 0, 0)),
        kt_spec(n_layers),
        v_spec(n_layers),
        pl.BlockSpec((None, N_HEADS, None, VT_ROWS, tile), lambda b, i: (b, 0, i, 0, 0)),
        head_major(HEAD_DIM),
    ]
    if fused:
        out_shape.insert(0, jax.ShapeDtypeStruct((n, D_MODEL), F32))
        out_specs.insert(0, x_spec)
    outs = pl.pallas_call(
        functools.partial(_inproj_kernel, prompt=True, n_prev=2 * (n_layers > 1), fused=fused),
        grid=(batch, n_t),
        in_specs=in_specs,
        out_specs=out_specs,
        out_shape=out_shape,
        compiler_params=_compiler_params(2),
        name="inproj_prompt",
    )(*operands)
    return outs if fused else (x, *outs)


def _inproj_sample(x, params, prev_stacks, *, tile):
    n = x.shape[0]
    layers = prev_stacks[0].shape[0] + 1 if prev_stacks else 1
    row_spec = lambda width: pl.BlockSpec((tile, width), lambda i: (i, 0))
    stack_spec = lambda depth: pl.BlockSpec((depth, tile, WIDTH), lambda i: (0, i, 0))
    row = lambda d: (row_spec(WIDTH), jax.ShapeDtypeStruct((n, WIDTH), d))
    stack = (stack_spec(layers), jax.ShapeDtypeStruct((layers, n, WIDTH), F32))
    out_specs, out_shape = zip(row(BF16), row(BF16), stack, stack, row(BF16), stack)
    return pl.pallas_call(
        functools.partial(_inproj_kernel, prompt=False, n_prev=len(prev_stacks), fused=False),
        grid=(n // tile,),
        in_specs=([row_spec(D_MODEL)] + [_const_spec(p) for p in params]
                  + [stack_spec(layers - 1)] * len(prev_stacks)),
        out_specs=list(out_specs),
        out_shape=list(out_shape),
        compiler_params=_compiler_params(1),
        name="inproj_sample",
    )(x, *[_array(p) for p in params], *prev_stacks)


def _outproj_kernel(x_ref, ya_ref, yb_ref, w_ref, o_ref, *, head_major):
    if head_major:
        yb = jnp.concatenate([yb_ref[h] for h in range(N_HEADS)], axis=1)
    else:
        yb = yb_ref[...]
    y = _dot(ya_ref[...], w_ref[:WIDTH, :]) + _dot(yb, w_ref[WIDTH:, :])
    o_ref[...] = x_ref[...] + y


def _outproj(x, ya, yb, w_bf, *, tile):
    n = x.shape[0]
    head_major = yb.ndim == 4
    if head_major:
        n_t = yb.shape[2] // tile
        yb_spec = pl.BlockSpec((None, N_HEADS, tile, HEAD_DIM),
                               lambda i: (i // n_t, 0, i % n_t, 0))
    else:
        yb_spec = pl.BlockSpec((tile, WIDTH), lambda i: (i, 0))
    return pl.pallas_call(
        functools.partial(_outproj_kernel, head_major=head_major),
        grid=(n // tile,),
        in_specs=[
            pl.BlockSpec((tile, D_MODEL), lambda i: (i, 0)),
            pl.BlockSpec((tile, WIDTH), lambda i: (i, 0)),
            yb_spec,
            _const_spec(w_bf),
        ],
        out_specs=pl.BlockSpec((tile, D_MODEL), lambda i: (i, 0)),
        out_shape=jax.ShapeDtypeStruct((n, D_MODEL), F32),
        compiler_params=_compiler_params(1),
        name="outproj",
    )(x, ya, yb, _array(w_bf))


SCORES_AHEAD = 3
VALUES_BEHIND = 1
ATTN_HEADS_PER_STEP = 2


def _attn_units(seq):
    n_qc = seq // LANE_CHUNK
    qc_per_tile = ATTN_TILE // LANE_CHUNK
    return [(j, qc, half)
            for j in range(seq // ATTN_TILE)
            for qc in range(j * qc_per_tile, n_qc)
            for half in range(2)]


def _attn_kernel(slopes_ref, q_ref, key_ref, pos_ref, vt_ref, gb_ref, sg_ref, lq1_ref, lk1_ref,
                 lq2_ref, lk2_ref, o_ref, *scratch, lam_init):
    heads_per_step = q_ref.shape[0]

    def one_head(hh, carry):
        slope = slopes_ref[pl.program_id(1) * heads_per_step + hh]
        _attn_head(slope, q_ref.at[hh], key_ref.at[hh], pos_ref, vt_ref.at[hh], gb_ref.at[hh],
                   sg_ref, lq1_ref, lk1_ref, lq2_ref, lk2_ref, o_ref.at[hh], *scratch,
                   lam_init=lam_init)
        return carry

    lax.fori_loop(0, heads_per_step, one_head, 0)


def _attn_head(slope, q_ref, key_ref, pos_ref, vt_ref, gb_ref, sg_ref, lq1_ref, lk1_ref,
               lq2_ref, lk2_ref, o_ref, qbd_ref, s_ref, m_ref, acc_ref, *, lam_init):
    seq = q_ref.shape[1]
    t, w = ATTN_TILE, LANE_CHUNK
    assert t == 2 * w
    n_qc = seq // w
    slope2 = slope * LOG2E

    key = lax.broadcasted_iota(jnp.int32, (w, w), 0)
    qry = lax.broadcasted_iota(jnp.int32, (w, w), 1)
    allowed = jnp.right_shift(key, 6) <= jnp.right_shift(qry, 6)
    ahead = jnp.maximum(key - qry, 0).astype(F32)
    dt = jnp.where(allowed, (-2.0 * slope2) * ahead, NEG_INF)

    row = lax.broadcasted_iota(jnp.int32, (HEAD_DIM, w), 0)
    rest = jnp.full((HEAD_DIM, w), slope2, F32)
    slope_rows = jnp.zeros((HEAD_DIM, w), F32)
    for piece in range(POS_SPLIT):
        part = rest.astype(BF16).astype(F32)
        slope_rows = jnp.where((row == piece) | (row == piece + POS_SPLIT), part, slope_rows)
        rest = rest - part
    slope_rows = slope_rows.astype(BF16)
    zero_half = jnp.zeros((HALF_DIM, w), BF16)

    lam = _lam_value(lq1_ref, lk1_ref, lq2_ref, lk2_ref, lam_init)

    def n_keys(j, qc):
        return w if qc * w == j * t else t

    def stacked_queries(qc, half):
        lanes = slice((half * n_qc + qc) * w, (half * n_qc + qc + 1) * w)
        own = slice(half * HALF_DIM, (half + 1) * HALF_DIM)
        other = slice((1 - half) * HALF_DIM, (2 - half) * HALF_DIM)
        qbd_ref[own, lanes] = q_ref[own, qc * w:(qc + 1) * w]
        qbd_ref[other, lanes] = zero_half
        qbd_ref[HEAD_DIM:, lanes] = slope_rows
        return lanes

    def scores(unit):
        j, qc, half = unit
        if j == 0:
            lanes = stacked_queries(qc, half)
        else:
            lanes = slice((half * n_qc + qc) * w, (half * n_qc + qc + 1) * w)
        nk = n_keys(j, qc)
        keys = jnp.concatenate([key_ref[j, :nk, :], pos_ref[j, :nk, :]], axis=1)
        return _dot(keys, qbd_ref[:, lanes])

    def softmax(unit, st):
        j, qc, half = unit
        lanes = slice((half * n_qc + qc) * w, (half * n_qc + qc + 1) * w)
        nk = n_keys(j, qc)
        if qc * w < (j + 1) * t:
            st = st + dt if nk == w else jnp.concatenate([st[:w], st[w:] + dt], axis=0)
        m_cur = jnp.max(st, axis=0, keepdims=True)
        if j == 0:
            m_new, alpha = m_cur, None
        else:
            m_old = m_ref[:, lanes]
            m_new = jnp.maximum(m_old, m_cur)
            alpha = jnp.exp2(m_old - m_new)
        m_ref[:, lanes] = m_new
        return jnp.exp2(st - m_new).astype(BF16), alpha

    def values(unit, p, alpha):
        j, qc, half = unit
        lanes = slice((half * n_qc + qc) * w, (half * n_qc + qc + 1) * w)
        pv = _dot(vt_ref[j, :, :n_keys(j, qc)], p)
        acc_ref[:, lanes] = pv if j == 0 else alpha * acc_ref[:, lanes] + pv

    def normalized(lanes):
        return acc_ref[:HEAD_DIM, lanes] * (1.0 / acc_ref[HEAD_DIM:HEAD_DIM + 1, lanes])

    def finalize(qc):
        rows = slice(qc * w, (qc + 1) * w)
        lanes1 = slice(qc * w, (qc + 1) * w)
        lanes2 = slice((n_qc + qc) * w, (n_qc + qc + 1) * w)
        ot = normalized(lanes1) - lam * normalized(lanes2)
        ot = ot * lax.rsqrt(jnp.mean(ot * ot, axis=0, keepdims=True) + NORM_EPS)
        o = (ot.T * sg_ref[...]) * (1.0 - lam_init)
        o_ref[rows, :] = (o * gb_ref[rows, :].astype(F32)).astype(BF16)

    units = _attn_units(seq)
    n_slots = s_ref.shape[0]

    def emit_scores(u):
        nk = n_keys(*units[u][:2])
        s_ref[u % n_slots, :nk, :] = scores(units[u])

    def load_scores(u):
        return s_ref[u % n_slots, :n_keys(*units[u][:2]), :]

    for u in range(min(SCORES_AHEAD, len(units))):
        emit_scores(u)
    pending_values = collections.deque()

    def run_oldest_values():
        unit_done, p, alpha = pending_values.popleft()
        values(unit_done, p, alpha)
        j_done, qc_done, half_done = unit_done
        if half_done == 1 and qc_done * w < (j_done + 1) * t:
            finalize(qc_done)

    for u, unit in enumerate(units):
        if u + SCORES_AHEAD < len(units):
            emit_scores(u + SCORES_AHEAD)
        pending_values.append((unit, *softmax(unit, load_scores(u))))
        if len(pending_values) > VALUES_BEHIND:
            run_oldest_values()
    while pending_values:
        run_oldest_values()


def _prompt_attention(q, keys, vt, gb, slopes, sg, lq1, lk1, lq2, lk2, *, lam_init):
    batch, _, _, seq = q.shape
    t = ATTN_TILE
    n_tiles = seq // t
    hps = ATTN_HEADS_PER_STEP
    seq_spec = pl.BlockSpec((None, hps, seq, HEAD_DIM), lambda b, h: (b, h, 0, 0))
    vec = _const_spec
    pos = jnp.arange(seq, dtype=jnp.int32)[:, None]
    lane = jnp.arange(HEAD_DIM, dtype=jnp.int32)[None, :]
    lo = jnp.bitwise_and(pos, CHUNK - 1)
    pos_cols = jnp.where(lane < POS_SPLIT, pos - lo, jnp.where(lane < 2 * POS_SPLIT, lo, 0))
    pos_cols = pos_cols.astype(BF16).reshape(n_tiles, t, HEAD_DIM)
    return pl.pallas_call(
        functools.partial(_attn_kernel, lam_init=lam_init),
        grid=(batch, N_HEADS // hps),
        in_specs=[
            pl.BlockSpec(memory_space=pltpu.SMEM),
            pl.BlockSpec((None, hps, HEAD_DIM, seq), lambda b, h: (b, h, 0, 0)),
            pl.BlockSpec((None, hps, n_tiles, t, HEAD_DIM), lambda b, h: (b, h, 0, 0, 0)),
            vec(pos_cols),
            pl.BlockSpec((None, hps, n_tiles, VT_ROWS, t), lambda b, h: (b, h, 0, 0, 0)),
            seq_spec, vec(sg), vec(lq1), vec(lk1), vec(lq2), vec(lk2)],
        out_specs=seq_spec,
        out_shape=jax.ShapeDtypeStruct((batch, N_HEADS, seq, HEAD_DIM), BF16),
        scratch_shapes=[
            pltpu.VMEM((KEY_WIDTH, 2 * seq), BF16),
            pltpu.VMEM((SCORES_AHEAD + 1, t, LANE_CHUNK), F32),
            pltpu.VMEM((1, 2 * seq), F32),
            pltpu.VMEM((VT_ROWS, 2 * seq), F32),
        ],
        compiler_params=_compiler_params(2),
        name="prompt_attn",
    )(slopes, q, keys, pos_cols, vt, gb, *[_array(p) for p in (sg, lq1, lk1, lq2, lk2)])


def _sample_attn_kernel(q_ref, kn_ref, vn_ref, ckt_ref, cv_ref, gb_ref, *rest, **static):
    for s in range(q_ref.shape[0]):
        _sample_attn_stream(q_ref.at[s], kn_ref.at[s], vn_ref.at[s], ckt_ref.at[s], cv_ref.at[s],
                            gb_ref.at[s], *rest[:-1], rest[-1].at[s], **static)


def _sample_attn_stream(q_ref, kn_ref, vn_ref, ckt_ref, cv_ref, gb_ref, sg_ref,
                        lq1_ref, lk1_ref, lq2_ref, lk2_ref, o_ref, *, lam_init, past_len):
    nq = q_ref.shape[0]
    per_head = 2 * nq
    n_rows = N_HEADS * per_head
    lam = _lam_value(lq1_ref, lk1_ref, lq2_ref, lk2_ref, lam_init)

    def alibi(n_cols, key_pos0):
        row = lax.broadcasted_iota(jnp.int32, (n_rows, n_cols), 0)
        col = lax.broadcasted_iota(jnp.int32, (n_rows, n_cols), 1)
        dist = jnp.abs(past_len + lax.rem(row, nq) - (key_pos0 + col)).astype(F32)
        slope = jnp.zeros((n_rows, n_cols), F32)
        for h in range(N_HEADS):
            slope = jnp.where(lax.div(row, per_head) == h, ALIBI_SLOPES[h] * LOG2E, slope)
        return slope * dist

    zpad = jnp.zeros((HEAD_DIM - nq, HEAD_DIM), BF16)
    s_past, s_new, v_new = [], [], []
    for h in range(N_HEADS):
        cols = _head_cols(h)
        q = q_ref[:, cols]
        lane = lax.broadcasted_iota(jnp.int32, q.shape, 1)
        zero = jnp.zeros_like(q)
        qbd = jnp.concatenate([jnp.where(lane < HALF_DIM, q, zero),
                               jnp.where(lane >= HALF_DIM, q, zero)], axis=0)
        s_past.append(_dot(qbd, ckt_ref[cols, :].astype(BF16)))
        s_new.append(_dot_nt(qbd, jnp.concatenate([kn_ref[:, cols].astype(BF16), zpad], axis=0)))
        v_new.append(jnp.concatenate([vn_ref[:, cols].astype(BF16), zpad], axis=0))
    s_past = jnp.concatenate(s_past, axis=0) - alibi(past_len, 0)
    s_new = jnp.concatenate(s_new, axis=0) - alibi(HEAD_DIM, past_len)
    real_new = lax.broadcasted_iota(jnp.int32, s_new.shape, 1) < nq
    s_new = jnp.where(real_new, s_new, NEG_INF)
    m = jnp.maximum(jnp.max(s_past, axis=-1, keepdims=True),
                    jnp.max(s_new, axis=-1, keepdims=True))
    p_past = jnp.exp2(s_past - m)
    p_new = jnp.exp2(s_new - m)
    inv_l = 1.0 / (jnp.sum(p_past, axis=-1, keepdims=True)
                   + jnp.sum(p_new, axis=-1, keepdims=True))
    p_past = p_past.astype(BF16)
    p_new = p_new.astype(BF16)
    outs = []
    for h in range(N_HEADS):
        rows = slice(h * per_head, (h + 1) * per_head)
        v_past = cv_ref[pl.ds(h, past_len, stride=N_HEADS), :].astype(BF16)
        acc = (_dot(p_past[rows], v_past) + _dot(p_new[rows], v_new[h])) * inv_l[rows]
        o = acc[:nq] - lam * acc[nq:]
        outs.append(o * lax.rsqrt(jnp.mean(o * o, axis=-1, keepdims=True) + NORM_EPS))
    o = jnp.concatenate(outs, axis=1) * jnp.tile(sg_ref[...], (1, N_HEADS)) * (1.0 - lam_init)
    o_ref[...] = (o * gb_ref[...].astype(F32)).astype(BF16)


def _sample_attention(q, k_new, v_new, cache_kt, cache_v, gb, sg, lq1, lk1, lq2, lk2,
                      *, layer, n_streams, n_new, lam_init):
    past_len = cache_kt.shape[3]
    per_step = SAMPLE_STREAMS_PER_STEP
    assert n_streams % per_step == 0
    new_spec = pl.BlockSpec((per_step, n_new, WIDTH), lambda b: (b, 0, 0))
    stacked_spec = pl.BlockSpec((None, per_step, n_new, WIDTH), lambda b: (layer, b, 0, 0))
    vec = _const_spec
    r3 = lambda a: a.reshape(*a.shape[:-2], n_streams, n_new, WIDTH)
    out = pl.pallas_call(
        functools.partial(_sample_attn_kernel, lam_init=lam_init, past_len=past_len),
        grid=(n_streams // per_step,),
        in_specs=[
            new_spec, stacked_spec, stacked_spec,
            pl.BlockSpec((None, per_step, WIDTH, past_len), lambda b: (layer, b, 0, 0)),
            pl.BlockSpec((None, per_step, past_len * N_HEADS, HEAD_DIM),
                         lambda b: (layer, b, 0, 0)),
            new_spec, vec(sg), vec(lq1), vec(lk1), vec(lq2), vec(lk2)],
        out_specs=new_spec,
        out_shape=jax.ShapeDtypeStruct((n_streams, n_new, WIDTH), BF16),
        compiler_params=_compiler_params(1),
        name="sample_attn",
    )(r3(q), r3(k_new), r3(v_new), cache_kt, cache_v, r3(gb),
      *[_array(p) for p in (sg, lq1, lk1, lq2, lk2)])
    return out.reshape(n_streams * n_new, WIDTH)


def kernel(x_prompt, x_sample, cache_k, cache_v, norm_g, w_in, sgu_norm_g, sgu_w, sgu_b,
           q_norm_g, k_norm_g, lambda_q1, lambda_k1, lambda_q2, lambda_k2, subln_g, w_out):
    depth = w_in.shape[0]
    batch, seq, _ = x_prompt.shape
    n_streams, n_new, _ = x_sample.shape
    past_len = cache_k.shape[2]
    assert seq % ATTN_TILE == 0
    assert SGU_CHUNK % n_new == 0 and past_len % CHUNK == 0 and n_new <= CHUNK
    sample_tile = min(n_streams * n_new, TOKEN_TILE)
    assert sample_tile % SGU_CHUNK == 0 and (n_streams * n_new) % sample_tile == 0

    slopes = jnp.asarray(ALIBI_SLOPES, F32)
    tril = jnp.tril(jnp.ones((SGU_CHUNK, SGU_CHUNK), F32))
    tril_new = jnp.tril(jnp.ones((n_new, n_new), F32))
    streams_per_chunk = SGU_CHUNK // n_new
    eye = jnp.eye(streams_per_chunk, dtype=F32)
    cache_kt = jnp.transpose(cache_k, (0, 1, 3, 4, 5, 2)).reshape(depth, n_streams, WIDTH, past_len)
    cache_vr = cache_v.reshape(depth, n_streams, past_len * N_HEADS, HEAD_DIM)

    rows = lambda a: a.reshape(depth, 1, -1).astype(F32)
    w_in_bf = w_in.astype(BF16)
    w_out_bf = w_out.astype(BF16)
    ng = rows(norm_g)
    sgug = rows(sgu_norm_g)
    gq = rows(jnp.tile(q_norm_g, (1, WIDTH // HALF_DIM)))
    gk = rows(jnp.tile(k_norm_g, (1, WIDTH // HALF_DIM)))
    sg = rows(subln_g)
    lam_vecs = tuple(rows(a) for a in (lambda_q1, lambda_k1, lambda_q2, lambda_k2))
    sguw_p = jnp.swapaxes(sgu_w * tril, -1, -2).astype(BF16)
    sgub_p = jnp.broadcast_to(sgu_b[..., None], (depth, N_HEADS, SGU_CHUNK, HEAD_DIM))
    w_new = sgu_w[:, :, :n_new, :n_new] * tril_new
    sguw_s = jnp.swapaxes(jnp.einsum("ab,lhts->lhatbs", eye, w_new).reshape(
        depth, N_HEADS, SGU_CHUNK, SGU_CHUNK), -1, -2).astype(BF16)
    sgub_s = jnp.broadcast_to(
        jnp.tile(sgu_b[:, :, :n_new], (1, 1, streams_per_chunk))[..., None],
        (depth, N_HEADS, SGU_CHUNK, HEAD_DIM))

    xp = x_prompt.reshape(batch * seq, D_MODEL)
    xs = x_sample.reshape(n_streams * n_new, D_MODEL)
    k_stack = v_stack = prompt_out = None
    sample_stacks = ()
    for i in range(depth):
        lam_init = _lam_init(i)
        of_layer = lambda a: _LayerParam(a, i)
        params_p = tuple(map(of_layer, (ng, w_in_bf, sgug, sguw_p, sgub_p, gq, gk)))
        params_s = tuple(map(of_layer, (ng, w_in_bf, sgug, sguw_s, sgub_s, gq, gk)))
        attn_params = tuple(map(of_layer, (sg, *lam_vecs)))

        xp, ya, q, keys, k_stack, v_stack, vt, gb = _inproj_prompt(
            xp, params_p, prompt_out, k_stack, v_stack, batch=batch, seq=seq)
        yb = _prompt_attention(q, keys, vt, gb, slopes, *attn_params, lam_init=lam_init)
        prompt_out = (ya, yb, of_layer(w_out_bf))

        ya, q, ks, vs, gb, vas = _inproj_sample(xs, params_s, sample_stacks, tile=sample_tile)
        sample_stacks = (ks, vs, vas)
        yb = _sample_attention(q, ks, vs, cache_kt, cache_vr, gb, *attn_params, layer=i,
                               n_streams=n_streams, n_new=n_new, lam_init=lam_init)
        xs = _outproj(xs, ya, yb, of_layer(w_out_bf), tile=sample_tile)

    xp = _outproj(xp, *prompt_out, tile=OUTPROJ_TILE)
    new_k_prompt = jnp.transpose(
        k_stack.reshape(depth, batch, N_HEADS, 2, HALF_DIM, seq), (0, 1, 5, 2, 3, 4))
    return (
        xp.reshape(batch, seq, D_MODEL),
        xs.reshape(n_streams, n_new, D_MODEL),
        new_k_prompt,
        v_stack.reshape(depth, batch, seq, N_HEADS, HEAD_DIM),
        ks.reshape(depth, n_streams, n_new, N_HEADS, 2, HALF_DIM),
        vs.reshape(depth, n_streams, n_new, N_HEADS, HEAD_DIM),
        vas.reshape(depth, n_streams, n_new, WIDTH),
    )
```

```python
import collections
import functools
import math
from typing import NamedTuple

import jax
import jax.numpy as jnp
from jax import lax
from jax.experimental import pallas as pl
from jax.experimental.pallas import tpu as pltpu

F32 = jnp.float32
BF16 = jnp.bfloat16

D_MODEL = 1024
N_HEADS = 4
HEAD_DIM = 128
HALF_DIM = 64
WIDTH = N_HEADS * HEAD_DIM
CHUNK = 64
SGU_CHUNK = 128
NORM_EPS = 1e-6
NEG_INF = -1e30
QK_SCALE = HALF_DIM ** -0.5
LOG2E = math.log2(math.e)
ALIBI_SLOPES = tuple(2.0 ** (-8.0 * (h + 1) / N_HEADS) for h in range(N_HEADS))

COL_U, COL_VA, COL_GA, COL_Q, COL_K, COL_V, COL_GB = (i * WIDTH for i in range(7))

VMEM_LIMIT_BYTES = 56 * 1024 * 1024

ATTN_TILE = 512
TOKEN_TILE = ATTN_TILE
SAMPLE_STREAMS_PER_STEP = 2
OUTPROJ_TILE = 2048
LANE_CHUNK = 256
KEY_WIDTH = 2 * HEAD_DIM
POS_SPLIT = 3
VT_ROWS = HEAD_DIM + 16


def _lam_init(layer_idx):
    return 0.8 - 0.6 * math.exp(-0.3 * layer_idx)


def _compiler_params(n_axes):
    return pltpu.CompilerParams(
        dimension_semantics=("arbitrary",) * n_axes,
        vmem_limit_bytes=VMEM_LIMIT_BYTES,
    )


class _LayerParam(NamedTuple):
    array: jax.Array
    layer: int


def _array(p):
    return p.array if isinstance(p, _LayerParam) else p


def _const_spec(p, single_buffer=False):
    if isinstance(p, _LayerParam):
        block = (None, *p.array.shape[1:])
        index = (p.layer,) + (0,) * (p.array.ndim - 1)
    else:
        block, index = p.shape, (0,) * p.ndim
    kwargs = dict(pipeline_mode=pl.Buffered(1)) if single_buffer else {}
    return pl.BlockSpec(block, lambda *grid_idx: index, **kwargs)


def _dot(a, b):
    return jnp.dot(a, b, preferred_element_type=F32)


def _dot_nt(a, b):
    return lax.dot_general(a, b, (((1,), (1,)), ((), ())), preferred_element_type=F32)


def _lam_value(lq1_ref, lk1_ref, lq2_ref, lk2_ref, lam_init):
    d1 = jnp.sum(lq1_ref[...] * lk1_ref[...], axis=-1, keepdims=True)
    d2 = jnp.sum(lq2_ref[...] * lk2_ref[...], axis=-1, keepdims=True)
    return jnp.exp(d1) - jnp.exp(d2) + lam_init


def _head_cols(h):
    return slice(h * HEAD_DIM, (h + 1) * HEAD_DIM)


def _inproj_kernel(*refs, prompt, n_prev, fused):
    x_ref, ng_ref, w_ref, sgug_ref, sguwt_ref, sgub_ref, gq_ref, gk_ref = refs[:8]
    n_in = 8
    if fused:
        ya_prev_ref, yb_prev_ref, wo_ref = refs[n_in:n_in + 3]
        n_in += 3
    prev = refs[n_in:n_in + n_prev]
    outs = refs[n_in + n_prev:]
    if fused:
        xo_ref, outs = outs[0], outs[1:]
    tile = x_ref.shape[0]

    first_half = lax.broadcasted_iota(jnp.int32, (tile, HEAD_DIM), 1) < HALF_DIM

    def group_rms_scale(z, group):
        scales = []
        for g in range(WIDTH // HEAD_DIM):
            sq = z[:, _head_cols(g)] ** 2
            if group == HEAD_DIM:
                total = jnp.sum(sq, axis=-1, keepdims=True)
            else:
                total = jnp.where(
                    first_half,
                    jnp.sum(jnp.where(first_half, sq, 0.0), axis=-1, keepdims=True),
                    jnp.sum(jnp.where(first_half, 0.0, sq), axis=-1, keepdims=True))
            scales.append(jnp.broadcast_to(lax.rsqrt(total * (1.0 / group) + NORM_EPS),
                                           (tile, HEAD_DIM)))
        return jnp.concatenate(scales, axis=1)

    if prompt:
        ya_ref, q_ref, key_ref, kt_ref, v_ref, vt_ref, gb_ref = outs
        last = kt_ref.shape[0] - 1
        if n_prev:
            kt_prev_ref, v_prev_ref = prev
            kt_ref[:last] = kt_prev_ref[...]
            v_ref[:last] = v_prev_ref[...]
        for h in range(N_HEADS):
            vt_ref[h, HEAD_DIM:, :] = jnp.ones((VT_ROWS - HEAD_DIM, tile), BF16)
    else:
        ya_ref, q_ref, k_ref, v_ref, gb_ref, va_ref = outs
        last = k_ref.shape[0] - 1
        for stacked_ref, prev_ref in zip((k_ref, v_ref, va_ref), prev):
            stacked_ref[:last] = prev_ref[...]

    x = x_ref[...]
    if fused:
        yb_prev = jnp.concatenate([yb_prev_ref[h] for h in range(N_HEADS)], axis=1)
        x = x + (_dot(ya_prev_ref[...], wo_ref[:WIDTH, :]) + _dot(yb_prev, wo_ref[WIDTH:, :]))
        xo_ref[...] = x
    hb = (x * ng_ref[...]).astype(BF16)
    row_scale = jnp.broadcast_to(
        lax.rsqrt(jnp.mean(x * x, axis=-1, keepdims=True) + NORM_EPS), (tile, WIDTH))

    def proj(col):
        return _dot(hb, w_ref[:, col:col + WIDTH]) * row_scale

    zq = proj(COL_Q)
    qn = (zq * group_rms_scale(zq, HALF_DIM)) * gq_ref[...] * (QK_SCALE * LOG2E)
    zk = proj(COL_K)
    kn = (zk * group_rms_scale(zk, HALF_DIM)) * gk_ref[...]
    if prompt:
        kt_ref[last] = kn.T
        for h in range(N_HEADS):
            cols = _head_cols(h)
            q_ref[h] = qn[:, cols].T.astype(BF16)
            key_ref[h] = kn[:, cols].astype(BF16)
    else:
        q_ref[...] = qn.astype(BF16)
        k_ref[last] = kn

    va = proj(COL_VA)
    if not prompt:
        va_ref[last] = va
    vn = (va * group_rms_scale(va, HEAD_DIM)) * sgug_ref[...]
    gate = proj(COL_U) * jax.nn.silu(proj(COL_GA))
    chunk_rows = [slice(c * SGU_CHUNK, (c + 1) * SGU_CHUNK) for c in range(tile // SGU_CHUNK)]
    for g in range(N_HEADS):
        cols = _head_cols(g)
        vn_t = jnp.concatenate([vn[r, cols].T for r in chunk_rows], axis=0).astype(BF16)
        mixed_t = _dot(vn_t, sguwt_ref[g])
        for r in chunk_rows:
            mixed = mixed_t[r, :].T + sgub_ref[g]
            ya_ref[r, cols] = (gate[r, cols] * mixed).astype(BF16)

    zv = proj(COL_V)
    pieces = (slice(0, tile // 2), slice(tile // 2, 3 * tile // 4), slice(3 * tile // 4, tile))
    gb = jnp.concatenate(
        [jax.nn.silu(_dot(hb[r], w_ref[:, COL_GB:COL_GB + WIDTH]) * row_scale[r]).astype(BF16)
         for r in pieces], axis=0)
    if prompt:
        for h in range(N_HEADS):
            cols = _head_cols(h)
            gb_ref[h] = gb[:, cols]
            vt_ref[h, :HEAD_DIM, :] = zv[:, cols].T.astype(BF16)
            v_ref[last, pl.ds(h, tile, stride=N_HEADS), :] = zv[:, cols]
    else:
        v_ref[last] = zv
        gb_ref[...] = gb


def _inproj_prompt(x, params, prev_out, k_stack, v_stack, *, batch, seq):
    tile = TOKEN_TILE
    n_t = seq // tile
    n = batch * seq
    n_layers = 1 if k_stack is None else k_stack.shape[0] + 1
    full = lambda p: _const_spec(p, single_buffer=True)
    head_major = lambda width: pl.BlockSpec((None, N_HEADS, tile, width),
                                            lambda b, i: (b, 0, i, 0))
    kt_spec = lambda layers: pl.BlockSpec((layers, None, WIDTH, tile),
                                          lambda b, i: (0, b, 0, i))
    v_spec = lambda layers: pl.BlockSpec((layers, tile * N_HEADS, HEAD_DIM),
                                         lambda b, i: (0, b * n_t + i, 0))
    x_spec = pl.BlockSpec((tile, D_MODEL), lambda b, i: (b * n_t + i, 0))
    ya_spec = pl.BlockSpec((tile, WIDTH), lambda b, i: (b * n_t + i, 0))
    in_specs = [x_spec] + [full(p) for p in params]
    operands = [x] + [_array(p) for p in params]
    fused = prev_out is not None
    if fused:
        ya_prev, yb_prev, w_out_prev = prev_out
        in_specs += [ya_spec, head_major(HEAD_DIM), full(w_out_prev)]
        operands += [ya_prev, yb_prev, _array(w_out_prev)]
    if n_layers > 1:
        in_specs += [kt_spec(n_layers - 1), v_spec(n_layers - 1)]
        operands += [k_stack, v_stack]
    out_shape = [
        jax.ShapeDtypeStruct((n, WIDTH), BF16),
        jax.ShapeDtypeStruct((batch, N_HEADS, HEAD_DIM, seq), BF16),
        jax.ShapeDtypeStruct((batch, N_HEADS, n_t, tile, HEAD_DIM), BF16),
        jax.ShapeDtypeStruct((n_layers, batch, WIDTH, seq), F32),
        jax.ShapeDtypeStruct((n_layers, n * N_HEADS, HEAD_DIM), F32),
        jax.ShapeDtypeStruct((batch, N_HEADS, n_t, VT_ROWS, tile), BF16),
        jax.ShapeDtypeStruct((batch, N_HEADS, seq, HEAD_DIM), BF16),
    ]
    out_specs = [
        ya_spec,
        pl.BlockSpec((None, N_HEADS, HEAD_DIM, tile), lambda b, i: (b, 0, 0, i)),
        pl.BlockSpec((None, N_HEADS, None, tile, HEAD_DIM), lambda b, i: (b, 0, i, 0, 0)),
        kt_spec(n_layers),
        v_spec(n_layers),
        pl.BlockSpec((None, N_HEADS, None, VT_ROWS, tile), lambda b, i: (b, 0, i, 0, 0)),
        head_major(HEAD_DIM),
    ]
    if fused:
        out_shape.insert(0, jax.ShapeDtypeStruct((n, D_MODEL), F32))
        out_specs.insert(0, x_spec)
    outs = pl.pallas_call(
        functools.partial(_inproj_kernel, prompt=True, n_prev=2 * (n_layers > 1), fused=fused),
        grid=(batch, n_t),
        in_specs=in_specs,
        out_specs=out_specs,
        out_shape=out_shape,
        compiler_params=_compiler_params(2),
        name="inproj_prompt",
    )(*operands)
    return outs if fused else (x, *outs)


def _inproj_sample(x, params, prev_stacks, *, tile):
    n = x.shape[0]
    layers = prev_stacks[0].shape[0] + 1 if prev_stacks else 1
    row_spec = lambda width: pl.BlockSpec((tile, width), lambda i: (i, 0))
    stack_spec = lambda depth: pl.BlockSpec((depth, tile, WIDTH), lambda i: (0, i, 0))
    row = lambda d: (row_spec(WIDTH), jax.ShapeDtypeStruct((n, WIDTH), d))
    stack = (stack_spec(layers), jax.ShapeDtypeStruct((layers, n, WIDTH), F32))
    out_specs, out_shape = zip(row(BF16), row(BF16), stack, stack, row(BF16), stack)
    return pl.pallas_call(
        functools.partial(_inproj_kernel, prompt=False, n_prev=len(prev_stacks), fused=False),
        grid=(n // tile,),
        in_specs=([row_spec(D_MODEL)] + [_const_spec(p) for p in params]
                  + [stack_spec(layers - 1)] * len(prev_stacks)),
        out_specs=list(out_specs),
        out_shape=list(out_shape),
        compiler_params=_compiler_params(1),
        name="inproj_sample",
    )(x, *[_array(p) for p in params], *prev_stacks)


def _outproj_kernel(x_ref, ya_ref, yb_ref, w_ref, o_ref, *, head_major):
    if head_major:
        yb = jnp.concatenate([yb_ref[h] for h in range(N_HEADS)], axis=1)
    else:
        yb = yb_ref[...]
    y = _dot(ya_ref[...], w_ref[:WIDTH, :]) + _dot(yb, w_ref[WIDTH:, :])
    o_ref[...] = x_ref[...] + y


def _outproj(x, ya, yb, w_bf, *, tile):
    n = x.shape[0]
    head_major = yb.ndim == 4
    if head_major:
        n_t = yb.shape[2] // tile
        yb_spec = pl.BlockSpec((None, N_HEADS, tile, HEAD_DIM),
                               lambda i: (i // n_t, 0, i % n_t, 0))
    else:
        yb_spec = pl.BlockSpec((tile, WIDTH), lambda i: (i, 0))
    return pl.pallas_call(
        functools.partial(_outproj_kernel, head_major=head_major),
        grid=(n // tile,),
        in_specs=[
            pl.BlockSpec((tile, D_MODEL), lambda i: (i, 0)),
            pl.BlockSpec((tile, WIDTH), lambda i: (i, 0)),
            yb_spec,
            _const_spec(w_bf),
        ],
        out_specs=pl.BlockSpec((tile, D_MODEL), lambda i: (i, 0)),
        out_shape=jax.ShapeDtypeStruct((n, D_MODEL), F32),
        compiler_params=_compiler_params(1),
        name="outproj",
    )(x, ya, yb, _array(w_bf))


SCORES_AHEAD = 3
VALUES_BEHIND = 1
ATTN_HEADS_PER_STEP = 2


def _attn_units(seq):
    n_qc = seq // LANE_CHUNK
    qc_per_tile = ATTN_TILE // LANE_CHUNK
    return [(j, qc, half)
            for j in range(seq // ATTN_TILE)
            for qc in range(j * qc_per_tile, n_qc)
            for half in range(2)]


def _attn_kernel(slopes_ref, q_ref, key_ref, pos_ref, vt_ref, gb_ref, sg_ref, lq1_ref, lk1_ref,
                 lq2_ref, lk2_ref, o_ref, *scratch, lam_init):
    heads_per_step = q_ref.shape[0]

    def one_head(hh, carry):
        slope = slopes_ref[pl.program_id(1) * heads_per_step + hh]
        _attn_head(slope, q_ref.at[hh], key_ref.at[hh], pos_ref, vt_ref.at[hh], gb_ref.at[hh],
                   sg_ref, lq1_ref, lk1_ref, lq2_ref, lk2_ref, o_ref.at[hh], *scratch,
                   lam_init=lam_init)
        return carry

    lax.fori_loop(0, heads_per_step, one_head, 0)


def _attn_head(slope, q_ref, key_ref, pos_ref, vt_ref, gb_ref, sg_ref, lq1_ref, lk1_ref,
               lq2_ref, lk2_ref, o_ref, qbd_ref, s_ref, m_ref, acc_ref, *, lam_init):
    seq = q_ref.shape[1]
    t, w = ATTN_TILE, LANE_CHUNK
    assert t == 2 * w
    n_qc = seq // w
    slope2 = slope * LOG2E

    key = lax.broadcasted_iota(jnp.int32, (w, w), 0)
    qry = lax.broadcasted_iota(jnp.int32, (w, w), 1)
    allowed = jnp.right_shift(key, 6) <= jnp.right_shift(qry, 6)
    ahead = jnp.maximum(key - qry, 0).astype(F32)
    dt = jnp.where(allowed, (-2.0 * slope2) * ahead, NEG_INF)

    row = lax.broadcasted_iota(jnp.int32, (HEAD_DIM, w), 0)
    rest = jnp.full((HEAD_DIM, w), slope2, F32)
    slope_rows = jnp.zeros((HEAD_DIM, w), F32)
    for piece in range(POS_SPLIT):
        part = rest.astype(BF16).astype(F32)
        slope_rows = jnp.where((row == piece) | (row == piece + POS_SPLIT), part, slope_rows)
        rest = rest - part
    slope_rows = slope_rows.astype(BF16)
    zero_half = jnp.zeros((HALF_DIM, w), BF16)

    lam = _lam_value(lq1_ref, lk1_ref, lq2_ref, lk2_ref, lam_init)

    def n_keys(j, qc):
        return w if qc * w == j * t else t

    def stacked_queries(qc, half):
        lanes = slice((half * n_qc + qc) * w, (half * n_qc + qc + 1) * w)
        own = slice(half * HALF_DIM, (half + 1) * HALF_DIM)
        other = slice((1 - half) * HALF_DIM, (2 - half) * HALF_DIM)
        qbd_ref[own, lanes] = q_ref[own, qc * w:(qc + 1) * w]
        qbd_ref[other, lanes] = zero_half
        qbd_ref[HEAD_DIM:, lanes] = slope_rows
        return lanes

    def scores(unit):
        j, qc, half = unit
        if j == 0:
            lanes = stacked_queries(qc, half)
        else:
            lanes = slice((half * n_qc + qc) * w, (half * n_qc + qc + 1) * w)
        nk = n_keys(j, qc)
        keys = jnp.concatenate([key_ref[j, :nk, :], pos_ref[j, :nk, :]], axis=1)
        return _dot(keys, qbd_ref[:, lanes])

    def softmax(unit, st):
        j, qc, half = unit
        lanes = slice((half * n_qc + qc) * w, (half * n_qc + qc + 1) * w)
        nk = n_keys(j, qc)
        if qc * w < (j + 1) * t:
            st = st + dt if nk == w else jnp.concatenate([st[:w], st[w:] + dt], axis=0)
        m_cur = jnp.max(st, axis=0, keepdims=True)
        if j == 0:
            m_new, alpha = m_cur, None
        else:
            m_old = m_ref[:, lanes]
            m_new = jnp.maximum(m_old, m_cur)
            alpha = jnp.exp2(m_old - m_new)
        m_ref[:, lanes] = m_new
        return jnp.exp2(st - m_new).astype(BF16), alpha

    def values(unit, p, alpha):
        j, qc, half = unit
        lanes = slice((half * n_qc + qc) * w, (half * n_qc + qc + 1) * w)
        pv = _dot(vt_ref[j, :, :n_keys(j, qc)], p)
        acc_ref[:, lanes] = pv if j == 0 else alpha * acc_ref[:, lanes] + pv

    def normalized(lanes):
        return acc_ref[:HEAD_DIM, lanes] * (1.0 / acc_ref[HEAD_DIM:HEAD_DIM + 1, lanes])

    def finalize(qc):
        rows = slice(qc * w, (qc + 1) * w)
        lanes1 = slice(qc * w, (qc + 1) * w)
        lanes2 = slice((n_qc + qc) * w, (n_qc + qc + 1) * w)
        ot = normalized(lanes1) - lam * normalized(lanes2)
        ot = ot * lax.rsqrt(jnp.mean(ot * ot, axis=0, keepdims=True) + NORM_EPS)
        o = (ot.T * sg_ref[...]) * (1.0 - lam_init)
        o_ref[rows, :] = (o * gb_ref[rows, :].astype(F32)).astype(BF16)

    units = _attn_units(seq)
    n_slots = s_ref.shape[0]

    def emit_scores(u):
        nk = n_keys(*units[u][:2])
        s_ref[u % n_slots, :nk, :] = scores(units[u])

    def load_scores(u):
        return s_ref[u % n_slots, :n_keys(*units[u][:2]), :]

    for u in range(min(SCORES_AHEAD, len(units))):
        emit_scores(u)
    pending_values = collections.deque()

    def run_oldest_values():
        unit_done, p, alpha = pending_values.popleft()
        values(unit_done, p, alpha)
        j_done, qc_done, half_done = unit_done
        if half_done == 1 and qc_done * w < (j_done + 1) * t:
            finalize(qc_done)

    for u, unit in enumerate(units):
        if u + SCORES_AHEAD < len(units):
            emit_scores(u + SCORES_AHEAD)
        pending_values.append((unit, *softmax(unit, load_scores(u))))
        if len(pending_values) > VALUES_BEHIND:
            run_oldest_values()
    while pending_values:
        run_oldest_values()


def _prompt_attention(q, keys, vt, gb, slopes, sg, lq1, lk1, lq2, lk2, *, lam_init):
    batch, _, _, seq = q.shape
    t = ATTN_TILE
    n_tiles = seq // t
    hps = ATTN_HEADS_PER_STEP
    seq_spec = pl.BlockSpec((None, hps, seq, HEAD_DIM), lambda b, h: (b, h, 0, 0))
    vec = _const_spec
    pos = jnp.arange(seq, dtype=jnp.int32)[:, None]
    lane = jnp.arange(HEAD_DIM, dtype=jnp.int32)[None, :]
    lo = jnp.bitwise_and(pos, CHUNK - 1)
    pos_cols = jnp.where(lane < POS_SPLIT, pos - lo, jnp.where(lane < 2 * POS_SPLIT, lo, 0))
    pos_cols = pos_cols.astype(BF16).reshape(n_tiles, t, HEAD_DIM)
    return pl.pallas_call(
        functools.partial(_attn_kernel, lam_init=lam_init),
        grid=(batch, N_HEADS // hps),
        in_specs=[
            pl.BlockSpec(memory_space=pltpu.SMEM),
            pl.BlockSpec((None, hps, HEAD_DIM, seq), lambda b, h: (b, h, 0, 0)),
            pl.BlockSpec((None, hps, n_tiles, t, HEAD_DIM), lambda b, h: (b, h, 0, 0, 0)),
            vec(pos_cols),
            pl.BlockSpec((None, hps, n_tiles, VT_ROWS, t), lambda b, h: (b, h, 0, 0, 0)),
            seq_spec, vec(sg), vec(lq1), vec(lk1), vec(lq2), vec(lk2)],
        out_specs=seq_spec,
        out_shape=jax.ShapeDtypeStruct((batch, N_HEADS, seq, HEAD_DIM), BF16),
        scratch_shapes=[
            pltpu.VMEM((KEY_WIDTH, 2 * seq), BF16),
            pltpu.VMEM((SCORES_AHEAD + 1, t, LANE_CHUNK), F32),
            pltpu.VMEM((1, 2 * seq), F32),
            pltpu.VMEM((VT_ROWS, 2 * seq), F32),
        ],
        compiler_params=_compiler_params(2),
        name="prompt_attn",
    )(slopes, q, keys, pos_cols, vt, gb, *[_array(p) for p in (sg, lq1, lk1, lq2, lk2)])


def _sample_attn_kernel(q_ref, kn_ref, vn_ref, ckt_ref, cv_ref, gb_ref, *rest, **static):
    for s in range(q_ref.shape[0]):
        _sample_attn_stream(q_ref.at[s], kn_ref.at[s], vn_ref.at[s], ckt_ref.at[s], cv_ref.at[s],
                            gb_ref.at[s], *rest[:-1], rest[-1].at[s], **static)


def _sample_attn_stream(q_ref, kn_ref, vn_ref, ckt_ref, cv_ref, gb_ref, sg_ref,
                        lq1_ref, lk1_ref, lq2_ref, lk2_ref, o_ref, *, lam_init, past_len):
    nq = q_ref.shape[0]
    per_head = 2 * nq
    n_rows = N_HEADS * per_head
    lam = _lam_value(lq1_ref, lk1_ref, lq2_ref, lk2_ref, lam_init)

    def alibi(n_cols, key_pos0):
        row = lax.broadcasted_iota(jnp.int32, (n_rows, n_cols), 0)
        col = lax.broadcasted_iota(jnp.int32, (n_rows, n_cols), 1)
        dist = jnp.abs(past_len + lax.rem(row, nq) - (key_pos0 + col)).astype(F32)
        slope = jnp.zeros((n_rows, n_cols), F32)
        for h in range(N_HEADS):
            slope = jnp.where(lax.div(row, per_head) == h, ALIBI_SLOPES[h] * LOG2E, slope)
        return slope * dist

    zpad = jnp.zeros((HEAD_DIM - nq, HEAD_DIM), BF16)
    s_past, s_new, v_new = [], [], []
    for h in range(N_HEADS):
        cols = _head_cols(h)
        q = q_ref[:, cols]
        lane = lax.broadcasted_iota(jnp.int32, q.shape, 1)
        zero = jnp.zeros_like(q)
        qbd = jnp.concatenate([jnp.where(lane < HALF_DIM, q, zero),
                               jnp.where(lane >= HALF_DIM, q, zero)], axis=0)
        s_past.append(_dot(qbd, ckt_ref[cols, :].astype(BF16)))
        s_new.append(_dot_nt(qbd, jnp.concatenate([kn_ref[:, cols].astype(BF16), zpad], axis=0)))
        v_new.append(jnp.concatenate([vn_ref[:, cols].astype(BF16), zpad], axis=0))
    s_past = jnp.concatenate(s_past, axis=0) - alibi(past_len, 0)
    s_new = jnp.concatenate(s_new, axis=0) - alibi(HEAD_DIM, past_len)
    real_new = lax.broadcasted_iota(jnp.int32, s_new.shape, 1) < nq
    s_new = jnp.where(real_new, s_new, NEG_INF)
    m = jnp.maximum(jnp.max(s_past, axis=-1, keepdims=True),
                    jnp.max(s_new, axis=-1, keepdims=True))
    p_past = jnp.exp2(s_past - m)
    p_new = jnp.exp2(s_new - m)
    inv_l = 1.0 / (jnp.sum(p_past, axis=-1, keepdims=True)
                   + jnp.sum(p_new, axis=-1, keepdims=True))
    p_past = p_past.astype(BF16)
    p_new = p_new.astype(BF16)
    outs = []
    for h in range(N_HEADS):
        rows = slice(h * per_head, (h + 1) * per_head)
        v_past = cv_ref[pl.ds(h, past_len, stride=N_HEADS), :].astype(BF16)
        acc = (_dot(p_past[rows], v_past) + _dot(p_new[rows], v_new[h])) * inv_l[rows]
        o = acc[:nq] - lam * acc[nq:]
        outs.append(o * lax.rsqrt(jnp.mean(o * o, axis=-1, keepdims=True) + NORM_EPS))
    o = jnp.concatenate(outs, axis=1) * jnp.tile(sg_ref[...], (1, N_HEADS)) * (1.0 - lam_init)
    o_ref[...] = (o * gb_ref[...].astype(F32)).astype(BF16)


def _sample_attention(q, k_new, v_new, cache_kt, cache_v, gb, sg, lq1, lk1, lq2, lk2,
                      *, layer, n_streams, n_new, lam_init):
    past_len = cache_kt.shape[3]
    per_step = SAMPLE_STREAMS_PER_STEP
    assert n_streams % per_step == 0
    new_spec = pl.BlockSpec((per_step, n_new, WIDTH), lambda b: (b, 0, 0))
    stacked_spec = pl.BlockSpec((None, per_step, n_new, WIDTH), lambda b: (layer, b, 0, 0))
    vec = _const_spec
    r3 = lambda a: a.reshape(*a.shape[:-2], n_streams, n_new, WIDTH)
    out = pl.pallas_call(
        functools.partial(_sample_attn_kernel, lam_init=lam_init, past_len=past_len),
        grid=(n_streams // per_step,),
        in_specs=[
            new_spec, stacked_spec, stacked_spec,
            pl.BlockSpec((None, per_step, WIDTH, past_len), lambda b: (layer, b, 0, 0)),
            pl.BlockSpec((None, per_step, past_len * N_HEADS, HEAD_DIM),
                         lambda b: (layer, b, 0, 0)),
            new_spec, vec(sg), vec(lq1), vec(lk1), vec(lq2), vec(lk2)],
        out_specs=new_spec,
        out_shape=jax.ShapeDtypeStruct((n_streams, n_new, WIDTH), BF16),
        compiler_params=_compiler_params(1),
        name="sample_attn",
    )(r3(q), r3(k_new), r3(v_new), cache_kt, cache_v, r3(gb),
      *[_array(p) for p in (sg, lq1, lk1, lq2, lk2)])
    return out.reshape(n_streams * n_new, WIDTH)


def kernel(x_prompt, x_sample, cache_k, cache_v, norm_g, w_in, sgu_norm_g, sgu_w, sgu_b,
           q_norm_g, k_norm_g, lambda_q1, lambda_k1, lambda_q2, lambda_k2, subln_g, w_out):
    depth = w_in.shape[0]
    batch, seq, _ = x_prompt.shape
    n_streams, n_new, _ = x_sample.shape
    past_len = cache_k.shape[2]
    assert seq % ATTN_TILE == 0
    assert SGU_CHUNK % n_new == 0 and past_len % CHUNK == 0 and n_new <= CHUNK
    sample_tile = min(n_streams * n_new, TOKEN_TILE)
    assert sample_tile % SGU_CHUNK == 0 and (n_streams * n_new) % sample_tile == 0

    slopes = jnp.asarray(ALIBI_SLOPES, F32)
    tril = jnp.tril(jnp.ones((SGU_CHUNK, SGU_CHUNK), F32))
    tril_new = jnp.tril(jnp.ones((n_new, n_new), F32))
    streams_per_chunk = SGU_CHUNK // n_new
    eye = jnp.eye(streams_per_chunk, dtype=F32)
    cache_kt = jnp.transpose(cache_k, (0, 1, 3, 4, 5, 2)).reshape(depth, n_streams, WIDTH, past_len)
    cache_vr = cache_v.reshape(depth, n_streams, past_len * N_HEADS, HEAD_DIM)

    rows = lambda a: a.reshape(depth, 1, -1).astype(F32)
    w_in_bf = w_in.astype(BF16)
    w_out_bf = w_out.astype(BF16)
    ng = rows(norm_g)
    sgug = rows(sgu_norm_g)
    gq = rows(jnp.tile(q_norm_g, (1, WIDTH // HALF_DIM)))
    gk = rows(jnp.tile(k_norm_g, (1, WIDTH // HALF_DIM)))
    sg = rows(subln_g)
    lam_vecs = tuple(rows(a) for a in (lambda_q1, lambda_k1, lambda_q2, lambda_k2))
    sguw_p = jnp.swapaxes(sgu_w * tril, -1, -2).astype(BF16)
    sgub_p = jnp.broadcast_to(sgu_b[..., None], (depth, N_HEADS, SGU_CHUNK, HEAD_DIM))
    w_new = sgu_w[:, :, :n_new, :n_new] * tril_new
    sguw_s = jnp.swapaxes(jnp.einsum("ab,lhts->lhatbs", eye, w_new).reshape(
        depth, N_HEADS, SGU_CHUNK, SGU_CHUNK), -1, -2).astype(BF16)
    sgub_s = jnp.broadcast_to(
        jnp.tile(sgu_b[:, :, :n_new], (1, 1, streams_per_chunk))[..., None],
        (depth, N_HEADS, SGU_CHUNK, HEAD_DIM))

    xp = x_prompt.reshape(batch * seq, D_MODEL)
    xs = x_sample.reshape(n_streams * n_new, D_MODEL)
    k_stack = v_stack = prompt_out = None
    sample_stacks = ()
    for i in range(depth):
        lam_init = _lam_init(i)
        of_layer = lambda a: _LayerParam(a, i)
        params_p = tuple(map(of_layer, (ng, w_in_bf, sgug, sguw_p, sgub_p, gq, gk)))
        params_s = tuple(map(of_layer, (ng, w_in_bf, sgug, sguw_s, sgub_s, gq, gk)))
        attn_params = tuple(map(of_layer, (sg, *lam_vecs)))

        xp, ya, q, keys, k_stack, v_stack, vt, gb = _inproj_prompt(
            xp, params_p, prompt_out, k_stack, v_stack, batch=batch, seq=seq)
        yb = _prompt_attention(q, keys, vt, gb, slopes, *attn_params, lam_init=lam_init)
        prompt_out = (ya, yb, of_layer(w_out_bf))

        ya, q, ks, vs, gb, vas = _inproj_sample(xs, params_s, sample_stacks, tile=sample_tile)
        sample_stacks = (ks, vs, vas)
        yb = _sample_attention(q, ks, vs, cache_kt, cache_vr, gb, *attn_params, layer=i,
                               n_streams=n_streams, n_new=n_new, lam_init=lam_init)
        xs = _outproj(xs, ya, yb, of_layer(w_out_bf), tile=sample_tile)

    xp = _outproj(xp, *prompt_out, tile=OUTPROJ_TILE)
    new_k_prompt = jnp.transpose(
        k_stack.reshape(depth, batch, N_HEADS, 2, HALF_DIM, seq), (0, 1, 5, 2, 3, 4))
    return (
        xp.reshape(batch, seq, D_MODEL),
        xs.reshape(n_streams, n_new, D_MODEL),
        new_k_prompt,
        v_stack.reshape(depth, batch, seq, N_HEADS, HEAD_DIM),
        ks.reshape(depth, n_streams, n_new, N_HEADS, 2, HALF_DIM),
        vs.reshape(depth, n_streams, n_new, N_HEADS, HEAD_DIM),
        vas.reshape(depth, n_streams, n_new, WIDTH),
    )
```

```python
import collections
import functools
import math
from typing import NamedTuple

import jax
import jax.numpy as jnp
from jax import lax
from jax.experimental import pallas as pl
from jax.experimental.pallas import tpu as pltpu

F32 = jnp.float32
BF16 = jnp.bfloat16

D_MODEL = 1024
N_HEADS = 4
HEAD_DIM = 128
HALF_DIM = 64
WIDTH = N_HEADS * HEAD_DIM
CHUNK = 64
SGU_CHUNK = 128
NORM_EPS = 1e-6
NEG_INF = -1e30
QK_SCALE = HALF_DIM ** -0.5
LOG2E = math.log2(math.e)
ALIBI_SLOPES = tuple(2.0 ** (-8.0 * (h + 1) / N_HEADS) for h in range(N_HEADS))

COL_U, COL_VA, COL_GA, COL_Q, COL_K, COL_V, COL_GB = (i * WIDTH for i in range(7))

VMEM_LIMIT_BYTES = 56 * 1024 * 1024

ATTN_TILE = 512
TOKEN_TILE = ATTN_TILE
SAMPLE_STREAMS_PER_STEP = 2
OUTPROJ_TILE = 2048
LANE_CHUNK = 256
KEY_WIDTH = 2 * HEAD_DIM
POS_SPLIT = 3
VT_ROWS = HEAD_DIM + 16


def _lam_init(layer_idx):
    return 0.8 - 0.6 * math.exp(-0.3 * layer_idx)


def _compiler_params(n_axes):
    return pltpu.CompilerParams(
        dimension_semantics=("arbitrary",) * n_axes,
        vmem_limit_bytes=VMEM_LIMIT_BYTES,
    )


class _LayerParam(NamedTuple):
    array: jax.Array
    layer: int


def _array(p):
    return p.array if isinstance(p, _LayerParam) else p


def _const_spec(p, single_buffer=False):
    if isinstance(p, _LayerParam):
        block = (None, *p.array.shape[1:])
        index = (p.layer,) + (0,) * (p.array.ndim - 1)
    else:
        block, index = p.shape, (0,) * p.ndim
    kwargs = dict(pipeline_mode=pl.Buffered(1)) if single_buffer else {}
    return pl.BlockSpec(block, lambda *grid_idx: index, **kwargs)


def _dot(a, b):
    return jnp.dot(a, b, preferred_element_type=F32)


def _dot_nt(a, b):
    return lax.dot_general(a, b, (((1,), (1,)), ((), ())), preferred_element_type=F32)


def _lam_value(lq1_ref, lk1_ref, lq2_ref, lk2_ref, lam_init):
    d1 = jnp.sum(lq1_ref[...] * lk1_ref[...], axis=-1, keepdims=True)
    d2 = jnp.sum(lq2_ref[...] * lk2_ref[...], axis=-1, keepdims=True)
    return jnp.exp(d1) - jnp.exp(d2) + lam_init


def _head_cols(h):
    return slice(h * HEAD_DIM, (h + 1) * HEAD_DIM)


def _inproj_kernel(*refs, prompt, n_prev, fused):
    x_ref, ng_ref, w_ref, sgug_ref, sguwt_ref, sgub_ref, gq_ref, gk_ref = refs[:8]
    n_in = 8
    if fused:
        ya_prev_ref, yb_prev_ref, wo_ref = refs[n_in:n_in + 3]
        n_in += 3
    prev = refs[n_in:n_in + n_prev]
    outs = refs[n_in + n_prev:]
    if fused:
        xo_ref, outs = outs[0], outs[1:]
    tile = x_ref.shape[0]

    first_half = lax.broadcasted_iota(jnp.int32, (tile, HEAD_DIM), 1) < HALF_DIM

    def group_rms_scale(z, group):
        scales = []
        for g in range(WIDTH // HEAD_DIM):
            sq = z[:, _head_cols(g)] ** 2
            if group == HEAD_DIM:
                total = jnp.sum(sq, axis=-1, keepdims=True)
            else:
                total = jnp.where(
                    first_half,
                    jnp.sum(jnp.where(first_half, sq, 0.0), axis=-1, keepdims=True),
                    jnp.sum(jnp.where(first_half, 0.0, sq), axis=-1, keepdims=True))
            scales.append(jnp.broadcast_to(lax.rsqrt(total * (1.0 / group) + NORM_EPS),
                                           (tile, HEAD_DIM)))
        return jnp.concatenate(scales, axis=1)

    if prompt:
        ya_ref, q_ref, key_ref, kt_ref, v_ref, vt_ref, gb_ref = outs
        last = kt_ref.shape[0] - 1
        if n_prev:
            kt_prev_ref, v_prev_ref = prev
            kt_ref[:last] = kt_prev_ref[...]
            v_ref[:last] = v_prev_ref[...]
        for h in range(N_HEADS):
            vt_ref[h, HEAD_DIM:, :] = jnp.ones((VT_ROWS - HEAD_DIM, tile), BF16)
    else:
        ya_ref, q_ref, k_ref, v_ref, gb_ref, va_ref = outs
        last = k_ref.shape[0] - 1
        for stacked_ref, prev_ref in zip((k_ref, v_ref, va_ref), prev):
            stacked_ref[:last] = prev_ref[...]

    x = x_ref[...]
    if fused:
        yb_prev = jnp.concatenate([yb_prev_ref[h] for h in range(N_HEADS)], axis=1)
        x = x + (_dot(ya_prev_ref[...], wo_ref[:WIDTH, :]) + _dot(yb_prev, wo_ref[WIDTH:, :]))
        xo_ref[...] = x
    hb = (x * ng_ref[...]).astype(BF16)
    row_scale = jnp.broadcast_to(
        lax.rsqrt(jnp.mean(x * x, axis=-1, keepdims=True) + NORM_EPS), (tile, WIDTH))

    def proj(col):
        return _dot(hb, w_ref[:, col:col + WIDTH]) * row_scale

    zq = proj(COL_Q)
    qn = (zq * group_rms_scale(zq, HALF_DIM)) * gq_ref[...] * (QK_SCALE * LOG2E)
    zk = proj(COL_K)
    kn = (zk * group_rms_scale(zk, HALF_DIM)) * gk_ref[...]
    if prompt:
        kt_ref[last] = kn.T
        for h in range(N_HEADS):
            cols = _head_cols(h)
            q_ref[h] = qn[:, cols].T.astype(BF16)
            key_ref[h] = kn[:, cols].astype(BF16)
    else:
        q_ref[...] = qn.astype(BF16)
        k_ref[last] = kn

    va = proj(COL_VA)
    if not prompt:
        va_ref[last] = va
    vn = (va * group_rms_scale(va, HEAD_DIM)) * sgug_ref[...]
    gate = proj(COL_U) * jax.nn.silu(proj(COL_GA))
    chunk_rows = [slice(c * SGU_CHUNK, (c + 1) * SGU_CHUNK) for c in range(tile // SGU_CHUNK)]
    for g in range(N_HEADS):
        cols = _head_cols(g)
        vn_t = jnp.concatenate([vn[r, cols].T for r in chunk_rows], axis=0).astype(BF16)
        mixed_t = _dot(vn_t, sguwt_ref[g])
        for r in chunk_rows:
            mixed = mixed_t[r, :].T + sgub_ref[g]
            ya_ref[r, cols] = (gate[r, cols] * mixed).astype(BF16)

    zv = proj(COL_V)
    pieces = (slice(0, tile // 2), slice(tile // 2, 3 * tile // 4), slice(3 * tile // 4, tile))
    gb = jnp.concatenate(
        [jax.nn.silu(_dot(hb[r], w_ref[:, COL_GB:COL_GB + WIDTH]) * row_scale[r]).astype(BF16)
         for r in pieces], axis=0)
    if prompt:
        for h in range(N_HEADS):
            cols = _head_cols(h)
            gb_ref[h] = gb[:, cols]
            vt_ref[h, :HEAD_DIM, :] = zv[:, cols].T.astype(BF16)
            v_ref[last, pl.ds(h, tile, stride=N_HEADS), :] = zv[:, cols]
    else:
        v_ref[last] = zv
        gb_ref[...] = gb


def _inproj_prompt(x, params, prev_out, k_stack, v_stack, *, batch, seq):
    tile = TOKEN_TILE
    n_t = seq // tile
    n = batch * seq
    n_layers = 1 if k_stack is None else k_stack.shape[0] + 1
    full = lambda p: _const_spec(p, single_buffer=True)
    head_major = lambda width: pl.BlockSpec((None, N_HEADS, tile, width),
                                            lambda b, i: (b, 0, i, 0))
    kt_spec = lambda layers: pl.BlockSpec((layers, None, WIDTH, tile),
                                          lambda b, i: (0, b, 0, i))
    v_spec = lambda layers: pl.BlockSpec((layers, tile * N_HEADS, HEAD_DIM),
                                         lambda b, i: (0, b * n_t + i, 0))
    x_spec = pl.BlockSpec((tile, D_MODEL), lambda b, i: (b * n_t + i, 0))
    ya_spec = pl.BlockSpec((tile, WIDTH), lambda b, i: (b * n_t + i, 0))
    in_specs = [x_spec] + [full(p) for p in params]
    operands = [x] + [_array(p) for p in params]
    fused = prev_out is not None
    if fused:
        ya_prev, yb_prev, w_out_prev = prev_out
        in_specs += [ya_spec, head_major(HEAD_DIM), full(w_out_prev)]
        operands += [ya_prev, yb_prev, _array(w_out_prev)]
    if n_layers > 1:
        in_specs += [kt_spec(n_layers - 1), v_spec(n_layers - 1)]
        operands += [k_stack, v_stack]
    out_shape = [
        jax.ShapeDtypeStruct((n, WIDTH), BF16),
        jax.ShapeDtypeStruct((batch, N_HEADS, HEAD_DIM, seq), BF16),
        jax.ShapeDtypeStruct((batch, N_HEADS, n_t, tile, HEAD_DIM), BF16),
        jax.ShapeDtypeStruct((n_layers, batch, WIDTH, seq), F32),
        jax.ShapeDtypeStruct((n_layers, n * N_HEADS, HEAD_DIM), F32),
        jax.ShapeDtypeStruct((batch, N_HEADS, n_t, VT_ROWS, tile), BF16),
        jax.ShapeDtypeStruct((batch, N_HEADS, seq, HEAD_DIM), BF16),
    ]
    out_specs = [
        ya_spec,
        pl.BlockSpec((None, N_HEADS, HEAD_DIM, tile), lambda b, i: (b, 0, 0, i)),
        pl.BlockSpec((None, N_HEADS, None, tile, HEAD_DIM), lambda b, i: (b, 0, i, 0, 0)),
        kt_spec(n_layers),
        v_spec(n_layers),
        pl.BlockSpec((None, N_HEADS, None, VT_ROWS, tile), lambda b, i: (b, 0, i, 0, 0)),
        head_major(HEAD_DIM),
    ]
    if fused:
        out_shape.insert(0, jax.ShapeDtypeStruct((n, D_MODEL), F32))
        out_specs.insert(0, x_spec)
    outs = pl.pallas_call(
        functools.partial(_inproj_kernel, prompt=True, n_prev=2 * (n_layers > 1), fused=fused),
        grid=(batch, n_t),
        in_specs=in_specs,
        out_specs=out_specs,
        out_shape=out_shape,
        compiler_params=_compiler_params(2),
        name="inproj_prompt",
    )(*operands)
    return outs if fused else (x, *outs)


def _inproj_sample(x, params, prev_stacks, *, tile):
    n = x.shape[0]
    layers = prev_stacks[0].shape[0] + 1 if prev_stacks else 1
    row_spec = lambda width: pl.BlockSpec((tile, width), lambda i: (i, 0))
    stack_spec = lambda depth: pl.BlockSpec((depth, tile, WIDTH), lambda i: (0, i, 0))
    row = lambda d: (row_spec(WIDTH), jax.ShapeDtypeStruct((n, WIDTH), d))
    stack = (stack_spec(layers), jax.ShapeDtypeStruct((layers, n, WIDTH), F32))
    out_specs, out_shape = zip(row(BF16), row(BF16), stack, stack, row(BF16), stack)
    return pl.pallas_call(
        functools.partial(_inproj_kernel, prompt=False, n_prev=len(prev_stacks), fused=False),
        grid=(n // tile,),
        in_specs=([row_spec(D_MODEL)] + [_const_spec(p) for p in params]
                  + [stack_spec(layers - 1)] * len(prev_stacks)),
        out_specs=list(out_specs),
        out_shape=list(out_shape),
        compiler_params=_compiler_params(1),
        name="inproj_sample",
    )(x, *[_array(p) for p in params], *prev_stacks)


def _outproj_kernel(x_ref, ya_ref, yb_ref, w_ref, o_ref, *, head_major):
    if head_major:
        yb = jnp.concatenate([yb_ref[h] for h in range(N_HEADS)], axis=1)
    else:
        yb = yb_ref[...]
    y = _dot(ya_ref[...], w_ref[:WIDTH, :]) + _dot(yb, w_ref[WIDTH:, :])
    o_ref[...] = x_ref[...] + y


def _outproj(x, ya, yb, w_bf, *, tile):
    n = x.shape[0]
    head_major = yb.ndim == 4
    if head_major:
        n_t = yb.shape[2] // tile
        yb_spec = pl.BlockSpec((None, N_HEADS, tile, HEAD_DIM),
                               lambda i: (i // n_t, 0, i % n_t, 0))
    else:
        yb_spec = pl.BlockSpec((tile, WIDTH), lambda i: (i, 0))
    return pl.pallas_call(
        functools.partial(_outproj_kernel, head_major=head_major),
        grid=(n // tile,),
        in_specs=[
            pl.BlockSpec((tile, D_MODEL), lambda i: (i, 0)),
            pl.BlockSpec((tile, WIDTH), lambda i: (i, 0)),
            yb_spec,
            _const_spec(w_bf),
        ],
        out_specs=pl.BlockSpec((tile, D_MODEL), lambda i: (i, 0)),
        out_shape=jax.ShapeDtypeStruct((n, D_MODEL), F32),
        compiler_params=_compiler_params(1),
        name="outproj",
    )(x, ya, yb, _array(w_bf))


SCORES_AHEAD = 3
VALUES_BEHIND = 2
ATTN_HEADS_PER_STEP = 2


def _attn_units(seq):
    n_qc = seq // LANE_CHUNK
    qc_per_tile = ATTN_TILE // LANE_CHUNK
    return [(j, qc, half)
            for j in range(seq // ATTN_TILE)
            for qc in range(j * qc_per_tile, n_qc)
            for half in range(2)]


def _attn_kernel(slopes_ref, q_ref, key_ref, pos_ref, vt_ref, gb_ref, sg_ref, lq1_ref, lk1_ref,
                 lq2_ref, lk2_ref, o_ref, *scratch, lam_init):
    heads_per_step = q_ref.shape[0]

    def one_head(hh, carry):
        slope = slopes_ref[pl.program_id(1) * heads_per_step + hh]
        _attn_head(slope, q_ref.at[hh], key_ref.at[hh], pos_ref, vt_ref.at[hh], gb_ref.at[hh],
                   sg_ref, lq1_ref, lk1_ref, lq2_ref, lk2_ref, o_ref.at[hh], *scratch,
                   lam_init=lam_init)
        return carry

    lax.fori_loop(0, heads_per_step, one_head, 0)


def _attn_head(slope, q_ref, key_ref, pos_ref, vt_ref, gb_ref, sg_ref, lq1_ref, lk1_ref,
               lq2_ref, lk2_ref, o_ref, qbd_ref, s_ref, m_ref, acc_ref, *, lam_init):
    seq = q_ref.shape[1]
    t, w = ATTN_TILE, LANE_CHUNK
    assert t == 2 * w
    n_qc = seq // w
    slope2 = slope * LOG2E

    key = lax.broadcasted_iota(jnp.int32, (w, w), 0)
    qry = lax.broadcasted_iota(jnp.int32, (w, w), 1)
    allowed = jnp.right_shift(key, 6) <= jnp.right_shift(qry, 6)
    ahead = jnp.maximum(key - qry, 0).astype(F32)
    dt = jnp.where(allowed, (-2.0 * slope2) * ahead, NEG_INF)

    row = lax.broadcasted_iota(jnp.int32, (HEAD_DIM, w), 0)
    rest = jnp.full((HEAD_DIM, w), slope2, F32)
    slope_rows = jnp.zeros((HEAD_DIM, w), F32)
    for piece in range(POS_SPLIT):
        part = rest.astype(BF16).astype(F32)
        slope_rows = jnp.where((row == piece) | (row == piece + POS_SPLIT), part, slope_rows)
        rest = rest - part
    slope_rows = slope_rows.astype(BF16)
    zero_half = jnp.zeros((HALF_DIM, w), BF16)

    lam = _lam_value(lq1_ref, lk1_ref, lq2_ref, lk2_ref, lam_init)

    def n_keys(j, qc):
        return w if qc * w == j * t else t

    def stacked_queries(qc, half):
        lanes = slice((half * n_qc + qc) * w, (half * n_qc + qc + 1) * w)
        own = slice(half * HALF_DIM, (half + 1) * HALF_DIM)
        other = slice((1 - half) * HALF_DIM, (2 - half) * HALF_DIM)
        qbd_ref[own, lanes] = q_ref[own, qc * w:(qc + 1) * w]
        qbd_ref[other, lanes] = zero_half
        qbd_ref[HEAD_DIM:, lanes] = slope_rows
        return lanes

    def scores(unit):
        j, qc, half = unit
        if j == 0:
            lanes = stacked_queries(qc, half)
        else:
            lanes = slice((half * n_qc + qc) * w, (half * n_qc + qc + 1) * w)
        nk = n_keys(j, qc)
        keys = jnp.concatenate([key_ref[j, :nk, :], pos_ref[j, :nk, :]], axis=1)
        return _dot(keys, qbd_ref[:, lanes])

    def softmax(unit, st):
        j, qc, half = unit
        lanes = slice((half * n_qc + qc) * w, (half * n_qc + qc + 1) * w)
        nk = n_keys(j, qc)
        if qc * w < (j + 1) * t:
            st = st + dt if nk == w else jnp.concatenate([st[:w], st[w:] + dt], axis=0)
        m_cur = jnp.max(st, axis=0, keepdims=True)
        if j == 0:
            m_new, alpha = m_cur, None
        else:
            m_old = m_ref[:, lanes]
            m_new = jnp.maximum(m_old, m_cur)
            alpha = jnp.exp2(m_old - m_new)
        m_ref[:, lanes] = m_new
        return jnp.exp2(st - m_new).astype(BF16), alpha

    def values(unit, p, alpha):
        j, qc, half = unit
        lanes = slice((half * n_qc + qc) * w, (half * n_qc + qc + 1) * w)
        pv = _dot(vt_ref[j, :, :n_keys(j, qc)], p)
        acc_ref[:, lanes] = pv if j == 0 else alpha * acc_ref[:, lanes] + pv

    def normalized(lanes):
        return acc_ref[:HEAD_DIM, lanes] * (1.0 / acc_ref[HEAD_DIM:HEAD_DIM + 1, lanes])

    def finalize(qc):
        rows = slice(qc * w, (qc + 1) * w)
        lanes1 = slice(qc * w, (qc + 1) * w)
        lanes2 = slice((n_qc + qc) * w, (n_qc + qc + 1) * w)
        ot = normalized(lanes1) - lam * normalized(lanes2)
        ot = ot * lax.rsqrt(jnp.mean(ot * ot, axis=0, keepdims=True) + NORM_EPS)
        o = (ot.T * sg_ref[...]) * (1.0 - lam_init)
        o_ref[rows, :] = (o * gb_ref[rows, :].astype(F32)).astype(BF16)

    units = _attn_units(seq)
    n_slots = s_ref.shape[0]

    def emit_scores(u):
        nk = n_keys(*units[u][:2])
        s_ref[u % n_slots, :nk, :] = scores(units[u])

    def load_scores(u):
        return s_ref[u % n_slots, :n_keys(*units[u][:2]), :]

    for u in range(min(SCORES_AHEAD, len(units))):
        emit_scores(u)
    pending_values = collections.deque()

    def run_oldest_values():
        unit_done, p, alpha = pending_values.popleft()
        values(unit_done, p, alpha)
        j_done, qc_done, half_done = unit_done
        if half_done == 1 and qc_done * w < (j_done + 1) * t:
            finalize(qc_done)

    for u, unit in enumerate(units):
        if u + SCORES_AHEAD < len(units):
            emit_scores(u + SCORES_AHEAD)
        pending_values.append((unit, *softmax(unit, load_scores(u))))
        if len(pending_values) > VALUES_BEHIND:
            run_oldest_values()
    while pending_values:
        run_oldest_values()


def _prompt_attention(q, keys, vt, gb, slopes, sg, lq1, lk1, lq2, lk2, *, lam_init):
    batch, _, _, seq = q.shape
    t = ATTN_TILE
    n_tiles = seq // t
    hps = ATTN_HEADS_PER_STEP
    seq_spec = pl.BlockSpec((None, hps, seq, HEAD_DIM), lambda b, h: (b, h, 0, 0))
    vec = _const_spec
    pos = jnp.arange(seq, dtype=jnp.int32)[:, None]
    lane = jnp.arange(HEAD_DIM, dtype=jnp.int32)[None, :]
    lo = jnp.bitwise_and(pos, CHUNK - 1)
    pos_cols = jnp.where(lane < POS_SPLIT, pos - lo, jnp.where(lane < 2 * POS_SPLIT, lo, 0))
    pos_cols = pos_cols.astype(BF16).reshape(n_tiles, t, HEAD_DIM)
    return pl.pallas_call(
        functools.partial(_attn_kernel, lam_init=lam_init),
        grid=(batch, N_HEADS // hps),
        in_specs=[
            pl.BlockSpec(memory_space=pltpu.SMEM),
            pl.BlockSpec((None, hps, HEAD_DIM, seq), lambda b, h: (b, h, 0, 0)),
            pl.BlockSpec((None, hps, n_tiles, t, HEAD_DIM), lambda b, h: (b, h, 0, 0, 0)),
            vec(pos_cols),
            pl.BlockSpec((None, hps, n_tiles, VT_ROWS, t), lambda b, h: (b, h, 0, 0, 0)),
            seq_spec, vec(sg), vec(lq1), vec(lk1), vec(lq2), vec(lk2)],
        out_specs=seq_spec,
        out_shape=jax.ShapeDtypeStruct((batch, N_HEADS, seq, HEAD_DIM), BF16),
        scratch_shapes=[
            pltpu.VMEM((KEY_WIDTH, 2 * seq), BF16),
            pltpu.VMEM((SCORES_AHEAD + 1, t, LANE_CHUNK), F32),
            pltpu.VMEM((1, 2 * seq), F32),
            pltpu.VMEM((VT_ROWS, 2 * seq), F32),
        ],
        compiler_params=_compiler_params(2),
        name="prompt_attn",
    )(slopes, q, keys, pos_cols, vt, gb, *[_array(p) for p in (sg, lq1, lk1, lq2, lk2)])


def _sample_attn_kernel(q_ref, kn_ref, vn_ref, ckt_ref, cv_ref, gb_ref, *rest, **static):
    for s in range(q_ref.shape[0]):
        _sample_attn_stream(q_ref.at[s], kn_ref.at[s], vn_ref.at[s], ckt_ref.at[s], cv_ref.at[s],
                            gb_ref.at[s], *rest[:-1], rest[-1].at[s], **static)


def _sample_attn_stream(q_ref, kn_ref, vn_ref, ckt_ref, cv_ref, gb_ref, sg_ref,
                        lq1_ref, lk1_ref, lq2_ref, lk2_ref, o_ref, *, lam_init, past_len):
    nq = q_ref.shape[0]
    per_head = 2 * nq
    n_rows = N_HEADS * per_head
    lam = _lam_value(lq1_ref, lk1_ref, lq2_ref, lk2_ref, lam_init)

    def alibi(n_cols, key_pos0):
        row = lax.broadcasted_iota(jnp.int32, (n_rows, n_cols), 0)
        col = lax.broadcasted_iota(jnp.int32, (n_rows, n_cols), 1)
        dist = jnp.abs(past_len + lax.rem(row, nq) - (key_pos0 + col)).astype(F32)
        slope = jnp.zeros((n_rows, n_cols), F32)
        for h in range(N_HEADS):
            slope = jnp.where(lax.div(row, per_head) == h, ALIBI_SLOPES[h] * LOG2E, slope)
        return slope * dist

    zpad = jnp.zeros((HEAD_DIM - nq, HEAD_DIM), BF16)
    s_past, s_new, v_new = [], [], []
    for h in range(N_HEADS):
        cols = _head_cols(h)
        q = q_ref[:, cols]
        lane = lax.broadcasted_iota(jnp.int32, q.shape, 1)
        zero = jnp.zeros_like(q)
        qbd = jnp.concatenate([jnp.where(lane < HALF_DIM, q, zero),
                               jnp.where(lane >= HALF_DIM, q, zero)], axis=0)
        s_past.append(_dot(qbd, ckt_ref[cols, :].astype(BF16)))
        s_new.append(_dot_nt(qbd, jnp.concatenate([kn_ref[:, cols].astype(BF16), zpad], axis=0)))
        v_new.append(jnp.concatenate([vn_ref[:, cols].astype(BF16), zpad], axis=0))
    s_past = jnp.concatenate(s_past, axis=0) - alibi(past_len, 0)
    s_new = jnp.concatenate(s_new, axis=0) - alibi(HEAD_DIM, past_len)
    real_new = lax.broadcasted_iota(jnp.int32, s_new.shape, 1) < nq
    s_new = jnp.where(real_new, s_new, NEG_INF)
    m = jnp.maximum(jnp.max(s_past, axis=-1, keepdims=True),
                    jnp.max(s_new, axis=-1, keepdims=True))
    p_past = jnp.exp2(s_past - m)
    p_new = jnp.exp2(s_new - m)
    inv_l = 1.0 / (jnp.sum(p_past, axis=-1, keepdims=True)
                   + jnp.sum(p_new, axis=-1, keepdims=True))
    p_past = p_past.astype(BF16)
    p_new = p_new.astype(BF16)
    outs = []
    for h in range(N_HEADS):
        rows = slice(h * per_head, (h + 1) * per_head)
        v_past = cv_ref[pl.ds(h, past_len, stride=N_HEADS), :].astype(BF16)
        acc = (_dot(p_past[rows], v_past) + _dot(p_new[rows], v_new[h])) * inv_l[rows]
        o = acc[:nq] - lam * acc[nq:]
        outs.append(o * lax.rsqrt(jnp.mean(o * o, axis=-1, keepdims=True) + NORM_EPS))
    o = jnp.concatenate(outs, axis=1) * jnp.tile(sg_ref[...], (1, N_HEADS)) * (1.0 - lam_init)
    o_ref[...] = (o * gb_ref[...].astype(F32)).astype(BF16)


def _sample_attention(q, k_new, v_new, cache_kt, cache_v, gb, sg, lq1, lk1, lq2, lk2,
                      *, layer, n_streams, n_new, lam_init):
    past_len = cache_kt.shape[3]
    per_step = SAMPLE_STREAMS_PER_STEP
    assert n_streams % per_step == 0
    new_spec = pl.BlockSpec((per_step, n_new, WIDTH), lambda b: (b, 0, 0))
    stacked_spec = pl.BlockSpec((None, per_step, n_new, WIDTH), lambda b: (layer, b, 0, 0))
    vec = _const_spec
    r3 = lambda a: a.reshape(*a.shape[:-2], n_streams, n_new, WIDTH)
    out = pl.pallas_call(
        functools.partial(_sample_attn_kernel, lam_init=lam_init, past_len=past_len),
        grid=(n_streams // per_step,),
        in_specs=[
            new_spec, stacked_spec, stacked_spec,
            pl.BlockSpec((None, per_step, WIDTH, past_len), lambda b: (layer, b, 0, 0)),
            pl.BlockSpec((None, per_step, past_len * N_HEADS, HEAD_DIM),
                         lambda b: (layer, b, 0, 0)),
            new_spec, vec(sg), vec(lq1), vec(lk1), vec(lq2), vec(lk2)],
        out_specs=new_spec,
        out_shape=jax.ShapeDtypeStruct((n_streams, n_new, WIDTH), BF16),
        compiler_params=_compiler_params(1),
        name="sample_attn",
    )(r3(q), r3(k_new), r3(v_new), cache_kt, cache_v, r3(gb),
      *[_array(p) for p in (sg, lq1, lk1, lq2, lk2)])
    return out.reshape(n_streams * n_new, WIDTH)


def kernel(x_prompt, x_sample, cache_k, cache_v, norm_g, w_in, sgu_norm_g, sgu_w, sgu_b,
           q_norm_g, k_norm_g, lambda_q1, lambda_k1, lambda_q2, lambda_k2, subln_g, w_out):
    depth = w_in.shape[0]
    batch, seq, _ = x_prompt.shape
    n_streams, n_new, _ = x_sample.shape
    past_len = cache_k.shape[2]
    assert seq % ATTN_TILE == 0
    assert SGU_CHUNK % n_new == 0 and past_len % CHUNK == 0 and n_new <= CHUNK
    sample_tile = min(n_streams * n_new, TOKEN_TILE)
    assert sample_tile % SGU_CHUNK == 0 and (n_streams * n_new) % sample_tile == 0

    slopes = jnp.asarray(ALIBI_SLOPES, F32)
    tril = jnp.tril(jnp.ones((SGU_CHUNK, SGU_CHUNK), F32))
    tril_new = jnp.tril(jnp.ones((n_new, n_new), F32))
    streams_per_chunk = SGU_CHUNK // n_new
    eye = jnp.eye(streams_per_chunk, dtype=F32)
    cache_kt = jnp.transpose(cache_k, (0, 1, 3, 4, 5, 2)).reshape(depth, n_streams, WIDTH, past_len)
    cache_vr = cache_v.reshape(depth, n_streams, past_len * N_HEADS, HEAD_DIM)

    rows = lambda a: a.reshape(depth, 1, -1).astype(F32)
    w_in_bf = w_in.astype(BF16)
    w_out_bf = w_out.astype(BF16)
    ng = rows(norm_g)
    sgug = rows(sgu_norm_g)
    gq = rows(jnp.tile(q_norm_g, (1, WIDTH // HALF_DIM)))
    gk = rows(jnp.tile(k_norm_g, (1, WIDTH // HALF_DIM)))
    sg = rows(subln_g)
    lam_vecs = tuple(rows(a) for a in (lambda_q1, lambda_k1, lambda_q2, lambda_k2))
    sguw_p = jnp.swapaxes(sgu_w * tril, -1, -2).astype(BF16)
    sgub_p = jnp.broadcast_to(sgu_b[..., None], (depth, N_HEADS, SGU_CHUNK, HEAD_DIM))
    w_new = sgu_w[:, :, :n_new, :n_new] * tril_new
    sguw_s = jnp.swapaxes(jnp.einsum("ab,lhts->lhatbs", eye, w_new).reshape(
        depth, N_HEADS, SGU_CHUNK, SGU_CHUNK), -1, -2).astype(BF16)
    sgub_s = jnp.broadcast_to(
        jnp.tile(sgu_b[:, :, :n_new], (1, 1, streams_per_chunk))[..., None],
        (depth, N_HEADS, SGU_CHUNK, HEAD_DIM))

    xp = x_prompt.reshape(batch * seq, D_MODEL)
    xs = x_sample.reshape(n_streams * n_new, D_MODEL)
    k_stack = v_stack = prompt_out = None
    sample_stacks = ()
    for i in range(depth):
        lam_init = _lam_init(i)
        of_layer = lambda a: _LayerParam(a, i)
        params_p = tuple(map(of_layer, (ng, w_in_bf, sgug, sguw_p, sgub_p, gq, gk)))
        params_s = tuple(map(of_layer, (ng, w_in_bf, sgug, sguw_s, sgub_s, gq, gk)))
        attn_params = tuple(map(of_layer, (sg, *lam_vecs)))

        xp, ya, q, keys, k_stack, v_stack, vt, gb = _inproj_prompt(
            xp, params_p, prompt_out, k_stack, v_stack, batch=batch, seq=seq)
        yb = _prompt_attention(q, keys, vt, gb, slopes, *attn_params, lam_init=lam_init)
        prompt_out = (ya, yb, of_layer(w_out_bf))

        ya, q, ks, vs, gb, vas = _inproj_sample(xs, params_s, sample_stacks, tile=sample_tile)
        sample_stacks = (ks, vs, vas)
        yb = _sample_attention(q, ks, vs, cache_kt, cache_vr, gb, *attn_params, layer=i,
                               n_streams=n_streams, n_new=n_new, lam_init=lam_init)
        xs = _outproj(xs, ya, yb, of_layer(w_out_bf), tile=sample_tile)

    xp = _outproj(xp, *prompt_out, tile=OUTPROJ_TILE)
    new_k_prompt = jnp.transpose(
        k_stack.reshape(depth, batch, N_HEADS, 2, HALF_DIM, seq), (0, 1, 5, 2, 3, 4))
    return (
        xp.reshape(batch, seq, D_MODEL),
        xs.reshape(n_streams, n_new, D_MODEL),
        new_k_prompt,
        v_stack.reshape(depth, batch, seq, N_HEADS, HEAD_DIM),
        ks.reshape(depth, n_streams, n_new, N_HEADS, 2, HALF_DIM),
        vs.reshape(depth, n_streams, n_new, N_HEADS, HEAD_DIM),
        vas.reshape(depth, n_streams, n_new, WIDTH),
    )
```

```python
import collections
import functools
import math
from typing import NamedTuple

import jax
import jax.numpy as jnp
from jax import lax
from jax.experimental import pallas as pl
from jax.experimental.pallas import tpu as pltpu

F32 = jnp.float32
BF16 = jnp.bfloat16

D_MODEL = 1024
N_HEADS = 4
HEAD_DIM = 128
HALF_DIM = 64
WIDTH = N_HEADS * HEAD_DIM
CHUNK = 64
SGU_CHUNK = 128
NORM_EPS = 1e-6
NEG_INF = -1e30
QK_SCALE = HALF_DIM ** -0.5
LOG2E = math.log2(math.e)
ALIBI_SLOPES = tuple(2.0 ** (-8.0 * (h + 1) / N_HEADS) for h in range(N_HEADS))

COL_U, COL_VA, COL_GA, COL_Q, COL_K, COL_V, COL_GB = (i * WIDTH for i in range(7))

VMEM_LIMIT_BYTES = 56 * 1024 * 1024

ATTN_TILE = 512
TOKEN_TILE = ATTN_TILE
SAMPLE_STREAMS_PER_STEP = 2
OUTPROJ_TILE = 2048
LANE_CHUNK = 256
KEY_WIDTH = 2 * HEAD_DIM
POS_SPLIT = 3
VT_ROWS = HEAD_DIM + 16


def _lam_init(layer_idx):
    return 0.8 - 0.6 * math.exp(-0.3 * layer_idx)


def _compiler_params(n_axes):
    return pltpu.CompilerParams(
        dimension_semantics=("arbitrary",) * n_axes,
        vmem_limit_bytes=VMEM_LIMIT_BYTES,
    )


class _LayerParam(NamedTuple):
    array: jax.Array
    layer: int


def _array(p):
    return p.array if isinstance(p, _LayerParam) else p


def _const_spec(p, single_buffer=False):
    if isinstance(p, _LayerParam):
        block = (None, *p.array.shape[1:])
        index = (p.layer,) + (0,) * (p.array.ndim - 1)
    else:
        block, index = p.shape, (0,) * p.ndim
    kwargs = dict(pipeline_mode=pl.Buffered(1)) if single_buffer else {}
    return pl.BlockSpec(block, lambda *grid_idx: index, **kwargs)


def _dot(a, b):
    return jnp.dot(a, b, preferred_element_type=F32)


def _dot_nt(a, b):
    return lax.dot_general(a, b, (((1,), (1,)), ((), ())), preferred_element_type=F32)


def _lam_value(lq1_ref, lk1_ref, lq2_ref, lk2_ref, lam_init):
    d1 = jnp.sum(lq1_ref[...] * lk1_ref[...], axis=-1, keepdims=True)
    d2 = jnp.sum(lq2_ref[...] * lk2_ref[...], axis=-1, keepdims=True)
    return jnp.exp(d1) - jnp.exp(d2) + lam_init


def _head_cols(h):
    return slice(h * HEAD_DIM, (h + 1) * HEAD_DIM)


def _inproj_kernel(*refs, prompt, n_prev, fused):
    x_ref, ng_ref, w_ref, sgug_ref, sguwt_ref, sgub_ref, gq_ref, gk_ref = refs[:8]
    n_in = 8
    if fused:
        ya_prev_ref, yb_prev_ref, wo_ref = refs[n_in:n_in + 3]
        n_in += 3
    prev = refs[n_in:n_in + n_prev]
    outs = refs[n_in + n_prev:]
    if fused:
        xo_ref, outs = outs[0], outs[1:]
    tile = x_ref.shape[0]

    first_half = lax.broadcasted_iota(jnp.int32, (tile, HEAD_DIM), 1) < HALF_DIM

    def group_rms_scale(z, group):
        scales = []
        for g in range(WIDTH // HEAD_DIM):
            sq = z[:, _head_cols(g)] ** 2
            if group == HEAD_DIM:
                total = jnp.sum(sq, axis=-1, keepdims=True)
            else:
                total = jnp.where(
                    first_half,
                    jnp.sum(jnp.where(first_half, sq, 0.0), axis=-1, keepdims=True),
                    jnp.sum(jnp.where(first_half, 0.0, sq), axis=-1, keepdims=True))
            scales.append(jnp.broadcast_to(lax.rsqrt(total * (1.0 / group) + NORM_EPS),
                                           (tile, HEAD_DIM)))
        return jnp.concatenate(scales, axis=1)

    if prompt:
        ya_ref, q_ref, key_ref, kt_ref, v_ref, vt_ref, gb_ref = outs
        last = kt_ref.shape[0] - 1
        if n_prev:
            kt_prev_ref, v_prev_ref = prev
            kt_ref[:last] = kt_prev_ref[...]
            v_ref[:last] = v_prev_ref[...]
        for h in range(N_HEADS):
            vt_ref[h, HEAD_DIM:, :] = jnp.ones((VT_ROWS - HEAD_DIM, tile), BF16)
    else:
        ya_ref, q_ref, k_ref, v_ref, gb_ref, va_ref = outs
        last = k_ref.shape[0] - 1
        for stacked_ref, prev_ref in zip((k_ref, v_ref, va_ref), prev):
            stacked_ref[:last] = prev_ref[...]

    x = x_ref[...]
    if fused:
        yb_prev = jnp.concatenate([yb_prev_ref[h] for h in range(N_HEADS)], axis=1)
        x = x + (_dot(ya_prev_ref[...], wo_ref[:WIDTH, :]) + _dot(yb_prev, wo_ref[WIDTH:, :]))
        xo_ref[...] = x
    hb = (x * ng_ref[...]).astype(BF16)
    row_scale = jnp.broadcast_to(
        lax.rsqrt(jnp.mean(x * x, axis=-1, keepdims=True) + NORM_EPS), (tile, WIDTH))

    def proj(col):
        return _dot(hb, w_ref[:, col:col + WIDTH]) * row_scale

    zq = proj(COL_Q)
    qn = (zq * group_rms_scale(zq, HALF_DIM)) * gq_ref[...] * (QK_SCALE * LOG2E)
    zk = proj(COL_K)
    kn = (zk * group_rms_scale(zk, HALF_DIM)) * gk_ref[...]
    if prompt:
        kt_ref[last] = kn.T
        for h in range(N_HEADS):
            cols = _head_cols(h)
            q_ref[h] = qn[:, cols].T.astype(BF16)
            key_ref[h] = kn[:, cols].astype(BF16)
    else:
        q_ref[...] = qn.astype(BF16)
        k_ref[last] = kn

    va = proj(COL_VA)
    if not prompt:
        va_ref[last] = va
    vn = (va * group_rms_scale(va, HEAD_DIM)) * sgug_ref[...]
    gate = proj(COL_U) * jax.nn.silu(proj(COL_GA))
    chunk_rows = [slice(c * SGU_CHUNK, (c + 1) * SGU_CHUNK) for c in range(tile // SGU_CHUNK)]
    for g in range(N_HEADS):
        cols = _head_cols(g)
        vn_t = jnp.concatenate([vn[r, cols].T for r in chunk_rows], axis=0).astype(BF16)
        mixed_t = _dot(vn_t, sguwt_ref[g])
        for r in chunk_rows:
            mixed = mixed_t[r, :].T + sgub_ref[g]
            ya_ref[r, cols] = (gate[r, cols] * mixed).astype(BF16)

    zv = proj(COL_V)
    pieces = (slice(0, tile // 2), slice(tile // 2, 3 * tile // 4), slice(3 * tile // 4, tile))
    gb = jnp.concatenate(
        [jax.nn.silu(_dot(hb[r], w_ref[:, COL_GB:COL_GB + WIDTH]) * row_scale[r]).astype(BF16)
         for r in pieces], axis=0)
    if prompt:
        for h in range(N_HEADS):
            cols = _head_cols(h)
            gb_ref[h] = gb[:, cols]
            vt_ref[h, :HEAD_DIM, :] = zv[:, cols].T.astype(BF16)
            v_ref[last, pl.ds(h, tile, stride=N_HEADS), :] = zv[:, cols]
    else:
        v_ref[last] = zv
        gb_ref[...] = gb


def _inproj_prompt(x, params, prev_out, k_stack, v_stack, *, batch, seq):
    tile = TOKEN_TILE
    n_t = seq // tile
    n = batch * seq
    n_layers = 1 if k_stack is None else k_stack.shape[0] + 1
    full = lambda p: _const_spec(p, single_buffer=True)
    head_major = lambda width: pl.BlockSpec((None, N_HEADS, tile, width),
                                            lambda b, i: (b, 0, i, 0))
    kt_spec = lambda layers: pl.BlockSpec((layers, None, WIDTH, tile),
                                          lambda b, i: (0, b, 0, i))
    v_spec = lambda layers: pl.BlockSpec((layers, tile * N_HEADS, HEAD_DIM),
                                         lambda b, i: (0, b * n_t + i, 0))
    x_spec = pl.BlockSpec((tile, D_MODEL), lambda b, i: (b * n_t + i, 0))
    ya_spec = pl.BlockSpec((tile, WIDTH), lambda b, i: (b * n_t + i, 0))
    in_specs = [x_spec] + [full(p) for p in params]
    operands = [x] + [_array(p) for p in params]
    fused = prev_out is not None
    if fused:
        ya_prev, yb_prev, w_out_prev = prev_out
        in_specs += [ya_spec, head_major(HEAD_DIM), full(w_out_prev)]
        operands += [ya_prev, yb_prev, _array(w_out_prev)]
    if n_layers > 1:
        in_specs += [kt_spec(n_layers - 1), v_spec(n_layers - 1)]
        operands += [k_stack, v_stack]
    out_shape = [
        jax.ShapeDtypeStruct((n, WIDTH), BF16),
        jax.ShapeDtypeStruct((batch, N_HEADS, HEAD_DIM, seq), BF16),
        jax.ShapeDtypeStruct((batch, N_HEADS, n_t, tile, HEAD_DIM), BF16),
        jax.ShapeDtypeStruct((n_layers, batch, WIDTH, seq), F32),
        jax.ShapeDtypeStruct((n_layers, n * N_HEADS, HEAD_DIM), F32),
        jax.ShapeDtypeStruct((batch, N_HEADS, n_t, VT_ROWS, tile), BF16),
        jax.ShapeDtypeStruct((batch, N_HEADS, seq, HEAD_DIM), BF16),
    ]
    out_specs = [
        ya_spec,
        pl.BlockSpec((None, N_HEADS, HEAD_DIM, tile), lambda b, i: (b, 0, 0, i)),
        pl.BlockSpec((None, N_HEADS, None, tile, HEAD_DIM), lambda b, i: (b, 0, i, 0, 0)),
        kt_spec(n_layers),
        v_spec(n_layers),
        pl.BlockSpec((None, N_HEADS, None, VT_ROWS, tile), lambda b, i: (b, 0, i, 0, 0)),
        head_major(HEAD_DIM),
    ]
    if fused:
        out_shape.insert(0, jax.ShapeDtypeStruct((n, D_MODEL), F32))
        out_specs.insert(0, x_spec)
    outs = pl.pallas_call(
        functools.partial(_inproj_kernel, prompt=True, n_prev=2 * (n_layers > 1), fused=fused),
        grid=(batch, n_t),
        in_specs=in_specs,
        out_specs=out_specs,
        out_shape=out_shape,
        compiler_params=_compiler_params(2),
        name="inproj_prompt",
    )(*operands)
    return outs if fused else (x, *outs)


def _inproj_sample(x, params, prev_stacks, *, tile):
    n = x.shape[0]
    layers = prev_stacks[0].shape[0] + 1 if prev_stacks else 1
    row_spec = lambda width: pl.BlockSpec((tile, width), lambda i: (i, 0))
    stack_spec = lambda depth: pl.BlockSpec((depth, tile, WIDTH), lambda i: (0, i, 0))
    row = lambda d: (row_spec(WIDTH), jax.ShapeDtypeStruct((n, WIDTH), d))
    stack = (stack_spec(layers), jax.ShapeDtypeStruct((layers, n, WIDTH), F32))
    out_specs, out_shape = zip(row(BF16), row(BF16), stack, stack, row(BF16), stack)
    return pl.pallas_call(
        functools.partial(_inproj_kernel, prompt=False, n_prev=len(prev_stacks), fused=False),
        grid=(n // tile,),
        in_specs=([row_spec(D_MODEL)] + [_const_spec(p) for p in params]
                  + [stack_spec(layers - 1)] * len(prev_stacks)),
        out_specs=list(out_specs),
        out_shape=list(out_shape),
        compiler_params=_compiler_params(1),
        name="inproj_sample",
    )(x, *[_array(p) for p in params], *prev_stacks)


def _outproj_kernel(x_ref, ya_ref, yb_ref, w_ref, o_ref, *, head_major):
    if head_major:
        yb = jnp.concatenate([yb_ref[h] for h in range(N_HEADS)], axis=1)
    else:
        yb = yb_ref[...]
    y = _dot(ya_ref[...], w_ref[:WIDTH, :]) + _dot(yb, w_ref[WIDTH:, :])
    o_ref[...] = x_ref[...] + y


def _outproj(x, ya, yb, w_bf, *, tile):
    n = x.shape[0]
    head_major = yb.ndim == 4
    if head_major:
        n_t = yb.shape[2] // tile
        yb_spec = pl.BlockSpec((None, N_HEADS, tile, HEAD_DIM),
                               lambda i: (i // n_t, 0, i % n_t, 0))
    else:
        yb_spec = pl.BlockSpec((tile, WIDTH), lambda i: (i, 0))
    return pl.pallas_call(
        functools.partial(_outproj_kernel, head_major=head_major),
        grid=(n // tile,),
        in_specs=[
            pl.BlockSpec((tile, D_MODEL), lambda i: (i, 0)),
            pl.BlockSpec((tile, WIDTH), lambda i: (i, 0)),
            yb_spec,
            _const_spec(w_bf),
        ],
        out_specs=pl.BlockSpec((tile, D_MODEL), lambda i: (i, 0)),
        out_shape=jax.ShapeDtypeStruct((n, D_MODEL), F32),
        compiler_params=_compiler_params(1),
        name="outproj",
    )(x, ya, yb, _array(w_bf))


SCORES_AHEAD = 3
VALUES_BEHIND = 3
ATTN_HEADS_PER_STEP = 2


def _attn_units(seq):
    n_qc = seq // LANE_CHUNK
    qc_per_tile = ATTN_TILE // LANE_CHUNK
    return [(j, qc, half)
            for j in range(seq // ATTN_TILE)
            for qc in range(j * qc_per_tile, n_qc)
            for half in range(2)]


def _attn_kernel(slopes_ref, q_ref, key_ref, pos_ref, vt_ref, gb_ref, sg_ref, lq1_ref, lk1_ref,
                 lq2_ref, lk2_ref, o_ref, *scratch, lam_init):
    heads_per_step = q_ref.shape[0]

    def one_head(hh, carry):
        slope = slopes_ref[pl.program_id(1) * heads_per_step + hh]
        _attn_head(slope, q_ref.at[hh], key_ref.at[hh], pos_ref, vt_ref.at[hh], gb_ref.at[hh],
                   sg_ref, lq1_ref, lk1_ref, lq2_ref, lk2_ref, o_ref.at[hh], *scratch,
                   lam_init=lam_init)
        return carry

    lax.fori_loop(0, heads_per_step, one_head, 0)


def _attn_head(slope, q_ref, key_ref, pos_ref, vt_ref, gb_ref, sg_ref, lq1_ref, lk1_ref,
               lq2_ref, lk2_ref, o_ref, qbd_ref, s_ref, m_ref, acc_ref, *, lam_init):
    seq = q_ref.shape[1]
    t, w = ATTN_TILE, LANE_CHUNK
    assert t == 2 * w
    n_qc = seq // w
    slope2 = slope * LOG2E

    key = lax.broadcasted_iota(jnp.int32, (w, w), 0)
    qry = lax.broadcasted_iota(jnp.int32, (w, w), 1)
    allowed = jnp.right_shift(key, 6) <= jnp.right_shift(qry, 6)
    ahead = jnp.maximum(key - qry, 0).astype(F32)
    dt = jnp.where(allowed, (-2.0 * slope2) * ahead, NEG_INF)

    row = lax.broadcasted_iota(jnp.int32, (HEAD_DIM, w), 0)
    rest = jnp.full((HEAD_DIM, w), slope2, F32)
    slope_rows = jnp.zeros((HEAD_DIM, w), F32)
    for piece in range(POS_SPLIT):
        part = rest.astype(BF16).astype(F32)
        slope_rows = jnp.where((row == piece) | (row == piece + POS_SPLIT), part, slope_rows)
        rest = rest - part
    slope_rows = slope_rows.astype(BF16)
    zero_half = jnp.zeros((HALF_DIM, w), BF16)

    lam = _lam_value(lq1_ref, lk1_ref, lq2_ref, lk2_ref, lam_init)

    def n_keys(j, qc):
        return w if qc * w == j * t else t

    def stacked_queries(qc, half):
        lanes = slice((half * n_qc + qc) * w, (half * n_qc + qc + 1) * w)
        own = slice(half * HALF_DIM, (half + 1) * HALF_DIM)
        other = slice((1 - half) * HALF_DIM, (2 - half) * HALF_DIM)
        qbd_ref[own, lanes] = q_ref[own, qc * w:(qc + 1) * w]
        qbd_ref[other, lanes] = zero_half
        qbd_ref[HEAD_DIM:, lanes] = slope_rows
        return lanes

    def scores(unit):
        j, qc, half = unit
        if j == 0:
            lanes = stacked_queries(qc, half)
        else:
            lanes = slice((half * n_qc + qc) * w, (half * n_qc + qc + 1) * w)
        nk = n_keys(j, qc)
        keys = jnp.concatenate([key_ref[j, :nk, :], pos_ref[j, :nk, :]], axis=1)
        return _dot(keys, qbd_ref[:, lanes])

    def softmax(unit, st):
        j, qc, half = unit
        lanes = slice((half * n_qc + qc) * w, (half * n_qc + qc + 1) * w)
        nk = n_keys(j, qc)
        if qc * w < (j + 1) * t:
            st = st + dt if nk == w else jnp.concatenate([st[:w], st[w:] + dt], axis=0)
        m_cur = jnp.max(st, axis=0, keepdims=True)
        if j == 0:
            m_new, alpha = m_cur, None
        else:
            m_old = m_ref[:, lanes]
            m_new = jnp.maximum(m_old, m_cur)
            alpha = jnp.exp2(m_old - m_new)
        m_ref[:, lanes] = m_new
        return jnp.exp2(st - m_new).astype(BF16), alpha

    def values(unit, p, alpha):
        j, qc, half = unit
        lanes = slice((half * n_qc + qc) * w, (half * n_qc + qc + 1) * w)
        pv = _dot(vt_ref[j, :, :n_keys(j, qc)], p)
        acc_ref[:, lanes] = pv if j == 0 else alpha * acc_ref[:, lanes] + pv

    def normalized(lanes):
        return acc_ref[:HEAD_DIM, lanes] * (1.0 / acc_ref[HEAD_DIM:HEAD_DIM + 1, lanes])

    def finalize(qc):
        rows = slice(qc * w, (qc + 1) * w)
        lanes1 = slice(qc * w, (qc + 1) * w)
        lanes2 = slice((n_qc + qc) * w, (n_qc + qc + 1) * w)
        ot = normalized(lanes1) - lam * normalized(lanes2)
        ot = ot * lax.rsqrt(jnp.mean(ot * ot, axis=0, keepdims=True) + NORM_EPS)
        o = (ot.T * sg_ref[...]) * (1.0 - lam_init)
        o_ref[rows, :] = (o * gb_ref[rows, :].astype(F32)).astype(BF16)

    units = _attn_units(seq)
    n_slots = s_ref.shape[0]

    def emit_scores(u):
        nk = n_keys(*units[u][:2])
        s_ref[u % n_slots, :nk, :] = scores(units[u])

    def load_scores(u):
        return s_ref[u % n_slots, :n_keys(*units[u][:2]), :]

    for u in range(min(SCORES_AHEAD, len(units))):
        emit_scores(u)
    pending_values = collections.deque()

    def run_oldest_values():
        unit_done, p, alpha = pending_values.popleft()
        values(unit_done, p, alpha)
        j_done, qc_done, half_done = unit_done
        if half_done == 1 and qc_done * w < (j_done + 1) * t:
            finalize(qc_done)

    for u, unit in enumerate(units):
        if u + SCORES_AHEAD < len(units):
            emit_scores(u + SCORES_AHEAD)
        pending_values.append((unit, *softmax(unit, load_scores(u))))
        if len(pending_values) > VALUES_BEHIND:
            run_oldest_values()
    while pending_values:
        run_oldest_values()


def _prompt_attention(q, keys, vt, gb, slopes, sg, lq1, lk1, lq2, lk2, *, lam_init):
    batch, _, _, seq = q.shape
    t = ATTN_TILE
    n_tiles = seq // t
    hps = ATTN_HEADS_PER_STEP
    seq_spec = pl.BlockSpec((None, hps, seq, HEAD_DIM), lambda b, h: (b, h, 0, 0))
    vec = _const_spec
    pos = jnp.arange(seq, dtype=jnp.int32)[:, None]
    lane = jnp.arange(HEAD_DIM, dtype=jnp.int32)[None, :]
    lo = jnp.bitwise_and(pos, CHUNK - 1)
    pos_cols = jnp.where(lane < POS_SPLIT, pos - lo, jnp.where(lane < 2 * POS_SPLIT, lo, 0))
    pos_cols = pos_cols.astype(BF16).reshape(n_tiles, t, HEAD_DIM)
    return pl.pallas_call(
        functools.partial(_attn_kernel, lam_init=lam_init),
        grid=(batch, N_HEADS // hps),
        in_specs=[
            pl.BlockSpec(memory_space=pltpu.SMEM),
            pl.BlockSpec((None, hps, HEAD_DIM, seq), lambda b, h: (b, h, 0, 0)),
            pl.BlockSpec((None, hps, n_tiles, t, HEAD_DIM), lambda b, h: (b, h, 0, 0, 0)),
            vec(pos_cols),
            pl.BlockSpec((None, hps, n_tiles, VT_ROWS, t), lambda b, h: (b, h, 0, 0, 0)),
            seq_spec, vec(sg), vec(lq1), vec(lk1), vec(lq2), vec(lk2)],
        out_specs=seq_spec,
        out_shape=jax.ShapeDtypeStruct((batch, N_HEADS, seq, HEAD_DIM), BF16),
        scratch_shapes=[
            pltpu.VMEM((KEY_WIDTH, 2 * seq), BF16),
            pltpu.VMEM((SCORES_AHEAD + 1, t, LANE_CHUNK), F32),
            pltpu.VMEM((1, 2 * seq), F32),
            pltpu.VMEM((VT_ROWS, 2 * seq), F32),
        ],
        compiler_params=_compiler_params(2),
        name="prompt_attn",
    )(slopes, q, keys, pos_cols, vt, gb, *[_array(p) for p in (sg, lq1, lk1, lq2, lk2)])


def _sample_attn_kernel(q_ref, kn_ref, vn_ref, ckt_ref, cv_ref, gb_ref, *rest, **static):
    for s in range(q_ref.shape[0]):
        _sample_attn_stream(q_ref.at[s], kn_ref.at[s], vn_ref.at[s], ckt_ref.at[s], cv_ref.at[s],
                            gb_ref.at[s], *rest[:-1], rest[-1].at[s], **static)


def _sample_attn_stream(q_ref, kn_ref, vn_ref, ckt_ref, cv_ref, gb_ref, sg_ref,
                        lq1_ref, lk1_ref, lq2_ref, lk2_ref, o_ref, *, lam_init, past_len):
    nq = q_ref.shape[0]
    per_head = 2 * nq
    n_rows = N_HEADS * per_head
    lam = _lam_value(lq1_ref, lk1_ref, lq2_ref, lk2_ref, lam_init)

    def alibi(n_cols, key_pos0):
        row = lax.broadcasted_iota(jnp.int32, (n_rows, n_cols), 0)
        col = lax.broadcasted_iota(jnp.int32, (n_rows, n_cols), 1)
        dist = jnp.abs(past_len + lax.rem(row, nq) - (key_pos0 + col)).astype(F32)
        slope = jnp.zeros((n_rows, n_cols), F32)
        for h in range(N_HEADS):
            slope = jnp.where(lax.div(row, per_head) == h, ALIBI_SLOPES[h] * LOG2E, slope)
        return slope * dist

    zpad = jnp.zeros((HEAD_DIM - nq, HEAD_DIM), BF16)
    s_past, s_new, v_new = [], [], []
    for h in range(N_HEADS):
        cols = _head_cols(h)
        q = q_ref[:, cols]
        lane = lax.broadcasted_iota(jnp.int32, q.shape, 1)
        zero = jnp.zeros_like(q)
        qbd = jnp.concatenate([jnp.where(lane < HALF_DIM, q, zero),
                               jnp.where(lane >= HALF_DIM, q, zero)], axis=0)
        s_past.append(_dot(qbd, ckt_ref[cols, :].astype(BF16)))
        s_new.append(_dot_nt(qbd, jnp.concatenate([kn_ref[:, cols].astype(BF16), zpad], axis=0)))
        v_new.append(jnp.concatenate([vn_ref[:, cols].astype(BF16), zpad], axis=0))
    s_past = jnp.concatenate(s_past, axis=0) - alibi(past_len, 0)
    s_new = jnp.concatenate(s_new, axis=0) - alibi(HEAD_DIM, past_len)
    real_new = lax.broadcasted_iota(jnp.int32, s_new.shape, 1) < nq
    s_new = jnp.where(real_new, s_new, NEG_INF)
    m = jnp.maximum(jnp.max(s_past, axis=-1, keepdims=True),
                    jnp.max(s_new, axis=-1, keepdims=True))
    p_past = jnp.exp2(s_past - m)
    p_new = jnp.exp2(s_new - m)
    inv_l = 1.0 / (jnp.sum(p_past, axis=-1, keepdims=True)
                   + jnp.sum(p_new, axis=-1, keepdims=True))
    p_past = p_past.astype(BF16)
    p_new = p_new.astype(BF16)
    outs = []
    for h in range(N_HEADS):
        rows = slice(h * per_head, (h + 1) * per_head)
        v_past = cv_ref[pl.ds(h, past_len, stride=N_HEADS), :].astype(BF16)
        acc = (_dot(p_past[rows], v_past) + _dot(p_new[rows], v_new[h])) * inv_l[rows]
        o = acc[:nq] - lam * acc[nq:]
        outs.append(o * lax.rsqrt(jnp.mean(o * o, axis=-1, keepdims=True) + NORM_EPS))
    o = jnp.concatenate(outs, axis=1) * jnp.tile(sg_ref[...], (1, N_HEADS)) * (1.0 - lam_init)
    o_ref[...] = (o * gb_ref[...].astype(F32)).astype(BF16)


def _sample_attention(q, k_new, v_new, cache_kt, cache_v, gb, sg, lq1, lk1, lq2, lk2,
                      *, layer, n_streams, n_new, lam_init):
    past_len = cache_kt.shape[3]
    per_step = SAMPLE_STREAMS_PER_STEP
    assert n_streams % per_step == 0
    new_spec = pl.BlockSpec((per_step, n_new, WIDTH), lambda b: (b, 0, 0))
    stacked_spec = pl.BlockSpec((None, per_step, n_new, WIDTH), lambda b: (layer, b, 0, 0))
    vec = _const_spec
    r3 = lambda a: a.reshape(*a.shape[:-2], n_streams, n_new, WIDTH)
    out = pl.pallas_call(
        functools.partial(_sample_attn_kernel, lam_init=lam_init, past_len=past_len),
        grid=(n_streams // per_step,),
        in_specs=[
            new_spec, stacked_spec, stacked_spec,
            pl.BlockSpec((None, per_step, WIDTH, past_len), lambda b: (layer, b, 0, 0)),
            pl.BlockSpec((None, per_step, past_len * N_HEADS, HEAD_DIM),
                         lambda b: (layer, b, 0, 0)),
            new_spec, vec(sg), vec(lq1), vec(lk1), vec(lq2), vec(lk2)],
        out_specs=new_spec,
        out_shape=jax.ShapeDtypeStruct((n_streams, n_new, WIDTH), BF16),
        compiler_params=_compiler_params(1),
        name="sample_attn",
    )(r3(q), r3(k_new), r3(v_new), cache_kt, cache_v, r3(gb),
      *[_array(p) for p in (sg, lq1, lk1, lq2, lk2)])
    return out.reshape(n_streams * n_new, WIDTH)


def kernel(x_prompt, x_sample, cache_k, cache_v, norm_g, w_in, sgu_norm_g, sgu_w, sgu_b,
           q_norm_g, k_norm_g, lambda_q1, lambda_k1, lambda_q2, lambda_k2, subln_g, w_out):
    depth = w_in.shape[0]
    batch, seq, _ = x_prompt.shape
    n_streams, n_new, _ = x_sample.shape
    past_len = cache_k.shape[2]
    assert seq % ATTN_TILE == 0
    assert SGU_CHUNK % n_new == 0 and past_len % CHUNK == 0 and n_new <= CHUNK
    sample_tile = min(n_streams * n_new, TOKEN_TILE)
    assert sample_tile % SGU_CHUNK == 0 and (n_streams * n_new) % sample_tile == 0

    slopes = jnp.asarray(ALIBI_SLOPES, F32)
    tril = jnp.tril(jnp.ones((SGU_CHUNK, SGU_CHUNK), F32))
    tril_new = jnp.tril(jnp.ones((n_new, n_new), F32))
    streams_per_chunk = SGU_CHUNK // n_new
    eye = jnp.eye(streams_per_chunk, dtype=F32)
    cache_kt = jnp.transpose(cache_k, (0, 1, 3, 4, 5, 2)).reshape(depth, n_streams, WIDTH, past_len)
    cache_vr = cache_v.reshape(depth, n_streams, past_len * N_HEADS, HEAD_DIM)

    rows = lambda a: a.reshape(depth, 1, -1).astype(F32)
    w_in_bf = w_in.astype(BF16)
    w_out_bf = w_out.astype(BF16)
    ng = rows(norm_g)
    sgug = rows(sgu_norm_g)
    gq = rows(jnp.tile(q_norm_g, (1, WIDTH // HALF_DIM)))
    gk = rows(jnp.tile(k_norm_g, (1, WIDTH // HALF_DIM)))
    sg = rows(subln_g)
    lam_vecs = tuple(rows(a) for a in (lambda_q1, lambda_k1, lambda_q2, lambda_k2))
    sguw_p = jnp.swapaxes(sgu_w * tril, -1, -2).astype(BF16)
    sgub_p = jnp.broadcast_to(sgu_b[..., None], (depth, N_HEADS, SGU_CHUNK, HEAD_DIM))
    w_new = sgu_w[:, :, :n_new, :n_new] * tril_new
    sguw_s = jnp.swapaxes(jnp.einsum("ab,lhts->lhatbs", eye, w_new).reshape(
        depth, N_HEADS, SGU_CHUNK, SGU_CHUNK), -1, -2).astype(BF16)
    sgub_s = jnp.broadcast_to(
        jnp.tile(sgu_b[:, :, :n_new], (1, 1, streams_per_chunk))[..., None],
        (depth, N_HEADS, SGU_CHUNK, HEAD_DIM))

    xp = x_prompt.reshape(batch * seq, D_MODEL)
    xs = x_sample.reshape(n_streams * n_new, D_MODEL)
    k_stack = v_stack = prompt_out = None
    sample_stacks = ()
    for i in range(depth):
        lam_init = _lam_init(i)
        of_layer = lambda a: _LayerParam(a, i)
        params_p = tuple(map(of_layer, (ng, w_in_bf, sgug, sguw_p, sgub_p, gq, gk)))
        params_s = tuple(map(of_layer, (ng, w_in_bf, sgug, sguw_s, sgub_s, gq, gk)))
        attn_params = tuple(map(of_layer, (sg, *lam_vecs)))

        xp, ya, q, keys, k_stack, v_stack, vt, gb = _inproj_prompt(
            xp, params_p, prompt_out, k_stack, v_stack, batch=batch, seq=seq)
        yb = _prompt_attention(q, keys, vt, gb, slopes, *attn_params, lam_init=lam_init)
        prompt_out = (ya, yb, of_layer(w_out_bf))

        ya, q, ks, vs, gb, vas = _inproj_sample(xs, params_s, sample_stacks, tile=sample_tile)
        sample_stacks = (ks, vs, vas)
        yb = _sample_attention(q, ks, vs, cache_kt, cache_vr, gb, *attn_params, layer=i,
                               n_streams=n_streams, n_new=n_new, lam_init=lam_init)
        xs = _outproj(xs, ya, yb, of_layer(w_out_bf), tile=sample_tile)

    xp = _outproj(xp, *prompt_out, tile=OUTPROJ_TILE)
    new_k_prompt = jnp.transpose(
        k_stack.reshape(depth, batch, N_HEADS, 2, HALF_DIM, seq), (0, 1, 5, 2, 3, 4))
    return (
        xp.reshape(batch, seq, D_MODEL),
        xs.reshape(n_streams, n_new, D_MODEL),
        new_k_prompt,
        v_stack.reshape(depth, batch, seq, N_HEADS, HEAD_DIM),
        ks.reshape(depth, n_streams, n_new, N_HEADS, 2, HALF_DIM),
        vs.reshape(depth, n_streams, n_new, N_HEADS, HEAD_DIM),
        vas.reshape(depth, n_streams, n_new, WIDTH),
    )
```

```python
import collections
import functools
import math
from typing import NamedTuple

import jax
import jax.numpy as jnp
from jax import lax
from jax.experimental import pallas as pl
from jax.experimental.pallas import tpu as pltpu

F32 = jnp.float32
BF16 = jnp.bfloat16

D_MODEL = 1024
N_HEADS = 4
HEAD_DIM = 128
HALF_DIM = 64
WIDTH = N_HEADS * HEAD_DIM
CHUNK = 64
SGU_CHUNK = 128
NORM_EPS = 1e-6
NEG_INF = -1e30
QK_SCALE = HALF_DIM ** -0.5
LOG2E = math.log2(math.e)
ALIBI_SLOPES = tuple(2.0 ** (-8.0 * (h + 1) / N_HEADS) for h in range(N_HEADS))

COL_U, COL_VA, COL_GA, COL_Q, COL_K, COL_V, COL_GB = (i * WIDTH for i in range(7))

VMEM_LIMIT_BYTES = 56 * 1024 * 1024

ATTN_TILE = 512
TOKEN_TILE = ATTN_TILE
SAMPLE_STREAMS_PER_STEP = 2
OUTPROJ_TILE = 2048
LANE_CHUNK = 256
KEY_WIDTH = 2 * HEAD_DIM
POS_SPLIT = 3
VT_ROWS = HEAD_DIM + 16


def _lam_init(layer_idx):
    return 0.8 - 0.6 * math.exp(-0.3 * layer_idx)


def _compiler_params(n_axes):
    return pltpu.CompilerParams(
        dimension_semantics=("arbitrary",) * n_axes,
        vmem_limit_bytes=VMEM_LIMIT_BYTES,
    )


class _LayerParam(NamedTuple):
    array: jax.Array
    layer: int


def _array(p):
    return p.array if isinstance(p, _LayerParam) else p


def _const_spec(p, single_buffer=False):
    if isinstance(p, _LayerParam):
        block = (None, *p.array.shape[1:])
        index = (p.layer,) + (0,) * (p.array.ndim - 1)
    else:
        block, index = p.shape, (0,) * p.ndim
    kwargs = dict(pipeline_mode=pl.Buffered(1)) if single_buffer else {}
    return pl.BlockSpec(block, lambda *grid_idx: index, **kwargs)


def _dot(a, b):
    return jnp.dot(a, b, preferred_element_type=F32)


def _dot_nt(a, b):
    return lax.dot_general(a, b, (((1,), (1,)), ((), ())), preferred_element_type=F32)


def _lam_value(lq1_ref, lk1_ref, lq2_ref, lk2_ref, lam_init):
    d1 = jnp.sum(lq1_ref[...] * lk1_ref[...], axis=-1, keepdims=True)
    d2 = jnp.sum(lq2_ref[...] * lk2_ref[...], axis=-1, keepdims=True)
    return jnp.exp(d1) - jnp.exp(d2) + lam_init


def _head_cols(h):
    return slice(h * HEAD_DIM, (h + 1) * HEAD_DIM)


def _inproj_kernel(*refs, prompt, n_prev, fused):
    x_ref, ng_ref, w_ref, sgug_ref, sguwt_ref, sgub_ref, gq_ref, gk_ref = refs[:8]
    n_in = 8
    if fused:
        ya_prev_ref, yb_prev_ref, wo_ref = refs[n_in:n_in + 3]
        n_in += 3
    prev = refs[n_in:n_in + n_prev]
    outs = refs[n_in + n_prev:]
    if fused:
        xo_ref, outs = outs[0], outs[1:]
    tile = x_ref.shape[0]

    first_half = lax.broadcasted_iota(jnp.int32, (tile, HEAD_DIM), 1) < HALF_DIM

    def group_rms_scale(z, group):
        scales = []
        for g in range(WIDTH // HEAD_DIM):
            sq = z[:, _head_cols(g)] ** 2
            if group == HEAD_DIM:
                total = jnp.sum(sq, axis=-1, keepdims=True)
            else:
                total = jnp.where(
                    first_half,
                    jnp.sum(jnp.where(first_half, sq, 0.0), axis=-1, keepdims=True),
                    jnp.sum(jnp.where(first_half, 0.0, sq), axis=-1, keepdims=True))
            scales.append(jnp.broadcast_to(lax.rsqrt(total * (1.0 / group) + NORM_EPS),
                                           (tile, HEAD_DIM)))
        return jnp.concatenate(scales, axis=1)

    if prompt:
        ya_ref, q_ref, key_ref, kt_ref, v_ref, vt_ref, gb_ref = outs
        last = kt_ref.shape[0] - 1
        if n_prev:
            kt_prev_ref, v_prev_ref = prev
            kt_ref[:last] = kt_prev_ref[...]
            v_ref[:last] = v_prev_ref[...]
        for h in range(N_HEADS):
            vt_ref[h, HEAD_DIM:, :] = jnp.ones((VT_ROWS - HEAD_DIM, tile), BF16)
    else:
        ya_ref, q_ref, k_ref, v_ref, gb_ref, va_ref = outs
        last = k_ref.shape[0] - 1
        for stacked_ref, prev_ref in zip((k_ref, v_ref, va_ref), prev):
            stacked_ref[:last] = prev_ref[...]

    x = x_ref[...]
    if fused:
        yb_prev = jnp.concatenate([yb_prev_ref[h] for h in range(N_HEADS)], axis=1)
        x = x + (_dot(ya_prev_ref[...], wo_ref[:WIDTH, :]) + _dot(yb_prev, wo_ref[WIDTH:, :]))
        xo_ref[...] = x
    hb = (x * ng_ref[...]).astype(BF16)
    row_scale = jnp.broadcast_to(
        lax.rsqrt(jnp.mean(x * x, axis=-1, keepdims=True) + NORM_EPS), (tile, WIDTH))

    def proj(col):
        return _dot(hb, w_ref[:, col:col + WIDTH]) * row_scale

    zq = proj(COL_Q)
    qn = (zq * group_rms_scale(zq, HALF_DIM)) * gq_ref[...] * (QK_SCALE * LOG2E)
    zk = proj(COL_K)
    kn = (zk * group_rms_scale(zk, HALF_DIM)) * gk_ref[...]
    if prompt:
        kt_ref[last] = kn.T
        for h in range(N_HEADS):
            cols = _head_cols(h)
            q_ref[h] = qn[:, cols].T.astype(BF16)
            key_ref[h] = kn[:, cols].astype(BF16)
    else:
        q_ref[...] = qn.astype(BF16)
        k_ref[last] = kn

    va = proj(COL_VA)
    if not prompt:
        va_ref[last] = va
    vn = (va * group_rms_scale(va, HEAD_DIM)) * sgug_ref[...]
    gate = proj(COL_U) * jax.nn.silu(proj(COL_GA))
    chunk_rows = [slice(c * SGU_CHUNK, (c + 1) * SGU_CHUNK) for c in range(tile // SGU_CHUNK)]
    for g in range(N_HEADS):
        cols = _head_cols(g)
        vn_t = jnp.concatenate([vn[r, cols].T for r in chunk_rows], axis=0).astype(BF16)
        mixed_t = _dot(vn_t, sguwt_ref[g])
        for r in chunk_rows:
            mixed = mixed_t[r, :].T + sgub_ref[g]
            ya_ref[r, cols] = (gate[r, cols] * mixed).astype(BF16)

    zv = proj(COL_V)
    pieces = (slice(0, tile // 2), slice(tile // 2, 3 * tile // 4), slice(3 * tile // 4, tile))
    gb = jnp.concatenate(
        [jax.nn.silu(_dot(hb[r], w_ref[:, COL_GB:COL_GB + WIDTH]) * row_scale[r]).astype(BF16)
         for r in pieces], axis=0)
    if prompt:
        for h in range(N_HEADS):
            cols = _head_cols(h)
            gb_ref[h] = gb[:, cols]
            vt_ref[h, :HEAD_DIM, :] = zv[:, cols].T.astype(BF16)
            v_ref[last, pl.ds(h, tile, stride=N_HEADS), :] = zv[:, cols]
    else:
        v_ref[last] = zv
        gb_ref[...] = gb


def _inproj_prompt(x, params, prev_out, k_stack, v_stack, *, batch, seq):
    tile = TOKEN_TILE
    n_t = seq // tile
    n = batch * seq
    n_layers = 1 if k_stack is None else k_stack.shape[0] + 1
    full = lambda p: _const_spec(p, single_buffer=True)
    head_major = lambda width: pl.BlockSpec((None, N_HEADS, tile, width),
                                            lambda b, i: (b, 0, i, 0))
    kt_spec = lambda layers: pl.BlockSpec((layers, None, WIDTH, tile),
                                          lambda b, i: (0, b, 0, i))
    v_spec = lambda layers: pl.BlockSpec((layers, tile * N_HEADS, HEAD_DIM),
                                         lambda b, i: (0, b * n_t + i, 0))
    x_spec = pl.BlockSpec((tile, D_MODEL), lambda b, i: (b * n_t + i, 0))
    ya_spec = pl.BlockSpec((tile, WIDTH), lambda b, i: (b * n_t + i, 0))
    in_specs = [x_spec] + [full(p) for p in params]
    operands = [x] + [_array(p) for p in params]
    fused = prev_out is not None
    if fused:
        ya_prev, yb_prev, w_out_prev = prev_out
        in_specs += [ya_spec, head_major(HEAD_DIM), full(w_out_prev)]
        operands += [ya_prev, yb_prev, _array(w_out_prev)]
    if n_layers > 1:
        in_specs += [kt_spec(n_layers - 1), v_spec(n_layers - 1)]
        operands += [k_stack, v_stack]
    out_shape = [
        jax.ShapeDtypeStruct((n, WIDTH), BF16),
        jax.ShapeDtypeStruct((batch, N_HEADS, HEAD_DIM, seq), BF16),
        jax.ShapeDtypeStruct((batch, N_HEADS, n_t, tile, HEAD_DIM), BF16),
        jax.ShapeDtypeStruct((n_layers, batch, WIDTH, seq), F32),
        jax.ShapeDtypeStruct((n_layers, n * N_HEADS, HEAD_DIM), F32),
        jax.ShapeDtypeStruct((batch, N_HEADS, n_t, VT_ROWS, tile), BF16),
        jax.ShapeDtypeStruct((batch, N_HEADS, seq, HEAD_DIM), BF16),
    ]
    out_specs = [
        ya_spec,
        pl.BlockSpec((None, N_HEADS, HEAD_DIM, tile), lambda b, i: (b, 0, 0, i)),
        pl.BlockSpec((None, N_HEADS, None, tile, HEAD_DIM), lambda b, i: (b, 0, i, 0, 0)),
        kt_spec(n_layers),
        v_spec(n_layers),
        pl.BlockSpec((None, N_HEADS, None, VT_ROWS, tile), lambda b, i: (b, 0, i, 0, 0)),
        head_major(HEAD_DIM),
    ]
    if fused:
        out_shape.insert(0, jax.ShapeDtypeStruct((n, D_MODEL), F32))
        out_specs.insert(0, x_spec)
    outs = pl.pallas_call(
        functools.partial(_inproj_kernel, prompt=True, n_prev=2 * (n_layers > 1), fused=fused),
        grid=(batch, n_t),
        in_specs=in_specs,
        out_specs=out_specs,
        out_shape=out_shape,
        compiler_params=_compiler_params(2),
        name="inproj_prompt",
    )(*operands)
    return outs if fused else (x, *outs)


def _inproj_sample(x, params, prev_stacks, *, tile):
    n = x.shape[0]
    layers = prev_stacks[0].shape[0] + 1 if prev_stacks else 1
    row_spec = lambda width: pl.BlockSpec((tile, width), lambda i: (i, 0))
    stack_spec = lambda depth: pl.BlockSpec((depth, tile, WIDTH), lambda i: (0, i, 0))
    row = lambda d: (row_spec(WIDTH), jax.ShapeDtypeStruct((n, WIDTH), d))
    stack = (stack_spec(layers), jax.ShapeDtypeStruct((layers, n, WIDTH), F32))
    out_specs, out_shape = zip(row(BF16), row(BF16), stack, stack, row(BF16), stack)
    return pl.pallas_call(
        functools.partial(_inproj_kernel, prompt=False, n_prev=len(prev_stacks), fused=False),
        grid=(n // tile,),
        in_specs=([row_spec(D_MODEL)] + [_const_spec(p) for p in params]
                  + [stack_spec(layers - 1)] * len(prev_stacks)),
        out_specs=list(out_specs),
        out_shape=list(out_shape),
        compiler_params=_compiler_params(1),
        name="inproj_sample",
    )(x, *[_array(p) for p in params], *prev_stacks)


def _outproj_kernel(x_ref, ya_ref, yb_ref, w_ref, o_ref, *, head_major):
    if head_major:
        yb = jnp.concatenate([yb_ref[h] for h in range(N_HEADS)], axis=1)
    else:
        yb = yb_ref[...]
    y = _dot(ya_ref[...], w_ref[:WIDTH, :]) + _dot(yb, w_ref[WIDTH:, :])
    o_ref[...] = x_ref[...] + y


def _outproj(x, ya, yb, w_bf, *, tile):
    n = x.shape[0]
    head_major = yb.ndim == 4
    if head_major:
        n_t = yb.shape[2] // tile
        yb_spec = pl.BlockSpec((None, N_HEADS, tile, HEAD_DIM),
                               lambda i: (i // n_t, 0, i % n_t, 0))
    else:
        yb_spec = pl.BlockSpec((tile, WIDTH), lambda i: (i, 0))
    return pl.pallas_call(
        functools.partial(_outproj_kernel, head_major=head_major),
        grid=(n // tile,),
        in_specs=[
            pl.BlockSpec((tile, D_MODEL), lambda i: (i, 0)),
            pl.BlockSpec((tile, WIDTH), lambda i: (i, 0)),
            yb_spec,
            _const_spec(w_bf),
        ],
        out_specs=pl.BlockSpec((tile, D_MODEL), lambda i: (i, 0)),
        out_shape=jax.ShapeDtypeStruct((n, D_MODEL), F32),
        compiler_params=_compiler_params(1),
        name="outproj",
    )(x, ya, yb, _array(w_bf))


SCORES_AHEAD = 3
VALUES_BEHIND = 2
ATTN_HEADS_PER_STEP = 2


def _attn_units(seq):
    n_qc = seq // LANE_CHUNK
    qc_per_tile = ATTN_TILE // LANE_CHUNK
    return [(j, qc, half)
            for j in range(seq // ATTN_TILE)
            for qc in range(j * qc_per_tile, n_qc)
            for half in range(2)]


def _attn_kernel(slopes_ref, q_ref, key_ref, pos_ref, vt_ref, gb_ref, sg_ref, lq1_ref, lk1_ref,
                 lq2_ref, lk2_ref, o_ref, *scratch, lam_init):
    heads_per_step = q_ref.shape[0]

    def one_head(hh, carry):
        slope = slopes_ref[pl.program_id(1) * heads_per_step + hh]
        _attn_head(slope, q_ref.at[hh], key_ref.at[hh], pos_ref, vt_ref.at[hh], gb_ref.at[hh],
                   sg_ref, lq1_ref, lk1_ref, lq2_ref, lk2_ref, o_ref.at[hh], *scratch,
                   lam_init=lam_init)
        return carry

    lax.fori_loop(0, heads_per_step, one_head, 0)


def _attn_head(slope, q_ref, key_ref, pos_ref, vt_ref, gb_ref, sg_ref, lq1_ref, lk1_ref,
               lq2_ref, lk2_ref, o_ref, qbd_ref, s_ref, m_ref, acc_ref, *, lam_init):
    seq = q_ref.shape[1]
    t, w = ATTN_TILE, LANE_CHUNK
    assert t == 2 * w
    n_qc = seq // w
    slope2 = slope * LOG2E

    key = lax.broadcasted_iota(jnp.int32, (w, w), 0)
    qry = lax.broadcasted_iota(jnp.int32, (w, w), 1)
    allowed = jnp.right_shift(key, 6) <= jnp.right_shift(qry, 6)
    ahead = jnp.maximum(key - qry, 0).astype(F32)
    dt = jnp.where(allowed, (-2.0 * slope2) * ahead, NEG_INF)

    row = lax.broadcasted_iota(jnp.int32, (HEAD_DIM, w), 0)
    rest = jnp.full((HEAD_DIM, w), slope2, F32)
    slope_rows = jnp.zeros((HEAD_DIM, w), F32)
    for piece in range(POS_SPLIT):
        part = rest.astype(BF16).astype(F32)
        slope_rows = jnp.where((row == piece) | (row == piece + POS_SPLIT), part, slope_rows)
        rest = rest - part
    slope_rows = slope_rows.astype(BF16)
    zero_half = jnp.zeros((HALF_DIM, w), BF16)

    lam = _lam_value(lq1_ref, lk1_ref, lq2_ref, lk2_ref, lam_init)

    def n_keys(j, qc):
        return w if qc * w == j * t else t

    def stacked_queries(qc, half):
        lanes = slice((half * n_qc + qc) * w, (half * n_qc + qc + 1) * w)
        own = slice(half * HALF_DIM, (half + 1) * HALF_DIM)
        other = slice((1 - half) * HALF_DIM, (2 - half) * HALF_DIM)
        qbd_ref[own, lanes] = q_ref[own, qc * w:(qc + 1) * w]
        qbd_ref[other, lanes] = zero_half
        qbd_ref[HEAD_DIM:, lanes] = slope_rows
        return lanes

    def scores(unit):
        j, qc, half = unit
        if j == 0:
            lanes = stacked_queries(qc, half)
        else:
            lanes = slice((half * n_qc + qc) * w, (half * n_qc + qc + 1) * w)
        nk = n_keys(j, qc)
        keys = jnp.concatenate([key_ref[j, :nk, :], pos_ref[j, :nk, :]], axis=1)
        return _dot(keys, qbd_ref[:, lanes])

    def softmax(unit, st):
        j, qc, half = unit
        lanes = slice((half * n_qc + qc) * w, (half * n_qc + qc + 1) * w)
        nk = n_keys(j, qc)
        if qc * w < (j + 1) * t:
            st = st + dt if nk == w else jnp.concatenate([st[:w], st[w:] + dt], axis=0)
        m_cur = jnp.max(st, axis=0, keepdims=True)
        if j == 0:
            m_new, alpha = m_cur, None
        else:
            m_old = m_ref[:, lanes]
            m_new = jnp.maximum(m_old, m_cur)
            alpha = jnp.exp2(m_old - m_new)
        m_ref[:, lanes] = m_new
        return jnp.exp2(st - m_new).astype(BF16), alpha

    def values(unit, p, alpha):
        j, qc, half = unit
        lanes = slice((half * n_qc + qc) * w, (half * n_qc + qc + 1) * w)
        pv = _dot(vt_ref[j, :, :n_keys(j, qc)], p)
        acc_ref[:, lanes] = pv if j == 0 else alpha * acc_ref[:, lanes] + pv

    def normalized(lanes):
        return acc_ref[:HEAD_DIM, lanes] * (1.0 / acc_ref[HEAD_DIM:HEAD_DIM + 1, lanes])

    def finalize(qc):
        rows = slice(qc * w, (qc + 1) * w)
        lanes1 = slice(qc * w, (qc + 1) * w)
        lanes2 = slice((n_qc + qc) * w, (n_qc + qc + 1) * w)
        ot = normalized(lanes1) - lam * normalized(lanes2)
        ot = ot * lax.rsqrt(jnp.mean(ot * ot, axis=0, keepdims=True) + NORM_EPS)
        o = (ot.T * sg_ref[...]) * (1.0 - lam_init)
        o_ref[rows, :] = (o * gb_ref[rows, :].astype(F32)).astype(BF16)

    units = _attn_units(seq)
    n_slots = s_ref.shape[0]

    def emit_scores(u):
        nk = n_keys(*units[u][:2])
        s_ref[u % n_slots, :nk, :] = scores(units[u])

    def load_scores(u):
        return s_ref[u % n_slots, :n_keys(*units[u][:2]), :]

    for u in range(min(SCORES_AHEAD, len(units))):
        emit_scores(u)
    pending_values = collections.deque()

    def run_oldest_values():
        unit_done, p, alpha = pending_values.popleft()
        values(unit_done, p, alpha)
        j_done, qc_done, half_done = unit_done
        if half_done == 1 and qc_done * w < (j_done + 1) * t:
            finalize(qc_done)

    for u, unit in enumerate(units):
        if u + SCORES_AHEAD < len(units):
            emit_scores(u + SCORES_AHEAD)
        pending_values.append((unit, *softmax(unit, load_scores(u))))
        if len(pending_values) > VALUES_BEHIND:
            run_oldest_values()
    while pending_values:
        run_oldest_values()


def _prompt_attention(q, keys, vt, gb, slopes, sg, lq1, lk1, lq2, lk2, *, lam_init):
    batch, _, _, seq = q.shape
    t = ATTN_TILE
    n_tiles = seq // t
    hps = ATTN_HEADS_PER_STEP
    seq_spec = pl.BlockSpec((None, hps, seq, HEAD_DIM), lambda b, h: (b, h, 0, 0))
    vec = _const_spec
    pos = jnp.arange(seq, dtype=jnp.int32)[:, None]
    lane = jnp.arange(HEAD_DIM, dtype=jnp.int32)[None, :]
    lo = jnp.bitwise_and(pos, CHUNK - 1)
    pos_cols = jnp.where(lane < POS_SPLIT, pos - lo, jnp.where(lane < 2 * POS_SPLIT, lo, 0))
    pos_cols = pos_cols.astype(BF16).reshape(n_tiles, t, HEAD_DIM)
    return pl.pallas_call(
        functools.partial(_attn_kernel, lam_init=lam_init),
        grid=(batch, N_HEADS // hps),
        in_specs=[
            pl.BlockSpec(memory_space=pltpu.SMEM),
            pl.BlockSpec((None, hps, HEAD_DIM, seq), lambda b, h: (b, h, 0, 0)),
            pl.BlockSpec((None, hps, n_tiles, t, HEAD_DIM), lambda b, h: (b, h, 0, 0, 0)),
            vec(pos_cols),
            pl.BlockSpec((None, hps, n_tiles, VT_ROWS, t), lambda b, h: (b, h, 0, 0, 0)),
            seq_spec, vec(sg), vec(lq1), vec(lk1), vec(lq2), vec(lk2)],
        out_specs=seq_spec,
        out_shape=jax.ShapeDtypeStruct((batch, N_HEADS, seq, HEAD_DIM), BF16),
        scratch_shapes=[
            pltpu.VMEM((KEY_WIDTH, 2 * seq), BF16),
            pltpu.VMEM((SCORES_AHEAD + 1, t, LANE_CHUNK), F32),
            pltpu.VMEM((1, 2 * seq), F32),
            pltpu.VMEM((VT_ROWS, 2 * seq), F32),
        ],
        compiler_params=_compiler_params(2),
        name="prompt_attn",
    )(slopes, q, keys, pos_cols, vt, gb, *[_array(p) for p in (sg, lq1, lk1, lq2, lk2)])


def _sample_attn_kernel(q_ref, kn_ref, vn_ref, ckt_ref, cv_ref, gb_ref, *rest, **static):
    for s in range(q_ref.shape[0]):
        _sample_attn_stream(q_ref.at[s], kn_ref.at[s], vn_ref.at[s], ckt_ref.at[s], cv_ref.at[s],
                            gb_ref.at[s], *rest[:-1], rest[-1].at[s], **static)


def _sample_attn_stream(q_ref, kn_ref, vn_ref, ckt_ref, cv_ref, gb_ref, sg_ref,
                        lq1_ref, lk1_ref, lq2_ref, lk2_ref, o_ref, *, lam_init, past_len):
    nq = q_ref.shape[0]
    per_head = 2 * nq
    n_rows = N_HEADS * per_head
    lam = _lam_value(lq1_ref, lk1_ref, lq2_ref, lk2_ref, lam_init)

    def alibi(n_cols, key_pos0):
        row = lax.broadcasted_iota(jnp.int32, (n_rows, n_cols), 0)
        col = lax.broadcasted_iota(jnp.int32, (n_rows, n_cols), 1)
        dist = jnp.abs(past_len + lax.rem(row, nq) - (key_pos0 + col)).astype(F32)
        slope = jnp.zeros((n_rows, n_cols), F32)
        for h in range(N_HEADS):
            slope = jnp.where(lax.div(row, per_head) == h, ALIBI_SLOPES[h] * LOG2E, slope)
        return slope * dist

    zpad = jnp.zeros((HEAD_DIM - nq, HEAD_DIM), BF16)
    s_past, s_new, v_new = [], [], []
    for h in range(N_HEADS):
        cols = _head_cols(h)
        q = q_ref[:, cols]
        lane = lax.broadcasted_iota(jnp.int32, q.shape, 1)
        zero = jnp.zeros_like(q)
        qbd = jnp.concatenate([jnp.where(lane < HALF_DIM, q, zero),
                               jnp.where(lane >= HALF_DIM, q, zero)], axis=0)
        s_past.append(_dot(qbd, ckt_ref[cols, :].astype(BF16)))
        s_new.append(_dot_nt(qbd, jnp.concatenate([kn_ref[:, cols].astype(BF16), zpad], axis=0)))
        v_new.append(jnp.concatenate([vn_ref[:, cols].astype(BF16), zpad], axis=0))
    s_past = jnp.concatenate(s_past, axis=0) - alibi(past_len, 0)
    s_new = jnp.concatenate(s_new, axis=0) - alibi(HEAD_DIM, past_len)
    real_new = lax.broadcasted_iota(jnp.int32, s_new.shape, 1) < nq
    s_new = jnp.where(real_new, s_new, NEG_INF)
    m = jnp.maximum(jnp.max(s_past, axis=-1, keepdims=True),
                    jnp.max(s_new, axis=-1, keepdims=True))
    p_past = jnp.exp2(s_past - m)
    p_new = jnp.exp2(s_new - m)
    inv_l = 1.0 / (jnp.sum(p_past, axis=-1, keepdims=True)
                   + jnp.sum(p_new, axis=-1, keepdims=True))
    p_past = p_past.astype(BF16)
    p_new = p_new.astype(BF16)
    outs = []
    for h in range(N_HEADS):
        rows = slice(h * per_head, (h + 1) * per_head)
        v_past = cv_ref[pl.ds(h, past_len, stride=N_HEADS), :].astype(BF16)
        acc = (_dot(p_past[rows], v_past) + _dot(p_new[rows], v_new[h])) * inv_l[rows]
        o = acc[:nq] - lam * acc[nq:]
        outs.append(o * lax.rsqrt(jnp.mean(o * o, axis=-1, keepdims=True) + NORM_EPS))
    o = jnp.concatenate(outs, axis=1) * jnp.tile(sg_ref[...], (1, N_HEADS)) * (1.0 - lam_init)
    o_ref[...] = (o * gb_ref[...].astype(F32)).astype(BF16)


def _sample_attention(q, k_new, v_new, cache_kt, cache_v, gb, sg, lq1, lk1, lq2, lk2,
                      *, layer, n_streams, n_new, lam_init):
    past_len = cache_kt.shape[3]
    per_step = SAMPLE_STREAMS_PER_STEP
    assert n_streams % per_step == 0
    new_spec = pl.BlockSpec((per_step, n_new, WIDTH), lambda b: (b, 0, 0))
    stacked_spec = pl.BlockSpec((None, per_step, n_new, WIDTH), lambda b: (layer, b, 0, 0))
    vec = _const_spec
    r3 = lambda a: a.reshape(*a.shape[:-2], n_streams, n_new, WIDTH)
    out = pl.pallas_call(
        functools.partial(_sample_attn_kernel, lam_init=lam_init, past_len=past_len),
        grid=(n_streams // per_step,),
        in_specs=[
            new_spec, stacked_spec, stacked_spec,
            pl.BlockSpec((None, per_step, WIDTH, past_len), lambda b: (layer, b, 0, 0)),
            pl.BlockSpec((None, per_step, past_len * N_HEADS, HEAD_DIM),
                         lambda b: (layer, b, 0, 0)),
            new_spec, vec(sg), vec(lq1), vec(lk1), vec(lq2), vec(lk2)],
        out_specs=new_spec,
        out_shape=jax.ShapeDtypeStruct((n_streams, n_new, WIDTH), BF16),
        compiler_params=_compiler_params(1),
        name="sample_attn",
    )(r3(q), r3(k_new), r3(v_new), cache_kt, cache_v, r3(gb),
      *[_array(p) for p in (sg, lq1, lk1, lq2, lk2)])
    return out.reshape(n_streams * n_new, WIDTH)


def kernel(x_prompt, x_sample, cache_k, cache_v, norm_g, w_in, sgu_norm_g, sgu_w, sgu_b,
           q_norm_g, k_norm_g, lambda_q1, lambda_k1, lambda_q2, lambda_k2, subln_g, w_out):
    depth = w_in.shape[0]
    batch, seq, _ = x_prompt.shape
    n_streams, n_new, _ = x_sample.shape
    past_len = cache_k.shape[2]
    assert seq % ATTN_TILE == 0
    assert SGU_CHUNK % n_new == 0 and past_len % CHUNK == 0 and n_new <= CHUNK
    sample_tile = min(n_streams * n_new, TOKEN_TILE)
    assert sample_tile % SGU_CHUNK == 0 and (n_streams * n_new) % sample_tile == 0

    slopes = jnp.asarray(ALIBI_SLOPES, F32)
    tril = jnp.tril(jnp.ones((SGU_CHUNK, SGU_CHUNK), F32))
    tril_new = jnp.tril(jnp.ones((n_new, n_new), F32))
    streams_per_chunk = SGU_CHUNK // n_new
    eye = jnp.eye(streams_per_chunk, dtype=F32)
    cache_kt = jnp.transpose(cache_k, (0, 1, 3, 4, 5, 2)).reshape(depth, n_streams, WIDTH, past_len)
    cache_vr = cache_v.reshape(depth, n_streams, past_len * N_HEADS, HEAD_DIM)

    rows = lambda a: a.reshape(depth, 1, -1).astype(F32)
    w_in_bf = w_in.astype(BF16)
    w_out_bf = w_out.astype(BF16)
    ng = rows(norm_g)
    sgug = rows(sgu_norm_g)
    gq = rows(jnp.tile(q_norm_g, (1, WIDTH // HALF_DIM)))
    gk = rows(jnp.tile(k_norm_g, (1, WIDTH // HALF_DIM)))
    sg = rows(subln_g)
    lam_vecs = tuple(rows(a) for a in (lambda_q1, lambda_k1, lambda_q2, lambda_k2))
    sguw_p = jnp.swapaxes(sgu_w * tril, -1, -2).astype(BF16)
    sgub_p = jnp.broadcast_to(sgu_b[..., None], (depth, N_HEADS, SGU_CHUNK, HEAD_DIM))
    w_new = sgu_w[:, :, :n_new, :n_new] * tril_new
    sguw_s = jnp.swapaxes(jnp.einsum("ab,lhts->lhatbs", eye, w_new).reshape(
        depth, N_HEADS, SGU_CHUNK, SGU_CHUNK), -1, -2).astype(BF16)
    sgub_s = jnp.broadcast_to(
        jnp.tile(sgu_b[:, :, :n_new], (1, 1, streams_per_chunk))[..., None],
        (depth, N_HEADS, SGU_CHUNK, HEAD_DIM))

    xp = x_prompt.reshape(batch * seq, D_MODEL)
    xs = x_sample.reshape(n_streams * n_new, D_MODEL)
    k_stack = v_stack = prompt_out = None
    sample_stacks = ()
    for i in range(depth):
        lam_init = _lam_init(i)
        of_layer = lambda a: _LayerParam(a, i)
        params_p = tuple(map(of_layer, (ng, w_in_bf, sgug, sguw_p, sgub_p, gq, gk)))
        params_s = tuple(map(of_layer, (ng, w_in_bf, sgug, sguw_s, sgub_s, gq, gk)))
        attn_params = tuple(map(of_layer, (sg, *lam_vecs)))

        xp, ya, q, keys, k_stack, v_stack, vt, gb = _inproj_prompt(
            xp, params_p, prompt_out, k_stack, v_stack, batch=batch, seq=seq)
        yb = _prompt_attention(q, keys, vt, gb, slopes, *attn_params, lam_init=lam_init)
        prompt_out = (ya, yb, of_layer(w_out_bf))

        ya, q, ks, vs, gb, vas = _inproj_sample(xs, params_s, sample_stacks, tile=sample_tile)
        sample_stacks = (ks, vs, vas)
        yb = _sample_attention(q, ks, vs, cache_kt, cache_vr, gb, *attn_params, layer=i,
                               n_streams=n_streams, n_new=n_new, lam_init=lam_init)
        xs = _outproj(xs, ya, yb, of_layer(w_out_bf), tile=sample_tile)

    xp = _outproj(xp, *prompt_out, tile=OUTPROJ_TILE)
    new_k_prompt = jnp.transpose(
        k_stack.reshape(depth, batch, N_HEADS, 2, HALF_DIM, seq), (0, 1, 5, 2, 3, 4))
    return (
        xp.reshape(batch, seq, D_MODEL),
        xs.reshape(n_streams, n_new, D_MODEL),
        new_k_prompt,
        v_stack.reshape(depth, batch, seq, N_HEADS, HEAD_DIM),
        ks.reshape(depth, n_streams, n_new, N_HEADS, 2, HALF_DIM),
        vs.reshape(depth, n_streams, n_new, N_HEADS, HEAD_DIM),
        vas.reshape(depth, n_streams, n_new, WIDTH),
    )
```
